```python
import math
import jax
import jax.numpy as jnp
from jax import lax
import numpy as np

D_MODEL = 1024
BATCH = 1
SEQ = 16384
DEPTH = 2

GRID_W = 64
CTX_LEN = 256

HG_HEADS = 4
HG_DK = 128
HG_DV = 128
HG_WIDTH = HG_HEADS * HG_DK
HG_VWIDTH = HG_HEADS * HG_DV
F_MIN = 1e-30

RET_HEADS = 4
RET_DK = 128
RET_DV = 256
RET_QK = RET_HEADS * RET_DK
RET_V = RET_HEADS * RET_DV
ROPE_BASE = 10000.0

CHUNK = 64

N_GROUPS = 4
EXPERTS_PER_GROUP = 8
N_EXPERTS = N_GROUPS * EXPERTS_PER_GROUP
TOP_K = 2
D_EXPERT = 256
MOE_BLOCK = 128

N_MOD = 6
EPS = 1e-6

IN_SIZES = (HG_WIDTH, HG_WIDTH, HG_WIDTH, HG_VWIDTH, HG_VWIDTH, RET_QK, RET_QK, RET_V, RET_V, D_MODEL, D_MODEL)
D_IN = sum(IN_SIZES)
IN_SPLITS = tuple(int(s) for s in np.cumsum(IN_SIZES)[:-1])

kernel_name = 'hybrid_hgrn2_retention_hmoe_block'


def rms_norm(a, w):
    a32 = a.astype(jnp.float32)
    out = a32 * lax.rsqrt(jnp.mean(a32 * a32, axis=-1, keepdims=True) + EPS) * w.astype(jnp.float32)
    return out.astype(a.dtype)


def modulate(h, shift, scale):
    return h * (1 + scale) + shift


def heads(a, n_heads):
    b, l, w = a.shape
    return a.reshape(b, l, n_heads, w // n_heads).transpose(0, 2, 1, 3)


def merge_heads(o):
    b, h, l, dh = o.shape
    return o.transpose(0, 2, 1, 3).reshape(b, l, h * dh)


def rope_1d(x, pos):
    d = x.shape[-1]
    inv = ROPE_BASE ** (-jnp.arange(0, d, 2, dtype=jnp.float32) / d)
    ang = pos.astype(jnp.float32)[:, None] * inv
    cos, sin = jnp.cos(ang), jnp.sin(ang)
    x1, x2 = jnp.split(x.astype(jnp.float32), 2, axis=-1)
    return jnp.concatenate([x1 * cos - x2 * sin, x1 * sin + x2 * cos], axis=-1).astype(x.dtype)


def rope_2d(x, row_pos, col_pos):
    xr, xc = jnp.split(x, 2, axis=-1)
    return jnp.concatenate([rope_1d(xr, row_pos), rope_1d(xc, col_pos)], axis=-1)


def chunk_recurrence(q, k, v, log_f, s0):
    b_, h_, t_, _ = q.shape
    dv = v.shape[-1]
    dg = log_f.shape[-1]
    n = t_ // CHUNK

    def blocks(a):
        return jnp.moveaxis(a.astype(jnp.float32).reshape(b_, h_, n, CHUNK, a.shape[-1]), 2, 0)

    mask = jnp.tril(jnp.ones((CHUNK, CHUNK), dtype=bool))[:, :, None]

    def step(s, xs):
        qc, kc, vc, gc = xs
        b = jnp.cumsum(gc, axis=2)
        b_last = b[:, :, -1:, :]
        o_inter = jnp.einsum('bhtd,bhde->bhte', qc * jnp.exp(b), s)
        diff = b[:, :, :, None, :] - b[:, :, None, :, :]
        rel = jnp.where(mask, jnp.exp(jnp.minimum(diff, 0.0)), 0.0)
        if dg == 1:
            scores = jnp.einsum('bhtd,bhsd->bhts', qc, kc) * rel[..., 0]
        else:
            scores = jnp.einsum('bhtd,bhsd,bhtsd->bhts', qc, kc, rel)
        o = o_inter + jnp.einsum('bhts,bhse->bhte', scores, vc)
        s_new = s * jnp.exp(b_last[:, :, 0, :])[..., None] + jnp.einsum('bhsd,bhse->bhde', kc * jnp.exp(b_last - b), vc)
        return s_new, o

    s_fin, o = lax.scan(step, s0, (blocks(q), blocks(k), blocks(v), blocks(log_f)))
    return jnp.moveaxis(o, 0, 2).reshape(b_, h_, t_, dv), s_fin


def flip_seq(a):
    return jnp.flip(a, axis=2)


def two_way(ctx_in, lat_in):
    qc, kfc, kbc, vc, gfc, gbc = ctx_in
    ql, kfl, kbl, vl, gfl, gbl = lat_in
    b_, h_, _, dk = qc.shape
    s0 = jnp.zeros((b_, h_, dk, vc.shape[-1]), jnp.float32)
    oc_f, sc_f = chunk_recurrence(qc, kfc, vc, gfc, s0)
    oc_b, sc_b = chunk_recurrence(flip_seq(qc), flip_seq(kbc), flip_seq(vc), flip_seq(gbc), s0)
    ol_f, _ = chunk_recurrence(ql, kfl, vl, gfl, sc_f)
    ol_b, _ = chunk_recurrence(flip_seq(ql), flip_seq(kbl), flip_seq(vl), flip_seq(gbl), sc_b)
    return oc_f + flip_seq(oc_b), ol_f + flip_seq(ol_b)


def hgrn_lower_bounds(logits):
    p = jax.nn.softmax(logits.astype(jnp.float32), axis=0)
    return jnp.clip(jnp.cumsum(p, axis=0) - p[0], 0.0, 1.0 - 1e-6)


def hgrn_forget(z, lb):
    z = z.astype(jnp.float32)
    sig = jax.nn.sigmoid(z)
    f = lb + (1 - lb) * sig
    log_f = jnp.log(jnp.maximum(f, F_MIN))
    k = (1 - lb) * (1 - sig)
    return heads(k, HG_HEADS), heads(log_f, HG_HEADS)


def token_mixer(h, lc, row_pos, col_pos, w_in, lb_f, lb_b, hg_norm_w, ret_logit, ret_gn_w, w_o_hg, w_o_ret, w_out):
    b_, l_, _ = h.shape
    t_ = l_ - lc
    proj = h @ w_in
    hq, hf_f, hf_b, hi, hg, rq, rk, rv, rg, ga, gb = jnp.split(proj, IN_SPLITS, axis=-1)

    q_hg = heads(jax.nn.silu(hq), HG_HEADS) * (HG_DK ** -0.5)
    i_hg = heads(hi, HG_HEADS)
    k_f, lf_f = hgrn_forget(hf_f, lb_f)
    k_b, lf_b = hgrn_forget(hf_b, lb_b)
    parts = (q_hg, k_f, k_b, i_hg, lf_f, lf_b)
    o_hg_c, o_hg_l = two_way([p[:, :, :lc] for p in parts], [p[:, :, lc:] for p in parts])
    o_hg = jnp.concatenate([o_hg_c, o_hg_l], axis=2)
    o_hg = o_hg * lax.rsqrt(jnp.mean(o_hg * o_hg, axis=-1, keepdims=True) + EPS)
    o_hg = merge_heads(o_hg) * hg_norm_w.astype(jnp.float32) * jax.nn.silu(hg.astype(jnp.float32))
    y_hg = o_hg.astype(h.dtype) @ w_o_hg

    q_r = heads(rq, RET_HEADS) * (RET_DK ** -0.5)
    k_r = heads(rk, RET_HEADS)
    v_r = heads(rv, RET_HEADS)
    q_c, q_l = q_r[:, :, :lc], rope_2d(q_r[:, :, lc:], row_pos, col_pos)
    k_c, k_l = k_r[:, :, :lc], rope_2d(k_r[:, :, lc:], row_pos, col_pos)
    log_gamma = jax.nn.log_sigmoid(ret_logit.astype(jnp.float32))

    def decay(direction, length):
        return jnp.broadcast_to(log_gamma[direction][None, :, None, None], (b_, RET_HEADS, length, 1))

    o_r_c, o_r_l = two_way([q_c, k_c, k_c, v_r[:, :, :lc], decay(0, lc), decay(1, lc)],
                           [q_l, k_l, k_l, v_r[:, :, lc:], decay(0, t_), decay(1, t_)])
    o_r = jnp.concatenate([o_r_c, o_r_l], axis=2)
    mu = jnp.mean(o_r, axis=-1, keepdims=True)
    var = jnp.mean(jnp.square(o_r - mu), axis=-1, keepdims=True)
    o_r = merge_heads((o_r - mu) * lax.rsqrt(var + EPS)) * ret_gn_w.astype(jnp.float32)
    o_r = o_r * jax.nn.silu(rg.astype(jnp.float32))
    y_ret = o_r.astype(h.dtype) @ w_o_ret

    y = (jax.nn.sigmoid(ga) * y_hg + jax.nn.sigmoid(gb) * y_ret) @ w_out
    return y[:, :lc], y[:, lc:]


def hierarchical_moe(h, wg, bg, we, be, w_gate, w_up, w_down):
    n_tok, d = h.shape
    h32 = h.astype(jnp.float32)
    g_logits = h32 @ wg.astype(jnp.float32) + bg.astype(jnp.float32)
    g_prob = jax.nn.softmax(g_logits, axis=-1)
    grp = jnp.argmax(g_logits, axis=-1)
    g_val = jnp.take_along_axis(g_prob, grp[:, None], axis=1)
    e_logits = (h32 @ we.astype(jnp.float32) + be.astype(jnp.float32)).reshape(n_tok, N_GROUPS, EXPERTS_PER_GROUP)
    e_in_grp = jnp.take_along_axis(e_logits, grp[:, None, None], axis=1)[:, 0]
    top_val, top_idx = lax.top_k(e_in_grp, TOP_K)
    weights = g_val * jax.nn.softmax(top_val, axis=-1)
    expert = grp[:, None] * EXPERTS_PER_GROUP + top_idx

    n_assign = n_tok * TOP_K
    flat_e = expert.reshape(-1)
    flat_tok = jnp.repeat(jnp.arange(n_tok), TOP_K)
    flat_w = weights.reshape(-1)
    order = jnp.argsort(flat_e)
    e_sorted = flat_e[order]
    tok_sorted = flat_tok[order]
    counts = jnp.bincount(flat_e, length=N_EXPERTS)
    padded = ((counts + MOE_BLOCK - 1) // MOE_BLOCK) * MOE_BLOCK
    seg_start = jnp.cumsum(counts) - counts
    pad_end = jnp.cumsum(padded)
    pad_start = pad_end - padded
    dest = pad_start[e_sorted] + (jnp.arange(n_assign) - seg_start[e_sorted])
    n_blocks = (n_assign + N_EXPERTS * (MOE_BLOCK - 1) + MOE_BLOCK - 1) // MOE_BLOCK
    buf_tok = jnp.full((n_blocks * MOE_BLOCK,), n_tok, jnp.int32).at[dest].set(tok_sorted.astype(jnp.int32))
    block_expert = jnp.minimum(jnp.searchsorted(pad_end, jnp.arange(n_blocks) * MOE_BLOCK, side='right'), N_EXPERTS - 1)
    h_pad = jnp.concatenate([h, jnp.zeros((1, d), h.dtype)], axis=0)
    x_blocks = h_pad[buf_tok].reshape(n_blocks, MOE_BLOCK, d)

    def expert_block(args):
        xb, e = args
        return (jax.nn.silu(xb @ w_gate[e]) * (xb @ w_up[e])) @ w_down[e]

    y_buf = lax.map(expert_block, (x_blocks, block_expert)).reshape(n_blocks * MOE_BLOCK, d)
    y_assign = y_buf[dest] * flat_w[order][:, None].astype(h.dtype)
    return jax.ops.segment_sum(y_assign, tok_sorted, num_segments=n_tok)


def setup_inputs(seed: int = 0) -> dict:
    key = jax.random.key(seed)
    ks = jax.random.split(key, 26)
    f32 = jnp.float32
    d = D_MODEL

    def nrm(k, shape, scale):
        return jax.random.normal(k, shape, f32) * scale

    base_decay_logit = jnp.asarray(np.log(2.0 ** (5 + np.arange(RET_HEADS)) - 1.0), f32)
    return {
        'x': nrm(ks[0], (BATCH, SEQ, d), 1.0),
        'c': nrm(ks[1], (BATCH, d), 1.0),
        'ctx': nrm(ks[2], (BATCH, CTX_LEN, d), 1.0),
        'c_ctx': nrm(ks[3], (d,), 1.0),
        'w_mod': nrm(ks[4], (DEPTH, d, N_MOD * d), 0.5 * d ** -0.5),
        'b_mod': nrm(ks[5], (DEPTH, N_MOD * d), 0.02),
        'norm_mix_w': 1.0 + nrm(ks[6], (DEPTH, d), 0.05),
        'norm_ffn_w': 1.0 + nrm(ks[7], (DEPTH, d), 0.05),
        'w_in': nrm(ks[8], (DEPTH, d, D_IN), d ** -0.5),
        'hgrn_lb_logits': nrm(ks[9], (2, DEPTH, HG_WIDTH), 1.0),
        'hgrn_norm_w': 1.0 + nrm(ks[10], (DEPTH, HG_VWIDTH), 0.05),
        'ret_decay_logit': base_decay_logit + nrm(ks[11], (DEPTH, 2, RET_HEADS), 0.1),
        'ret_gn_w': 1.0 + nrm(ks[12], (DEPTH, RET_V), 0.05),
        'w_o_hgrn': nrm(ks[13], (DEPTH, HG_VWIDTH, d), HG_VWIDTH ** -0.5),
        'w_o_ret': nrm(ks[14], (DEPTH, RET_V, d), RET_V ** -0.5),
        'w_out': nrm(ks[15], (DEPTH, d, d), d ** -0.5),
        'router_group_w': nrm(ks[16], (DEPTH, d, N_GROUPS), d ** -0.5),
        'router_group_b': nrm(ks[17], (DEPTH, N_GROUPS), 0.01),
        'router_expert_w': nrm(ks[18], (DEPTH, d, N_EXPERTS), d ** -0.5),
        'router_expert_b': nrm(ks[19], (DEPTH, N_EXPERTS), 0.01),
        'expert_w_gate': nrm(ks[20], (DEPTH, N_EXPERTS, d, D_EXPERT), d ** -0.5),
        'expert_w_up': nrm(ks[21], (DEPTH, N_EXPERTS, d, D_EXPERT), d ** -0.5),
        'expert_w_down': nrm(ks[22], (DEPTH, N_EXPERTS, D_EXPERT, d), D_EXPERT ** -0.5),
        'final_norm_w': 1.0 + nrm(ks[23], (d,), 0.05),
    }


def reference(x, c, ctx, c_ctx, w_mod, b_mod, norm_mix_w, norm_ffn_w, w_in, hgrn_lb_logits, hgrn_norm_w,
              ret_decay_logit, ret_gn_w, w_o_hgrn, w_o_ret, w_out, router_group_w, router_group_b,
              router_expert_w, router_expert_b, expert_w_gate, expert_w_up, expert_w_down, final_norm_w):
    b_, t_, d = x.shape
    lc = ctx.shape[1]
    ROWS = t_ // GRID_W
    row_pos = jnp.repeat(jnp.arange(ROWS), GRID_W)
    col_pos = jnp.tile(jnp.arange(GRID_W), ROWS)
    lb_fwd = hgrn_lower_bounds(hgrn_lb_logits[0])
    lb_bwd = hgrn_lower_bounds(hgrn_lb_logits[1])
    silu_c = jax.nn.silu(c)
    silu_cc = jax.nn.silu(c_ctx)

    for layer in range(DEPTH):
        mod_x = (silu_c @ w_mod[layer] + b_mod[layer])[:, None, :]
        mod_c = (silu_cc @ w_mod[layer] + b_mod[layer])[None, None, :]
        sh1x, sc1x, g1x, sh2x, sc2x, g2x = jnp.split(mod_x, N_MOD, axis=-1)
        sh1c, sc1c, g1c, sh2c, sc2c, g2c = jnp.split(mod_c, N_MOD, axis=-1)

        h = jnp.concatenate([modulate(rms_norm(ctx, norm_mix_w[layer]), sh1c, sc1c),
                             modulate(rms_norm(x, norm_mix_w[layer]), sh1x, sc1x)], axis=1)
        y_c, y_x = token_mixer(h, lc, row_pos, col_pos, w_in[layer], lb_fwd[layer], lb_bwd[layer],
                               hgrn_norm_w[layer], ret_decay_logit[layer], ret_gn_w[layer],
                               w_o_hgrn[layer], w_o_ret[layer], w_out[layer])
        x = x + g1x * y_x

        moe_args = (router_group_w[layer], router_group_b[layer], router_expert_w[layer], router_expert_b[layer],
                    expert_w_gate[layer], expert_w_up[layer], expert_w_down[layer])
        h_x = modulate(rms_norm(x, norm_ffn_w[layer]), sh2x, sc2x)
        if layer < DEPTH - 1:
            ctx = ctx + g1c * y_c
            h_c = modulate(rms_norm(ctx, norm_ffn_w[layer]), sh2c, sc2c)
            tokens = jnp.concatenate([h_c, h_x], axis=1).reshape(-1, d)
            y = hierarchical_moe(tokens, *moe_args).reshape(b_, lc + t_, d)
            ctx = ctx + g2c * y[:, :lc]
            x = x + g2x * y[:, lc:]
        else:
            y = hierarchical_moe(h_x.reshape(-1, d), *moe_args).reshape(b_, t_, d)
            x = x + g2x * y

    return rms_norm(x, final_norm_w)
```

```python
import functools

import jax
import jax.numpy as jnp
import numpy as np
from jax import lax
from jax.experimental import pallas as pl
from jax.experimental.pallas import tpu as pltpu

F32 = jnp.float32
BF16 = jnp.bfloat16
U32 = jnp.uint32

GRID_W = 64
HG_HEADS = 4
HG_DK = 128
HG_DV = 128
F_MIN = 1e-30
RET_HEADS = 4
RET_DK = 128
RET_DV = 256
ROPE_BASE = 10000.0
N_GROUPS = 4
EXPERTS_PER_GROUP = 8
N_EXPERTS = N_GROUPS * EXPERTS_PER_GROUP
N_MOD = 6
EPS = 1e-6

LANE = 128
COL_TILE = 512
SLABS_PER_TILE = COL_TILE // LANE
HG_CHUNK = 64
HG_SUB = 16
HG_BLOCK = 256
RET_BLOCK = 256
MOE_SEGMENTS = 4
MOE_BLOCK = 128
ROUTE_W = 8
VMEM_LIMIT = 56 * 1024 * 1024

NT_DIMS = (((1,), (1,)), ((), ()))
TN_DIMS = (((0,), (0,)), ((), ()))

SEG_SLABS = (4, 4, 4, 4, 4, 4, 4, 8, 8, 8, 8)
SEG_ORDER = (7, 8, 9, 10, 0, 1, 2, 3, 4, 5, 6)
_starts = {}
_pos = 0
for _seg in SEG_ORDER:
    _starts[_seg] = _pos
    _pos += SEG_SLABS[_seg]
SEG_START = tuple(_starts[_seg] for _seg in range(len(SEG_SLABS)))
(S_HQ, S_KF, S_KB, S_HI, S_HGATE, S_RQ, S_RK, S_RV, S_RG, S_GA, S_GB) = SEG_START
N_SLABS = sum(SEG_SLABS)


def _params(sem):
    return pltpu.CompilerParams(dimension_semantics=sem, vmem_limit_bytes=VMEM_LIMIT)


def _sigmoid(x):
    return 1.0 / (1.0 + jnp.exp(-x))


def _silu(x):
    return x * _sigmoid(x)


def _largest_divisor(n, cap, multiple):
    best = None
    for d in range(multiple, cap + 1, multiple):
        if n % d == 0:
            best = d
    assert best is not None, (n, cap, multiple)
    return best


def _row_select(row0, n_rows, lc, mods_ref, k, d):
    rows = row0 + lax.broadcasted_iota(jnp.int32, (n_rows, 1), 0)
    lat = mods_ref[0:1, k * d:(k + 1) * d]
    ctx = mods_ref[1:2, k * d:(k + 1) * d]
    return jnp.where(rows < lc, ctx, lat)


def _mod_kernel(cc_ref, w_ref, b_ref, o_ref):
    cc = cc_ref[...]
    s = _silu(cc)
    o_ref[0] = jnp.dot(s, w_ref[0], precision=lax.Precision.HIGHEST,
                       preferred_element_type=F32) + b_ref[0]


def _modulation(cc, w_mod, b_mod):
    depth, d, n = w_mod.shape
    tn = _largest_divisor(n, 1536, LANE)
    return pl.pallas_call(
        _mod_kernel,
        grid=(depth, n // tn),
        in_specs=[pl.BlockSpec((8, d), lambda l, j: (0, 0)),
                  pl.BlockSpec((1, d, tn), lambda l, j: (l, 0, j)),
                  pl.BlockSpec((1, 1, tn), lambda l, j: (l, 0, j))],
        out_specs=pl.BlockSpec((1, 8, tn), lambda l, j: (l, 0, j)),
        out_shape=jax.ShapeDtypeStruct((depth, 8, n), F32),
        compiler_params=_params(("arbitrary", "arbitrary")),
        name="modulation",
    )(cc, w_mod, b_mod.reshape(depth, 1, n))


def _inproj_kernel(x_ref, mods_ref, nw_ref, w_ref, lbl_ref, cos_ref, sin_ref,
                   p_ref, lf_ref, h_ref, *, layer, lc, tm, d):
    i = pl.program_id(0)
    j = pl.program_id(1)

    @pl.when(j == 0)
    def _():
        x = x_ref[...]
        xn = x * lax.rsqrt(jnp.mean(x * x, axis=-1, keepdims=True) + EPS) * nw_ref[...]
        shift = _row_select(i * tm, tm, lc, mods_ref, 0, d)
        scale = _row_select(i * tm, tm, lc, mods_ref, 1, d)
        h_ref[...] = (xn * (1.0 + scale) + shift).astype(BF16)

    acc = jnp.dot(h_ref[...], w_ref[...], preferred_element_type=F32)

    def put(val):
        v = val.astype(BF16)
        for s in range(SLABS_PER_TILE):
            p_ref[s] = v[:, s * LANE:(s + 1) * LANE]

    tiles = lambda seg: tuple(range(SEG_START[seg] // SLABS_PER_TILE,
                                    (SEG_START[seg] + SEG_SLABS[seg]) // SLABS_PER_TILE))
    in_tiles = lambda segs: functools.reduce(
        jnp.logical_or, [j == t for seg in segs for t in tiles(seg)])

    @pl.when(in_tiles((0,)))
    def _():
        put(_silu(acc) * (HG_DK ** -0.5))

    @pl.when(in_tiles((1, 2)))
    def _():
        logits = lbl_ref[jnp.clip(j - tiles(1)[0], 0, 1)]
        e = jnp.exp(logits - jnp.max(logits, axis=0, keepdims=True))
        p = e / jnp.sum(e, axis=0, keepdims=True)
        lb = jnp.zeros_like(p[0:1])
        for r in range(1, layer + 1):
            lb = lb + p[r:r + 1]
        lb = jnp.clip(lb, 0.0, 1.0 - 1e-6)
        sig = _sigmoid(acc)
        f = lb + (1.0 - lb) * sig
        lf = jnp.log(jnp.maximum(f, F_MIN))
        for s in range(SLABS_PER_TILE):
            lf_ref[s] = lf[:, s * LANE:(s + 1) * LANE]
        put((1.0 - lb) * (1.0 - sig))

    @pl.when(in_tiles((3, 7)))
    def _():
        put(acc)

    @pl.when(in_tiles((4, 8)))
    def _():
        put(_silu(acc))

    @pl.when(in_tiles((5, 6)))
    def _():
        scale = jnp.where(j == tiles(5)[0], RET_DK ** -0.5, 1.0)
        xq = acc * scale
        n = xq.shape[1]
        lane = lax.broadcasted_iota(jnp.int32, xq.shape, 1)
        partner = jnp.where((lane & 32) == 0, pltpu.roll(xq, n - 32, axis=1),
                            pltpu.roll(xq, 32, axis=1))
        cos = jnp.concatenate([cos_ref[...]] * SLABS_PER_TILE, axis=1)
        sin = jnp.concatenate([sin_ref[...]] * SLABS_PER_TILE, axis=1)
        put(xq * cos + partner * sin)

    @pl.when(in_tiles((9, 10)))
    def _():
        put(_sigmoid(acc))


def _inproj(xc, mods, norm_w, w_in_bf, lb_logits, cos_t, sin_t, *, layer, lc):
    l, d = xc.shape
    d_in = w_in_bf.shape[1]
    assert d_in == N_SLABS * LANE
    tm = _largest_divisor(l, 1280, LANE)
    n_col = d_in // COL_TILE
    kf_tile = S_KF // SLABS_PER_TILE
    kern = functools.partial(_inproj_kernel, layer=layer, lc=lc, tm=tm, d=d)
    return pl.pallas_call(
        kern,
        grid=(l // tm, n_col),
        in_specs=[pl.BlockSpec((tm, d), lambda i, j: (i, 0)),
                  pl.BlockSpec(mods.shape, lambda i, j: (0, 0)),
                  pl.BlockSpec((1, d), lambda i, j: (0, 0)),
                  pl.BlockSpec((d, COL_TILE), lambda i, j: (0, j)),
                  pl.BlockSpec(lb_logits.shape, lambda i, j: (0, 0, 0)),
                  pl.BlockSpec((tm, LANE), lambda i, j: (i, 0)),
                  pl.BlockSpec((tm, LANE), lambda i, j: (i, 0))],
        out_specs=[pl.BlockSpec((SLABS_PER_TILE, tm, LANE), lambda i, j: (j, i, 0)),
                   pl.BlockSpec((SLABS_PER_TILE, tm, LANE),
                                lambda i, j: (jnp.clip(j - kf_tile, 0, 1), i, 0))],
        out_shape=[jax.ShapeDtypeStruct((N_SLABS, l, LANE), BF16),
                   jax.ShapeDtypeStruct((2 * HG_HEADS, l, LANE), F32)],
        scratch_shapes=[pltpu.VMEM((tm, d), BF16)],
        compiler_params=_params(("arbitrary", "arbitrary")),
        name="inproj",
    )(xc, mods, norm_w.reshape(1, d), w_in_bf, lb_logits, cos_t, sin_t)


def _tri(n, lower):
    r = lax.broadcasted_iota(jnp.int32, (n, n), 0)
    c = lax.broadcasted_iota(jnp.int32, (n, n), 1)
    return jnp.where((c <= r) if lower else (c >= r), 1.0, 0.0).astype(F32)


def _cum_log_decay(g, fwd):
    return jnp.dot(_tri(g.shape[0], fwd), g, precision=lax.Precision.HIGHEST,
                   preferred_element_type=F32)


def _hg_bwd_kernel(kb_ref, v_ref, lf_ref, sb_ref, s_ref, *, n_chunks):
    @pl.when(pl.program_id(0) == 0)
    def _():
        s_ref[...] = jnp.zeros_like(s_ref)

    def body(n, carry):
        cc = n_chunks - 1 - n // HG_HEADS
        h = n % HG_HEADS
        r0 = pl.multiple_of(cc * HG_CHUNK, HG_CHUNK)
        bb = _cum_log_decay(lf_ref[h, pl.ds(r0, HG_CHUNK), :], False)
        k = kb_ref[h, pl.ds(r0, HG_CHUNK), :].astype(F32)
        v = v_ref[h, pl.ds(r0, HG_CHUNK), :]
        s = s_ref[h]
        sb_ref[cc, h] = s.astype(BF16)
        kt = (k * jnp.exp(bb[0:1, :] - bb)).astype(BF16)
        s_ref[h] = s * jnp.exp(bb[0:1, :]) + lax.dot_general(
            v, kt, TN_DIMS, preferred_element_type=F32)
        return carry

    lax.fori_loop(0, n_chunks * HG_HEADS, body, 0)


def _bwd_block_order(i, n_ctx_blocks, n_blocks):
    return jnp.where(i < n_ctx_blocks, n_ctx_blocks - 1 - i, n_blocks - 1 - (i - n_ctx_blocks))


def _hg_bwd(p3, lf3, *, lc):
    l = p3.shape[1]
    tb = HG_BLOCK
    nb, nbc = l // tb, lc // tb
    ncb = tb // HG_CHUNK
    order = lambda i: _bwd_block_order(i, nbc, nb)
    kern = functools.partial(_hg_bwd_kernel, n_chunks=ncb)
    return pl.pallas_call(
        kern,
        grid=(nb,),
        in_specs=[pl.BlockSpec((HG_HEADS, tb, LANE), lambda i: (S_KB // HG_HEADS, order(i), 0)),
                  pl.BlockSpec((HG_HEADS, tb, LANE), lambda i: (S_HI // HG_HEADS, order(i), 0)),
                  pl.BlockSpec((HG_HEADS, tb, LANE), lambda i: (1, order(i), 0))],
        out_specs=pl.BlockSpec((ncb, HG_HEADS, HG_DV, HG_DK), lambda i: (order(i), 0, 0, 0)),
        out_shape=jax.ShapeDtypeStruct((l // HG_CHUNK, HG_HEADS, HG_DV, HG_DK), BF16),
        scratch_shapes=[pltpu.VMEM((HG_HEADS, HG_DV, HG_DK), F32)],
        compiler_params=_params(("arbitrary",)),
        name="hgrn_bwd_state",
    )(p3, p3, lf3)


def _hg_intra(q, k32_ref, b_ref, b, v, fwd):
    c = HG_CHUNK
    k32 = k32_ref[...]
    lane = lax.broadcasted_iota(jnp.int32, (HG_SUB, c), 1)
    n_sub = c // HG_SUB
    blocks = []
    for blk in range(n_sub):
        r = blk * HG_SUB
        q_blk = q[r:r + HG_SUB]
        b_blk = b[r:r + HG_SUB]
        has_off = blk > 0 if fwd else blk < n_sub - 1
        if has_off:
            ref = b_ref[r - 1:r, :] if fwd else b_ref[r + HG_SUB:r + HG_SUB + 1, :]
            qt = (q_blk * jnp.exp(b_blk - ref)).astype(BF16)
            kt = (k32 * jnp.exp(jnp.minimum(ref - b, 0.0))).astype(BF16)
            a = lax.dot_general(qt, kt, NT_DIMS, preferred_element_type=F32)
        else:
            a = jnp.zeros((HG_SUB, c), F32)
        for jj in range(HG_SUB):
            s = r + jj
            e = jnp.exp(jnp.minimum(b_blk - b_ref[s:s + 1, :], 0.0))
            col = jnp.sum(q_blk * k32_ref[s:s + 1, :] * e, axis=1, keepdims=True)
            a = jnp.where(lane == s, col, a)
        blocks.append(a)
    a = jnp.concatenate(blocks, axis=0)
    ri = lax.broadcasted_iota(jnp.int32, (c, c), 0)
    ci = lax.broadcasted_iota(jnp.int32, (c, c), 1)
    a = jnp.where((ci <= ri) if fwd else (ci >= ri), a, 0.0)
    return jnp.dot(a.astype(BF16), v, preferred_element_type=F32)


def _hg_fwd_kernel(q_ref, kf_ref, kb_ref, v_ref, gate_ref, lf_ref, sb_ref, nw_ref,
                   o_ref, s_ref, k32_ref, b_ref, *, n_chunks):
    @pl.when(pl.program_id(0) == 0)
    def _():
        s_ref[...] = jnp.zeros_like(s_ref)

    def body(n, carry):
        cc = n // HG_HEADS
        h = n % HG_HEADS
        r0 = pl.multiple_of(cc * HG_CHUNK, HG_CHUNK)
        rows = pl.ds(r0, HG_CHUNK)
        q = q_ref[h, rows, :].astype(F32)
        v = v_ref[h, rows, :]

        bf = _cum_log_decay(lf_ref[h, rows, :], True)
        kf = kf_ref[h, rows, :].astype(F32)
        k32_ref[...] = kf
        b_ref[...] = bf
        s = s_ref[h]
        o = lax.dot_general((q * jnp.exp(bf)).astype(BF16), s.astype(BF16), NT_DIMS,
                            preferred_element_type=F32)
        o = o + _hg_intra(q, k32_ref, b_ref, bf, v, True)
        b_last = bf[HG_CHUNK - 1:HG_CHUNK, :]
        kt = (kf * jnp.exp(b_last - bf)).astype(BF16)
        s_ref[h] = s * jnp.exp(b_last) + lax.dot_general(v, kt, TN_DIMS,
                                                         preferred_element_type=F32)

        bb = _cum_log_decay(lf_ref[HG_HEADS + h, rows, :], False)
        k32_ref[...] = kb_ref[h, rows, :].astype(F32)
        b_ref[...] = bb
        o = o + lax.dot_general((q * jnp.exp(bb)).astype(BF16), sb_ref[cc, h], NT_DIMS,
                                preferred_element_type=F32)
        o = o + _hg_intra(q, k32_ref, b_ref, bb, v, False)

        o = o * lax.rsqrt(jnp.mean(o * o, axis=-1, keepdims=True) + EPS)
        o = o * nw_ref[h] * gate_ref[h, rows, :].astype(F32)
        o_ref[h, rows, :] = o.astype(BF16)
        return carry

    lax.fori_loop(0, n_chunks * HG_HEADS, body, 0)


def _hg_fwd(p3, lf3, sb, norm_w):
    l = p3.shape[1]
    tb = HG_BLOCK
    nb = l // tb
    ncb = tb // HG_CHUNK
    seg = lambda s: pl.BlockSpec((HG_HEADS, tb, LANE), lambda i: (s // HG_HEADS, i, 0))
    kern = functools.partial(_hg_fwd_kernel, n_chunks=ncb)
    return pl.pallas_call(
        kern,
        grid=(nb,),
        in_specs=[seg(S_HQ), seg(S_KF), seg(S_KB), seg(S_HI), seg(S_HGATE),
                  pl.BlockSpec((2 * HG_HEADS, tb, LANE), lambda i: (0, i, 0)),
                  pl.BlockSpec((ncb, HG_HEADS, HG_DV, HG_DK), lambda i: (i, 0, 0, 0)),
                  pl.BlockSpec((HG_HEADS, 1, HG_DV), lambda i: (0, 0, 0))],
        out_specs=pl.BlockSpec((HG_HEADS, tb, LANE), lambda i: (0, i, 0)),
        out_shape=jax.ShapeDtypeStruct((HG_HEADS, l, LANE), BF16),
        scratch_shapes=[pltpu.VMEM((HG_HEADS, HG_DV, HG_DK), F32),
                        pltpu.VMEM((HG_CHUNK, HG_DK), F32),
                        pltpu.VMEM((HG_CHUNK, HG_DK), F32)],
        compiler_params=_params(("arbitrary",)),
        name="hgrn_fwd",
    )(p3, p3, p3, p3, p3, lf3, sb, norm_w.reshape(HG_HEADS, 1, HG_DV))


def _log_sigmoid(x):
    return jnp.minimum(x, 0.0) - jnp.log1p(jnp.exp(-jnp.abs(x)))


def _ret_log_gamma(logit_ref, direction, h, shape):
    return _log_sigmoid(jnp.full(shape, logit_ref[direction, h], F32))


def _ret_bwd_kernel(logit_ref, k_ref, v_ref, sb_ref, s_ref):
    c = RET_BLOCK

    @pl.when(pl.program_id(0) == 0)
    def _():
        s_ref[...] = jnp.zeros_like(s_ref)

    t = lax.broadcasted_iota(jnp.int32, (c, RET_DK), 0).astype(F32)
    for h in range(RET_HEADS):
        lg = _ret_log_gamma(logit_ref, 1, h, (c, RET_DK))
        s = s_ref[h]
        sb_ref[0, h] = s.astype(BF16)
        kt = (k_ref[h].astype(F32) * jnp.exp(t * lg)).astype(BF16)
        v = jnp.concatenate([v_ref[2 * h], v_ref[2 * h + 1]], axis=1)
        s_ref[h] = s * jnp.exp(c * lg[0:1, 0:1]) + lax.dot_general(
            kt, v, TN_DIMS, preferred_element_type=F32)


def _ret_bwd(logit, p3, *, lc):
    l = p3.shape[1]
    c = RET_BLOCK
    nb, nbc = l // c, lc // c
    order = lambda i: _bwd_block_order(i, nbc, nb)
    return pl.pallas_call(
        _ret_bwd_kernel,
        grid=(nb,),
        in_specs=[pl.BlockSpec(memory_space=pltpu.SMEM),
                  pl.BlockSpec((RET_HEADS, c, LANE), lambda i: (S_RK // RET_HEADS, order(i), 0)),
                  pl.BlockSpec((2 * RET_HEADS, c, LANE),
                               lambda i: (S_RV // (2 * RET_HEADS), order(i), 0))],
        out_specs=pl.BlockSpec((1, RET_HEADS, RET_DK, RET_DV), lambda i: (order(i), 0, 0, 0)),
        out_shape=jax.ShapeDtypeStruct((nb, RET_HEADS, RET_DK, RET_DV), BF16),
        scratch_shapes=[pltpu.VMEM((RET_HEADS, RET_DK, RET_DV), F32)],
        compiler_params=_params(("arbitrary",)),
        name="ret_bwd_state",
    )(logit, p3, p3)


def _ret_fwd_kernel(logit_ref, q_ref, k_ref, v_ref, gate_ref, sb_ref, gnw_ref,
                    o_ref, s_ref, dmat_ref):
    c = RET_BLOCK

    @pl.when(pl.program_id(0) == 0)
    def _():
        s_ref[...] = jnp.zeros_like(s_ref)
        ri = lax.broadcasted_iota(jnp.int32, (c, c), 0)
        ci = lax.broadcasted_iota(jnp.int32, (c, c), 1)
        dist = (ri - ci).astype(F32)
        for h in range(RET_HEADS):
            lgf = _ret_log_gamma(logit_ref, 0, h, (c, c))
            lgb = _ret_log_gamma(logit_ref, 1, h, (c, c))
            dmat_ref[h] = (jnp.where(ci <= ri, jnp.exp(jnp.maximum(dist, 0.0) * lgf), 0.0)
                           + jnp.where(ci >= ri, jnp.exp(jnp.maximum(-dist, 0.0) * lgb), 0.0))

    t = lax.broadcasted_iota(jnp.int32, (c, RET_DK), 0).astype(F32)
    for h in range(RET_HEADS):
        lgf = _ret_log_gamma(logit_ref, 0, h, (c, RET_DK))
        lgb = _ret_log_gamma(logit_ref, 1, h, (c, RET_DK))
        q = q_ref[h]
        k = k_ref[h]
        q32 = q.astype(F32)
        v = jnp.concatenate([v_ref[2 * h], v_ref[2 * h + 1]], axis=1)
        s = s_ref[h]
        sc = lax.dot_general(q, k, NT_DIMS, preferred_element_type=F32) * dmat_ref[h]
        o = jnp.dot(sc.astype(BF16), v, preferred_element_type=F32)
        o = o + jnp.dot((q32 * jnp.exp((t + 1.0) * lgf)).astype(BF16), s.astype(BF16),
                        preferred_element_type=F32)
        o = o + jnp.dot((q32 * jnp.exp((c - t) * lgb)).astype(BF16), sb_ref[0, h],
                        preferred_element_type=F32)
        kt = (k.astype(F32) * jnp.exp((c - 1.0 - t) * lgf)).astype(BF16)
        s_ref[h] = s * jnp.exp(c * lgf[0:1, 0:1]) + lax.dot_general(
            kt, v, TN_DIMS, preferred_element_type=F32)

        mu = jnp.mean(o, axis=-1, keepdims=True)
        dev = o - mu
        var = jnp.mean(dev * dev, axis=-1, keepdims=True)
        o = dev * lax.rsqrt(var + EPS) * gnw_ref[h]
        gate = jnp.concatenate([gate_ref[2 * h], gate_ref[2 * h + 1]], axis=1).astype(F32)
        o_ref[:, h * RET_DV:(h + 1) * RET_DV] = (o * gate).astype(BF16)


def _ret_fwd(logit, p3, sb, gn_w):
    l = p3.shape[1]
    c = RET_BLOCK
    nb = l // c
    return pl.pallas_call(
        _ret_fwd_kernel,
        grid=(nb,),
        in_specs=[pl.BlockSpec(memory_space=pltpu.SMEM),
                  pl.BlockSpec((RET_HEADS, c, LANE), lambda i: (S_RQ // RET_HEADS, i, 0)),
                  pl.BlockSpec((RET_HEADS, c, LANE), lambda i: (S_RK // RET_HEADS, i, 0)),
                  pl.BlockSpec((2 * RET_HEADS, c, LANE), lambda i: (S_RV // (2 * RET_HEADS), i, 0)),
                  pl.BlockSpec((2 * RET_HEADS, c, LANE), lambda i: (S_RG // (2 * RET_HEADS), i, 0)),
                  pl.BlockSpec((1, RET_HEADS, RET_DK, RET_DV), lambda i: (i, 0, 0, 0)),
                  pl.BlockSpec((RET_HEADS, 1, RET_DV), lambda i: (0, 0, 0))],
        out_specs=pl.BlockSpec((c, RET_HEADS * RET_DV), lambda i: (i, 0)),
        out_shape=jax.ShapeDtypeStruct((l, RET_HEADS * RET_DV), BF16),
        scratch_shapes=[pltpu.VMEM((RET_HEADS, RET_DK, RET_DV), F32),
                        pltpu.VMEM((RET_HEADS, c, c), F32)],
        compiler_params=_params(("arbitrary",)),
        name="ret_fwd",
    )(logit, p3, p3, p3, p3, sb, gn_w.reshape(RET_HEADS, 1, RET_DV))


def _pack_bf16_pairs(v):
    w = v.shape[1] // 2
    lo = pltpu.bitcast(v[:, :w].astype(BF16).astype(F32), U32)
    hi = pltpu.bitcast(v[:, w:].astype(BF16).astype(F32), U32)
    return (lo >> 16) | (hi & jnp.uint32(0xFFFF0000))


def _unpack_bf16_pairs(u):
    lo = pltpu.bitcast(u << 16, F32)
    hi = pltpu.bitcast(u & jnp.uint32(0xFFFF0000), F32)
    return lo, hi


def _route(logits):
    lane = lax.broadcasted_iota(jnp.int32, logits.shape, 1)
    big = jnp.int32(10 ** 6)
    neg = -jnp.inf
    gl = jnp.where(lane < N_GROUPS, logits, neg)
    gmax = jnp.max(gl, axis=1, keepdims=True)
    grp = jnp.min(jnp.where(gl == gmax, lane, big), axis=1, keepdims=True)
    g_val = 1.0 / jnp.sum(jnp.exp(gl - gmax), axis=1, keepdims=True)
    lo = N_GROUPS + EXPERTS_PER_GROUP * grp
    el = jnp.where((lane >= lo) & (lane < lo + EXPERTS_PER_GROUP), logits, neg)
    v1 = jnp.max(el, axis=1, keepdims=True)
    i1 = jnp.min(jnp.where(el == v1, lane, big), axis=1, keepdims=True)
    el2 = jnp.where(lane == i1, neg, el)
    v2 = jnp.max(el2, axis=1, keepdims=True)
    i2 = jnp.min(jnp.where(el2 == v2, lane, big), axis=1, keepdims=True)
    r = jnp.exp(v2 - v1)
    w1 = g_val / (1.0 + r)
    w2 = w1 * r
    return ((i1 - N_GROUPS).astype(F32), (i2 - N_GROUPS).astype(F32), w1, w2)


def _merge_kernel(ohg_ref, oret_ref, ga_ref, gb_ref, x_ref, mods_ref, wohg_ref, woret_ref,
                  wout_ref, nw_ref, wrh_ref, wrl_ref, rb_ref,
                  xo_ref, hp_ref, route_ref, *, lc, tm, d):
    i = pl.program_id(0)
    ohg = jnp.concatenate([ohg_ref[s] for s in range(HG_HEADS)], axis=1)
    ga = jnp.concatenate([ga_ref[s] for s in range(d // LANE)], axis=1).astype(F32)
    gb = jnp.concatenate([gb_ref[s] for s in range(d // LANE)], axis=1).astype(F32)
    y_hg = jnp.dot(ohg, wohg_ref[...], preferred_element_type=F32)
    y_ret = jnp.dot(oret_ref[...], woret_ref[...], preferred_element_type=F32)
    m = (ga * y_hg + gb * y_ret).astype(BF16)
    y = jnp.dot(m, wout_ref[...], preferred_element_type=F32)
    x = x_ref[...] + _row_select(i * tm, tm, lc, mods_ref, 2, d) * y
    xo_ref[...] = x

    xn = x * lax.rsqrt(jnp.mean(x * x, axis=-1, keepdims=True) + EPS) * nw_ref[...]
    h = (xn * (1.0 + _row_select(i * tm, tm, lc, mods_ref, 4, d))
         + _row_select(i * tm, tm, lc, mods_ref, 3, d))
    hp_ref[...] = _pack_bf16_pairs(h)

    h_hi = h.astype(BF16)
    h_lo = (h - h_hi.astype(F32)).astype(BF16)
    logits = (jnp.dot(h_hi, wrh_ref[...], preferred_element_type=F32)
              + jnp.dot(h_lo, wrh_ref[...], preferred_element_type=F32)
              + jnp.dot(h_hi, wrl_ref[...], preferred_element_type=F32)) + rb_ref[...]
    e0, e1, w0, w1 = _route(logits)
    lane = lax.broadcasted_iota(jnp.int32, (tm, ROUTE_W), 1)
    route_ref[...] = jnp.where(lane == 0, e0, jnp.where(lane == 1, e1,
                               jnp.where(lane == 2, w0, jnp.where(lane == 3, w1, 0.0))))


def _merge(ohg, oret, p3, xc, mods, wohg, woret, wout, norm_w, wr_hi, wr_lo, rbias, *, lc):
    l, d = xc.shape
    tm = _largest_divisor(l, 640, LANE)
    n_slab = d // LANE
    full = lambda a: pl.BlockSpec(a.shape, lambda i: (0,) * a.ndim)
    kern = functools.partial(_merge_kernel, lc=lc, tm=tm, d=d)
    return pl.pallas_call(
        kern,
        grid=(l // tm,),
        in_specs=[pl.BlockSpec((HG_HEADS, tm, LANE), lambda i: (0, i, 0)),
                  pl.BlockSpec((tm, oret.shape[1]), lambda i: (i, 0)),
                  pl.BlockSpec((n_slab, tm, LANE), lambda i: (S_GA // n_slab, i, 0)),
                  pl.BlockSpec((n_slab, tm, LANE), lambda i: (S_GB // n_slab, i, 0)),
                  pl.BlockSpec((tm, d), lambda i: (i, 0)),
                  full(mods), full(wohg), full(woret), full(wout),
                  pl.BlockSpec((1, d), lambda i: (0, 0)),
                  full(wr_hi), full(wr_lo), full(rbias)],
        out_specs=[pl.BlockSpec((tm, d), lambda i: (i, 0)),
                   pl.BlockSpec((tm, d // 2), lambda i: (i, 0)),
                   pl.BlockSpec((tm, ROUTE_W), lambda i: (i, 0))],
        out_shape=[jax.ShapeDtypeStruct((l, d), F32),
                   jax.ShapeDtypeStruct((l, d // 2), U32),
                   jax.ShapeDtypeStruct((l, ROUTE_W), F32)],
        compiler_params=_params(("arbitrary",)),
        name="merge_router",
    )(ohg, oret, p3, p3, xc, mods, wohg, woret, wout, norm_w.reshape(1, d), wr_hi, wr_lo, rbias)


def _expert_kernel(be_ref, tok_ref, nused_ref, hp_ref, wg_ref, wu_ref, wd_ref, y_ref, xg_ref,
                   *, nbs):
    s = pl.program_id(0)
    b = pl.program_id(1)

    @pl.when(b < nused_ref[s])
    def _():
        base = (s * nbs + b) * MOE_BLOCK

        def gather(r, carry):
            tok = tok_ref[base + r]
            xg_ref[pl.ds(r, 1), :] = hp_ref[0, pl.ds(tok, 1), :]
            return carry

        lax.fori_loop(0, MOE_BLOCK, gather, 0, unroll=8)
        lo, hi = _unpack_bf16_pairs(xg_ref[...])
        x = jnp.concatenate([lo, hi], axis=1).astype(BF16)
        g = jnp.dot(x, wg_ref[0], preferred_element_type=F32)
        u = jnp.dot(x, wu_ref[0], preferred_element_type=F32)
        a = (_silu(g) * u).astype(BF16)
        y_ref[0] = _pack_bf16_pairs(jnp.dot(a, wd_ref[0], preferred_element_type=F32))

    @pl.when(b >= nused_ref[s])
    def _():
        y_ref[...] = jnp.zeros_like(y_ref)


def _experts(block_expert, buf_tok, n_used, hp_seg, wg, wu, wd, *, nbs):
    n_seg, seg_tokens, half = hp_seg.shape
    d = 2 * half
    d_e = wg.shape[2]
    kern = functools.partial(_expert_kernel, nbs=nbs)
    grid_spec = pltpu.PrefetchScalarGridSpec(
        num_scalar_prefetch=3,
        grid=(n_seg, nbs),
        in_specs=[pl.BlockSpec((1, seg_tokens, half), lambda s, b, be, tok, nu: (s, 0, 0)),
                  pl.BlockSpec((1, d, d_e), lambda s, b, be, tok, nu: (be[s * nbs + b], 0, 0)),
                  pl.BlockSpec((1, d, d_e), lambda s, b, be, tok, nu: (be[s * nbs + b], 0, 0)),
                  pl.BlockSpec((1, d_e, d), lambda s, b, be, tok, nu: (be[s * nbs + b], 0, 0))],
        out_specs=pl.BlockSpec((1, MOE_BLOCK, half), lambda s, b, be, tok, nu: (s, b, 0)),
        scratch_shapes=[pltpu.VMEM((MOE_BLOCK, half), U32)])
    return pl.pallas_call(
        kern,
        grid_spec=grid_spec,
        out_shape=jax.ShapeDtypeStruct((n_seg, nbs * MOE_BLOCK, half), U32),
        compiler_params=_params(("arbitrary", "arbitrary")),
        name="experts",
    )(block_expert, buf_tok, n_used, hp_seg, wg, wu, wd)


def _combine_kernel(dest_ref, y_ref, route_ref, x_ref, mods_ref, fw_ref, o_ref, g0_ref, g1_ref,
                    *, lc, tm, d, tiles_per_seg, final):
    s = pl.program_id(0)
    t = pl.program_id(1)
    row0 = (s * tiles_per_seg + t) * tm

    def gather(r, carry):
        a = 2 * (row0 + r)
        g0_ref[pl.ds(r, 1), :] = y_ref[0, pl.ds(dest_ref[a], 1), :]
        g1_ref[pl.ds(r, 1), :] = y_ref[0, pl.ds(dest_ref[a + 1], 1), :]
        return carry

    lax.fori_loop(0, tm, gather, 0, unroll=8)
    lo0, hi0 = _unpack_bf16_pairs(g0_ref[...])
    lo1, hi1 = _unpack_bf16_pairs(g1_ref[...])
    w0 = route_ref[:, 2:3]
    w1 = route_ref[:, 3:4]
    y = jnp.concatenate([w0 * lo0 + w1 * lo1, w0 * hi0 + w1 * hi1], axis=1)
    x = x_ref[...] + _row_select(row0, tm, lc, mods_ref, 5, d) * y
    if final:
        x = x * lax.rsqrt(jnp.mean(x * x, axis=-1, keepdims=True) + EPS) * fw_ref[...]
    o_ref[...] = x


def _combine(dest, ybuf, route, xc, mods, final_w, *, lc, final):
    l, d = xc.shape
    n_seg, rows, half = ybuf.shape
    seg_tokens = l // n_seg
    tm = _largest_divisor(seg_tokens, 320, 8)
    tiles = seg_tokens // tm
    kern = functools.partial(_combine_kernel, lc=lc, tm=tm, d=d, tiles_per_seg=tiles, final=final)
    row = lambda s, t, dest: (s * tiles + t, 0)
    grid_spec = pltpu.PrefetchScalarGridSpec(
        num_scalar_prefetch=1,
        grid=(n_seg, tiles),
        in_specs=[pl.BlockSpec((1, rows, half), lambda s, t, dest: (s, 0, 0),
                               pipeline_mode=pl.Buffered(1)),
                  pl.BlockSpec((tm, ROUTE_W), row),
                  pl.BlockSpec((tm, d), row),
                  pl.BlockSpec(mods.shape, lambda s, t, dest: (0, 0)),
                  pl.BlockSpec((1, d), lambda s, t, dest: (0, 0))],
        out_specs=pl.BlockSpec((tm, d), row),
        scratch_shapes=[pltpu.VMEM((tm, half), U32), pltpu.VMEM((tm, half), U32)])
    return pl.pallas_call(
        kern,
        grid_spec=grid_spec,
        out_shape=jax.ShapeDtypeStruct((l, d), F32),
        compiler_params=_params(("arbitrary", "arbitrary")),
        name="combine",
    )(dest, ybuf, route, xc, mods, final_w.reshape(1, d))


def _dispatch_plan(route, n_seg, nbs):
    l = route.shape[0]
    seg_tokens = l // n_seg
    n_assign = 2 * seg_tokens
    e = route[:, 0:2].astype(jnp.int32).reshape(n_seg, n_assign)
    onehot = (e[:, :, None] == jnp.arange(N_EXPERTS, dtype=jnp.int32)).astype(jnp.int32)
    csum = jnp.cumsum(onehot, axis=1)
    rank = jnp.sum(csum * onehot, axis=2) - 1
    counts = csum[:, -1, :]
    padded = ((counts + MOE_BLOCK - 1) // MOE_BLOCK) * MOE_BLOCK
    pad_end = jnp.cumsum(padded, axis=1)
    pad_start = pad_end - padded
    dest = jnp.take_along_axis(pad_start, e, axis=1) + rank
    tok = jnp.broadcast_to(jnp.arange(n_assign, dtype=jnp.int32) // 2, (n_seg, n_assign))
    seg_id = jnp.broadcast_to(jnp.arange(n_seg, dtype=jnp.int32)[:, None], (n_seg, n_assign))
    buf_tok = jnp.zeros((n_seg, nbs * MOE_BLOCK), jnp.int32).at[seg_id, dest].set(tok)
    starts = jnp.arange(nbs, dtype=jnp.int32) * MOE_BLOCK
    block_expert = jnp.minimum(
        jnp.sum((pad_end[:, None, :] <= starts[None, :, None]).astype(jnp.int32), axis=2),
        N_EXPERTS - 1)
    n_used = pad_end[:, -1] // MOE_BLOCK
    return (dest.reshape(-1).astype(jnp.int32), buf_tok.reshape(-1),
            block_expert.reshape(-1).astype(jnp.int32), n_used.astype(jnp.int32))


def _rope_tables(lc, t):
    quarter = RET_DK // 4
    inv = ROPE_BASE ** (-jnp.arange(0, 2 * quarter, 2, dtype=F32) / (2 * quarter))
    pos = jnp.arange(t)
    ang_r = (pos // GRID_W).astype(F32)[:, None] * inv
    ang_c = (pos % GRID_W).astype(F32)[:, None] * inv
    cos = jnp.concatenate([jnp.cos(ang_r)] * 2 + [jnp.cos(ang_c)] * 2, axis=1)
    sin = jnp.concatenate([-jnp.sin(ang_r), jnp.sin(ang_r), -jnp.sin(ang_c), jnp.sin(ang_c)], axis=1)
    cos = jnp.concatenate([jnp.ones((lc, RET_DK), F32), cos], axis=0)
    sin = jnp.concatenate([jnp.zeros((lc, RET_DK), F32), sin], axis=0)
    return cos, sin


def kernel(x, c, ctx, c_ctx, w_mod, b_mod, norm_mix_w, norm_ffn_w, w_in, hgrn_lb_logits, hgrn_norm_w,
           ret_decay_logit, ret_gn_w, w_o_hgrn, w_o_ret, w_out, router_group_w, router_group_b,
           router_expert_w, router_expert_b, expert_w_gate, expert_w_up, expert_w_down, final_norm_w):
    b_, t_, d = x.shape
    assert b_ == 1
    lc = ctx.shape[1]
    depth = w_mod.shape[0]
    l = lc + t_
    assert lc % HG_BLOCK == 0 and l % HG_BLOCK == 0 and lc % RET_BLOCK == 0 and l % RET_BLOCK == 0
    n_seg = MOE_SEGMENTS
    assert l % n_seg == 0
    seg_tokens = l // n_seg
    nbs = (2 * seg_tokens + N_EXPERTS * (MOE_BLOCK - 1) + MOE_BLOCK - 1) // MOE_BLOCK

    xc = jnp.concatenate([ctx[0], x[0]], axis=0)
    cc = jnp.zeros((8, d), F32).at[0].set(c[0]).at[1].set(c_ctx)
    mods_all = _modulation(cc, w_mod, b_mod)
    cos_t, sin_t = _rope_tables(lc, t_)

    col_start = np.cumsum((0,) + SEG_SLABS[:-1]) * LANE
    for layer in range(depth):
        mods = mods_all[layer]
        w_in_slabs = jnp.concatenate(
            [w_in[layer][:, col_start[seg]:col_start[seg] + SEG_SLABS[seg] * LANE]
             for seg in SEG_ORDER], axis=1).astype(BF16)
        p3, lf3 = _inproj(xc, mods, norm_mix_w[layer], w_in_slabs, hgrn_lb_logits,
                          cos_t, sin_t, layer=layer, lc=lc)
        sb_hg = _hg_bwd(p3, lf3, lc=lc)
        ohg = _hg_fwd(p3, lf3, sb_hg, hgrn_norm_w[layer])
        sb_ret = _ret_bwd(ret_decay_logit[layer], p3, lc=lc)
        oret = _ret_fwd(ret_decay_logit[layer], p3, sb_ret, ret_gn_w[layer])

        wr = jnp.concatenate([router_group_w[layer], router_expert_w[layer]], axis=1)
        wr = jnp.pad(wr, ((0, 0), (0, LANE - wr.shape[1])))
        wr_hi = wr.astype(BF16)
        wr_lo = (wr - wr_hi.astype(F32)).astype(BF16)
        rbias = jnp.pad(jnp.concatenate([router_group_b[layer], router_expert_b[layer]]),
                        (0, LANE - N_GROUPS - N_EXPERTS)).reshape(1, LANE)
        xc, hp, route = _merge(ohg, oret, p3, xc, mods, w_o_hgrn[layer].astype(BF16),
                               w_o_ret[layer].astype(BF16), w_out[layer].astype(BF16),
                               norm_ffn_w[layer], wr_hi, wr_lo, rbias, lc=lc)

        dest, buf_tok, block_expert, n_used = _dispatch_plan(route, n_seg, nbs)
        ybuf = _experts(block_expert, buf_tok, n_used, hp.reshape(n_seg, seg_tokens, d // 2),
                        expert_w_gate[layer].astype(BF16), expert_w_up[layer].astype(BF16),
                        expert_w_down[layer].astype(BF16), nbs=nbs)
        xc = _combine(dest, ybuf, route, xc, mods, final_norm_w, lc=lc,
                      final=(layer == depth - 1))

    return xc[lc:][None]
```

```python
import functools

import jax
import jax.numpy as jnp
import numpy as np
from jax import lax
from jax.experimental import pallas as pl
from jax.experimental.pallas import tpu as pltpu

F32 = jnp.float32
BF16 = jnp.bfloat16
U32 = jnp.uint32

GRID_W = 64
HG_HEADS = 4
HG_DK = 128
HG_DV = 128
F_MIN = 1e-30
RET_HEADS = 4
RET_DK = 128
RET_DV = 256
ROPE_BASE = 10000.0
N_GROUPS = 4
EXPERTS_PER_GROUP = 8
N_EXPERTS = N_GROUPS * EXPERTS_PER_GROUP
N_MOD = 6
EPS = 1e-6

LANE = 128
COL_TILE = 512
SLABS_PER_TILE = COL_TILE // LANE
HG_CHUNK = 64
HG_SUB = 16
HG_BLOCK = 256
LOG2E = 1.4426950408889634
HG_FAST_SUB = 32
HG_CAP = 80.0
CUM_GROUP = 256
RET_BLOCK = 256
MOE_SEGMENTS = 4
MOE_BLOCK = 128
ROUTE_W = 8
RANK_BITS = 16
VMEM_LIMIT = 56 * 1024 * 1024

NT_DIMS = (((1,), (1,)), ((), ()))
TN_DIMS = (((0,), (0,)), ((), ()))

SEG_SLABS = (4, 4, 4, 4, 4, 4, 4, 8, 8, 8, 8)
SEG_ORDER = (7, 8, 9, 10, 0, 1, 2, 3, 4, 5, 6)
_starts = {}
_pos = 0
for _seg in SEG_ORDER:
    _starts[_seg] = _pos
    _pos += SEG_SLABS[_seg]
SEG_START = tuple(_starts[_seg] for _seg in range(len(SEG_SLABS)))
(S_HQ, S_KF, S_KB, S_HI, S_HGATE, S_RQ, S_RK, S_RV, S_RG, S_GA, S_GB) = SEG_START
N_SLABS = sum(SEG_SLABS)


def _params(sem):
    return pltpu.CompilerParams(dimension_semantics=sem, vmem_limit_bytes=VMEM_LIMIT)


def _sigmoid(x):
    return 1.0 / (1.0 + jnp.exp(-x))


def _silu(x):
    return x * _sigmoid(x)


def _largest_divisor(n, cap, multiple):
    best = None
    for d in range(multiple, cap + 1, multiple):
        if n % d == 0:
            best = d
    assert best is not None, (n, cap, multiple)
    return best


def _row_select(row0, n_rows, lc, mods_ref, k, d):
    rows = row0 + lax.broadcasted_iota(jnp.int32, (n_rows, 1), 0)
    lat = mods_ref[0:1, k * d:(k + 1) * d]
    ctx = mods_ref[1:2, k * d:(k + 1) * d]
    return jnp.where(rows < lc, ctx, lat)


def _mod_kernel(cc_ref, w_ref, b_ref, o_ref):
    cc = cc_ref[...]
    s = _silu(cc)
    o_ref[0] = jnp.dot(s, w_ref[0], precision=lax.Precision.HIGHEST,
                       preferred_element_type=F32) + b_ref[0]


def _modulation(cc, w_mod, b_mod):
    depth, d, n = w_mod.shape
    tn = _largest_divisor(n, 1536, LANE)
    return pl.pallas_call(
        _mod_kernel,
        grid=(depth, n // tn),
        in_specs=[pl.BlockSpec((8, d), lambda l, j: (0, 0)),
                  pl.BlockSpec((1, d, tn), lambda l, j: (l, 0, j)),
                  pl.BlockSpec((1, 1, tn), lambda l, j: (l, 0, j))],
        out_specs=pl.BlockSpec((1, 8, tn), lambda l, j: (l, 0, j)),
        out_shape=jax.ShapeDtypeStruct((depth, 8, n), F32),
        compiler_params=_params(("arbitrary", "arbitrary")),
        name="modulation",
    )(cc, w_mod, b_mod.reshape(depth, 1, n))


def _inproj_kernel(x_ref, mods_ref, nw_ref, w_ref, lbl_ref, cos_ref, sin_ref,
                   p_ref, lf_ref, h_ref, *, layer, lc, tm, d):
    i = pl.program_id(0)
    j = pl.program_id(1)

    @pl.when(j == 0)
    def _():
        x = x_ref[...]
        xn = x * lax.rsqrt(jnp.mean(x * x, axis=-1, keepdims=True) + EPS) * nw_ref[...]
        shift = _row_select(i * tm, tm, lc, mods_ref, 0, d)
        scale = _row_select(i * tm, tm, lc, mods_ref, 1, d)
        h_ref[...] = (xn * (1.0 + scale) + shift).astype(BF16)

    acc = jnp.dot(h_ref[...], w_ref[...], preferred_element_type=F32)

    def put(val):
        v = val.astype(BF16)
        for s in range(SLABS_PER_TILE):
            p_ref[s] = v[:, s * LANE:(s + 1) * LANE]

    tiles = lambda seg: tuple(range(SEG_START[seg] // SLABS_PER_TILE,
                                    (SEG_START[seg] + SEG_SLABS[seg]) // SLABS_PER_TILE))
    in_tiles = lambda segs: functools.reduce(
        jnp.logical_or, [j == t for seg in segs for t in tiles(seg)])

    @pl.when(in_tiles((0,)))
    def _():
        put(_silu(acc) * (HG_DK ** -0.5))

    @pl.when(in_tiles((1, 2)))
    def _():
        logits = lbl_ref[jnp.clip(j - tiles(1)[0], 0, 1)]
        e = jnp.exp(logits - jnp.max(logits, axis=0, keepdims=True))
        p = e / jnp.sum(e, axis=0, keepdims=True)
        lb = jnp.zeros_like(p[0:1])
        for r in range(1, layer + 1):
            lb = lb + p[r:r + 1]
        lb = jnp.clip(lb, 0.0, 1.0 - 1e-6)
        sig = _sigmoid(acc)
        f = lb + (1.0 - lb) * sig
        lf = jnp.log(jnp.maximum(f, F_MIN))
        put((1.0 - lb) * (1.0 - sig))
        g = CUM_GROUP
        ri = lax.broadcasted_iota(jnp.int32, (g, g), 0)
        ci = lax.broadcasted_iota(jnp.int32, (g, g), 1)
        same_chunk = (ri // HG_CHUNK) == (ci // HG_CHUNK)
        before = jnp.where(same_chunk & (ci <= ri), 1.0, 0.0)
        after = jnp.where(same_chunk & (ci >= ri), 1.0, 0.0)
        tri = jnp.where(j == tiles(1)[0], before, after).astype(BF16)
        for r0 in range(0, tm, g):
            rest = lf[r0:r0 + g]
            cum = jnp.zeros((g, COL_TILE), F32)
            for _ in range(3):
                term = rest.astype(BF16)
                cum = cum + jnp.dot(tri, term, preferred_element_type=F32)
                rest = rest - term.astype(F32)
            cum = cum * LOG2E
            for s in range(SLABS_PER_TILE):
                lf_ref[s, r0:r0 + g, :] = cum[:, s * LANE:(s + 1) * LANE]

    @pl.when(in_tiles((3, 7)))
    def _():
        put(acc)

    @pl.when(in_tiles((4, 8)))
    def _():
        put(_silu(acc))

    @pl.when(in_tiles((5, 6)))
    def _():
        scale = jnp.where(j == tiles(5)[0], RET_DK ** -0.5, 1.0)
        xq = acc * scale
        n = xq.shape[1]
        lane = lax.broadcasted_iota(jnp.int32, xq.shape, 1)
        partner = jnp.where((lane & 32) == 0, pltpu.roll(xq, n - 32, axis=1),
                            pltpu.roll(xq, 32, axis=1))
        cos = jnp.concatenate([cos_ref[...]] * SLABS_PER_TILE, axis=1)
        sin = jnp.concatenate([sin_ref[...]] * SLABS_PER_TILE, axis=1)
        put(xq * cos + partner * sin)

    @pl.when(in_tiles((9, 10)))
    def _():
        put(_sigmoid(acc))


def _inproj(xc, mods, norm_w, w_in_bf, lb_logits, cos_t, sin_t, *, layer, lc):
    l, d = xc.shape
    d_in = w_in_bf.shape[1]
    assert d_in == N_SLABS * LANE
    tm = _largest_divisor(l, 1280, CUM_GROUP)
    n_col = d_in // COL_TILE
    kf_tile = S_KF // SLABS_PER_TILE
    kern = functools.partial(_inproj_kernel, layer=layer, lc=lc, tm=tm, d=d)
    return pl.pallas_call(
        kern,
        grid=(l // tm, n_col),
        in_specs=[pl.BlockSpec((tm, d), lambda i, j: (i, 0)),
                  pl.BlockSpec(mods.shape, lambda i, j: (0, 0)),
                  pl.BlockSpec((1, d), lambda i, j: (0, 0)),
                  pl.BlockSpec((d, COL_TILE), lambda i, j: (0, j)),
                  pl.BlockSpec(lb_logits.shape, lambda i, j: (0, 0, 0)),
                  pl.BlockSpec((tm, LANE), lambda i, j: (i, 0)),
                  pl.BlockSpec((tm, LANE), lambda i, j: (i, 0))],
        out_specs=[pl.BlockSpec((SLABS_PER_TILE, tm, LANE), lambda i, j: (j, i, 0)),
                   pl.BlockSpec((SLABS_PER_TILE, tm, LANE),
                                lambda i, j: (jnp.clip(j - kf_tile, 0, 1), i, 0))],
        out_shape=[jax.ShapeDtypeStruct((N_SLABS, l, LANE), BF16),
                   jax.ShapeDtypeStruct((2 * HG_HEADS, l, LANE), F32)],
        scratch_shapes=[pltpu.VMEM((tm, d), BF16)],
        compiler_params=_params(("arbitrary", "arbitrary")),
        name="inproj",
    )(xc, mods, norm_w.reshape(1, d), w_in_bf, lb_logits, cos_t, sin_t)


def _hg_bwd_kernel(kb_ref, v_ref, bc_ref, sb_ref, s_ref, *, n_chunks):
    @pl.when(pl.program_id(0) == 0)
    def _():
        s_ref[...] = jnp.zeros_like(s_ref)

    def body(n, carry):
        cc = n_chunks - 1 - n
        rows = pl.ds(pl.multiple_of(cc * HG_CHUNK, HG_CHUNK), HG_CHUNK)
        for h in range(HG_HEADS):
            bb = bc_ref[h, rows, :]
            k = kb_ref[h, rows, :].astype(F32)
            s = s_ref[h]
            sb_ref[cc, h] = s.astype(BF16)
            kt = (k * jnp.exp2(bb[0:1, :] - bb)).astype(BF16)
            s_ref[h] = s * jnp.exp2(bb[0:1, :]) + lax.dot_general(
                v_ref[h, rows, :], kt, TN_DIMS, preferred_element_type=F32)
        return carry

    lax.fori_loop(0, n_chunks, body, 0)


def _bwd_block_order(i, n_ctx_blocks, n_blocks):
    return jnp.where(i < n_ctx_blocks, n_ctx_blocks - 1 - i, n_blocks - 1 - (i - n_ctx_blocks))


def _hg_bwd(p3, lf3, *, lc):
    l = p3.shape[1]
    tb = HG_BLOCK
    nb, nbc = l // tb, lc // tb
    ncb = tb // HG_CHUNK
    order = lambda i: _bwd_block_order(i, nbc, nb)
    kern = functools.partial(_hg_bwd_kernel, n_chunks=ncb)
    return pl.pallas_call(
        kern,
        grid=(nb,),
        in_specs=[pl.BlockSpec((HG_HEADS, tb, LANE), lambda i: (S_KB // HG_HEADS, order(i), 0)),
                  pl.BlockSpec((HG_HEADS, tb, LANE), lambda i: (S_HI // HG_HEADS, order(i), 0)),
                  pl.BlockSpec((HG_HEADS, tb, LANE), lambda i: (1, order(i), 0))],
        out_specs=pl.BlockSpec((ncb, HG_HEADS, HG_DV, HG_DK), lambda i: (order(i), 0, 0, 0)),
        out_shape=jax.ShapeDtypeStruct((l // HG_CHUNK, HG_HEADS, HG_DV, HG_DK), BF16),
        scratch_shapes=[pltpu.VMEM((HG_HEADS, HG_DV, HG_DK), F32)],
        compiler_params=_params(("arbitrary",)),
        name="hgrn_bwd_state",
    )(p3, p3, lf3)


def _hg_scores(q, q_edge, k32, b, fwd, sub, exact, k32_ref=None, b_ref=None):
    c = HG_CHUNK
    n_sub = c // sub
    cap = 0.0 if exact else HG_CAP
    lane = lax.broadcasted_iota(jnp.int32, (sub, c), 1)
    blocks = []
    for blk in range(n_sub):
        r = blk * sub
        q_blk = q[r:r + sub]
        b_blk = b[r:r + sub]
        edge = blk == 0 if fwd else blk == n_sub - 1
        if edge and exact:
            a = jnp.zeros((sub, c), F32)
        else:
            if edge:
                qt = q_edge[r:r + sub]
                ref_minus_b = -b
            else:
                ref = b[r - 1:r] if fwd else b[r + sub:r + sub + 1]
                qt = q_blk * jnp.exp2(b_blk - ref)
                ref_minus_b = ref - b
            kt = (k32 * jnp.exp2(jnp.minimum(ref_minus_b, cap))).astype(BF16)
            a = lax.dot_general(qt.astype(BF16), kt, NT_DIMS, preferred_element_type=F32)
        if exact:
            for jj in range(sub):
                s = r + jj
                e = jnp.exp2(b_blk - b_ref[s:s + 1, :])
                col = jnp.sum(q_blk * k32_ref[s:s + 1, :] * e, axis=1, keepdims=True)
                a = jnp.where(lane == s, col, a)
        blocks.append(a)
    a = jnp.concatenate(blocks, axis=0)
    ri = lax.broadcasted_iota(jnp.int32, (c, c), 0)
    ci = lax.broadcasted_iota(jnp.int32, (c, c), 1)
    return jnp.where((ci <= ri) if fwd else (ci >= ri), a, 0.0)


def _hg_head_scores(refs, cc, h, sub, exact):
    (q_ref, kf_ref, kb_ref, v_ref, _, bc_ref, sb_ref, _, _, s_ref, k32_ref, b_ref) = refs
    rows = pl.ds(pl.multiple_of(cc * HG_CHUNK, HG_CHUNK), HG_CHUNK)
    q = q_ref[h, rows, :].astype(F32)
    v = v_ref[h, rows, :]
    bf = bc_ref[h, rows, :]
    bb = bc_ref[HG_HEADS + h, rows, :]
    kf = kf_ref[h, rows, :].astype(F32)
    kb = kb_ref[h, rows, :].astype(F32)
    qf = q * jnp.exp2(bf)
    qb = q * jnp.exp2(bb)
    s = s_ref[h]
    inter = lax.dot_general(jnp.concatenate([qf, qb], axis=1).astype(BF16),
                            jnp.concatenate([s.astype(BF16), sb_ref[cc, h]], axis=1),
                            NT_DIMS, preferred_element_type=F32)
    if exact:
        k32_ref[...] = kf
        b_ref[...] = bf
    a = _hg_scores(q, qf, kf, bf, True, sub, exact, k32_ref, b_ref)
    if exact:
        k32_ref[...] = kb
        b_ref[...] = bb
    a = a + _hg_scores(q, qb, kb, bb, False, sub, exact, k32_ref, b_ref)
    b_last = bf[HG_CHUNK - 1:HG_CHUNK, :]
    kt = (kf * jnp.exp2(b_last - bf)).astype(BF16)
    s_ref[h] = s * jnp.exp2(b_last) + lax.dot_general(v, kt, TN_DIMS,
                                                      preferred_element_type=F32)
    return inter, a, v


def _hg_head_finish(refs, cc, h, inter, a, v):
    gate_ref, nw_ref, o_ref = refs[4], refs[7], refs[8]
    rows = pl.ds(pl.multiple_of(cc * HG_CHUNK, HG_CHUNK), HG_CHUNK)
    o = inter + jnp.dot(a.astype(BF16), v, preferred_element_type=F32)
    o = o * lax.rsqrt(jnp.mean(o * o, axis=-1, keepdims=True) + EPS)
    o = o * nw_ref[h] * gate_ref[h, rows, :].astype(F32)
    o_ref[h, rows, :] = o.astype(BF16)


def _hg_min_block_decay(bc_ref):
    n = bc_ref.shape[1] // HG_FAST_SUB
    first = lax.broadcasted_iota(jnp.int32, (n, LANE), 0) % 2 == 0
    worst = None
    for h in range(HG_HEADS):
        ends = bc_ref.at[h][pl.ds(HG_FAST_SUB - 1, n, stride=HG_FAST_SUB), :]
        fwd = jnp.where(first, ends, ends - pltpu.roll(ends, 1, axis=0))
        starts = bc_ref.at[HG_HEADS + h][pl.ds(0, n, stride=HG_FAST_SUB), :]
        bwd = jnp.where(first, starts - pltpu.roll(starts, n - 1, axis=0), starts)
        m = jnp.minimum(fwd, bwd)
        worst = m if worst is None else jnp.minimum(worst, m)
    return jnp.min(worst)


def _hg_fwd_kernel(*refs, n_chunks):
    bc_ref, s_ref = refs[5], refs[9]

    @pl.when(pl.program_id(0) == 0)
    def _():
        s_ref[...] = jnp.zeros_like(s_ref)

    factored_ok = _hg_min_block_decay(bc_ref) >= -HG_CAP

    @pl.when(factored_ok)
    def _():
        def body(cc, carry):
            staged = [_hg_head_scores(refs, cc, h, HG_FAST_SUB, False) for h in range(HG_HEADS)]
            for h, parts in enumerate(staged):
                _hg_head_finish(refs, cc, h, *parts)
            return carry
        lax.fori_loop(0, n_chunks, body, 0)

    @pl.when(jnp.logical_not(factored_ok))
    def _():
        def body(n, carry):
            cc, h = n // HG_HEADS, n % HG_HEADS
            _hg_head_finish(refs, cc, h, *_hg_head_scores(refs, cc, h, HG_SUB, True))
            return carry
        lax.fori_loop(0, n_chunks * HG_HEADS, body, 0)


def _hg_fwd(p3, lf3, sb, norm_w):
    l = p3.shape[1]
    tb = HG_BLOCK
    nb = l // tb
    ncb = tb // HG_CHUNK
    seg = lambda s: pl.BlockSpec((HG_HEADS, tb, LANE), lambda i: (s // HG_HEADS, i, 0))
    kern = functools.partial(_hg_fwd_kernel, n_chunks=ncb)
    return pl.pallas_call(
        kern,
        grid=(nb,),
        in_specs=[seg(S_HQ), seg(S_KF), seg(S_KB), seg(S_HI), seg(S_HGATE),
                  pl.BlockSpec((2 * HG_HEADS, tb, LANE), lambda i: (0, i, 0)),
                  pl.BlockSpec((ncb, HG_HEADS, HG_DV, HG_DK), lambda i: (i, 0, 0, 0)),
                  pl.BlockSpec((HG_HEADS, 1, HG_DV), lambda i: (0, 0, 0))],
        out_specs=pl.BlockSpec((HG_HEADS, tb, LANE), lambda i: (0, i, 0)),
        out_shape=jax.ShapeDtypeStruct((HG_HEADS, l, LANE), BF16),
        scratch_shapes=[pltpu.VMEM((HG_HEADS, HG_DV, HG_DK), F32),
                        pltpu.VMEM((HG_CHUNK, HG_DK), F32),
                        pltpu.VMEM((HG_CHUNK, HG_DK), F32)],
        compiler_params=_params(("arbitrary",)),
        name="hgrn_fwd",
    )(p3, p3, p3, p3, p3, lf3, sb, norm_w.reshape(HG_HEADS, 1, HG_DV))


def _log_sigmoid(x):
    return jnp.minimum(x, 0.0) - jnp.log1p(jnp.exp(-jnp.abs(x)))


def _ret_log_gamma(logit_ref, direction, h, shape):
    return _log_sigmoid(jnp.full(shape, logit_ref[direction, h], F32))


def _ret_bwd_kernel(logit_ref, k_ref, v_ref, sb_ref, s_ref):
    c = RET_BLOCK

    @pl.when(pl.program_id(0) == 0)
    def _():
        s_ref[...] = jnp.zeros_like(s_ref)

    t = lax.broadcasted_iota(jnp.int32, (c, RET_DK), 0).astype(F32)
    for h in range(RET_HEADS):
        lg = _ret_log_gamma(logit_ref, 1, h, (c, RET_DK))
        s = s_ref[h]
        sb_ref[0, h] = s.astype(BF16)
        kt = (k_ref[h].astype(F32) * jnp.exp(t * lg)).astype(BF16)
        v = jnp.concatenate([v_ref[2 * h], v_ref[2 * h + 1]], axis=1)
        s_ref[h] = s * jnp.exp(c * lg[0:1, 0:1]) + lax.dot_general(
            kt, v, TN_DIMS, preferred_element_type=F32)


def _ret_bwd(logit, p3, *, lc):
    l = p3.shape[1]
    c = RET_BLOCK
    nb, nbc = l // c, lc // c
    order = lambda i: _bwd_block_order(i, nbc, nb)
    return pl.pallas_call(
        _ret_bwd_kernel,
        grid=(nb,),
        in_specs=[pl.BlockSpec(memory_space=pltpu.SMEM),
                  pl.BlockSpec((RET_HEADS, c, LANE), lambda i: (S_RK // RET_HEADS, order(i), 0)),
                  pl.BlockSpec((2 * RET_HEADS, c, LANE),
                               lambda i: (S_RV // (2 * RET_HEADS), order(i), 0))],
        out_specs=pl.BlockSpec((1, RET_HEADS, RET_DK, RET_DV), lambda i: (order(i), 0, 0, 0)),
        out_shape=jax.ShapeDtypeStruct((nb, RET_HEADS, RET_DK, RET_DV), BF16),
        scratch_shapes=[pltpu.VMEM((RET_HEADS, RET_DK, RET_DV), F32)],
        compiler_params=_params(("arbitrary",)),
        name="ret_bwd_state",
    )(logit, p3, p3)


def _ret_fwd_kernel(logit_ref, q_ref, k_ref, v_ref, gate_ref, sb_ref, gnw_ref,
                    o_ref, s_ref, dmat_ref):
    c = RET_BLOCK

    @pl.when(pl.program_id(0) == 0)
    def _():
        s_ref[...] = jnp.zeros_like(s_ref)
        ri = lax.broadcasted_iota(jnp.int32, (c, c), 0)
        ci = lax.broadcasted_iota(jnp.int32, (c, c), 1)
        dist = (ri - ci).astype(F32)
        for h in range(RET_HEADS):
            lgf = _ret_log_gamma(logit_ref, 0, h, (c, c))
            lgb = _ret_log_gamma(logit_ref, 1, h, (c, c))
            dmat_ref[h] = (jnp.where(ci <= ri, jnp.exp(jnp.maximum(dist, 0.0) * lgf), 0.0)
                           + jnp.where(ci >= ri, jnp.exp(jnp.maximum(-dist, 0.0) * lgb), 0.0))

    t = lax.broadcasted_iota(jnp.int32, (c, RET_DK), 0).astype(F32)
    for h in range(RET_HEADS):
        lgf = _ret_log_gamma(logit_ref, 0, h, (c, RET_DK))
        lgb = _ret_log_gamma(logit_ref, 1, h, (c, RET_DK))
        q = q_ref[h]
        k = k_ref[h]
        q32 = q.astype(F32)
        v = jnp.concatenate([v_ref[2 * h], v_ref[2 * h + 1]], axis=1)
        s = s_ref[h]
        sc = lax.dot_general(q, k, NT_DIMS, preferred_element_type=F32) * dmat_ref[h]
        o = jnp.dot(sc.astype(BF16), v, preferred_element_type=F32)
        o = o + jnp.dot((q32 * jnp.exp((t + 1.0) * lgf)).astype(BF16), s.astype(BF16),
                        preferred_element_type=F32)
        o = o + jnp.dot((q32 * jnp.exp((c - t) * lgb)).astype(BF16), sb_ref[0, h],
                        preferred_element_type=F32)
        kt = (k.astype(F32) * jnp.exp((c - 1.0 - t) * lgf)).astype(BF16)
        s_ref[h] = s * jnp.exp(c * lgf[0:1, 0:1]) + lax.dot_general(
            kt, v, TN_DIMS, preferred_element_type=F32)

        mu = jnp.mean(o, axis=-1, keepdims=True)
        dev = o - mu
        var = jnp.mean(dev * dev, axis=-1, keepdims=True)
        o = dev * lax.rsqrt(var + EPS) * gnw_ref[h]
        gate = jnp.concatenate([gate_ref[2 * h], gate_ref[2 * h + 1]], axis=1).astype(F32)
        o_ref[:, h * RET_DV:(h + 1) * RET_DV] = (o * gate).astype(BF16)


def _ret_fwd(logit, p3, sb, gn_w):
    l = p3.shape[1]
    c = RET_BLOCK
    nb = l // c
    return pl.pallas_call(
        _ret_fwd_kernel,
        grid=(nb,),
        in_specs=[pl.BlockSpec(memory_space=pltpu.SMEM),
                  pl.BlockSpec((RET_HEADS, c, LANE), lambda i: (S_RQ // RET_HEADS, i, 0)),
                  pl.BlockSpec((RET_HEADS, c, LANE), lambda i: (S_RK // RET_HEADS, i, 0)),
                  pl.BlockSpec((2 * RET_HEADS, c, LANE), lambda i: (S_RV // (2 * RET_HEADS), i, 0)),
                  pl.BlockSpec((2 * RET_HEADS, c, LANE), lambda i: (S_RG // (2 * RET_HEADS), i, 0)),
                  pl.BlockSpec((1, RET_HEADS, RET_DK, RET_DV), lambda i: (i, 0, 0, 0)),
                  pl.BlockSpec((RET_HEADS, 1, RET_DV), lambda i: (0, 0, 0))],
        out_specs=pl.BlockSpec((c, RET_HEADS * RET_DV), lambda i: (i, 0)),
        out_shape=jax.ShapeDtypeStruct((l, RET_HEADS * RET_DV), BF16),
        scratch_shapes=[pltpu.VMEM((RET_HEADS, RET_DK, RET_DV), F32),
                        pltpu.VMEM((RET_HEADS, c, c), F32)],
        compiler_params=_params(("arbitrary",)),
        name="ret_fwd",
    )(logit, p3, p3, p3, p3, sb, gn_w.reshape(RET_HEADS, 1, RET_DV))


def _pack_bf16_pairs(v):
    w = v.shape[1] // 2
    lo = pltpu.bitcast(v[:, :w].astype(BF16).astype(F32), U32)
    hi = pltpu.bitcast(v[:, w:].astype(BF16).astype(F32), U32)
    return (lo >> 16) | (hi & jnp.uint32(0xFFFF0000))


def _unpack_bf16_pairs(u):
    lo = pltpu.bitcast(u << 16, F32)
    hi = pltpu.bitcast(u & jnp.uint32(0xFFFF0000), F32)
    return lo, hi


def _route(logits):
    lane = lax.broadcasted_iota(jnp.int32, logits.shape, 1)
    big = jnp.int32(10 ** 6)
    neg = -jnp.inf
    gl = jnp.where(lane < N_GROUPS, logits, neg)
    gmax = jnp.max(gl, axis=1, keepdims=True)
    grp = jnp.min(jnp.where(gl == gmax, lane, big), axis=1, keepdims=True)
    g_val = 1.0 / jnp.sum(jnp.exp(gl - gmax), axis=1, keepdims=True)
    lo = N_GROUPS + EXPERTS_PER_GROUP * grp
    el = jnp.where((lane >= lo) & (lane < lo + EXPERTS_PER_GROUP), logits, neg)
    v1 = jnp.max(el, axis=1, keepdims=True)
    i1 = jnp.min(jnp.where(el == v1, lane, big), axis=1, keepdims=True)
    el2 = jnp.where(lane == i1, neg, el)
    v2 = jnp.max(el2, axis=1, keepdims=True)
    i2 = jnp.min(jnp.where(el2 == v2, lane, big), axis=1, keepdims=True)
    r = jnp.exp(v2 - v1)
    w1 = g_val / (1.0 + r)
    w2 = w1 * r
    return ((i1 - N_GROUPS).astype(F32), (i2 - N_GROUPS).astype(F32), w1, w2)


def _merge_kernel(ohg_ref, oret_ref, ga_ref, gb_ref, x_ref, mods_ref, wohg_ref, woret_ref,
                  wout_ref, nw_ref, wrh_ref, wrl_ref, rb_ref,
                  xo_ref, hp_ref, route_ref, cnt_out_ref, cnt_ref, *, lc, tm, d, tiles_per_seg):
    i = pl.program_id(0)
    ohg = jnp.concatenate([ohg_ref[s] for s in range(HG_HEADS)], axis=1)
    ga = jnp.concatenate([ga_ref[s] for s in range(d // LANE)], axis=1).astype(F32)
    gb = jnp.concatenate([gb_ref[s] for s in range(d // LANE)], axis=1).astype(F32)
    y_hg = jnp.dot(ohg, wohg_ref[...], preferred_element_type=F32)
    y_ret = jnp.dot(oret_ref[...], woret_ref[...], preferred_element_type=F32)
    m = (ga * y_hg + gb * y_ret).astype(BF16)
    y = jnp.dot(m, wout_ref[...], preferred_element_type=F32)
    x = x_ref[...] + _row_select(i * tm, tm, lc, mods_ref, 2, d) * y
    xo_ref[...] = x

    xn = x * lax.rsqrt(jnp.mean(x * x, axis=-1, keepdims=True) + EPS) * nw_ref[...]
    h = (xn * (1.0 + _row_select(i * tm, tm, lc, mods_ref, 4, d))
         + _row_select(i * tm, tm, lc, mods_ref, 3, d))
    hp_ref[...] = _pack_bf16_pairs(h)

    h_hi = h.astype(BF16)
    h_lo = (h - h_hi.astype(F32)).astype(BF16)
    logits = (jnp.dot(h_hi, wrh_ref[...], preferred_element_type=F32)
              + jnp.dot(h_lo, wrh_ref[...], preferred_element_type=F32)
              + jnp.dot(h_hi, wrl_ref[...], preferred_element_type=F32)) + rb_ref[...]
    e0, e1, w0, w1 = _route(logits)

    @pl.when(i % tiles_per_seg == 0)
    def _():
        cnt_ref[...] = jnp.zeros_like(cnt_ref)

    lane_e = lax.broadcasted_iota(jnp.int32, (tm, LANE), 1).astype(F32)
    hot0 = lane_e == e0
    hot1 = lane_e == e1
    hot = jnp.where(hot0 | hot1, 1.0, 0.0)
    earlier = (lax.broadcasted_iota(jnp.int32, (tm, tm), 1)
               < lax.broadcasted_iota(jnp.int32, (tm, tm), 0))
    before = jnp.dot(jnp.where(earlier, 1.0, 0.0).astype(BF16), hot.astype(BF16),
                     preferred_element_type=F32) + cnt_ref[0:1, :]
    rank0 = jnp.sum(jnp.where(hot0, before, 0.0), axis=1, keepdims=True)
    rank1 = jnp.sum(jnp.where(hot1, before, 0.0), axis=1, keepdims=True)
    cnt_ref[...] = cnt_ref[...] + jnp.sum(hot, axis=0, keepdims=True)
    cnt_out_ref[0] = cnt_ref[...]

    lane = lax.broadcasted_iota(jnp.int32, (tm, ROUTE_W), 1)
    rec = jnp.zeros((tm, ROUTE_W), F32)
    for k, val in enumerate((e0, e1, w0, w1, rank0, rank1)):
        rec = jnp.where(lane == k, val, rec)
    route_ref[...] = rec


def _merge(ohg, oret, p3, xc, mods, wohg, woret, wout, norm_w, wr_hi, wr_lo, rbias, *, lc, n_seg):
    l, d = xc.shape
    seg_tokens = l // n_seg
    tm = _largest_divisor(seg_tokens, 832, 16)
    tiles_per_seg = seg_tokens // tm
    n_slab = d // LANE
    full = lambda a: pl.BlockSpec(a.shape, lambda i: (0,) * a.ndim)
    kern = functools.partial(_merge_kernel, lc=lc, tm=tm, d=d, tiles_per_seg=tiles_per_seg)
    return pl.pallas_call(
        kern,
        grid=(l // tm,),
        in_specs=[pl.BlockSpec((HG_HEADS, tm, LANE), lambda i: (0, i, 0)),
                  pl.BlockSpec((tm, oret.shape[1]), lambda i: (i, 0)),
                  pl.BlockSpec((n_slab, tm, LANE), lambda i: (S_GA // n_slab, i, 0)),
                  pl.BlockSpec((n_slab, tm, LANE), lambda i: (S_GB // n_slab, i, 0)),
                  pl.BlockSpec((tm, d), lambda i: (i, 0)),
                  full(mods), full(wohg), full(woret), full(wout),
                  pl.BlockSpec((1, d), lambda i: (0, 0)),
                  full(wr_hi), full(wr_lo), full(rbias)],
        out_specs=[pl.BlockSpec((tm, d), lambda i: (i, 0)),
                   pl.BlockSpec((tm, d // 2), lambda i: (i, 0)),
                   pl.BlockSpec((tm, ROUTE_W), lambda i: (i, 0)),
                   pl.BlockSpec((1, 8, LANE), lambda i: (i // tiles_per_seg, 0, 0))],
        out_shape=[jax.ShapeDtypeStruct((l, d), F32),
                   jax.ShapeDtypeStruct((l, d // 2), U32),
                   jax.ShapeDtypeStruct((l, ROUTE_W), F32),
                   jax.ShapeDtypeStruct((n_seg, 8, LANE), F32)],
        scratch_shapes=[pltpu.VMEM((8, LANE), F32)],
        compiler_params=_params(("arbitrary",)),
        name="merge_router",
    )(ohg, oret, p3, p3, xc, mods, wohg, woret, wout, norm_w.reshape(1, d), wr_hi, wr_lo, rbias)


def _slot(code_ref, start_ref, s, a):
    code = code_ref[a]
    return start_ref[s * N_EXPERTS + (code >> RANK_BITS)] + (code & ((1 << RANK_BITS) - 1))


def _expert_kernel(be_ref, nused_ref, code_ref, start_ref, hp_ref, wg_ref, wu_ref, wd_ref, y_ref,
                   xs_ref, *, seg_tokens):
    s = pl.program_id(0)
    b = pl.program_id(1)

    @pl.when(b == 0)
    def _():
        xs_ref[...] = jnp.zeros_like(xs_ref)

        def scatter(t, carry):
            a = 2 * (s * seg_tokens + t)
            row = hp_ref[0, pl.ds(t, 1), :]
            xs_ref[pl.ds(_slot(code_ref, start_ref, s, a), 1), :] = row
            xs_ref[pl.ds(_slot(code_ref, start_ref, s, a + 1), 1), :] = row
            return carry

        lax.fori_loop(0, seg_tokens, scatter, 0, unroll=4)

    @pl.when(b < nused_ref[s])
    def _():
        r0 = pl.multiple_of(b * MOE_BLOCK, MOE_BLOCK)
        lo, hi = _unpack_bf16_pairs(xs_ref[pl.ds(r0, MOE_BLOCK), :])
        x = jnp.concatenate([lo, hi], axis=1).astype(BF16)
        g = jnp.dot(x, wg_ref[0], preferred_element_type=F32)
        u = jnp.dot(x, wu_ref[0], preferred_element_type=F32)
        a = (_silu(g) * u).astype(BF16)
        y_ref[0] = _pack_bf16_pairs(jnp.dot(a, wd_ref[0], preferred_element_type=F32))

    @pl.when(b >= nused_ref[s])
    def _():
        y_ref[...] = jnp.zeros_like(y_ref)


def _experts(block_expert, n_used, code, pad_start, hp_seg, wg, wu, wd, *, nbs):
    n_seg, seg_tokens, half = hp_seg.shape
    d = 2 * half
    d_e = wg.shape[2]
    kern = functools.partial(_expert_kernel, seg_tokens=seg_tokens)
    w_idx = lambda s, b, be, *_: (be[s * nbs + b], 0, 0)
    grid_spec = pltpu.PrefetchScalarGridSpec(
        num_scalar_prefetch=4,
        grid=(n_seg, nbs),
        in_specs=[pl.BlockSpec((1, seg_tokens, half), lambda s, b, *_: (s, 0, 0)),
                  pl.BlockSpec((1, d, d_e), w_idx),
                  pl.BlockSpec((1, d, d_e), w_idx),
                  pl.BlockSpec((1, d_e, d), w_idx)],
        out_specs=pl.BlockSpec((1, MOE_BLOCK, half), lambda s, b, *_: (s, b, 0)),
        scratch_shapes=[pltpu.VMEM((nbs * MOE_BLOCK, half), U32)])
    return pl.pallas_call(
        kern,
        grid_spec=grid_spec,
        out_shape=jax.ShapeDtypeStruct((n_seg, nbs * MOE_BLOCK, half), U32),
        compiler_params=_params(("arbitrary", "arbitrary")),
        name="experts",
    )(block_expert, n_used, code, pad_start, hp_seg, wg, wu, wd)


def _combine_kernel(code_ref, start_ref, y_ref, route_ref, x_ref, mods_ref, fw_ref, o_ref,
                    g0_ref, g1_ref, *, lc, tm, d, tiles_per_seg, final):
    s = pl.program_id(0)
    t = pl.program_id(1)
    row0 = (s * tiles_per_seg + t) * tm

    def gather(r, carry):
        a = 2 * (row0 + r)
        g0_ref[pl.ds(r, 1), :] = y_ref[0, pl.ds(_slot(code_ref, start_ref, s, a), 1), :]
        g1_ref[pl.ds(r, 1), :] = y_ref[0, pl.ds(_slot(code_ref, start_ref, s, a + 1), 1), :]
        return carry

    lax.fori_loop(0, tm, gather, 0, unroll=8)
    lo0, hi0 = _unpack_bf16_pairs(g0_ref[...])
    lo1, hi1 = _unpack_bf16_pairs(g1_ref[...])
    w0 = route_ref[:, 2:3]
    w1 = route_ref[:, 3:4]
    y = jnp.concatenate([w0 * lo0 + w1 * lo1, w0 * hi0 + w1 * hi1], axis=1)
    x = x_ref[...] + _row_select(row0, tm, lc, mods_ref, 5, d) * y
    if final:
        x = x * lax.rsqrt(jnp.mean(x * x, axis=-1, keepdims=True) + EPS) * fw_ref[...]
    o_ref[...] = x


def _combine(code, pad_start, ybuf, route, xc, mods, final_w, *, lc, final):
    l, d = xc.shape
    n_seg, rows, half = ybuf.shape
    seg_tokens = l // n_seg
    tm = _largest_divisor(seg_tokens, 320, 8)
    tiles = seg_tokens // tm
    kern = functools.partial(_combine_kernel, lc=lc, tm=tm, d=d, tiles_per_seg=tiles, final=final)
    row = lambda s, t, *_: (s * tiles + t, 0)
    grid_spec = pltpu.PrefetchScalarGridSpec(
        num_scalar_prefetch=2,
        grid=(n_seg, tiles),
        in_specs=[pl.BlockSpec((1, rows, half), lambda s, t, *_: (s, 0, 0),
                               pipeline_mode=pl.Buffered(1)),
                  pl.BlockSpec((tm, ROUTE_W), row),
                  pl.BlockSpec((tm, d), row),
                  pl.BlockSpec(mods.shape, lambda s, t, *_: (0, 0)),
                  pl.BlockSpec((1, d), lambda s, t, *_: (0, 0))],
        out_specs=pl.BlockSpec((tm, d), row),
        scratch_shapes=[pltpu.VMEM((tm, half), U32), pltpu.VMEM((tm, half), U32)])
    return pl.pallas_call(
        kern,
        grid_spec=grid_spec,
        out_shape=jax.ShapeDtypeStruct((l, d), F32),
        compiler_params=_params(("arbitrary", "arbitrary")),
        name="combine",
    )(code, pad_start, ybuf, route, xc, mods, final_w.reshape(1, d))


def _dispatch_plan(route, counts, nbs):
    n_seg = counts.shape[0]
    counts = counts[:, 0, :N_EXPERTS].astype(jnp.int32)
    padded = ((counts + MOE_BLOCK - 1) // MOE_BLOCK) * MOE_BLOCK
    pad_end = jnp.cumsum(padded, axis=1)
    pad_start = pad_end - padded
    starts = jnp.arange(nbs, dtype=jnp.int32) * MOE_BLOCK
    block_expert = jnp.minimum(
        jnp.sum((pad_end[:, None, :] <= starts[None, :, None]).astype(jnp.int32), axis=2),
        N_EXPERTS - 1)
    n_used = pad_end[:, -1] // MOE_BLOCK
    e = route[:, 0:2].astype(jnp.int32)
    rank = route[:, 4:6].astype(jnp.int32)
    code = (e * (1 << RANK_BITS) + rank).reshape(-1)
    return (code, pad_start.reshape(-1), block_expert.reshape(-1).astype(jnp.int32),
            n_used.astype(jnp.int32))


def _rope_tables(lc, t):
    quarter = RET_DK // 4
    inv = ROPE_BASE ** (-jnp.arange(0, 2 * quarter, 2, dtype=F32) / (2 * quarter))
    pos = jnp.arange(t)
    ang_r = (pos // GRID_W).astype(F32)[:, None] * inv
    ang_c = (pos % GRID_W).astype(F32)[:, None] * inv
    cos = jnp.concatenate([jnp.cos(ang_r)] * 2 + [jnp.cos(ang_c)] * 2, axis=1)
    sin = jnp.concatenate([-jnp.sin(ang_r), jnp.sin(ang_r), -jnp.sin(ang_c), jnp.sin(ang_c)], axis=1)
    cos = jnp.concatenate([jnp.ones((lc, RET_DK), F32), cos], axis=0)
    sin = jnp.concatenate([jnp.zeros((lc, RET_DK), F32), sin], axis=0)
    return cos, sin


def kernel(x, c, ctx, c_ctx, w_mod, b_mod, norm_mix_w, norm_ffn_w, w_in, hgrn_lb_logits, hgrn_norm_w,
           ret_decay_logit, ret_gn_w, w_o_hgrn, w_o_ret, w_out, router_group_w, router_group_b,
           router_expert_w, router_expert_b, expert_w_gate, expert_w_up, expert_w_down, final_norm_w):
    b_, t_, d = x.shape
    assert b_ == 1
    lc = ctx.shape[1]
    depth = w_mod.shape[0]
    l = lc + t_
    assert lc % HG_BLOCK == 0 and l % HG_BLOCK == 0 and lc % RET_BLOCK == 0 and l % RET_BLOCK == 0
    n_seg = MOE_SEGMENTS
    assert l % n_seg == 0
    seg_tokens = l // n_seg
    nbs = (2 * seg_tokens + N_EXPERTS * (MOE_BLOCK - 1) + MOE_BLOCK - 1) // MOE_BLOCK

    xc = jnp.concatenate([ctx[0], x[0]], axis=0)
    cc = jnp.zeros((8, d), F32).at[0].set(c[0]).at[1].set(c_ctx)
    mods_all = _modulation(cc, w_mod, b_mod)
    cos_t, sin_t = _rope_tables(lc, t_)

    col_start = np.cumsum((0,) + SEG_SLABS[:-1]) * LANE
    for layer in range(depth):
        mods = mods_all[layer]
        w_in_slabs = jnp.concatenate(
            [w_in[layer][:, col_start[seg]:col_start[seg] + SEG_SLABS[seg] * LANE]
             for seg in SEG_ORDER], axis=1).astype(BF16)
        p3, lf3 = _inproj(xc, mods, norm_mix_w[layer], w_in_slabs, hgrn_lb_logits,
                          cos_t, sin_t, layer=layer, lc=lc)
        sb_hg = _hg_bwd(p3, lf3, lc=lc)
        ohg = _hg_fwd(p3, lf3, sb_hg, hgrn_norm_w[layer])
        sb_ret = _ret_bwd(ret_decay_logit[layer], p3, lc=lc)
        oret = _ret_fwd(ret_decay_logit[layer], p3, sb_ret, ret_gn_w[layer])

        wr = jnp.concatenate([router_group_w[layer], router_expert_w[layer]], axis=1)
        wr = jnp.pad(wr, ((0, 0), (0, LANE - wr.shape[1])))
        wr_hi = wr.astype(BF16)
        wr_lo = (wr - wr_hi.astype(F32)).astype(BF16)
        rbias = jnp.pad(jnp.concatenate([router_group_b[layer], router_expert_b[layer]]),
                        (0, LANE - N_GROUPS - N_EXPERTS)).reshape(1, LANE)
        xc, hp, route, counts = _merge(ohg, oret, p3, xc, mods, w_o_hgrn[layer].astype(BF16),
                                       w_o_ret[layer].astype(BF16), w_out[layer].astype(BF16),
                                       norm_ffn_w[layer], wr_hi, wr_lo, rbias, lc=lc, n_seg=n_seg)

        code, pad_start, block_expert, n_used = _dispatch_plan(route, counts, nbs)
        ybuf = _experts(block_expert, n_used, code, pad_start,
                        hp.reshape(n_seg, seg_tokens, d // 2),
                        expert_w_gate[layer].astype(BF16), expert_w_up[layer].astype(BF16),
                        expert_w_down[layer].astype(BF16), nbs=nbs)
        xc = _combine(code, pad_start, ybuf, route, xc, mods, final_norm_w, lc=lc,
                      final=(layer == depth - 1))

    return xc[lc:][None]
```

```python
import functools

import jax
import jax.numpy as jnp
from jax import lax
from jax.experimental import pallas as pl
from jax.experimental.pallas import tpu as pltpu

F32 = jnp.float32
BF16 = jnp.bfloat16
U32 = jnp.uint32

GRID_W = 64
HG_HEADS = 4
HG_DK = 128
HG_DV = 128
F_MIN = 1e-30
RET_HEADS = 4
RET_DK = 128
RET_DV = 256
ROPE_BASE = 10000.0
N_GROUPS = 4
EXPERTS_PER_GROUP = 8
N_EXPERTS = N_GROUPS * EXPERTS_PER_GROUP
N_MOD = 6
EPS = 1e-6

LANE = 128
COL_TILE = 512
SLABS_PER_TILE = COL_TILE // LANE
HG_CHUNK = 64
HG_SUB = 16
HG_BLOCK = 256
LOG2E = 1.4426950408889634
HG_FAST_SUB = 32
HG_CAP = 100.0
CUM_GROUP = 256
CUM_TERMS = 2
RET_BLOCK = 256
MOE_SEGMENTS = 4
MOE_BLOCK = 128
ROUTE_W = 8
VMEM_LIMIT = 56 * 1024 * 1024

NT_DIMS = (((1,), (1,)), ((), ()))
TN_DIMS = (((0,), (0,)), ((), ()))

SEG_SLABS = (4, 4, 4, 4, 4, 4, 4, 8, 8, 8, 8)
SEG_ORDER = (7, 8, 9, 10, 0, 1, 2, 3, 4, 5, 6)
_starts = {}
_pos = 0
for _seg in SEG_ORDER:
    _starts[_seg] = _pos
    _pos += SEG_SLABS[_seg]
SEG_START = tuple(_starts[_seg] for _seg in range(len(SEG_SLABS)))
(S_HQ, S_KF, S_KB, S_HI, S_HGATE, S_RQ, S_RK, S_RV, S_RG, S_GA, S_GB) = SEG_START
assert SEG_ORDER == tuple(range(SEG_ORDER[0], len(SEG_SLABS))) + tuple(range(SEG_ORDER[0]))
TILE_ROTATION = sum(SEG_SLABS[:SEG_ORDER[0]]) // SLABS_PER_TILE
N_SLABS = sum(SEG_SLABS)


def _params(sem):
    return pltpu.CompilerParams(dimension_semantics=sem, vmem_limit_bytes=VMEM_LIMIT)


def _sigmoid(x):
    return 1.0 / (1.0 + jnp.exp(-x))


def _silu(x):
    return x * _sigmoid(x)


def _largest_divisor(n, cap, multiple):
    best = None
    for d in range(multiple, cap + 1, multiple):
        if n % d == 0:
            best = d
    assert best is not None, (n, cap, multiple)
    return best


def _row_select(row0, n_rows, lc, mods_ref, k, d):
    rows = row0 + lax.broadcasted_iota(jnp.int32, (n_rows, 1), 0)
    lat = mods_ref[0:1, k * d:(k + 1) * d]
    ctx = mods_ref[1:2, k * d:(k + 1) * d]
    return jnp.where(rows < lc, ctx, lat)


def _mod_kernel(cc_ref, w_ref, b_ref, o_ref):
    cc = cc_ref[...]
    s = _silu(cc)
    o_ref[0] = jnp.dot(s, w_ref[0], precision=lax.Precision.HIGHEST,
                       preferred_element_type=F32) + b_ref[0]


def _modulation(cc, w_mod, b_mod):
    depth, d, n = w_mod.shape
    tn = _largest_divisor(n, 1536, LANE)
    return pl.pallas_call(
        _mod_kernel,
        grid=(depth, n // tn),
        in_specs=[pl.BlockSpec((8, d), lambda l, j: (0, 0)),
                  pl.BlockSpec((1, d, tn), lambda l, j: (l, 0, j)),
                  pl.BlockSpec((1, 1, tn), lambda l, j: (l, 0, j))],
        out_specs=pl.BlockSpec((1, 8, tn), lambda l, j: (l, 0, j)),
        out_shape=jax.ShapeDtypeStruct((depth, 8, n), F32),
        compiler_params=_params(("arbitrary", "arbitrary")),
        name="modulation",
    )(cc, w_mod, b_mod.reshape(depth, 1, n))


def _inproj_kernel(x_ref, mods_ref, nw_ref, w_ref, lbl_ref, cos_ref, sin_ref,
                   p_ref, lf_ref, h_ref, *, layer, lc, tm, d):
    i = pl.program_id(0)
    j = pl.program_id(1)

    @pl.when(j == 0)
    def _():
        x = x_ref[...]
        xn = x * lax.rsqrt(jnp.mean(x * x, axis=-1, keepdims=True) + EPS) * nw_ref[...]
        shift = _row_select(i * tm, tm, lc, mods_ref, 0, d)
        scale = _row_select(i * tm, tm, lc, mods_ref, 1, d)
        h_ref[...] = (xn * (1.0 + scale) + shift).astype(BF16)

    def for_row_groups(epilogue):
        for r0 in range(0, tm, CUM_GROUP):
            rows = slice(r0, r0 + CUM_GROUP)
            acc = jnp.dot(h_ref[rows, :], w_ref[...], preferred_element_type=F32)
            epilogue(acc, rows)

    def put(val, rows):
        v = val.astype(BF16)
        for s in range(SLABS_PER_TILE):
            p_ref[s, rows, :] = v[:, s * LANE:(s + 1) * LANE]

    tiles = lambda seg: tuple(range(SEG_START[seg] // SLABS_PER_TILE,
                                    (SEG_START[seg] + SEG_SLABS[seg]) // SLABS_PER_TILE))
    in_tiles = lambda segs: functools.reduce(
        jnp.logical_or, [j == t for seg in segs for t in tiles(seg)])

    @pl.when(in_tiles((0,)))
    def _():
        for_row_groups(lambda acc, rows: put(_silu(acc) * (HG_DK ** -0.5), rows))

    @pl.when(in_tiles((1, 2)))
    def _():
        logits = lbl_ref[jnp.clip(j - tiles(1)[0], 0, 1)]
        e = jnp.exp(logits - jnp.max(logits, axis=0, keepdims=True))
        p = e / jnp.sum(e, axis=0, keepdims=True)
        lb = jnp.zeros_like(p[0:1])
        for r in range(1, layer + 1):
            lb = lb + p[r:r + 1]
        lb = jnp.clip(lb, 0.0, 1.0 - 1e-6)
        g = CUM_GROUP
        ri = lax.broadcasted_iota(jnp.int32, (g, g), 0)
        ci = lax.broadcasted_iota(jnp.int32, (g, g), 1)
        same_chunk = (ri // HG_CHUNK) == (ci // HG_CHUNK)
        before = jnp.where(same_chunk & (ci <= ri), 1.0, 0.0)
        after = jnp.where(same_chunk & (ci >= ri), 1.0, 0.0)
        tri = jnp.where(j == tiles(1)[0], before, after).astype(BF16)

        def epilogue(acc, rows):
            sig = _sigmoid(acc)
            put((1.0 - lb) * (1.0 - sig), rows)
            rest = jnp.log(jnp.maximum(lb + (1.0 - lb) * sig, F_MIN))
            cum = jnp.zeros((g, COL_TILE), F32)
            for _ in range(CUM_TERMS):
                term = rest.astype(BF16)
                cum = cum + jnp.dot(tri, term, preferred_element_type=F32)
                rest = rest - term.astype(F32)
            cum = cum * LOG2E
            for s in range(SLABS_PER_TILE):
                lf_ref[s, rows, :] = cum[:, s * LANE:(s + 1) * LANE]

        for_row_groups(epilogue)

    @pl.when(in_tiles((3, 7)))
    def _():
        for_row_groups(put)

    @pl.when(in_tiles((4, 8)))
    def _():
        for_row_groups(lambda acc, rows: put(_silu(acc), rows))

    @pl.when(in_tiles((5, 6)))
    def _():
        scale = jnp.where(j == tiles(5)[0], RET_DK ** -0.5, 1.0)

        def epilogue(acc, rows):
            xq = acc * scale
            n = xq.shape[1]
            lane = lax.broadcasted_iota(jnp.int32, xq.shape, 1)
            partner = jnp.where((lane & 32) == 0, pltpu.roll(xq, n - 32, axis=1),
                                pltpu.roll(xq, 32, axis=1))
            cos = jnp.concatenate([cos_ref[rows, :]] * SLABS_PER_TILE, axis=1)
            sin = jnp.concatenate([sin_ref[rows, :]] * SLABS_PER_TILE, axis=1)
            put(xq * cos + partner * sin, rows)

        for_row_groups(epilogue)

    @pl.when(in_tiles((9, 10)))
    def _():
        for_row_groups(lambda acc, rows: put(_sigmoid(acc), rows))


def _inproj(xc, mods, norm_w, w_in_bf, lb_logits, cos_t, sin_t, *, layer, lc):
    l, d = xc.shape
    d_in = w_in_bf.shape[1]
    assert d_in == N_SLABS * LANE
    tm = _largest_divisor(l, 1280, CUM_GROUP)
    n_col = d_in // COL_TILE
    kf_tile = S_KF // SLABS_PER_TILE
    kern = functools.partial(_inproj_kernel, layer=layer, lc=lc, tm=tm, d=d)
    return pl.pallas_call(
        kern,
        grid=(l // tm, n_col),
        in_specs=[pl.BlockSpec((tm, d), lambda i, j: (i, 0)),
                  pl.BlockSpec(mods.shape, lambda i, j: (0, 0)),
                  pl.BlockSpec((1, d), lambda i, j: (0, 0)),
                  pl.BlockSpec((d, COL_TILE), lambda i, j: (0, (j + TILE_ROTATION) % n_col)),
                  pl.BlockSpec(lb_logits.shape, lambda i, j: (0, 0, 0)),
                  pl.BlockSpec((tm, LANE), lambda i, j: (i, 0)),
                  pl.BlockSpec((tm, LANE), lambda i, j: (i, 0))],
        out_specs=[pl.BlockSpec((SLABS_PER_TILE, tm, LANE), lambda i, j: (j, i, 0)),
                   pl.BlockSpec((SLABS_PER_TILE, tm, LANE),
                                lambda i, j: (jnp.clip(j - kf_tile, 0, 1), i, 0))],
        out_shape=[jax.ShapeDtypeStruct((N_SLABS, l, LANE), BF16),
                   jax.ShapeDtypeStruct((2 * HG_HEADS, l, LANE), F32)],
        scratch_shapes=[pltpu.VMEM((tm, d), BF16)],
        compiler_params=_params(("arbitrary", "arbitrary")),
        name="inproj",
    )(xc, mods, norm_w.reshape(1, d), w_in_bf, lb_logits, cos_t, sin_t)


def _hg_bwd_kernel(kb_ref, v_ref, bc_ref, sb_ref, s_ref, *, n_chunks):
    @pl.when(pl.program_id(0) == 0)
    def _():
        s_ref[...] = jnp.zeros_like(s_ref)

    def body(n, carry):
        cc = n_chunks - 1 - n
        rows = pl.ds(pl.multiple_of(cc * HG_CHUNK, HG_CHUNK), HG_CHUNK)
        for h in range(HG_HEADS):
            bb = bc_ref[h, rows, :]
            k = kb_ref[h, rows, :].astype(F32)
            s = s_ref[h]
            sb_ref[cc, h] = s.astype(BF16)
            kt = (k * jnp.exp2(bb[0:1, :] - bb)).astype(BF16)
            s_ref[h] = s * jnp.exp2(bb[0:1, :]) + lax.dot_general(
                v_ref[h, rows, :], kt, TN_DIMS, preferred_element_type=F32)
        return carry

    lax.fori_loop(0, n_chunks, body, 0)


def _bwd_block_order(i, n_ctx_blocks, n_blocks):
    return jnp.where(i < n_ctx_blocks, n_ctx_blocks - 1 - i, n_blocks - 1 - (i - n_ctx_blocks))


def _hg_bwd(p3, lf3, *, lc):
    l = p3.shape[1]
    tb = HG_BLOCK
    nb, nbc = l // tb, lc // tb
    ncb = tb // HG_CHUNK
    order = lambda i: _bwd_block_order(i, nbc, nb)
    kern = functools.partial(_hg_bwd_kernel, n_chunks=ncb)
    return pl.pallas_call(
        kern,
        grid=(nb,),
        in_specs=[pl.BlockSpec((HG_HEADS, tb, LANE), lambda i: (S_KB // HG_HEADS, order(i), 0)),
                  pl.BlockSpec((HG_HEADS, tb, LANE), lambda i: (S_HI // HG_HEADS, order(i), 0)),
                  pl.BlockSpec((HG_HEADS, tb, LANE), lambda i: (1, order(i), 0))],
        out_specs=pl.BlockSpec((ncb, HG_HEADS, HG_DV, HG_DK), lambda i: (order(i), 0, 0, 0)),
        out_shape=jax.ShapeDtypeStruct((l // HG_CHUNK, HG_HEADS, HG_DV, HG_DK), BF16),
        scratch_shapes=[pltpu.VMEM((HG_HEADS, HG_DV, HG_DK), F32)],
        compiler_params=_params(("arbitrary",)),
        name="hgrn_bwd_state",
    )(p3, p3, lf3)


def _hg_scores(q, q_edge, k32, b, fwd, sub, exact, k32_ref=None, b_ref=None):
    c = HG_CHUNK
    n_sub = c // sub
    cap = 0.0 if exact else HG_CAP
    lane = lax.broadcasted_iota(jnp.int32, (sub, c), 1)
    blocks = []
    for blk in range(n_sub):
        r = blk * sub
        q_blk = q[r:r + sub]
        b_blk = b[r:r + sub]
        edge = blk == 0 if fwd else blk == n_sub - 1
        if edge and exact:
            a = jnp.zeros((sub, c), F32)
        else:
            if edge:
                qt = q_edge[r:r + sub]
                ref_minus_b = -b
            else:
                ref = b[r - 1:r] if fwd else b[r + sub:r + sub + 1]
                qt = q_blk * jnp.exp2(b_blk - ref)
                ref_minus_b = ref - b
            kt = (k32 * jnp.exp2(jnp.minimum(ref_minus_b, cap))).astype(BF16)
            a = lax.dot_general(qt.astype(BF16), kt, NT_DIMS, preferred_element_type=F32)
        if exact:
            for jj in range(sub):
                s = r + jj
                e = jnp.exp2(b_blk - b_ref[s:s + 1, :])
                col = jnp.sum(q_blk * k32_ref[s:s + 1, :] * e, axis=1, keepdims=True)
                a = jnp.where(lane == s, col, a)
        blocks.append(a)
    a = jnp.concatenate(blocks, axis=0)
    ri = lax.broadcasted_iota(jnp.int32, (c, c), 0)
    ci = lax.broadcasted_iota(jnp.int32, (c, c), 1)
    return jnp.where((ci <= ri) if fwd else (ci >= ri), a, 0.0)


def _hg_head_scores(refs, cc, h, sub, exact):
    (q_ref, kf_ref, kb_ref, v_ref, _, bc_ref, sb_ref, _, _, s_ref, k32_ref, b_ref) = refs
    rows = pl.ds(pl.multiple_of(cc * HG_CHUNK, HG_CHUNK), HG_CHUNK)
    q = q_ref[h, rows, :].astype(F32)
    v = v_ref[h, rows, :]
    bf = bc_ref[h, rows, :]
    bb = bc_ref[HG_HEADS + h, rows, :]
    kf = kf_ref[h, rows, :].astype(F32)
    kb = kb_ref[h, rows, :].astype(F32)
    qf = q * jnp.exp2(bf)
    qb = q * jnp.exp2(bb)
    s = s_ref[h]
    inter = lax.dot_general(jnp.concatenate([qf, qb], axis=1).astype(BF16),
                            jnp.concatenate([s.astype(BF16), sb_ref[cc, h]], axis=1),
                            NT_DIMS, preferred_element_type=F32)
    if exact:
        k32_ref[...] = kf
        b_ref[...] = bf
    a = _hg_scores(q, qf, kf, bf, True, sub, exact, k32_ref, b_ref)
    if exact:
        k32_ref[...] = kb
        b_ref[...] = bb
    a = a + _hg_scores(q, qb, kb, bb, False, sub, exact, k32_ref, b_ref)
    b_last = bf[HG_CHUNK - 1:HG_CHUNK, :]
    kt = (kf * jnp.exp2(b_last - bf)).astype(BF16)
    s_ref[h] = s * jnp.exp2(b_last) + lax.dot_general(v, kt, TN_DIMS,
                                                      preferred_element_type=F32)
    return inter, a, v


def _hg_head_finish(refs, cc, h, inter, a, v):
    gate_ref, nw_ref, o_ref = refs[4], refs[7], refs[8]
    rows = pl.ds(pl.multiple_of(cc * HG_CHUNK, HG_CHUNK), HG_CHUNK)
    o = inter + jnp.dot(a.astype(BF16), v, preferred_element_type=F32)
    o = o * lax.rsqrt(jnp.mean(o * o, axis=-1, keepdims=True) + EPS)
    o = o * nw_ref[h] * gate_ref[h, rows, :].astype(F32)
    o_ref[h, rows, :] = o.astype(BF16)


def _hg_min_block_decay(bc_ref):
    n = bc_ref.shape[1] // HG_FAST_SUB
    first = lax.broadcasted_iota(jnp.int32, (n, LANE), 0) % 2 == 0
    worst = None
    for h in range(HG_HEADS):
        ends = bc_ref.at[h][pl.ds(HG_FAST_SUB - 1, n, stride=HG_FAST_SUB), :]
        fwd = jnp.where(first, ends, ends - pltpu.roll(ends, 1, axis=0))
        starts = bc_ref.at[HG_HEADS + h][pl.ds(0, n, stride=HG_FAST_SUB), :]
        bwd = jnp.where(first, starts - pltpu.roll(starts, n - 1, axis=0), starts)
        m = jnp.minimum(fwd, bwd)
        worst = m if worst is None else jnp.minimum(worst, m)
    return jnp.min(worst)


def _hg_fwd_kernel(*refs, n_chunks):
    bc_ref, s_ref = refs[5], refs[9]

    @pl.when(pl.program_id(0) == 0)
    def _():
        s_ref[...] = jnp.zeros_like(s_ref)

    factored_ok = _hg_min_block_decay(bc_ref) >= -HG_CAP

    @pl.when(factored_ok)
    def _():
        def body(cc, carry):
            staged = [_hg_head_scores(refs, cc, h, HG_FAST_SUB, False) for h in range(HG_HEADS)]
            for h, parts in enumerate(staged):
                _hg_head_finish(refs, cc, h, *parts)
            return carry
        lax.fori_loop(0, n_chunks, body, 0)

    @pl.when(jnp.logical_not(factored_ok))
    def _():
        def body(n, carry):
            cc, h = n // HG_HEADS, n % HG_HEADS
            _hg_head_finish(refs, cc, h, *_hg_head_scores(refs, cc, h, HG_SUB, True))
            return carry
        lax.fori_loop(0, n_chunks * HG_HEADS, body, 0)


def _hg_fwd(p3, lf3, sb, norm_w):
    l = p3.shape[1]
    tb = HG_BLOCK
    nb = l // tb
    ncb = tb // HG_CHUNK
    seg = lambda s: pl.BlockSpec((HG_HEADS, tb, LANE), lambda i: (s // HG_HEADS, i, 0))
    kern = functools.partial(_hg_fwd_kernel, n_chunks=ncb)
    return pl.pallas_call(
        kern,
        grid=(nb,),
        in_specs=[seg(S_HQ), seg(S_KF), seg(S_KB), seg(S_HI), seg(S_HGATE),
                  pl.BlockSpec((2 * HG_HEADS, tb, LANE), lambda i: (0, i, 0)),
                  pl.BlockSpec((ncb, HG_HEADS, HG_DV, HG_DK), lambda i: (i, 0, 0, 0)),
                  pl.BlockSpec((HG_HEADS, 1, HG_DV), lambda i: (0, 0, 0))],
        out_specs=pl.BlockSpec((HG_HEADS, tb, LANE), lambda i: (0, i, 0)),
        out_shape=jax.ShapeDtypeStruct((HG_HEADS, l, LANE), BF16),
        scratch_shapes=[pltpu.VMEM((HG_HEADS, HG_DV, HG_DK), F32),
                        pltpu.VMEM((HG_CHUNK, HG_DK), F32),
                        pltpu.VMEM((HG_CHUNK, HG_DK), F32)],
        compiler_params=_params(("arbitrary",)),
        name="hgrn_fwd",
    )(p3, p3, p3, p3, p3, lf3, sb, norm_w.reshape(HG_HEADS, 1, HG_DV))


def _log_sigmoid(x):
    return jnp.minimum(x, 0.0) - jnp.log1p(jnp.exp(-jnp.abs(x)))


def _ret_log_gamma(logit_ref, direction, h, shape):
    return _log_sigmoid(jnp.full(shape, logit_ref[direction, h], F32))


def _ret_bwd_kernel(logit_ref, k_ref, v_ref, sb_ref, s_ref):
    c = RET_BLOCK

    @pl.when(pl.program_id(0) == 0)
    def _():
        s_ref[...] = jnp.zeros_like(s_ref)

    t = lax.broadcasted_iota(jnp.int32, (c, RET_DK), 0).astype(F32)
    for h in range(RET_HEADS):
        lg = _ret_log_gamma(logit_ref, 1, h, (c, RET_DK))
        s = s_ref[h]
        sb_ref[0, h] = s.astype(BF16)
        kt = (k_ref[h].astype(F32) * jnp.exp(t * lg)).astype(BF16)
        v = jnp.concatenate([v_ref[2 * h], v_ref[2 * h + 1]], axis=1)
        s_ref[h] = s * jnp.exp(c * lg[0:1, 0:1]) + lax.dot_general(
            kt, v, TN_DIMS, preferred_element_type=F32)


def _ret_bwd(logit, p3, *, lc):
    l = p3.shape[1]
    c = RET_BLOCK
    nb, nbc = l // c, lc // c
    order = lambda i: _bwd_block_order(i, nbc, nb)
    return pl.pallas_call(
        _ret_bwd_kernel,
        grid=(nb,),
        in_specs=[pl.BlockSpec(memory_space=pltpu.SMEM),
                  pl.BlockSpec((RET_HEADS, c, LANE), lambda i: (S_RK // RET_HEADS, order(i), 0)),
                  pl.BlockSpec((2 * RET_HEADS, c, LANE),
                               lambda i: (S_RV // (2 * RET_HEADS), order(i), 0))],
        out_specs=pl.BlockSpec((1, RET_HEADS, RET_DK, RET_DV), lambda i: (order(i), 0, 0, 0)),
        out_shape=jax.ShapeDtypeStruct((nb, RET_HEADS, RET_DK, RET_DV), BF16),
        scratch_shapes=[pltpu.VMEM((RET_HEADS, RET_DK, RET_DV), F32)],
        compiler_params=_params(("arbitrary",)),
        name="ret_bwd_state",
    )(logit, p3, p3)


def _ret_fwd_kernel(logit_ref, q_ref, k_ref, v_ref, gate_ref, sb_ref, gnw_ref,
                    o_ref, s_ref, dmat_ref):
    c = RET_BLOCK

    @pl.when(pl.program_id(0) == 0)
    def _():
        s_ref[...] = jnp.zeros_like(s_ref)
        ri = lax.broadcasted_iota(jnp.int32, (c, c), 0)
        ci = lax.broadcasted_iota(jnp.int32, (c, c), 1)
        dist = (ri - ci).astype(F32)
        for h in range(RET_HEADS):
            lgf = _ret_log_gamma(logit_ref, 0, h, (c, c))
            lgb = _ret_log_gamma(logit_ref, 1, h, (c, c))
            dmat_ref[h] = (jnp.where(ci <= ri, jnp.exp(jnp.maximum(dist, 0.0) * lgf), 0.0)
                           + jnp.where(ci >= ri, jnp.exp(jnp.maximum(-dist, 0.0) * lgb), 0.0))

    t = lax.broadcasted_iota(jnp.int32, (c, RET_DK), 0).astype(F32)
    for h in range(RET_HEADS):
        lgf = _ret_log_gamma(logit_ref, 0, h, (c, RET_DK))
        lgb = _ret_log_gamma(logit_ref, 1, h, (c, RET_DK))
        q = q_ref[h]
        k = k_ref[h]
        q32 = q.astype(F32)
        v = jnp.concatenate([v_ref[2 * h], v_ref[2 * h + 1]], axis=1)
        s = s_ref[h]
        sc = lax.dot_general(q, k, NT_DIMS, preferred_element_type=F32) * dmat_ref[h]
        o = jnp.dot(sc.astype(BF16), v, preferred_element_type=F32)
        o = o + jnp.dot((q32 * jnp.exp((t + 1.0) * lgf)).astype(BF16), s.astype(BF16),
                        preferred_element_type=F32)
        o = o + jnp.dot((q32 * jnp.exp((c - t) * lgb)).astype(BF16), sb_ref[0, h],
                        preferred_element_type=F32)
        kt = (k.astype(F32) * jnp.exp((c - 1.0 - t) * lgf)).astype(BF16)
        s_ref[h] = s * jnp.exp(c * lgf[0:1, 0:1]) + lax.dot_general(
            kt, v, TN_DIMS, preferred_element_type=F32)

        mu = jnp.mean(o, axis=-1, keepdims=True)
        dev = o - mu
        var = jnp.mean(dev * dev, axis=-1, keepdims=True)
        o = dev * lax.rsqrt(var + EPS) * gnw_ref[h]
        gate = jnp.concatenate([gate_ref[2 * h], gate_ref[2 * h + 1]], axis=1).astype(F32)
        o_ref[:, h * RET_DV:(h + 1) * RET_DV] = (o * gate).astype(BF16)


def _ret_fwd(logit, p3, sb, gn_w):
    l = p3.shape[1]
    c = RET_BLOCK
    nb = l // c
    return pl.pallas_call(
        _ret_fwd_kernel,
        grid=(nb,),
        in_specs=[pl.BlockSpec(memory_space=pltpu.SMEM),
                  pl.BlockSpec((RET_HEADS, c, LANE), lambda i: (S_RQ // RET_HEADS, i, 0)),
                  pl.BlockSpec((RET_HEADS, c, LANE), lambda i: (S_RK // RET_HEADS, i, 0)),
                  pl.BlockSpec((2 * RET_HEADS, c, LANE), lambda i: (S_RV // (2 * RET_HEADS), i, 0)),
                  pl.BlockSpec((2 * RET_HEADS, c, LANE), lambda i: (S_RG // (2 * RET_HEADS), i, 0)),
                  pl.BlockSpec((1, RET_HEADS, RET_DK, RET_DV), lambda i: (i, 0, 0, 0)),
                  pl.BlockSpec((RET_HEADS, 1, RET_DV), lambda i: (0, 0, 0))],
        out_specs=pl.BlockSpec((c, RET_HEADS * RET_DV), lambda i: (i, 0)),
        out_shape=jax.ShapeDtypeStruct((l, RET_HEADS * RET_DV), BF16),
        scratch_shapes=[pltpu.VMEM((RET_HEADS, RET_DK, RET_DV), F32),
                        pltpu.VMEM((RET_HEADS, c, c), F32)],
        compiler_params=_params(("arbitrary",)),
        name="ret_fwd",
    )(logit, p3, p3, p3, p3, sb, gn_w.reshape(RET_HEADS, 1, RET_DV))


def _pack_bf16_pairs(v):
    w = v.shape[1] // 2
    lo = pltpu.bitcast(v[:, :w].astype(BF16).astype(F32), U32)
    hi = pltpu.bitcast(v[:, w:].astype(BF16).astype(F32), U32)
    return (lo >> 16) | (hi & jnp.uint32(0xFFFF0000))


def _unpack_bf16_pairs(u):
    lo = pltpu.bitcast(u << 16, F32)
    hi = pltpu.bitcast(u & jnp.uint32(0xFFFF0000), F32)
    return lo, hi


def _route(logits):
    lane = lax.broadcasted_iota(jnp.int32, logits.shape, 1)
    big = jnp.int32(10 ** 6)
    neg = -jnp.inf
    gl = jnp.where(lane < N_GROUPS, logits, neg)
    gmax = jnp.max(gl, axis=1, keepdims=True)
    grp = jnp.min(jnp.where(gl == gmax, lane, big), axis=1, keepdims=True)
    g_val = 1.0 / jnp.sum(jnp.exp(gl - gmax), axis=1, keepdims=True)
    lo = N_GROUPS + EXPERTS_PER_GROUP * grp
    el = jnp.where((lane >= lo) & (lane < lo + EXPERTS_PER_GROUP), logits, neg)
    v1 = jnp.max(el, axis=1, keepdims=True)
    i1 = jnp.min(jnp.where(el == v1, lane, big), axis=1, keepdims=True)
    el2 = jnp.where(lane == i1, neg, el)
    v2 = jnp.max(el2, axis=1, keepdims=True)
    i2 = jnp.min(jnp.where(el2 == v2, lane, big), axis=1, keepdims=True)
    r = jnp.exp(v2 - v1)
    w1 = g_val / (1.0 + r)
    w2 = w1 * r
    return ((i1 - N_GROUPS).astype(F32), (i2 - N_GROUPS).astype(F32), w1, w2)


def _merge_kernel(ohg_ref, oret_ref, ga_ref, gb_ref, x_ref, mods_ref, wohg_ref, woret_ref,
                  wout_ref, nw_ref, wrh_ref, wrl_ref, rb_ref,
                  xo_ref, hp_ref, route_ref, cnt_out_ref, cnt_ref, *, lc, tm, d, tiles_per_seg):
    i = pl.program_id(0)
    ohg = jnp.concatenate([ohg_ref[s] for s in range(HG_HEADS)], axis=1)
    ga = jnp.concatenate([ga_ref[s] for s in range(d // LANE)], axis=1).astype(F32)
    gb = jnp.concatenate([gb_ref[s] for s in range(d // LANE)], axis=1).astype(F32)
    y_hg = jnp.dot(ohg, wohg_ref[...], preferred_element_type=F32)
    y_ret = jnp.dot(oret_ref[...], woret_ref[...], preferred_element_type=F32)
    m = (ga * y_hg + gb * y_ret).astype(BF16)
    y = jnp.dot(m, wout_ref[...], preferred_element_type=F32)
    x = x_ref[...] + _row_select(i * tm, tm, lc, mods_ref, 2, d) * y
    xo_ref[...] = x

    xn = x * lax.rsqrt(jnp.mean(x * x, axis=-1, keepdims=True) + EPS) * nw_ref[...]
    h = (xn * (1.0 + _row_select(i * tm, tm, lc, mods_ref, 4, d))
         + _row_select(i * tm, tm, lc, mods_ref, 3, d))
    hp_ref[...] = _pack_bf16_pairs(h)

    h_hi = h.astype(BF16)
    h_lo = (h - h_hi.astype(F32)).astype(BF16)
    logits = (jnp.dot(h_hi, wrh_ref[...], preferred_element_type=F32)
              + jnp.dot(h_lo, wrh_ref[...], preferred_element_type=F32)
              + jnp.dot(h_hi, wrl_ref[...], preferred_element_type=F32)) + rb_ref[...]
    e0, e1, w0, w1 = _route(logits)

    @pl.when(i % tiles_per_seg == 0)
    def _():
        cnt_ref[...] = jnp.zeros_like(cnt_ref)

    lane_e = lax.broadcasted_iota(jnp.int32, (tm, LANE), 1).astype(F32)
    hot0 = lane_e == e0
    hot1 = lane_e == e1
    hot = jnp.where(hot0 | hot1, 1.0, 0.0)
    earlier = (lax.broadcasted_iota(jnp.int32, (tm, tm), 1)
               < lax.broadcasted_iota(jnp.int32, (tm, tm), 0))
    before = jnp.dot(jnp.where(earlier, 1.0, 0.0).astype(BF16), hot.astype(BF16),
                     preferred_element_type=F32) + cnt_ref[0:1, :]
    rank0 = jnp.sum(jnp.where(hot0, before, 0.0), axis=1, keepdims=True)
    rank1 = jnp.sum(jnp.where(hot1, before, 0.0), axis=1, keepdims=True)
    cnt_ref[...] = cnt_ref[...] + jnp.sum(hot, axis=0, keepdims=True)
    cnt_out_ref[0] = cnt_ref[...]

    lane = lax.broadcasted_iota(jnp.int32, (tm, ROUTE_W), 1)
    rec = jnp.zeros((tm, ROUTE_W), F32)
    for k, val in enumerate((e0, e1, w0, w1, rank0, rank1)):
        rec = jnp.where(lane == k, val, rec)
    route_ref[...] = rec


def _merge(ohg, oret, p3, xc, mods, wohg, woret, wout, norm_w, wr_hi, wr_lo, rbias, *, lc, n_seg):
    l, d = xc.shape
    seg_tokens = l // n_seg
    tm = _largest_divisor(seg_tokens, 832, 16)
    tiles_per_seg = seg_tokens // tm
    n_slab = d // LANE
    full = lambda a: pl.BlockSpec(a.shape, lambda i: (0,) * a.ndim)
    kern = functools.partial(_merge_kernel, lc=lc, tm=tm, d=d, tiles_per_seg=tiles_per_seg)
    return pl.pallas_call(
        kern,
        grid=(l // tm,),
        in_specs=[pl.BlockSpec((HG_HEADS, tm, LANE), lambda i: (0, i, 0)),
                  pl.BlockSpec((tm, oret.shape[1]), lambda i: (i, 0)),
                  pl.BlockSpec((n_slab, tm, LANE), lambda i: (S_GA // n_slab, i, 0)),
                  pl.BlockSpec((n_slab, tm, LANE), lambda i: (S_GB // n_slab, i, 0)),
                  pl.BlockSpec((tm, d), lambda i: (i, 0)),
                  full(mods), full(wohg), full(woret), full(wout),
                  pl.BlockSpec((1, d), lambda i: (0, 0)),
                  full(wr_hi), full(wr_lo), full(rbias)],
        out_specs=[pl.BlockSpec((tm, d), lambda i: (i, 0)),
                   pl.BlockSpec((tm, d // 2), lambda i: (i, 0)),
                   pl.BlockSpec((tm, ROUTE_W), lambda i: (i, 0)),
                   pl.BlockSpec((1, 8, LANE), lambda i: (i // tiles_per_seg, 0, 0))],
        out_shape=[jax.ShapeDtypeStruct((l, d), F32),
                   jax.ShapeDtypeStruct((l, d // 2), U32),
                   jax.ShapeDtypeStruct((l, ROUTE_W), F32),
                   jax.ShapeDtypeStruct((n_seg, 8, LANE), F32)],
        scratch_shapes=[pltpu.VMEM((8, LANE), F32)],
        compiler_params=_params(("arbitrary",)),
        name="merge_router",
    )(ohg, oret, p3, p3, xc, mods, wohg, woret, wout, norm_w.reshape(1, d), wr_hi, wr_lo, rbias)


def _expert_kernel(be_ref, nused_ref, slot_ref, hp_ref, wg_ref, wu_ref, wd_ref, y_ref,
                   xs_ref, *, seg_tokens):
    s = pl.program_id(0)
    b = pl.program_id(1)

    @pl.when(b == 0)
    def _():
        xs_ref[...] = jnp.zeros_like(xs_ref)

        def scatter(t, carry):
            a = 2 * (s * seg_tokens + t)
            row = hp_ref[0, pl.ds(t, 1), :]
            xs_ref[pl.ds(slot_ref[a], 1), :] = row
            xs_ref[pl.ds(slot_ref[a + 1], 1), :] = row
            return carry

        lax.fori_loop(0, seg_tokens, scatter, 0, unroll=4)

    @pl.when(b < nused_ref[s])
    def _():
        r0 = pl.multiple_of(b * MOE_BLOCK, MOE_BLOCK)
        lo, hi = _unpack_bf16_pairs(xs_ref[pl.ds(r0, MOE_BLOCK), :])
        x = jnp.concatenate([lo, hi], axis=1).astype(BF16)
        g = jnp.dot(x, wg_ref[0], preferred_element_type=F32)
        u = jnp.dot(x, wu_ref[0], preferred_element_type=F32)
        a = (_silu(g) * u).astype(BF16)
        y_ref[0] = _pack_bf16_pairs(jnp.dot(a, wd_ref[0], preferred_element_type=F32))

    @pl.when(b >= nused_ref[s])
    def _():
        y_ref[...] = jnp.zeros_like(y_ref)


def _experts(block_expert, n_used, slot, hp_seg, wg, wu, wd, *, nbs):
    n_seg, seg_tokens, half = hp_seg.shape
    d = 2 * half
    d_e = wg.shape[2]
    kern = functools.partial(_expert_kernel, seg_tokens=seg_tokens)
    w_idx = lambda s, b, be, *_: (be[s * nbs + b], 0, 0)
    grid_spec = pltpu.PrefetchScalarGridSpec(
        num_scalar_prefetch=3,
        grid=(n_seg, nbs),
        in_specs=[pl.BlockSpec((1, seg_tokens, half), lambda s, b, *_: (s, 0, 0)),
                  pl.BlockSpec((1, d, d_e), w_idx),
                  pl.BlockSpec((1, d, d_e), w_idx),
                  pl.BlockSpec((1, d_e, d), w_idx)],
        out_specs=pl.BlockSpec((1, MOE_BLOCK, half), lambda s, b, *_: (s, b, 0)),
        scratch_shapes=[pltpu.VMEM((nbs * MOE_BLOCK, half), U32)])
    return pl.pallas_call(
        kern,
        grid_spec=grid_spec,
        out_shape=jax.ShapeDtypeStruct((n_seg, nbs * MOE_BLOCK, half), U32),
        compiler_params=_params(("arbitrary", "arbitrary")),
        name="experts",
    )(block_expert, n_used, slot, hp_seg, wg, wu, wd)


def _combine_kernel(slot_ref, y_ref, route_ref, x_ref, mods_ref, fw_ref, o_ref,
                    g0_ref, g1_ref, *, lc, tm, d, tiles_per_seg, final):
    s = pl.program_id(0)
    t = pl.program_id(1)
    row0 = (s * tiles_per_seg + t) * tm

    def gather(r, carry):
        a = 2 * (row0 + r)
        g0_ref[pl.ds(r, 1), :] = y_ref[0, pl.ds(slot_ref[a], 1), :]
        g1_ref[pl.ds(r, 1), :] = y_ref[0, pl.ds(slot_ref[a + 1], 1), :]
        return carry

    lax.fori_loop(0, tm, gather, 0, unroll=8)
    lo0, hi0 = _unpack_bf16_pairs(g0_ref[...])
    lo1, hi1 = _unpack_bf16_pairs(g1_ref[...])
    w0 = route_ref[:, 2:3]
    w1 = route_ref[:, 3:4]
    y = jnp.concatenate([w0 * lo0 + w1 * lo1, w0 * hi0 + w1 * hi1], axis=1)
    x = x_ref[...] + _row_select(row0, tm, lc, mods_ref, 5, d) * y
    if final:
        x = x * lax.rsqrt(jnp.mean(x * x, axis=-1, keepdims=True) + EPS) * fw_ref[...]
    o_ref[...] = x


def _combine(slot, ybuf, route, xc, mods, final_w, *, lc, final):
    l, d = xc.shape
    n_seg, rows, half = ybuf.shape
    seg_tokens = l // n_seg
    tm = _largest_divisor(seg_tokens, 320, 8)
    tiles = seg_tokens // tm
    kern = functools.partial(_combine_kernel, lc=lc, tm=tm, d=d, tiles_per_seg=tiles, final=final)
    row = lambda s, t, *_: (s * tiles + t, 0)
    grid_spec = pltpu.PrefetchScalarGridSpec(
        num_scalar_prefetch=1,
        grid=(n_seg, tiles),
        in_specs=[pl.BlockSpec((1, rows, half), lambda s, t, *_: (s, 0, 0),
                               pipeline_mode=pl.Buffered(1)),
                  pl.BlockSpec((tm, ROUTE_W), row),
                  pl.BlockSpec((tm, d), row),
                  pl.BlockSpec(mods.shape, lambda s, t, *_: (0, 0)),
                  pl.BlockSpec((1, d), lambda s, t, *_: (0, 0))],
        out_specs=pl.BlockSpec((tm, d), row),
        scratch_shapes=[pltpu.VMEM((tm, half), U32), pltpu.VMEM((tm, half), U32)])
    return pl.pallas_call(
        kern,
        grid_spec=grid_spec,
        out_shape=jax.ShapeDtypeStruct((l, d), F32),
        compiler_params=_params(("arbitrary", "arbitrary")),
        name="combine",
    )(slot, ybuf, route, xc, mods, final_w.reshape(1, d))


def _dispatch_plan(route, counts, nbs):
    n_seg = counts.shape[0]
    counts = counts[:, 0, :N_EXPERTS].astype(jnp.int32)
    padded = ((counts + MOE_BLOCK - 1) // MOE_BLOCK) * MOE_BLOCK
    pad_end = jnp.cumsum(padded, axis=1)
    pad_start = pad_end - padded
    starts = jnp.arange(nbs, dtype=jnp.int32) * MOE_BLOCK
    block_expert = jnp.minimum(
        jnp.sum((pad_end[:, None, :] <= starts[None, :, None]).astype(jnp.int32), axis=2),
        N_EXPERTS - 1)
    n_used = pad_end[:, -1] // MOE_BLOCK
    e = route[:, 0:2].astype(jnp.int32).reshape(n_seg, -1, 1)
    rank = route[:, 4:6].astype(jnp.int32).reshape(n_seg, -1)
    hot = e == jnp.arange(N_EXPERTS, dtype=jnp.int32)
    slot = rank + jnp.sum(jnp.where(hot, pad_start[:, None, :], 0), axis=2)
    return (slot.reshape(-1), block_expert.reshape(-1).astype(jnp.int32),
            n_used.astype(jnp.int32))


def _rope_tables(lc, t):
    quarter = RET_DK // 4
    inv = ROPE_BASE ** (-jnp.arange(0, 2 * quarter, 2, dtype=F32) / (2 * quarter))
    pos = jnp.arange(t)
    ang_r = (pos // GRID_W).astype(F32)[:, None] * inv
    ang_c = (pos % GRID_W).astype(F32)[:, None] * inv
    cos = jnp.concatenate([jnp.cos(ang_r)] * 2 + [jnp.cos(ang_c)] * 2, axis=1)
    sin = jnp.concatenate([-jnp.sin(ang_r), jnp.sin(ang_r), -jnp.sin(ang_c), jnp.sin(ang_c)], axis=1)
    cos = jnp.concatenate([jnp.ones((lc, RET_DK), F32), cos], axis=0)
    sin = jnp.concatenate([jnp.zeros((lc, RET_DK), F32), sin], axis=0)
    return cos, sin


def kernel(x, c, ctx, c_ctx, w_mod, b_mod, norm_mix_w, norm_ffn_w, w_in, hgrn_lb_logits, hgrn_norm_w,
           ret_decay_logit, ret_gn_w, w_o_hgrn, w_o_ret, w_out, router_group_w, router_group_b,
           router_expert_w, router_expert_b, expert_w_gate, expert_w_up, expert_w_down, final_norm_w):
    b_, t_, d = x.shape
    assert b_ == 1
    lc = ctx.shape[1]
    depth = w_mod.shape[0]
    l = lc + t_
    assert lc % HG_BLOCK == 0 and l % HG_BLOCK == 0 and lc % RET_BLOCK == 0 and l % RET_BLOCK == 0
    n_seg = MOE_SEGMENTS
    assert l % n_seg == 0
    seg_tokens = l // n_seg
    nbs = (2 * seg_tokens + N_EXPERTS * (MOE_BLOCK - 1) + MOE_BLOCK - 1) // MOE_BLOCK

    xc = jnp.concatenate([ctx[0], x[0]], axis=0)
    cc = jnp.zeros((8, d), F32).at[0].set(c[0]).at[1].set(c_ctx)
    mods_all = _modulation(cc, w_mod, b_mod)
    cos_t, sin_t = _rope_tables(lc, t_)

    for layer in range(depth):
        mods = mods_all[layer]
        p3, lf3 = _inproj(xc, mods, norm_mix_w[layer], w_in[layer].astype(BF16), hgrn_lb_logits,
                          cos_t, sin_t, layer=layer, lc=lc)
        sb_hg = _hg_bwd(p3, lf3, lc=lc)
        ohg = _hg_fwd(p3, lf3, sb_hg, hgrn_norm_w[layer])
        sb_ret = _ret_bwd(ret_decay_logit[layer], p3, lc=lc)
        oret = _ret_fwd(ret_decay_logit[layer], p3, sb_ret, ret_gn_w[layer])

        wr = jnp.concatenate([router_group_w[layer], router_expert_w[layer]], axis=1)
        wr = jnp.pad(wr, ((0, 0), (0, LANE - wr.shape[1])))
        wr_hi = wr.astype(BF16)
        wr_lo = (wr - wr_hi.astype(F32)).astype(BF16)
        rbias = jnp.pad(jnp.concatenate([router_group_b[layer], router_expert_b[layer]]),
                        (0, LANE - N_GROUPS - N_EXPERTS)).reshape(1, LANE)
        xc, hp, route, counts = _merge(ohg, oret, p3, xc, mods, w_o_hgrn[layer].astype(BF16),
                                       w_o_ret[layer].astype(BF16), w_out[layer].astype(BF16),
                                       norm_ffn_w[layer], wr_hi, wr_lo, rbias, lc=lc, n_seg=n_seg)

        slot, block_expert, n_used = _dispatch_plan(route, counts, nbs)
        ybuf = _experts(block_expert, n_used, slot, hp.reshape(n_seg, seg_tokens, d // 2),
                        expert_w_gate[layer].astype(BF16), expert_w_up[layer].astype(BF16),
                        expert_w_down[layer].astype(BF16), nbs=nbs)
        xc = _combine(slot, ybuf, route, xc, mods, final_norm_w, lc=lc,
                      final=(layer == depth - 1))

    return xc[lc:][None]
```

```python
import functools

import jax
import jax.numpy as jnp
import numpy as np
from jax import lax
from jax.experimental import pallas as pl
from jax.experimental.pallas import tpu as pltpu

F32 = jnp.float32
BF16 = jnp.bfloat16
U32 = jnp.uint32

GRID_W = 64
HG_HEADS = 4
HG_DK = 128
HG_DV = 128
F_MIN = 1e-30
RET_HEADS = 4
RET_DK = 128
RET_DV = 256
ROPE_BASE = 10000.0
N_GROUPS = 4
EXPERTS_PER_GROUP = 8
N_EXPERTS = N_GROUPS * EXPERTS_PER_GROUP
N_MOD = 6
EPS = 1e-6

LANE = 128
COL_TILE = 512
SLABS_PER_TILE = COL_TILE // LANE
HG_CHUNK = 64
HG_SUB = 16
HG_BLOCK = 256
LOG2E = 1.4426950408889634
HG_FAST_SUB = 32
HG_CAP = 100.0
CUM_GROUP = 256
CUM_TERMS = 2
RET_BLOCK = 256
MOE_SEGMENTS = 5
MOE_BLOCK = 256
MOE_BLOCKS_PER_STEP = 1
ROUTE_W = 8
VMEM_LIMIT = 56 * 1024 * 1024

NT_DIMS = (((1,), (1,)), ((), ()))
TN_DIMS = (((0,), (0,)), ((), ()))

SEG_SLABS = (4, 4, 4, 4, 4, 4, 4, 8, 8, 8, 8)
SEG_ORDER = (7, 8, 9, 10, 0, 1, 2, 3, 4, 5, 6)
_starts = {}
_pos = 0
for _seg in SEG_ORDER:
    _starts[_seg] = _pos
    _pos += SEG_SLABS[_seg]
SEG_START = tuple(_starts[_seg] for _seg in range(len(SEG_SLABS)))
(S_HQ, S_KF, S_KB, S_HI, S_HGATE, S_RQ, S_RK, S_RV, S_RG, S_GA, S_GB) = SEG_START
assert SEG_ORDER == tuple(range(SEG_ORDER[0], len(SEG_SLABS))) + tuple(range(SEG_ORDER[0]))
TILE_ROTATION = sum(SEG_SLABS[:SEG_ORDER[0]]) // SLABS_PER_TILE
N_SLABS = sum(SEG_SLABS)


def _params(sem):
    return pltpu.CompilerParams(dimension_semantics=sem, vmem_limit_bytes=VMEM_LIMIT)


def _sigmoid(x):
    return 1.0 / (1.0 + jnp.exp(-x))


def _silu(x):
    return x * _sigmoid(x)


def _largest_divisor(n, cap, multiple):
    best = None
    for d in range(multiple, cap + 1, multiple):
        if n % d == 0:
            best = d
    assert best is not None, (n, cap, multiple)
    return best


def _row_select(row0, n_rows, lc, mods_ref, k, d):
    rows = row0 + lax.broadcasted_iota(jnp.int32, (n_rows, 1), 0)
    lat = mods_ref[0:1, k * d:(k + 1) * d]
    ctx = mods_ref[1:2, k * d:(k + 1) * d]
    return jnp.where(rows < lc, ctx, lat)


def _mod_kernel(cc_ref, w_ref, b_ref, o_ref):
    cc = cc_ref[...]
    s = _silu(cc)
    o_ref[0] = jnp.dot(s, w_ref[0], precision=lax.Precision.HIGHEST,
                       preferred_element_type=F32) + b_ref[0]


def _modulation(cc, w_mod, b_mod):
    depth, d, n = w_mod.shape
    tn = _largest_divisor(n, 1536, LANE)
    return pl.pallas_call(
        _mod_kernel,
        grid=(depth, n // tn),
        in_specs=[pl.BlockSpec((8, d), lambda l, j: (0, 0)),
                  pl.BlockSpec((1, d, tn), lambda l, j: (l, 0, j)),
                  pl.BlockSpec((1, 1, tn), lambda l, j: (l, 0, j))],
        out_specs=pl.BlockSpec((1, 8, tn), lambda l, j: (l, 0, j)),
        out_shape=jax.ShapeDtypeStruct((depth, 8, n), F32),
        compiler_params=_params(("arbitrary", "arbitrary")),
        name="modulation",
    )(cc, w_mod, b_mod.reshape(depth, 1, n))


def _inproj_kernel(x_ref, mods_ref, nw_ref, w_ref, lbl_ref, cos_ref, sin_ref,
                   p_ref, lf_ref, h_ref, *, layer, lc, tm, d):
    i = pl.program_id(0)
    j = pl.program_id(1)

    @pl.when(j == 0)
    def _():
        x = x_ref[...]
        xn = x * lax.rsqrt(jnp.mean(x * x, axis=-1, keepdims=True) + EPS) * nw_ref[...]
        shift = _row_select(i * tm, tm, lc, mods_ref, 0, d)
        scale = _row_select(i * tm, tm, lc, mods_ref, 1, d)
        h_ref[...] = (xn * (1.0 + scale) + shift).astype(BF16)

    def for_row_groups(epilogue):
        for r0 in range(0, tm, CUM_GROUP):
            rows = slice(r0, r0 + CUM_GROUP)
            acc = jnp.dot(h_ref[rows, :], w_ref[...], preferred_element_type=F32)
            epilogue(acc, rows)

    def put(val, rows):
        v = val.astype(BF16)
        for s in range(SLABS_PER_TILE):
            p_ref[s, rows, :] = v[:, s * LANE:(s + 1) * LANE]

    tiles = lambda seg: tuple(range(SEG_START[seg] // SLABS_PER_TILE,
                                    (SEG_START[seg] + SEG_SLABS[seg]) // SLABS_PER_TILE))
    in_tiles = lambda segs: functools.reduce(
        jnp.logical_or, [j == t for seg in segs for t in tiles(seg)])

    @pl.when(in_tiles((0,)))
    def _():
        for_row_groups(lambda acc, rows: put(_silu(acc) * (HG_DK ** -0.5), rows))

    @pl.when(in_tiles((1, 2)))
    def _():
        logits = lbl_ref[jnp.clip(j - tiles(1)[0], 0, 1)]
        e = jnp.exp(logits - jnp.max(logits, axis=0, keepdims=True))
        p = e / jnp.sum(e, axis=0, keepdims=True)
        lb = jnp.zeros_like(p[0:1])
        for r in range(1, layer + 1):
            lb = lb + p[r:r + 1]
        lb = jnp.clip(lb, 0.0, 1.0 - 1e-6)
        g = CUM_GROUP
        ri = lax.broadcasted_iota(jnp.int32, (g, g), 0)
        ci = lax.broadcasted_iota(jnp.int32, (g, g), 1)
        same_chunk = (ri // HG_CHUNK) == (ci // HG_CHUNK)
        before = jnp.where(same_chunk & (ci <= ri), 1.0, 0.0)
        after = jnp.where(same_chunk & (ci >= ri), 1.0, 0.0)
        tri = jnp.where(j == tiles(1)[0], before, after).astype(BF16)

        def epilogue(acc, rows):
            sig = _sigmoid(acc)
            put((1.0 - lb) * (1.0 - sig), rows)
            rest = jnp.log(jnp.maximum(lb + (1.0 - lb) * sig, F_MIN))
            cum = jnp.zeros((g, COL_TILE), F32)
            for _ in range(CUM_TERMS):
                term = rest.astype(BF16)
                cum = cum + jnp.dot(tri, term, preferred_element_type=F32)
                rest = rest - term.astype(F32)
            cum = cum * LOG2E
            for s in range(SLABS_PER_TILE):
                lf_ref[s, rows, :] = cum[:, s * LANE:(s + 1) * LANE]

        for_row_groups(epilogue)

    @pl.when(in_tiles((3, 7)))
    def _():
        for_row_groups(put)

    @pl.when(in_tiles((4, 8)))
    def _():
        for_row_groups(lambda acc, rows: put(_silu(acc), rows))

    @pl.when(in_tiles((5, 6)))
    def _():
        scale = jnp.where(j == tiles(5)[0], RET_DK ** -0.5, 1.0)

        def epilogue(acc, rows):
            xq = acc * scale
            n = xq.shape[1]
            lane = lax.broadcasted_iota(jnp.int32, xq.shape, 1)
            partner = jnp.where((lane & 32) == 0, pltpu.roll(xq, n - 32, axis=1),
                                pltpu.roll(xq, 32, axis=1))
            cos = jnp.concatenate([cos_ref[rows, :]] * SLABS_PER_TILE, axis=1)
            sin = jnp.concatenate([sin_ref[rows, :]] * SLABS_PER_TILE, axis=1)
            put(xq * cos + partner * sin, rows)

        for_row_groups(epilogue)

    @pl.when(in_tiles((9, 10)))
    def _():
        for_row_groups(lambda acc, rows: put(_sigmoid(acc), rows))


def _inproj(xc, mods, norm_w, w_in_bf, lb_logits, cos_t, sin_t, *, layer, lc):
    l, d = xc.shape
    d_in = w_in_bf.shape[2]
    assert d_in == N_SLABS * LANE
    tm = _largest_divisor(l, 1280, CUM_GROUP)
    n_col = d_in // COL_TILE
    kf_tile = S_KF // SLABS_PER_TILE
    kern = functools.partial(_inproj_kernel, layer=layer, lc=lc, tm=tm, d=d)
    return pl.pallas_call(
        kern,
        grid=(l // tm, n_col),
        in_specs=[pl.BlockSpec((tm, d), lambda i, j: (i, 0)),
                  pl.BlockSpec(mods.shape, lambda i, j: (0, 0)),
                  pl.BlockSpec((1, d), lambda i, j: (0, 0)),
                  pl.BlockSpec((None, d, COL_TILE),
                               lambda i, j: (layer, 0, (j + TILE_ROTATION) % n_col)),
                  pl.BlockSpec(lb_logits.shape, lambda i, j: (0, 0, 0)),
                  pl.BlockSpec((tm, LANE), lambda i, j: (i, 0)),
                  pl.BlockSpec((tm, LANE), lambda i, j: (i, 0))],
        out_specs=[pl.BlockSpec((SLABS_PER_TILE, tm, LANE), lambda i, j: (j, i, 0)),
                   pl.BlockSpec((SLABS_PER_TILE, tm, LANE),
                                lambda i, j: (jnp.clip(j - kf_tile, 0, 1), i, 0))],
        out_shape=[jax.ShapeDtypeStruct((N_SLABS, l, LANE), BF16),
                   jax.ShapeDtypeStruct((2 * HG_HEADS, l, LANE), F32)],
        scratch_shapes=[pltpu.VMEM((tm, d), BF16)],
        compiler_params=_params(("arbitrary", "arbitrary")),
        name="inproj",
    )(xc, mods, norm_w.reshape(1, d), w_in_bf, lb_logits, cos_t, sin_t)


def _hg_bwd_kernel(kb_ref, v_ref, bc_ref, sb_ref, s_ref, *, n_chunks):
    @pl.when(pl.program_id(0) == 0)
    def _():
        s_ref[...] = jnp.zeros_like(s_ref)

    def body(n, carry):
        cc = n_chunks - 1 - n
        rows = pl.ds(pl.multiple_of(cc * HG_CHUNK, HG_CHUNK), HG_CHUNK)
        for h in range(HG_HEADS):
            bb = bc_ref[h, rows, :]
            k = kb_ref[h, rows, :].astype(F32)
            s = s_ref[h]
            sb_ref[cc, h] = s.astype(BF16)
            kt = (k * jnp.exp2(bb[0:1, :] - bb)).astype(BF16)
            s_ref[h] = s * jnp.exp2(bb[0:1, :]) + lax.dot_general(
                v_ref[h, rows, :], kt, TN_DIMS, preferred_element_type=F32)
        return carry

    lax.fori_loop(0, n_chunks, body, 0)


def _bwd_block_order(i, n_ctx_blocks, n_blocks):
    return jnp.where(i < n_ctx_blocks, n_ctx_blocks - 1 - i, n_blocks - 1 - (i - n_ctx_blocks))


def _hg_bwd(p3, lf3, *, lc):
    l = p3.shape[1]
    tb = HG_BLOCK
    nb, nbc = l // tb, lc // tb
    ncb = tb // HG_CHUNK
    order = lambda i: _bwd_block_order(i, nbc, nb)
    kern = functools.partial(_hg_bwd_kernel, n_chunks=ncb)
    return pl.pallas_call(
        kern,
        grid=(nb,),
        in_specs=[pl.BlockSpec((HG_HEADS, tb, LANE), lambda i: (S_KB // HG_HEADS, order(i), 0)),
                  pl.BlockSpec((HG_HEADS, tb, LANE), lambda i: (S_HI // HG_HEADS, order(i), 0)),
                  pl.BlockSpec((HG_HEADS, tb, LANE), lambda i: (1, order(i), 0))],
        out_specs=pl.BlockSpec((ncb, HG_HEADS, HG_DV, HG_DK), lambda i: (order(i), 0, 0, 0)),
        out_shape=jax.ShapeDtypeStruct((l // HG_CHUNK, HG_HEADS, HG_DV, HG_DK), BF16),
        scratch_shapes=[pltpu.VMEM((HG_HEADS, HG_DV, HG_DK), F32)],
        compiler_params=_params(("arbitrary",)),
        name="hgrn_bwd_state",
    )(p3, p3, lf3)


def _hg_scores(q, q_edge, k32, b, fwd, sub, exact, k32_ref=None, b_ref=None):
    c = HG_CHUNK
    n_sub = c // sub
    cap = 0.0 if exact else HG_CAP
    lane = lax.broadcasted_iota(jnp.int32, (sub, c), 1)
    blocks = []
    for blk in range(n_sub):
        r = blk * sub
        q_blk = q[r:r + sub]
        b_blk = b[r:r + sub]
        edge = blk == 0 if fwd else blk == n_sub - 1
        if edge and exact:
            a = jnp.zeros((sub, c), F32)
        else:
            if edge:
                qt = q_edge[r:r + sub]
                ref_minus_b = -b
            else:
                ref = b[r - 1:r] if fwd else b[r + sub:r + sub + 1]
                qt = q_blk * jnp.exp2(b_blk - ref)
                ref_minus_b = ref - b
            kt = (k32 * jnp.exp2(jnp.minimum(ref_minus_b, cap))).astype(BF16)
            a = lax.dot_general(qt.astype(BF16), kt, NT_DIMS, preferred_element_type=F32)
        if exact:
            for jj in range(sub):
                s = r + jj
                e = jnp.exp2(b_blk - b_ref[s:s + 1, :])
                col = jnp.sum(q_blk * k32_ref[s:s + 1, :] * e, axis=1, keepdims=True)
                a = jnp.where(lane == s, col, a)
        blocks.append(a)
    a = jnp.concatenate(blocks, axis=0)
    ri = lax.broadcasted_iota(jnp.int32, (c, c), 0)
    ci = lax.broadcasted_iota(jnp.int32, (c, c), 1)
    return jnp.where((ci <= ri) if fwd else (ci >= ri), a, 0.0)


def _hg_head_scores(refs, cc, h, sub, exact):
    (q_ref, kf_ref, kb_ref, v_ref, _, bc_ref, sb_ref, _, _, s_ref, k32_ref, b_ref) = refs
    rows = pl.ds(pl.multiple_of(cc * HG_CHUNK, HG_CHUNK), HG_CHUNK)
    q = q_ref[h, rows, :].astype(F32)
    v = v_ref[h, rows, :]
    bf = bc_ref[h, rows, :]
    bb = bc_ref[HG_HEADS + h, rows, :]
    kf = kf_ref[h, rows, :].astype(F32)
    kb = kb_ref[h, rows, :].astype(F32)
    qf = q * jnp.exp2(bf)
    qb = q * jnp.exp2(bb)
    s = s_ref[h]
    inter = lax.dot_general(jnp.concatenate([qf, qb], axis=1).astype(BF16),
                            jnp.concatenate([s.astype(BF16), sb_ref[cc, h]], axis=1),
                            NT_DIMS, preferred_element_type=F32)
    if exact:
        k32_ref[...] = kf
        b_ref[...] = bf
    a = _hg_scores(q, qf, kf, bf, True, sub, exact, k32_ref, b_ref)
    if exact:
        k32_ref[...] = kb
        b_ref[...] = bb
    a = a + _hg_scores(q, qb, kb, bb, False, sub, exact, k32_ref, b_ref)
    b_last = bf[HG_CHUNK - 1:HG_CHUNK, :]
    kt = (kf * jnp.exp2(b_last - bf)).astype(BF16)
    s_ref[h] = s * jnp.exp2(b_last) + lax.dot_general(v, kt, TN_DIMS,
                                                      preferred_element_type=F32)
    return inter, a, v


def _hg_head_finish(refs, cc, h, inter, a, v):
    gate_ref, nw_ref, o_ref = refs[4], refs[7], refs[8]
    rows = pl.ds(pl.multiple_of(cc * HG_CHUNK, HG_CHUNK), HG_CHUNK)
    o = inter + jnp.dot(a.astype(BF16), v, preferred_element_type=F32)
    o = o * lax.rsqrt(jnp.mean(o * o, axis=-1, keepdims=True) + EPS)
    o = o * nw_ref[h] * gate_ref[h, rows, :].astype(F32)
    o_ref[h, rows, :] = o.astype(BF16)


def _hg_min_block_decay(bc_ref):
    n = bc_ref.shape[1] // HG_FAST_SUB
    first = lax.broadcasted_iota(jnp.int32, (n, LANE), 0) % 2 == 0
    worst = None
    for h in range(HG_HEADS):
        ends = bc_ref.at[h][pl.ds(HG_FAST_SUB - 1, n, stride=HG_FAST_SUB), :]
        fwd = jnp.where(first, ends, ends - pltpu.roll(ends, 1, axis=0))
        starts = bc_ref.at[HG_HEADS + h][pl.ds(0, n, stride=HG_FAST_SUB), :]
        bwd = jnp.where(first, starts - pltpu.roll(starts, n - 1, axis=0), starts)
        m = jnp.minimum(fwd, bwd)
        worst = m if worst is None else jnp.minimum(worst, m)
    return jnp.min(worst)


def _hg_fwd_kernel(*refs, n_chunks):
    bc_ref, s_ref = refs[5], refs[9]

    @pl.when(pl.program_id(0) == 0)
    def _():
        s_ref[...] = jnp.zeros_like(s_ref)

    factored_ok = _hg_min_block_decay(bc_ref) >= -HG_CAP

    @pl.when(factored_ok)
    def _():
        def body(cc, carry):
            staged = [_hg_head_scores(refs, cc, h, HG_FAST_SUB, False) for h in range(HG_HEADS)]
            for h, parts in enumerate(staged):
                _hg_head_finish(refs, cc, h, *parts)
            return carry
        lax.fori_loop(0, n_chunks, body, 0)

    @pl.when(jnp.logical_not(factored_ok))
    def _():
        def body(n, carry):
            cc, h = n // HG_HEADS, n % HG_HEADS
            _hg_head_finish(refs, cc, h, *_hg_head_scores(refs, cc, h, HG_SUB, True))
            return carry
        lax.fori_loop(0, n_chunks * HG_HEADS, body, 0)


def _hg_fwd(p3, lf3, sb, norm_w):
    l = p3.shape[1]
    tb = HG_BLOCK
    nb = l // tb
    ncb = tb // HG_CHUNK
    seg = lambda s: pl.BlockSpec((HG_HEADS, tb, LANE), lambda i: (s // HG_HEADS, i, 0))
    kern = functools.partial(_hg_fwd_kernel, n_chunks=ncb)
    return pl.pallas_call(
        kern,
        grid=(nb,),
        in_specs=[seg(S_HQ), seg(S_KF), seg(S_KB), seg(S_HI), seg(S_HGATE),
                  pl.BlockSpec((2 * HG_HEADS, tb, LANE), lambda i: (0, i, 0)),
                  pl.BlockSpec((ncb, HG_HEADS, HG_DV, HG_DK), lambda i: (i, 0, 0, 0)),
                  pl.BlockSpec((HG_HEADS, 1, HG_DV), lambda i: (0, 0, 0))],
        out_specs=pl.BlockSpec((HG_HEADS, tb, LANE), lambda i: (0, i, 0)),
        out_shape=jax.ShapeDtypeStruct((HG_HEADS, l, LANE), BF16),
        scratch_shapes=[pltpu.VMEM((HG_HEADS, HG_DV, HG_DK), F32),
                        pltpu.VMEM((HG_CHUNK, HG_DK), F32),
                        pltpu.VMEM((HG_CHUNK, HG_DK), F32)],
        compiler_params=_params(("arbitrary",)),
        name="hgrn_fwd",
    )(p3, p3, p3, p3, p3, lf3, sb, norm_w.reshape(HG_HEADS, 1, HG_DV))


def _log_sigmoid(x):
    return jnp.minimum(x, 0.0) - jnp.log1p(jnp.exp(-jnp.abs(x)))


def _ret_log_gamma(logit_ref, direction, h, shape):
    return _log_sigmoid(jnp.full(shape, logit_ref[direction, h], F32))


def _ret_bwd_kernel(logit_ref, k_ref, v_ref, sb_ref, s_ref):
    c = RET_BLOCK

    @pl.when(pl.program_id(0) == 0)
    def _():
        s_ref[...] = jnp.zeros_like(s_ref)

    t = lax.broadcasted_iota(jnp.int32, (c, RET_DK), 0).astype(F32)
    for h in range(RET_HEADS):
        lg = _ret_log_gamma(logit_ref, 1, h, (c, RET_DK))
        s = s_ref[h]
        sb_ref[0, h] = s.astype(BF16)
        kt = (k_ref[h].astype(F32) * jnp.exp(t * lg)).astype(BF16)
        v = jnp.concatenate([v_ref[2 * h], v_ref[2 * h + 1]], axis=1)
        s_ref[h] = s * jnp.exp(c * lg[0:1, 0:1]) + lax.dot_general(
            kt, v, TN_DIMS, preferred_element_type=F32)


def _ret_bwd(logit, p3, *, lc):
    l = p3.shape[1]
    c = RET_BLOCK
    nb, nbc = l // c, lc // c
    order = lambda i: _bwd_block_order(i, nbc, nb)
    return pl.pallas_call(
        _ret_bwd_kernel,
        grid=(nb,),
        in_specs=[pl.BlockSpec(memory_space=pltpu.SMEM),
                  pl.BlockSpec((RET_HEADS, c, LANE), lambda i: (S_RK // RET_HEADS, order(i), 0)),
                  pl.BlockSpec((2 * RET_HEADS, c, LANE),
                               lambda i: (S_RV // (2 * RET_HEADS), order(i), 0))],
        out_specs=pl.BlockSpec((1, RET_HEADS, RET_DK, RET_DV), lambda i: (order(i), 0, 0, 0)),
        out_shape=jax.ShapeDtypeStruct((nb, RET_HEADS, RET_DK, RET_DV), BF16),
        scratch_shapes=[pltpu.VMEM((RET_HEADS, RET_DK, RET_DV), F32)],
        compiler_params=_params(("arbitrary",)),
        name="ret_bwd_state",
    )(logit, p3, p3)


def _ret_fwd_kernel(logit_ref, q_ref, k_ref, v_ref, gate_ref, sb_ref, gnw_ref,
                    o_ref, s_ref, dmat_ref):
    c = RET_BLOCK

    @pl.when(pl.program_id(0) == 0)
    def _():
        s_ref[...] = jnp.zeros_like(s_ref)
        ri = lax.broadcasted_iota(jnp.int32, (c, c), 0)
        ci = lax.broadcasted_iota(jnp.int32, (c, c), 1)
        dist = (ri - ci).astype(F32)
        for h in range(RET_HEADS):
            lgf = _ret_log_gamma(logit_ref, 0, h, (c, c))
            lgb = _ret_log_gamma(logit_ref, 1, h, (c, c))
            dmat_ref[h] = (jnp.where(ci <= ri, jnp.exp(jnp.maximum(dist, 0.0) * lgf), 0.0)
                           + jnp.where(ci >= ri, jnp.exp(jnp.maximum(-dist, 0.0) * lgb), 0.0))

    t = lax.broadcasted_iota(jnp.int32, (c, RET_DK), 0).astype(F32)
    for h in range(RET_HEADS):
        lgf = _ret_log_gamma(logit_ref, 0, h, (c, RET_DK))
        lgb = _ret_log_gamma(logit_ref, 1, h, (c, RET_DK))
        q = q_ref[h]
        k = k_ref[h]
        q32 = q.astype(F32)
        v = jnp.concatenate([v_ref[2 * h], v_ref[2 * h + 1]], axis=1)
        s = s_ref[h]
        sc = lax.dot_general(q, k, NT_DIMS, preferred_element_type=F32) * dmat_ref[h]
        o = jnp.dot(sc.astype(BF16), v, preferred_element_type=F32)
        o = o + jnp.dot((q32 * jnp.exp((t + 1.0) * lgf)).astype(BF16), s.astype(BF16),
                        preferred_element_type=F32)
        o = o + jnp.dot((q32 * jnp.exp((c - t) * lgb)).astype(BF16), sb_ref[0, h],
                        preferred_element_type=F32)
        kt = (k.astype(F32) * jnp.exp((c - 1.0 - t) * lgf)).astype(BF16)
        s_ref[h] = s * jnp.exp(c * lgf[0:1, 0:1]) + lax.dot_general(
            kt, v, TN_DIMS, preferred_element_type=F32)

        mu = jnp.mean(o, axis=-1, keepdims=True)
        dev = o - mu
        var = jnp.mean(dev * dev, axis=-1, keepdims=True)
        o = dev * lax.rsqrt(var + EPS) * gnw_ref[h]
        gate = jnp.concatenate([gate_ref[2 * h], gate_ref[2 * h + 1]], axis=1).astype(F32)
        o_ref[:, h * RET_DV:(h + 1) * RET_DV] = (o * gate).astype(BF16)


def _ret_fwd(logit, p3, sb, gn_w):
    l = p3.shape[1]
    c = RET_BLOCK
    nb = l // c
    return pl.pallas_call(
        _ret_fwd_kernel,
        grid=(nb,),
        in_specs=[pl.BlockSpec(memory_space=pltpu.SMEM),
                  pl.BlockSpec((RET_HEADS, c, LANE), lambda i: (S_RQ // RET_HEADS, i, 0)),
                  pl.BlockSpec((RET_HEADS, c, LANE), lambda i: (S_RK // RET_HEADS, i, 0)),
                  pl.BlockSpec((2 * RET_HEADS, c, LANE), lambda i: (S_RV // (2 * RET_HEADS), i, 0)),
                  pl.BlockSpec((2 * RET_HEADS, c, LANE), lambda i: (S_RG // (2 * RET_HEADS), i, 0)),
                  pl.BlockSpec((1, RET_HEADS, RET_DK, RET_DV), lambda i: (i, 0, 0, 0)),
                  pl.BlockSpec((RET_HEADS, 1, RET_DV), lambda i: (0, 0, 0))],
        out_specs=pl.BlockSpec((c, RET_HEADS * RET_DV), lambda i: (i, 0)),
        out_shape=jax.ShapeDtypeStruct((l, RET_HEADS * RET_DV), BF16),
        scratch_shapes=[pltpu.VMEM((RET_HEADS, RET_DK, RET_DV), F32),
                        pltpu.VMEM((RET_HEADS, c, c), F32)],
        compiler_params=_params(("arbitrary",)),
        name="ret_fwd",
    )(logit, p3, p3, p3, p3, sb, gn_w.reshape(RET_HEADS, 1, RET_DV))


def _pack_bf16_pairs(v):
    w = v.shape[1] // 2
    lo = pltpu.bitcast(v[:, :w].astype(BF16).astype(F32), U32)
    hi = pltpu.bitcast(v[:, w:].astype(BF16).astype(F32), U32)
    return (lo >> 16) | (hi & jnp.uint32(0xFFFF0000))


def _unpack_bf16_pairs(u):
    lo = pltpu.bitcast(u << 16, F32)
    hi = pltpu.bitcast(u & jnp.uint32(0xFFFF0000), F32)
    return lo, hi


def _route(logits):
    lane = lax.broadcasted_iota(jnp.int32, logits.shape, 1)
    big = jnp.int32(10 ** 6)
    neg = -jnp.inf
    gl = jnp.where(lane < N_GROUPS, logits, neg)
    gmax = jnp.max(gl, axis=1, keepdims=True)
    grp = jnp.min(jnp.where(gl == gmax, lane, big), axis=1, keepdims=True)
    g_val = 1.0 / jnp.sum(jnp.exp(gl - gmax), axis=1, keepdims=True)
    lo = N_GROUPS + EXPERTS_PER_GROUP * grp
    el = jnp.where((lane >= lo) & (lane < lo + EXPERTS_PER_GROUP), logits, neg)
    v1 = jnp.max(el, axis=1, keepdims=True)
    i1 = jnp.min(jnp.where(el == v1, lane, big), axis=1, keepdims=True)
    el2 = jnp.where(lane == i1, neg, el)
    v2 = jnp.max(el2, axis=1, keepdims=True)
    i2 = jnp.min(jnp.where(el2 == v2, lane, big), axis=1, keepdims=True)
    r = jnp.exp(v2 - v1)
    w1 = g_val / (1.0 + r)
    w2 = w1 * r
    return ((i1 - N_GROUPS).astype(F32), (i2 - N_GROUPS).astype(F32), w1, w2)


def _merge_kernel(ohg_ref, oret_ref, ga_ref, gb_ref, x_ref, mods_ref, wohg_ref, woret_ref,
                  wout_ref, nw_ref, wrh_ref, wrl_ref, rb_ref,
                  xo_ref, hp_ref, route_ref, route_t_ref, cnt_out_ref, cnt_ref,
                  *, lc, tm, d, tiles_per_seg):
    i = pl.program_id(0)
    ohg = jnp.concatenate([ohg_ref[s] for s in range(HG_HEADS)], axis=1)
    ga = jnp.concatenate([ga_ref[s] for s in range(d // LANE)], axis=1).astype(F32)
    gb = jnp.concatenate([gb_ref[s] for s in range(d // LANE)], axis=1).astype(F32)
    y_hg = jnp.dot(ohg, wohg_ref[...], preferred_element_type=F32)
    y_ret = jnp.dot(oret_ref[...], woret_ref[...], preferred_element_type=F32)
    m = (ga * y_hg + gb * y_ret).astype(BF16)
    y = jnp.dot(m, wout_ref[...], preferred_element_type=F32)
    x = x_ref[...] + _row_select(i * tm, tm, lc, mods_ref, 2, d) * y
    xo_ref[...] = x

    xn = x * lax.rsqrt(jnp.mean(x * x, axis=-1, keepdims=True) + EPS) * nw_ref[...]
    h = (xn * (1.0 + _row_select(i * tm, tm, lc, mods_ref, 4, d))
         + _row_select(i * tm, tm, lc, mods_ref, 3, d))
    hp_ref[...] = _pack_bf16_pairs(h)

    h_hi = h.astype(BF16)
    h_lo = (h - h_hi.astype(F32)).astype(BF16)
    logits = (jnp.dot(h_hi, wrh_ref[...], preferred_element_type=F32)
              + jnp.dot(h_lo, wrh_ref[...], preferred_element_type=F32)
              + jnp.dot(h_hi, wrl_ref[...], preferred_element_type=F32)) + rb_ref[...]
    e0, e1, w0, w1 = _route(logits)

    @pl.when(i % tiles_per_seg == 0)
    def _():
        cnt_ref[...] = jnp.zeros_like(cnt_ref)

    lane_e = lax.broadcasted_iota(jnp.int32, (tm, LANE), 1).astype(F32)
    hot0 = lane_e == e0
    hot1 = lane_e == e1
    hot = jnp.where(hot0 | hot1, 1.0, 0.0)
    earlier = (lax.broadcasted_iota(jnp.int32, (tm, tm), 1)
               < lax.broadcasted_iota(jnp.int32, (tm, tm), 0))
    before = jnp.dot(jnp.where(earlier, 1.0, 0.0).astype(BF16), hot.astype(BF16),
                     preferred_element_type=F32) + cnt_ref[0:1, :]
    rank0 = jnp.sum(jnp.where(hot0, before, 0.0), axis=1, keepdims=True)
    rank1 = jnp.sum(jnp.where(hot1, before, 0.0), axis=1, keepdims=True)
    cnt_ref[...] = cnt_ref[...] + jnp.sum(hot, axis=0, keepdims=True)
    cnt_out_ref[0] = cnt_ref[...]

    lane = lax.broadcasted_iota(jnp.int32, (tm, LANE), 1)
    rec = jnp.zeros((tm, LANE), F32)
    for k, val in enumerate((e0, e1, w0, w1, rank0, rank1)):
        rec = jnp.where(lane == k, val, rec)
    route_ref[...] = rec[:, :ROUTE_W]
    eye = jnp.where(lax.broadcasted_iota(jnp.int32, (ROUTE_W, LANE), 0)
                    == lax.broadcasted_iota(jnp.int32, (ROUTE_W, LANE), 1), 1.0, 0.0)
    route_t_ref[0] = lax.dot_general(eye, rec, NT_DIMS, precision=lax.Precision.HIGHEST,
                                     preferred_element_type=F32)


def _merge(ohg, oret, p3, xc, mods, wohg, woret, wout, norm_w, wr_hi, wr_lo, rbias, *, lc, n_seg):
    l, d = xc.shape
    seg_tokens = l // n_seg
    tm = _largest_divisor(seg_tokens, 832, 16)
    tiles_per_seg = seg_tokens // tm
    n_slab = d // LANE
    full = lambda a: pl.BlockSpec(a.shape, lambda i: (0,) * a.ndim)
    kern = functools.partial(_merge_kernel, lc=lc, tm=tm, d=d, tiles_per_seg=tiles_per_seg)
    return pl.pallas_call(
        kern,
        grid=(l // tm,),
        in_specs=[pl.BlockSpec((HG_HEADS, tm, LANE), lambda i: (0, i, 0)),
                  pl.BlockSpec((tm, oret.shape[1]), lambda i: (i, 0)),
                  pl.BlockSpec((n_slab, tm, LANE), lambda i: (S_GA // n_slab, i, 0)),
                  pl.BlockSpec((n_slab, tm, LANE), lambda i: (S_GB // n_slab, i, 0)),
                  pl.BlockSpec((tm, d), lambda i: (i, 0)),
                  full(mods), full(wohg), full(woret), full(wout),
                  pl.BlockSpec((1, d), lambda i: (0, 0)),
                  full(wr_hi), full(wr_lo), full(rbias)],
        out_specs=[pl.BlockSpec((tm, d), lambda i: (i, 0)),
                   pl.BlockSpec((tm, d // 2), lambda i: (i, 0)),
                   pl.BlockSpec((tm, ROUTE_W), lambda i: (i, 0)),
                   pl.BlockSpec((1, ROUTE_W, tm), lambda i: (i, 0, 0)),
                   pl.BlockSpec((1, 8, LANE), lambda i: (i // tiles_per_seg, 0, 0))],
        out_shape=[jax.ShapeDtypeStruct((l, d), F32),
                   jax.ShapeDtypeStruct((l, d // 2), U32),
                   jax.ShapeDtypeStruct((l, ROUTE_W), F32),
                   jax.ShapeDtypeStruct((l // tm, ROUTE_W, tm), F32),
                   jax.ShapeDtypeStruct((n_seg, 8, LANE), F32)],
        scratch_shapes=[pltpu.VMEM((8, LANE), F32)],
        compiler_params=_params(("arbitrary",)),
        name="merge_router",
    )(ohg, oret, p3, p3, xc, mods, wohg, woret, wout, norm_w.reshape(1, d), wr_hi, wr_lo, rbias)


def _expert_kernel(be_ref, nused_ref, slot_ref, hp_ref, *rest, seg_tokens, n_tokens):
    w_refs, (y_ref, xs_ref) = rest[:3 * MOE_BLOCKS_PER_STEP], rest[3 * MOE_BLOCKS_PER_STEP:]
    s = pl.program_id(0)
    b = pl.program_id(1)

    @pl.when(b == 0)
    def _():
        xs_ref[...] = jnp.zeros_like(xs_ref)

        def scatter(t, carry):
            a = s * seg_tokens + t
            row = hp_ref[0, pl.ds(t, 1), :]
            xs_ref[pl.ds(slot_ref[a], 1), :] = row
            xs_ref[pl.ds(slot_ref[n_tokens + a], 1), :] = row
            return carry

        lax.fori_loop(0, seg_tokens, scatter, 0, unroll=4)

    first = b * MOE_BLOCKS_PER_STEP

    @pl.when(first < nused_ref[s])
    def _():
        for k in range(MOE_BLOCKS_PER_STEP):
            wg_ref, wu_ref, wd_ref = w_refs[3 * k:3 * k + 3]
            r0 = pl.multiple_of((first + k) * MOE_BLOCK, MOE_BLOCK)
            lo, hi = _unpack_bf16_pairs(xs_ref[pl.ds(r0, MOE_BLOCK), :])
            x = jnp.concatenate([lo, hi], axis=1).astype(BF16)
            g = jnp.dot(x, wg_ref[0], preferred_element_type=F32)
            u = jnp.dot(x, wu_ref[0], preferred_element_type=F32)
            a = (_silu(g) * u).astype(BF16)
            y_ref[0, k * MOE_BLOCK:(k + 1) * MOE_BLOCK, :] = _pack_bf16_pairs(
                jnp.dot(a, wd_ref[0], preferred_element_type=F32))

    @pl.when(first >= nused_ref[s])
    def _():
        y_ref[...] = jnp.zeros_like(y_ref)


def _experts(block_expert, n_used, slot, hp_seg, wg, wu, wd, *, nbs, layer):
    n_seg, seg_tokens, half = hp_seg.shape
    d = 2 * half
    d_e = wg.shape[3]
    per = MOE_BLOCKS_PER_STEP
    kern = functools.partial(_expert_kernel, seg_tokens=seg_tokens, n_tokens=n_seg * seg_tokens)
    w_specs = []
    for k in range(per):
        w_idx = lambda s, b, be, *_, k=k: (layer, be[s * nbs + b * per + k], 0, 0)
        w_specs += [pl.BlockSpec((None, 1, d, d_e), w_idx), pl.BlockSpec((None, 1, d, d_e), w_idx),
                    pl.BlockSpec((None, 1, d_e, d), w_idx)]
    grid_spec = pltpu.PrefetchScalarGridSpec(
        num_scalar_prefetch=3,
        grid=(n_seg, nbs // per),
        in_specs=[pl.BlockSpec((1, seg_tokens, half), lambda s, b, *_: (s, 0, 0))] + w_specs,
        out_specs=pl.BlockSpec((1, per * MOE_BLOCK, half), lambda s, b, *_: (s, b, 0)),
        scratch_shapes=[pltpu.VMEM((nbs * MOE_BLOCK, half), U32)])
    return pl.pallas_call(
        kern,
        grid_spec=grid_spec,
        out_shape=jax.ShapeDtypeStruct((n_seg, nbs * MOE_BLOCK, half), U32),
        compiler_params=_params(("arbitrary", "arbitrary")),
        name="experts",
    )(block_expert, n_used, slot, hp_seg, *((wg, wu, wd) * per))


def _combine_kernel(slot_ref, y_ref, route_ref, x_ref, mods_ref, fw_ref, o_ref,
                    g0_ref, g1_ref, *, lc, tm, d, tiles_per_seg, n_tokens, final):
    s = pl.program_id(0)
    t = pl.program_id(1)
    row0 = (s * tiles_per_seg + t) * tm

    def gather(r, carry):
        a = row0 + r
        g0_ref[pl.ds(r, 1), :] = y_ref[0, pl.ds(slot_ref[a], 1), :]
        g1_ref[pl.ds(r, 1), :] = y_ref[0, pl.ds(slot_ref[n_tokens + a], 1), :]
        return carry

    lax.fori_loop(0, tm, gather, 0, unroll=8)
    lo0, hi0 = _unpack_bf16_pairs(g0_ref[...])
    lo1, hi1 = _unpack_bf16_pairs(g1_ref[...])
    w0 = route_ref[:, 2:3]
    w1 = route_ref[:, 3:4]
    y = jnp.concatenate([w0 * lo0 + w1 * lo1, w0 * hi0 + w1 * hi1], axis=1)
    x = x_ref[...] + _row_select(row0, tm, lc, mods_ref, 5, d) * y
    if final:
        x = x * lax.rsqrt(jnp.mean(x * x, axis=-1, keepdims=True) + EPS) * fw_ref[...]
    o_ref[...] = x


def _combine(slot, ybuf, route, xc, mods, final_w, *, lc, final):
    l, d = xc.shape
    n_seg, rows, half = ybuf.shape
    seg_tokens = l // n_seg
    tm = _largest_divisor(seg_tokens, 320, 8)
    tiles = seg_tokens // tm
    kern = functools.partial(_combine_kernel, lc=lc, tm=tm, d=d, tiles_per_seg=tiles, n_tokens=l,
                             final=final)
    row = lambda s, t, *_: (s * tiles + t, 0)
    grid_spec = pltpu.PrefetchScalarGridSpec(
        num_scalar_prefetch=1,
        grid=(n_seg, tiles),
        in_specs=[pl.BlockSpec((1, rows, half), lambda s, t, *_: (s, 0, 0),
                               pipeline_mode=pl.Buffered(1)),
                  pl.BlockSpec((tm, ROUTE_W), row),
                  pl.BlockSpec((tm, d), row),
                  pl.BlockSpec(mods.shape, lambda s, t, *_: (0, 0)),
                  pl.BlockSpec((1, d), lambda s, t, *_: (0, 0))],
        out_specs=pl.BlockSpec((tm, d), row),
        scratch_shapes=[pltpu.VMEM((tm, half), U32), pltpu.VMEM((tm, half), U32)])
    return pl.pallas_call(
        kern,
        grid_spec=grid_spec,
        out_shape=jax.ShapeDtypeStruct((l, d), F32),
        compiler_params=_params(("arbitrary", "arbitrary")),
        name="combine",
    )(slot, ybuf, route, xc, mods, final_w.reshape(1, d))


def _dispatch_plan(route_t, counts, nbs):
    n_seg = counts.shape[0]
    counts = counts[:, 0, :N_EXPERTS].astype(jnp.int32)
    padded = ((counts + MOE_BLOCK - 1) // MOE_BLOCK) * MOE_BLOCK
    pad_end = jnp.cumsum(padded, axis=1)
    pad_start = pad_end - padded
    starts = jnp.arange(nbs, dtype=jnp.int32) * MOE_BLOCK
    block_expert = jnp.minimum(
        jnp.sum((pad_end[:, None, :] <= starts[None, :, None]).astype(jnp.int32), axis=2),
        N_EXPERTS - 1)
    n_used = pad_end[:, -1] // MOE_BLOCK
    n_tiles, _, tm = route_t.shape
    tile_start = jnp.repeat(pad_start, n_tiles // n_seg, axis=0)
    slots = []
    for k in range(2):
        e = route_t[:, k, :].astype(jnp.int32)
        rank = route_t[:, 4 + k, :].astype(jnp.int32)
        start = jnp.zeros_like(rank)
        for x in range(N_EXPERTS):
            start = jnp.where(e == x, tile_start[:, x:x + 1], start)
        slots.append((rank + start).reshape(-1))
    return (jnp.concatenate(slots), block_expert.reshape(-1).astype(jnp.int32),
            n_used.astype(jnp.int32))


def _rope_tables(lc, t):
    quarter = RET_DK // 4
    n_rows = t // GRID_W
    inv = ROPE_BASE ** (-np.arange(0, 2 * quarter, 2, dtype=np.float64) / (2 * quarter))
    ang_r = np.arange(n_rows, dtype=np.float64)[:, None] * inv
    ang_c = np.arange(GRID_W, dtype=np.float64)[:, None] * inv

    def expand(row_part, col_part):
        r = jnp.broadcast_to(jnp.asarray(row_part, F32)[:, None, :], (n_rows, GRID_W, 2 * quarter))
        c = jnp.broadcast_to(jnp.asarray(col_part, F32)[None, :, :], (n_rows, GRID_W, 2 * quarter))
        return jnp.concatenate([r, c], axis=2).reshape(t, RET_DK)

    two = lambda a: np.concatenate([a, a], axis=1)
    cos = expand(two(np.cos(ang_r)), two(np.cos(ang_c)))
    sin = expand(np.concatenate([-np.sin(ang_r), np.sin(ang_r)], axis=1),
                 np.concatenate([-np.sin(ang_c), np.sin(ang_c)], axis=1))
    cos = jnp.concatenate([jnp.ones((lc, RET_DK), F32), cos], axis=0)
    sin = jnp.concatenate([jnp.zeros((lc, RET_DK), F32), sin], axis=0)
    return cos, sin


def kernel(x, c, ctx, c_ctx, w_mod, b_mod, norm_mix_w, norm_ffn_w, w_in, hgrn_lb_logits, hgrn_norm_w,
           ret_decay_logit, ret_gn_w, w_o_hgrn, w_o_ret, w_out, router_group_w, router_group_b,
           router_expert_w, router_expert_b, expert_w_gate, expert_w_up, expert_w_down, final_norm_w):
    b_, t_, d = x.shape
    assert b_ == 1
    lc = ctx.shape[1]
    depth = w_mod.shape[0]
    l = lc + t_
    assert lc % HG_BLOCK == 0 and l % HG_BLOCK == 0 and lc % RET_BLOCK == 0 and l % RET_BLOCK == 0
    n_seg = MOE_SEGMENTS
    assert l % n_seg == 0
    seg_tokens = l // n_seg
    nbs = pl.cdiv(2 * seg_tokens + N_EXPERTS * (MOE_BLOCK - 1), MOE_BLOCK)
    nbs = pl.cdiv(nbs, MOE_BLOCKS_PER_STEP) * MOE_BLOCKS_PER_STEP

    xc = jnp.concatenate([ctx[0], x[0]], axis=0)
    cc = jnp.zeros((8, d), F32).at[0].set(c[0]).at[1].set(c_ctx)
    mods_all = _modulation(cc, w_mod, b_mod)
    cos_t, sin_t = _rope_tables(lc, t_)
    w_in_bf = w_in.astype(BF16)
    w_gate_bf, w_up_bf, w_down_bf = (w.astype(BF16)
                                     for w in (expert_w_gate, expert_w_up, expert_w_down))

    for layer in range(depth):
        mods = mods_all[layer]
        p3, lf3 = _inproj(xc, mods, norm_mix_w[layer], w_in_bf, hgrn_lb_logits,
                          cos_t, sin_t, layer=layer, lc=lc)
        sb_hg = _hg_bwd(p3, lf3, lc=lc)
        ohg = _hg_fwd(p3, lf3, sb_hg, hgrn_norm_w[layer])
        sb_ret = _ret_bwd(ret_decay_logit[layer], p3, lc=lc)
        oret = _ret_fwd(ret_decay_logit[layer], p3, sb_ret, ret_gn_w[layer])

        wr = jnp.concatenate([router_group_w[layer], router_expert_w[layer]], axis=1)
        wr = jnp.pad(wr, ((0, 0), (0, LANE - wr.shape[1])))
        wr_hi = wr.astype(BF16)
        wr_lo = (wr - wr_hi.astype(F32)).astype(BF16)
        rbias = jnp.pad(jnp.concatenate([router_group_b[layer], router_expert_b[layer]]),
                        (0, LANE - N_GROUPS - N_EXPERTS)).reshape(1, LANE)
        xc, hp, route, route_t, counts = _merge(
            ohg, oret, p3, xc, mods, w_o_hgrn[layer].astype(BF16), w_o_ret[layer].astype(BF16),
            w_out[layer].astype(BF16), norm_ffn_w[layer], wr_hi, wr_lo, rbias, lc=lc, n_seg=n_seg)

        slot, block_expert, n_used = _dispatch_plan(route_t, counts, nbs)
        ybuf = _experts(block_expert, n_used, slot, hp.reshape(n_seg, seg_tokens, d // 2),
                        w_gate_bf, w_up_bf, w_down_bf, nbs=nbs, layer=layer)
        xc = _combine(slot, ybuf, route, xc, mods, final_norm_w, lc=lc,
                      final=(layer == depth - 1))

    return xc[lc:][None]
```

```python
import functools

import jax
import jax.numpy as jnp
import numpy as np
from jax import lax
from jax.experimental import pallas as pl
from jax.experimental.pallas import tpu as pltpu

F32 = jnp.float32
BF16 = jnp.bfloat16
U32 = jnp.uint32

GRID_W = 64
HG_HEADS = 4
HG_DK = 128
HG_DV = 128
F_MIN = 1e-30
RET_HEADS = 4
RET_DK = 128
RET_DV = 256
ROPE_BASE = 10000.0
N_GROUPS = 4
EXPERTS_PER_GROUP = 8
N_EXPERTS = N_GROUPS * EXPERTS_PER_GROUP
N_MOD = 6
EPS = 1e-6

LANE = 128
COL_TILE = 512
SLABS_PER_TILE = COL_TILE // LANE
HG_CHUNK = 64
HG_SUB = 16
HG_BLOCK = 256
LOG2E = 1.4426950408889634
HG_FAST_SUB = 32
HG_CAP = 100.0
CUM_GROUP = 256
CUM_TERMS = 2
RET_BLOCK = 256
MOE_SEGMENTS = 5
MOE_BLOCK = 256
MOE_BLOCKS_PER_STEP = 1
MERGE_GROUPS = 1
ROUTE_W = 8
VMEM_LIMIT = 56 * 1024 * 1024

NT_DIMS = (((1,), (1,)), ((), ()))
TN_DIMS = (((0,), (0,)), ((), ()))

SEG_SLABS = (4, 4, 4, 4, 4, 4, 4, 8, 8, 8, 8)
SEG_ORDER = (7, 8, 9, 10, 0, 1, 2, 3, 4, 5, 6)
_starts = {}
_pos = 0
for _seg in SEG_ORDER:
    _starts[_seg] = _pos
    _pos += SEG_SLABS[_seg]
SEG_START = tuple(_starts[_seg] for _seg in range(len(SEG_SLABS)))
(S_HQ, S_KF, S_KB, S_HI, S_HGATE, S_RQ, S_RK, S_RV, S_RG, S_GA, S_GB) = SEG_START
assert SEG_ORDER == tuple(range(SEG_ORDER[0], len(SEG_SLABS))) + tuple(range(SEG_ORDER[0]))
TILE_ROTATION = sum(SEG_SLABS[:SEG_ORDER[0]]) // SLABS_PER_TILE
N_SLABS = sum(SEG_SLABS)


def _params(sem):
    return pltpu.CompilerParams(dimension_semantics=sem, vmem_limit_bytes=VMEM_LIMIT)


def _sigmoid(x):
    return 1.0 / (1.0 + jnp.exp(-x))


def _silu(x):
    return x * _sigmoid(x)


def _largest_divisor(n, cap, multiple):
    best = None
    for d in range(multiple, cap + 1, multiple):
        if n % d == 0:
            best = d
    assert best is not None, (n, cap, multiple)
    return best


def _row_select(row0, n_rows, lc, mods_ref, k, d):
    rows = row0 + lax.broadcasted_iota(jnp.int32, (n_rows, 1), 0)
    lat = mods_ref[0:1, k * d:(k + 1) * d]
    ctx = mods_ref[1:2, k * d:(k + 1) * d]
    return jnp.where(rows < lc, ctx, lat)


def _mod_kernel(cc_ref, w_ref, b_ref, o_ref):
    cc = cc_ref[...]
    s = _silu(cc)
    o_ref[0] = jnp.dot(s, w_ref[0], precision=lax.Precision.HIGHEST,
                       preferred_element_type=F32) + b_ref[0]


def _modulation(cc, w_mod, b_mod):
    depth, d, n = w_mod.shape
    tn = _largest_divisor(n, 1536, LANE)
    return pl.pallas_call(
        _mod_kernel,
        grid=(depth, n // tn),
        in_specs=[pl.BlockSpec((8, d), lambda l, j: (0, 0)),
                  pl.BlockSpec((1, d, tn), lambda l, j: (l, 0, j)),
                  pl.BlockSpec((1, 1, tn), lambda l, j: (l, 0, j))],
        out_specs=pl.BlockSpec((1, 8, tn), lambda l, j: (l, 0, j)),
        out_shape=jax.ShapeDtypeStruct((depth, 8, n), F32),
        compiler_params=_params(("arbitrary", "arbitrary")),
        name="modulation",
    )(cc, w_mod, b_mod.reshape(depth, 1, n))


def _inproj_kernel(x_ref, mods_ref, nw_ref, w_ref, lbl_ref, cos_ref, sin_ref,
                   p_ref, lf_ref, h_ref, *, layer, lc, tm, d):
    i = pl.program_id(0)
    j = pl.program_id(1)

    @pl.when(j == 0)
    def _():
        x = x_ref[...]
        xn = x * lax.rsqrt(jnp.mean(x * x, axis=-1, keepdims=True) + EPS) * nw_ref[...]
        shift = _row_select(i * tm, tm, lc, mods_ref, 0, d)
        scale = _row_select(i * tm, tm, lc, mods_ref, 1, d)
        h_ref[...] = (xn * (1.0 + scale) + shift).astype(BF16)

    def for_row_groups(epilogue):
        for r0 in range(0, tm, CUM_GROUP):
            rows = slice(r0, r0 + CUM_GROUP)
            acc = jnp.dot(h_ref[rows, :], w_ref[...], preferred_element_type=F32)
            epilogue(acc, rows)

    def put(val, rows):
        v = val.astype(BF16)
        for s in range(SLABS_PER_TILE):
            p_ref[s, rows, :] = v[:, s * LANE:(s + 1) * LANE]

    tiles = lambda seg: tuple(range(SEG_START[seg] // SLABS_PER_TILE,
                                    (SEG_START[seg] + SEG_SLABS[seg]) // SLABS_PER_TILE))
    in_tiles = lambda segs: functools.reduce(
        jnp.logical_or, [j == t for seg in segs for t in tiles(seg)])

    @pl.when(in_tiles((0,)))
    def _():
        for_row_groups(lambda acc, rows: put(_silu(acc) * (HG_DK ** -0.5), rows))

    @pl.when(in_tiles((1, 2)))
    def _():
        logits = lbl_ref[jnp.clip(j - tiles(1)[0], 0, 1)]
        e = jnp.exp(logits - jnp.max(logits, axis=0, keepdims=True))
        p = e / jnp.sum(e, axis=0, keepdims=True)
        lb = jnp.zeros_like(p[0:1])
        for r in range(1, layer + 1):
            lb = lb + p[r:r + 1]
        lb = jnp.clip(lb, 0.0, 1.0 - 1e-6)
        g = CUM_GROUP
        ri = lax.broadcasted_iota(jnp.int32, (g, g), 0)
        ci = lax.broadcasted_iota(jnp.int32, (g, g), 1)
        same_chunk = (ri // HG_CHUNK) == (ci // HG_CHUNK)
        before = jnp.where(same_chunk & (ci <= ri), 1.0, 0.0)
        after = jnp.where(same_chunk & (ci >= ri), 1.0, 0.0)
        tri = jnp.where(j == tiles(1)[0], before, after).astype(BF16)

        def epilogue(acc, rows):
            sig = _sigmoid(acc)
            put((1.0 - lb) * (1.0 - sig), rows)
            rest = jnp.log(jnp.maximum(lb + (1.0 - lb) * sig, F_MIN))
            cum = jnp.zeros((g, COL_TILE), F32)
            for _ in range(CUM_TERMS):
                term = rest.astype(BF16)
                cum = cum + jnp.dot(tri, term, preferred_element_type=F32)
                rest = rest - term.astype(F32)
            cum = cum * LOG2E
            for s in range(SLABS_PER_TILE):
                lf_ref[s, rows, :] = cum[:, s * LANE:(s + 1) * LANE]

        for_row_groups(epilogue)

    @pl.when(in_tiles((3, 7)))
    def _():
        for_row_groups(put)

    @pl.when(in_tiles((4, 8)))
    def _():
        for_row_groups(lambda acc, rows: put(_silu(acc), rows))

    @pl.when(in_tiles((5, 6)))
    def _():
        scale = jnp.where(j == tiles(5)[0], RET_DK ** -0.5, 1.0)

        def epilogue(acc, rows):
            xq = acc * scale
            n = xq.shape[1]
            lane = lax.broadcasted_iota(jnp.int32, xq.shape, 1)
            partner = jnp.where((lane & 32) == 0, pltpu.roll(xq, n - 32, axis=1),
                                pltpu.roll(xq, 32, axis=1))
            cos = jnp.concatenate([cos_ref[rows, :]] * SLABS_PER_TILE, axis=1)
            sin = jnp.concatenate([sin_ref[rows, :]] * SLABS_PER_TILE, axis=1)
            put(xq * cos + partner * sin, rows)

        for_row_groups(epilogue)

    @pl.when(in_tiles((9, 10)))
    def _():
        for_row_groups(lambda acc, rows: put(_sigmoid(acc), rows))


def _inproj(xc, mods, norm_w, w_in_bf, lb_logits, cos_t, sin_t, *, layer, lc):
    l, d = xc.shape
    d_in = w_in_bf.shape[2]
    assert d_in == N_SLABS * LANE
    tm = _largest_divisor(l, 1280, CUM_GROUP)
    n_col = d_in // COL_TILE
    kf_tile = S_KF // SLABS_PER_TILE
    kern = functools.partial(_inproj_kernel, layer=layer, lc=lc, tm=tm, d=d)
    return pl.pallas_call(
        kern,
        grid=(l // tm, n_col),
        in_specs=[pl.BlockSpec((tm, d), lambda i, j: (i, 0)),
                  pl.BlockSpec(mods.shape, lambda i, j: (0, 0)),
                  pl.BlockSpec((1, d), lambda i, j: (0, 0)),
                  pl.BlockSpec((None, d, COL_TILE),
                               lambda i, j: (layer, 0, (j + TILE_ROTATION) % n_col)),
                  pl.BlockSpec(lb_logits.shape, lambda i, j: (0, 0, 0)),
                  pl.BlockSpec((tm, LANE), lambda i, j: (i, 0)),
                  pl.BlockSpec((tm, LANE), lambda i, j: (i, 0))],
        out_specs=[pl.BlockSpec((SLABS_PER_TILE, tm, LANE), lambda i, j: (j, i, 0)),
                   pl.BlockSpec((SLABS_PER_TILE, tm, LANE),
                                lambda i, j: (jnp.clip(j - kf_tile, 0, 1), i, 0))],
        out_shape=[jax.ShapeDtypeStruct((N_SLABS, l, LANE), BF16),
                   jax.ShapeDtypeStruct((2 * HG_HEADS, l, LANE), F32)],
        scratch_shapes=[pltpu.VMEM((tm, d), BF16)],
        compiler_params=_params(("arbitrary", "arbitrary")),
        name="inproj",
    )(xc, mods, norm_w.reshape(1, d), w_in_bf, lb_logits, cos_t, sin_t)


def _hg_bwd_kernel(kb_ref, v_ref, bc_ref, sb_ref, s_ref, *, n_chunks):
    @pl.when(pl.program_id(0) == 0)
    def _():
        s_ref[...] = jnp.zeros_like(s_ref)

    def body(n, carry):
        cc = n_chunks - 1 - n
        rows = pl.ds(pl.multiple_of(cc * HG_CHUNK, HG_CHUNK), HG_CHUNK)
        for h in range(HG_HEADS):
            bb = bc_ref[h, rows, :]
            k = kb_ref[h, rows, :].astype(F32)
            s = s_ref[h]
            sb_ref[cc, h] = s.astype(BF16)
            kt = (k * jnp.exp2(bb[0:1, :] - bb)).astype(BF16)
            s_ref[h] = s * jnp.exp2(bb[0:1, :]) + lax.dot_general(
                v_ref[h, rows, :], kt, TN_DIMS, preferred_element_type=F32)
        return carry

    lax.fori_loop(0, n_chunks, body, 0)


def _bwd_block_order(i, n_ctx_blocks, n_blocks):
    return jnp.where(i < n_ctx_blocks, n_ctx_blocks - 1 - i, n_blocks - 1 - (i - n_ctx_blocks))


def _mixer_bwd_kernel(logit_ref, kb_ref, hv_ref, bc_ref, rk_ref, rv_ref,
                      hsb_ref, rsb_ref, hs_ref, rs_ref, *, n_chunks):
    _hg_bwd_kernel(kb_ref, hv_ref, bc_ref, hsb_ref, hs_ref, n_chunks=n_chunks)
    _ret_bwd_kernel(logit_ref, rk_ref, rv_ref, rsb_ref, rs_ref)


def _mixer_bwd(logit, p3, lf3, *, lc):
    l = p3.shape[1]
    tb = HG_BLOCK
    assert tb == RET_BLOCK
    nb, nbc = l // tb, lc // tb
    ncb = tb // HG_CHUNK
    order = lambda i: _bwd_block_order(i, nbc, nb)
    slabs = lambda n, start: pl.BlockSpec((n, tb, LANE), lambda i: (start // n, order(i), 0))
    kern = functools.partial(_mixer_bwd_kernel, n_chunks=ncb)
    return pl.pallas_call(
        kern,
        grid=(nb,),
        in_specs=[pl.BlockSpec(memory_space=pltpu.SMEM),
                  slabs(HG_HEADS, S_KB), slabs(HG_HEADS, S_HI),
                  pl.BlockSpec((HG_HEADS, tb, LANE), lambda i: (1, order(i), 0)),
                  slabs(RET_HEADS, S_RK), slabs(2 * RET_HEADS, S_RV)],
        out_specs=[pl.BlockSpec((ncb, HG_HEADS, HG_DV, HG_DK), lambda i: (order(i), 0, 0, 0)),
                   pl.BlockSpec((1, RET_HEADS, RET_DK, RET_DV), lambda i: (order(i), 0, 0, 0))],
        out_shape=[jax.ShapeDtypeStruct((l // HG_CHUNK, HG_HEADS, HG_DV, HG_DK), BF16),
                   jax.ShapeDtypeStruct((nb, RET_HEADS, RET_DK, RET_DV), BF16)],
        scratch_shapes=[pltpu.VMEM((HG_HEADS, HG_DV, HG_DK), F32),
                        pltpu.VMEM((RET_HEADS, RET_DK, RET_DV), F32)],
        compiler_params=_params(("arbitrary",)),
        name="mixer_bwd_state",
    )(logit, p3, p3, lf3, p3, p3)


def _hg_scores(q, q_edge, k32, b, fwd, sub, exact, k32_ref=None, b_ref=None):
    c = HG_CHUNK
    n_sub = c // sub
    cap = 0.0 if exact else HG_CAP
    lane = lax.broadcasted_iota(jnp.int32, (sub, c), 1)
    blocks = []
    for blk in range(n_sub):
        r = blk * sub
        q_blk = q[r:r + sub]
        b_blk = b[r:r + sub]
        edge = blk == 0 if fwd else blk == n_sub - 1
        if edge and exact:
            a = jnp.zeros((sub, c), F32)
        else:
            if edge:
                qt = q_edge[r:r + sub]
                ref_minus_b = -b
            else:
                ref = b[r - 1:r] if fwd else b[r + sub:r + sub + 1]
                qt = q_blk * jnp.exp2(b_blk - ref)
                ref_minus_b = ref - b
            kt = (k32 * jnp.exp2(jnp.minimum(ref_minus_b, cap))).astype(BF16)
            a = lax.dot_general(qt.astype(BF16), kt, NT_DIMS, preferred_element_type=F32)
        if exact:
            for jj in range(sub):
                s = r + jj
                e = jnp.exp2(b_blk - b_ref[s:s + 1, :])
                col = jnp.sum(q_blk * k32_ref[s:s + 1, :] * e, axis=1, keepdims=True)
                a = jnp.where(lane == s, col, a)
        blocks.append(a)
    a = jnp.concatenate(blocks, axis=0)
    ri = lax.broadcasted_iota(jnp.int32, (c, c), 0)
    ci = lax.broadcasted_iota(jnp.int32, (c, c), 1)
    return jnp.where((ci <= ri) if fwd else (ci >= ri), a, 0.0)


def _hg_head_scores(refs, cc, h, sub, exact):
    (q_ref, kf_ref, kb_ref, v_ref, _, bc_ref, sb_ref, _, _, s_ref, k32_ref, b_ref) = refs
    rows = pl.ds(pl.multiple_of(cc * HG_CHUNK, HG_CHUNK), HG_CHUNK)
    q = q_ref[h, rows, :].astype(F32)
    v = v_ref[h, rows, :]
    bf = bc_ref[h, rows, :]
    bb = bc_ref[HG_HEADS + h, rows, :]
    kf = kf_ref[h, rows, :].astype(F32)
    kb = kb_ref[h, rows, :].astype(F32)
    qf = q * jnp.exp2(bf)
    qb = q * jnp.exp2(bb)
    s = s_ref[h]
    inter = lax.dot_general(jnp.concatenate([qf, qb], axis=1).astype(BF16),
                            jnp.concatenate([s.astype(BF16), sb_ref[cc, h]], axis=1),
                            NT_DIMS, preferred_element_type=F32)
    if exact:
        k32_ref[...] = kf
        b_ref[...] = bf
    a = _hg_scores(q, qf, kf, bf, True, sub, exact, k32_ref, b_ref)
    if exact:
        k32_ref[...] = kb
        b_ref[...] = bb
    a = a + _hg_scores(q, qb, kb, bb, False, sub, exact, k32_ref, b_ref)
    b_last = bf[HG_CHUNK - 1:HG_CHUNK, :]
    kt = (kf * jnp.exp2(b_last - bf)).astype(BF16)
    s_ref[h] = s * jnp.exp2(b_last) + lax.dot_general(v, kt, TN_DIMS,
                                                      preferred_element_type=F32)
    return inter, a, v


def _hg_head_finish(refs, cc, h, inter, a, v):
    gate_ref, nw_ref, o_ref = refs[4], refs[7], refs[8]
    rows = pl.ds(pl.multiple_of(cc * HG_CHUNK, HG_CHUNK), HG_CHUNK)
    o = inter + jnp.dot(a.astype(BF16), v, preferred_element_type=F32)
    o = o * lax.rsqrt(jnp.mean(o * o, axis=-1, keepdims=True) + EPS)
    o = o * nw_ref[h] * gate_ref[h, rows, :].astype(F32)
    o_ref[h, rows, :] = o.astype(BF16)


def _hg_min_block_decay(bc_ref):
    n = bc_ref.shape[1] // HG_FAST_SUB
    first = lax.broadcasted_iota(jnp.int32, (n, LANE), 0) % 2 == 0
    worst = None
    for h in range(HG_HEADS):
        ends = bc_ref.at[h][pl.ds(HG_FAST_SUB - 1, n, stride=HG_FAST_SUB), :]
        fwd = jnp.where(first, ends, ends - pltpu.roll(ends, 1, axis=0))
        starts = bc_ref.at[HG_HEADS + h][pl.ds(0, n, stride=HG_FAST_SUB), :]
        bwd = jnp.where(first, starts - pltpu.roll(starts, n - 1, axis=0), starts)
        m = jnp.minimum(fwd, bwd)
        worst = m if worst is None else jnp.minimum(worst, m)
    return jnp.min(worst)


def _hg_fwd_kernel(*refs, n_chunks):
    bc_ref, s_ref = refs[5], refs[9]

    @pl.when(pl.program_id(0) == 0)
    def _():
        s_ref[...] = jnp.zeros_like(s_ref)

    factored_ok = _hg_min_block_decay(bc_ref) >= -HG_CAP

    @pl.when(factored_ok)
    def _():
        def body(cc, carry):
            staged = [_hg_head_scores(refs, cc, h, HG_FAST_SUB, False) for h in range(HG_HEADS)]
            for h, parts in enumerate(staged):
                _hg_head_finish(refs, cc, h, *parts)
            return carry
        lax.fori_loop(0, n_chunks, body, 0)

    @pl.when(jnp.logical_not(factored_ok))
    def _():
        def body(n, carry):
            cc, h = n // HG_HEADS, n % HG_HEADS
            _hg_head_finish(refs, cc, h, *_hg_head_scores(refs, cc, h, HG_SUB, True))
            return carry
        lax.fori_loop(0, n_chunks * HG_HEADS, body, 0)


def _log_sigmoid(x):
    return jnp.minimum(x, 0.0) - jnp.log1p(jnp.exp(-jnp.abs(x)))


def _ret_log_gamma(logit_ref, direction, h, shape):
    return _log_sigmoid(jnp.full(shape, logit_ref[direction, h], F32))


def _ret_bwd_kernel(logit_ref, k_ref, v_ref, sb_ref, s_ref):
    c = RET_BLOCK

    @pl.when(pl.program_id(0) == 0)
    def _():
        s_ref[...] = jnp.zeros_like(s_ref)

    t = lax.broadcasted_iota(jnp.int32, (c, RET_DK), 0).astype(F32)
    for h in range(RET_HEADS):
        lg = _ret_log_gamma(logit_ref, 1, h, (c, RET_DK))
        s = s_ref[h]
        sb_ref[0, h] = s.astype(BF16)
        kt = (k_ref[h].astype(F32) * jnp.exp(t * lg)).astype(BF16)
        v = jnp.concatenate([v_ref[2 * h], v_ref[2 * h + 1]], axis=1)
        s_ref[h] = s * jnp.exp(c * lg[0:1, 0:1]) + lax.dot_general(
            kt, v, TN_DIMS, preferred_element_type=F32)


def _ret_fwd_kernel(logit_ref, q_ref, k_ref, v_ref, gate_ref, sb_ref, gnw_ref,
                    o_ref, s_ref, dmat_ref):
    c = RET_BLOCK

    @pl.when(pl.program_id(0) == 0)
    def _():
        s_ref[...] = jnp.zeros_like(s_ref)
        ri = lax.broadcasted_iota(jnp.int32, (c, c), 0)
        ci = lax.broadcasted_iota(jnp.int32, (c, c), 1)
        dist = (ri - ci).astype(F32)
        for h in range(RET_HEADS):
            lgf = _ret_log_gamma(logit_ref, 0, h, (c, c))
            lgb = _ret_log_gamma(logit_ref, 1, h, (c, c))
            dmat_ref[h] = (jnp.where(ci <= ri, jnp.exp(jnp.maximum(dist, 0.0) * lgf), 0.0)
                           + jnp.where(ci >= ri, jnp.exp(jnp.maximum(-dist, 0.0) * lgb), 0.0))

    t = lax.broadcasted_iota(jnp.int32, (c, RET_DK), 0).astype(F32)
    for h in range(RET_HEADS):
        lgf = _ret_log_gamma(logit_ref, 0, h, (c, RET_DK))
        lgb = _ret_log_gamma(logit_ref, 1, h, (c, RET_DK))
        q = q_ref[h]
        k = k_ref[h]
        q32 = q.astype(F32)
        v = jnp.concatenate([v_ref[2 * h], v_ref[2 * h + 1]], axis=1)
        s = s_ref[h]
        sc = lax.dot_general(q, k, NT_DIMS, preferred_element_type=F32) * dmat_ref[h]
        o = jnp.dot(sc.astype(BF16), v, preferred_element_type=F32)
        o = o + jnp.dot((q32 * jnp.exp((t + 1.0) * lgf)).astype(BF16), s.astype(BF16),
                        preferred_element_type=F32)
        o = o + jnp.dot((q32 * jnp.exp((c - t) * lgb)).astype(BF16), sb_ref[0, h],
                        preferred_element_type=F32)
        kt = (k.astype(F32) * jnp.exp((c - 1.0 - t) * lgf)).astype(BF16)
        s_ref[h] = s * jnp.exp(c * lgf[0:1, 0:1]) + lax.dot_general(
            kt, v, TN_DIMS, preferred_element_type=F32)

        mu = jnp.mean(o, axis=-1, keepdims=True)
        dev = o - mu
        var = jnp.mean(dev * dev, axis=-1, keepdims=True)
        o = dev * lax.rsqrt(var + EPS) * gnw_ref[h]
        gate = jnp.concatenate([gate_ref[2 * h], gate_ref[2 * h + 1]], axis=1).astype(F32)
        o_ref[:, h * RET_DV:(h + 1) * RET_DV] = (o * gate).astype(BF16)


N_HG_FWD_IN = 8
N_RET_FWD_IN = 7


def _mixer_fwd_kernel(*refs, n_chunks):
    hg_in = refs[:N_HG_FWD_IN]
    ret_in = refs[N_HG_FWD_IN:N_HG_FWD_IN + N_RET_FWD_IN]
    hg_out, ret_out, hs_ref, k32_ref, b_ref, rs_ref, dmat_ref = refs[N_HG_FWD_IN + N_RET_FWD_IN:]
    _hg_fwd_kernel(*hg_in, hg_out, hs_ref, k32_ref, b_ref, n_chunks=n_chunks)
    _ret_fwd_kernel(*ret_in, ret_out, rs_ref, dmat_ref)


def _mixer_fwd(logit, p3, lf3, hsb, rsb, hg_norm_w, gn_w):
    l = p3.shape[1]
    tb = HG_BLOCK
    assert tb == RET_BLOCK
    nb = l // tb
    ncb = tb // HG_CHUNK
    slabs = lambda n, start: pl.BlockSpec((n, tb, LANE), lambda i: (start // n, i, 0))
    kern = functools.partial(_mixer_fwd_kernel, n_chunks=ncb)
    return pl.pallas_call(
        kern,
        grid=(nb,),
        in_specs=[slabs(HG_HEADS, S_HQ), slabs(HG_HEADS, S_KF), slabs(HG_HEADS, S_KB),
                  slabs(HG_HEADS, S_HI), slabs(HG_HEADS, S_HGATE),
                  pl.BlockSpec((2 * HG_HEADS, tb, LANE), lambda i: (0, i, 0)),
                  pl.BlockSpec((ncb, HG_HEADS, HG_DV, HG_DK), lambda i: (i, 0, 0, 0)),
                  pl.BlockSpec((HG_HEADS, 1, HG_DV), lambda i: (0, 0, 0)),
                  pl.BlockSpec(memory_space=pltpu.SMEM),
                  slabs(RET_HEADS, S_RQ), slabs(RET_HEADS, S_RK),
                  slabs(2 * RET_HEADS, S_RV), slabs(2 * RET_HEADS, S_RG),
                  pl.BlockSpec((1, RET_HEADS, RET_DK, RET_DV), lambda i: (i, 0, 0, 0)),
                  pl.BlockSpec((RET_HEADS, 1, RET_DV), lambda i: (0, 0, 0))],
        out_specs=[pl.BlockSpec((HG_HEADS, tb, LANE), lambda i: (0, i, 0)),
                   pl.BlockSpec((tb, RET_HEADS * RET_DV), lambda i: (i, 0))],
        out_shape=[jax.ShapeDtypeStruct((HG_HEADS, l, LANE), BF16),
                   jax.ShapeDtypeStruct((l, RET_HEADS * RET_DV), BF16)],
        scratch_shapes=[pltpu.VMEM((HG_HEADS, HG_DV, HG_DK), F32),
                        pltpu.VMEM((HG_CHUNK, HG_DK), F32),
                        pltpu.VMEM((HG_CHUNK, HG_DK), F32),
                        pltpu.VMEM((RET_HEADS, RET_DK, RET_DV), F32),
                        pltpu.VMEM((RET_HEADS, tb, tb), F32)],
        compiler_params=_params(("arbitrary",)),
        name="mixer_fwd",
    )(p3, p3, p3, p3, p3, lf3, hsb, hg_norm_w.reshape(HG_HEADS, 1, HG_DV),
      logit, p3, p3, p3, p3, rsb, gn_w.reshape(RET_HEADS, 1, RET_DV))


def _pack_bf16_pairs(v):
    w = v.shape[1] // 2
    lo = pltpu.bitcast(v[:, :w].astype(BF16).astype(F32), U32)
    hi = pltpu.bitcast(v[:, w:].astype(BF16).astype(F32), U32)
    return (lo >> 16) | (hi & jnp.uint32(0xFFFF0000))


def _unpack_bf16_pairs(u):
    lo = pltpu.bitcast(u << 16, F32)
    hi = pltpu.bitcast(u & jnp.uint32(0xFFFF0000), F32)
    return lo, hi


def _route(logits):
    lane = lax.broadcasted_iota(jnp.int32, logits.shape, 1)
    big = jnp.int32(10 ** 6)
    neg = -jnp.inf
    gl = jnp.where(lane < N_GROUPS, logits, neg)
    gmax = jnp.max(gl, axis=1, keepdims=True)
    grp = jnp.min(jnp.where(gl == gmax, lane, big), axis=1, keepdims=True)
    g_val = 1.0 / jnp.sum(jnp.exp(gl - gmax), axis=1, keepdims=True)
    lo = N_GROUPS + EXPERTS_PER_GROUP * grp
    el = jnp.where((lane >= lo) & (lane < lo + EXPERTS_PER_GROUP), logits, neg)
    v1 = jnp.max(el, axis=1, keepdims=True)
    i1 = jnp.min(jnp.where(el == v1, lane, big), axis=1, keepdims=True)
    el2 = jnp.where(lane == i1, neg, el)
    v2 = jnp.max(el2, axis=1, keepdims=True)
    i2 = jnp.min(jnp.where(el2 == v2, lane, big), axis=1, keepdims=True)
    r = jnp.exp(v2 - v1)
    w1 = g_val / (1.0 + r)
    w2 = w1 * r
    return ((i1 - N_GROUPS).astype(F32), (i2 - N_GROUPS).astype(F32), w1, w2)


def _merge_kernel(ohg_ref, oret_ref, ga_ref, gb_ref, x_ref, mods_ref, wohg_ref, woret_ref,
                  wout_ref, nw_ref, wrh_ref, wrl_ref, rb_ref,
                  xo_ref, hp_ref, route_ref, route_t_ref, cnt_out_ref, cnt_ref,
                  *, lc, tm, d, tiles_per_seg):
    i = pl.program_id(0)
    tg = tm // MERGE_GROUPS
    routed = []
    for g in range(MERGE_GROUPS):
        rows = slice(g * tg, (g + 1) * tg)
        row0 = i * tm + g * tg
        ohg = jnp.concatenate([ohg_ref[s, rows, :] for s in range(HG_HEADS)], axis=1)
        ga = jnp.concatenate([ga_ref[s, rows, :] for s in range(d // LANE)], axis=1).astype(F32)
        gb = jnp.concatenate([gb_ref[s, rows, :] for s in range(d // LANE)], axis=1).astype(F32)
        y_hg = jnp.dot(ohg, wohg_ref[...], preferred_element_type=F32)
        y_ret = jnp.dot(oret_ref[rows, :], woret_ref[...], preferred_element_type=F32)
        m = (ga * y_hg + gb * y_ret).astype(BF16)
        y = jnp.dot(m, wout_ref[...], preferred_element_type=F32)
        x = x_ref[rows, :] + _row_select(row0, tg, lc, mods_ref, 2, d) * y
        xo_ref[rows, :] = x

        xn = x * lax.rsqrt(jnp.mean(x * x, axis=-1, keepdims=True) + EPS) * nw_ref[...]
        h = (xn * (1.0 + _row_select(row0, tg, lc, mods_ref, 4, d))
             + _row_select(row0, tg, lc, mods_ref, 3, d))
        hp_ref[rows, :] = _pack_bf16_pairs(h)

        h_hi = h.astype(BF16)
        h_lo = (h - h_hi.astype(F32)).astype(BF16)
        logits = (jnp.dot(h_hi, wrh_ref[...], preferred_element_type=F32)
                  + jnp.dot(h_lo, wrh_ref[...], preferred_element_type=F32)
                  + jnp.dot(h_hi, wrl_ref[...], preferred_element_type=F32)) + rb_ref[...]
        routed.append(_route(logits))
    e0, e1, w0, w1 = (jnp.concatenate(parts, axis=0) for parts in zip(*routed))

    @pl.when(i % tiles_per_seg == 0)
    def _():
        cnt_ref[...] = jnp.zeros_like(cnt_ref)

    lane_e = lax.broadcasted_iota(jnp.int32, (tm, LANE), 1).astype(F32)
    hot0 = lane_e == e0
    hot1 = lane_e == e1
    hot = jnp.where(hot0 | hot1, 1.0, 0.0)
    earlier = (lax.broadcasted_iota(jnp.int32, (tm, tm), 1)
               < lax.broadcasted_iota(jnp.int32, (tm, tm), 0))
    before = jnp.dot(jnp.where(earlier, 1.0, 0.0).astype(BF16), hot.astype(BF16),
                     preferred_element_type=F32) + cnt_ref[0:1, :]
    rank0 = jnp.sum(jnp.where(hot0, before, 0.0), axis=1, keepdims=True)
    rank1 = jnp.sum(jnp.where(hot1, before, 0.0), axis=1, keepdims=True)
    cnt_ref[...] = cnt_ref[...] + jnp.sum(hot, axis=0, keepdims=True)
    cnt_out_ref[0] = cnt_ref[...]

    lane = lax.broadcasted_iota(jnp.int32, (tm, LANE), 1)
    rec = jnp.zeros((tm, LANE), F32)
    for k, val in enumerate((e0, e1, w0, w1, rank0, rank1)):
        rec = jnp.where(lane == k, val, rec)
    route_ref[...] = rec[:, :ROUTE_W]
    eye = jnp.where(lax.broadcasted_iota(jnp.int32, (ROUTE_W, LANE), 0)
                    == lax.broadcasted_iota(jnp.int32, (ROUTE_W, LANE), 1), 1.0, 0.0)
    route_t_ref[0] = lax.dot_general(eye, rec, NT_DIMS, precision=lax.Precision.HIGHEST,
                                     preferred_element_type=F32)


def _merge(ohg, oret, p3, xc, mods, wohg, woret, wout, norm_w, wr_hi, wr_lo, rbias, *, lc, n_seg):
    l, d = xc.shape
    seg_tokens = l // n_seg
    tm = _largest_divisor(seg_tokens, 832, 16 * MERGE_GROUPS)
    tiles_per_seg = seg_tokens // tm
    n_slab = d // LANE
    full = lambda a: pl.BlockSpec(a.shape, lambda i: (0,) * a.ndim)
    kern = functools.partial(_merge_kernel, lc=lc, tm=tm, d=d, tiles_per_seg=tiles_per_seg)
    return pl.pallas_call(
        kern,
        grid=(l // tm,),
        in_specs=[pl.BlockSpec((HG_HEADS, tm, LANE), lambda i: (0, i, 0)),
                  pl.BlockSpec((tm, oret.shape[1]), lambda i: (i, 0)),
                  pl.BlockSpec((n_slab, tm, LANE), lambda i: (S_GA // n_slab, i, 0)),
                  pl.BlockSpec((n_slab, tm, LANE), lambda i: (S_GB // n_slab, i, 0)),
                  pl.BlockSpec((tm, d), lambda i: (i, 0)),
                  full(mods), full(wohg), full(woret), full(wout),
                  pl.BlockSpec((1, d), lambda i: (0, 0)),
                  full(wr_hi), full(wr_lo), full(rbias)],
        out_specs=[pl.BlockSpec((tm, d), lambda i: (i, 0)),
                   pl.BlockSpec((tm, d // 2), lambda i: (i, 0)),
                   pl.BlockSpec((tm, ROUTE_W), lambda i: (i, 0)),
                   pl.BlockSpec((1, ROUTE_W, tm), lambda i: (i, 0, 0)),
                   pl.BlockSpec((1, 8, LANE), lambda i: (i // tiles_per_seg, 0, 0))],
        out_shape=[jax.ShapeDtypeStruct((l, d), F32),
                   jax.ShapeDtypeStruct((l, d // 2), U32),
                   jax.ShapeDtypeStruct((l, ROUTE_W), F32),
                   jax.ShapeDtypeStruct((l // tm, ROUTE_W, tm), F32),
                   jax.ShapeDtypeStruct((n_seg, 8, LANE), F32)],
        scratch_shapes=[pltpu.VMEM((8, LANE), F32)],
        compiler_params=_params(("arbitrary",)),
        name="merge_router",
    )(ohg, oret, p3, p3, xc, mods, wohg, woret, wout, norm_w.reshape(1, d), wr_hi, wr_lo, rbias)


def _expert_kernel(be_ref, nused_ref, slot_ref, hp_ref, *rest, seg_tokens, n_tokens):
    w_refs, (y_ref, xs_ref) = rest[:3 * MOE_BLOCKS_PER_STEP], rest[3 * MOE_BLOCKS_PER_STEP:]
    s = pl.program_id(0)
    b = pl.program_id(1)

    @pl.when(b == 0)
    def _():
        xs_ref[...] = jnp.zeros_like(xs_ref)

        def scatter(t, carry):
            a = s * seg_tokens + t
            row = hp_ref[0, pl.ds(t, 1), :]
            xs_ref[pl.ds(slot_ref[a], 1), :] = row
            xs_ref[pl.ds(slot_ref[n_tokens + a], 1), :] = row
            return carry

        lax.fori_loop(0, seg_tokens, scatter, 0, unroll=4)

    first = b * MOE_BLOCKS_PER_STEP

    @pl.when(first < nused_ref[s])
    def _():
        for k in range(MOE_BLOCKS_PER_STEP):
            wg_ref, wu_ref, wd_ref = w_refs[3 * k:3 * k + 3]
            r0 = pl.multiple_of((first + k) * MOE_BLOCK, MOE_BLOCK)
            lo, hi = _unpack_bf16_pairs(xs_ref[pl.ds(r0, MOE_BLOCK), :])
            x = jnp.concatenate([lo, hi], axis=1).astype(BF16)
            g = jnp.dot(x, wg_ref[0], preferred_element_type=F32)
            u = jnp.dot(x, wu_ref[0], preferred_element_type=F32)
            a = (_silu(g) * u).astype(BF16)
            y_ref[0, k * MOE_BLOCK:(k + 1) * MOE_BLOCK, :] = _pack_bf16_pairs(
                jnp.dot(a, wd_ref[0], preferred_element_type=F32))


def _experts(block_expert, n_used, slot, hp_seg, wg, wu, wd, *, nbs, layer):
    n_seg, seg_tokens, half = hp_seg.shape
    d = 2 * half
    d_e = wg.shape[3]
    per = MOE_BLOCKS_PER_STEP
    kern = functools.partial(_expert_kernel, seg_tokens=seg_tokens, n_tokens=n_seg * seg_tokens)
    last_step = lambda s, nu: jnp.maximum(nu[s] - 1, 0) // per
    w_specs = []
    for k in range(per):
        w_idx = lambda s, b, be, nu, *_, k=k: (
            layer, be[s * nbs + jnp.minimum(b, last_step(s, nu)) * per + k], 0, 0)
        w_specs += [pl.BlockSpec((None, 1, d, d_e), w_idx), pl.BlockSpec((None, 1, d, d_e), w_idx),
                    pl.BlockSpec((None, 1, d_e, d), w_idx)]
    grid_spec = pltpu.PrefetchScalarGridSpec(
        num_scalar_prefetch=3,
        grid=(n_seg, nbs // per),
        in_specs=[pl.BlockSpec((1, seg_tokens, half), lambda s, b, *_: (s, 0, 0))] + w_specs,
        out_specs=pl.BlockSpec((1, per * MOE_BLOCK, half),
                               lambda s, b, be, nu, *_: (s, jnp.minimum(b, last_step(s, nu)), 0)),
        scratch_shapes=[pltpu.VMEM((nbs * MOE_BLOCK, half), U32)])
    return pl.pallas_call(
        kern,
        grid_spec=grid_spec,
        out_shape=jax.ShapeDtypeStruct((n_seg, nbs * MOE_BLOCK, half), U32),
        compiler_params=_params(("arbitrary", "arbitrary")),
        name="experts",
    )(block_expert, n_used, slot, hp_seg, *((wg, wu, wd) * per))


def _combine_kernel(slot_ref, y_ref, route_ref, x_ref, mods_ref, fw_ref, o_ref,
                    g0_ref, g1_ref, *, lc, tm, d, tiles_per_seg, n_tokens, final):
    s = pl.program_id(0)
    t = pl.program_id(1)
    row0 = (s * tiles_per_seg + t) * tm

    def gather(r, carry):
        a = row0 + r
        g0_ref[pl.ds(r, 1), :] = y_ref[0, pl.ds(slot_ref[a], 1), :]
        g1_ref[pl.ds(r, 1), :] = y_ref[0, pl.ds(slot_ref[n_tokens + a], 1), :]
        return carry

    lax.fori_loop(0, tm, gather, 0, unroll=8)
    lo0, hi0 = _unpack_bf16_pairs(g0_ref[...])
    lo1, hi1 = _unpack_bf16_pairs(g1_ref[...])
    w0 = route_ref[:, 2:3]
    w1 = route_ref[:, 3:4]
    y = jnp.concatenate([w0 * lo0 + w1 * lo1, w0 * hi0 + w1 * hi1], axis=1)
    x = x_ref[...] + _row_select(row0, tm, lc, mods_ref, 5, d) * y
    if final:
        x = x * lax.rsqrt(jnp.mean(x * x, axis=-1, keepdims=True) + EPS) * fw_ref[...]
    o_ref[...] = x


def _combine(slot, ybuf, route, xc, mods, final_w, *, lc, final):
    l, d = xc.shape
    n_seg, rows, half = ybuf.shape
    seg_tokens = l // n_seg
    tm = _largest_divisor(seg_tokens, 320, 8)
    tiles = seg_tokens // tm
    kern = functools.partial(_combine_kernel, lc=lc, tm=tm, d=d, tiles_per_seg=tiles, n_tokens=l,
                             final=final)
    row = lambda s, t, *_: (s * tiles + t, 0)
    if final:
        assert lc % tm == 0
        out_rows = l - lc
        out_row = lambda s, t, *_: (jnp.maximum(s * tiles + t - lc // tm, 0), 0)
    else:
        out_rows, out_row = l, row
    grid_spec = pltpu.PrefetchScalarGridSpec(
        num_scalar_prefetch=1,
        grid=(n_seg, tiles),
        in_specs=[pl.BlockSpec((1, rows, half), lambda s, t, *_: (s, 0, 0),
                               pipeline_mode=pl.Buffered(1)),
                  pl.BlockSpec((tm, ROUTE_W), row),
                  pl.BlockSpec((tm, d), row),
                  pl.BlockSpec(mods.shape, lambda s, t, *_: (0, 0)),
                  pl.BlockSpec((1, d), lambda s, t, *_: (0, 0))],
        out_specs=pl.BlockSpec((tm, d), out_row),
        scratch_shapes=[pltpu.VMEM((tm, half), U32), pltpu.VMEM((tm, half), U32)])
    return pl.pallas_call(
        kern,
        grid_spec=grid_spec,
        out_shape=jax.ShapeDtypeStruct((out_rows, d), F32),
        compiler_params=_params(("arbitrary", "arbitrary")),
        name="combine",
    )(slot, ybuf, route, xc, mods, final_w.reshape(1, d))


def _dispatch_plan(route_t, counts, nbs):
    n_seg = counts.shape[0]
    counts = counts[:, 0, :N_EXPERTS].astype(jnp.int32)
    padded = ((counts + MOE_BLOCK - 1) // MOE_BLOCK) * MOE_BLOCK
    pad_end = jnp.cumsum(padded, axis=1)
    pad_start = pad_end - padded
    starts = jnp.arange(nbs, dtype=jnp.int32) * MOE_BLOCK
    block_expert = jnp.minimum(
        jnp.sum((pad_end[:, None, :] <= starts[None, :, None]).astype(jnp.int32), axis=2),
        N_EXPERTS - 1)
    n_used = pad_end[:, -1] // MOE_BLOCK
    n_tiles, _, tm = route_t.shape
    tile_start = jnp.repeat(pad_start, n_tiles // n_seg, axis=0)
    slots = []
    for k in range(2):
        e = route_t[:, k, :].astype(jnp.int32)
        rank = route_t[:, 4 + k, :].astype(jnp.int32)
        start = jnp.zeros_like(rank)
        for x in range(N_EXPERTS):
            start = jnp.where(e == x, tile_start[:, x:x + 1], start)
        slots.append((rank + start).reshape(-1))
    return (jnp.concatenate(slots), block_expert.reshape(-1).astype(jnp.int32),
            n_used.astype(jnp.int32))


def _rope_tables(lc, t):
    quarter = RET_DK // 4
    n_rows = t // GRID_W
    inv = ROPE_BASE ** (-np.arange(0, 2 * quarter, 2, dtype=np.float64) / (2 * quarter))
    ang_r = np.arange(n_rows, dtype=np.float64)[:, None] * inv
    ang_c = np.arange(GRID_W, dtype=np.float64)[:, None] * inv

    def expand(row_part, col_part):
        r = jnp.broadcast_to(jnp.asarray(row_part, F32)[:, None, :], (n_rows, GRID_W, 2 * quarter))
        c = jnp.broadcast_to(jnp.asarray(col_part, F32)[None, :, :], (n_rows, GRID_W, 2 * quarter))
        return jnp.concatenate([r, c], axis=2).reshape(t, RET_DK)

    two = lambda a: np.concatenate([a, a], axis=1)
    cos = expand(two(np.cos(ang_r)), two(np.cos(ang_c)))
    sin = expand(np.concatenate([-np.sin(ang_r), np.sin(ang_r)], axis=1),
                 np.concatenate([-np.sin(ang_c), np.sin(ang_c)], axis=1))
    cos = jnp.concatenate([jnp.ones((lc, RET_DK), F32), cos], axis=0)
    sin = jnp.concatenate([jnp.zeros((lc, RET_DK), F32), sin], axis=0)
    return cos, sin


def kernel(x, c, ctx, c_ctx, w_mod, b_mod, norm_mix_w, norm_ffn_w, w_in, hgrn_lb_logits, hgrn_norm_w,
           ret_decay_logit, ret_gn_w, w_o_hgrn, w_o_ret, w_out, router_group_w, router_group_b,
           router_expert_w, router_expert_b, expert_w_gate, expert_w_up, expert_w_down, final_norm_w):
    b_, t_, d = x.shape
    assert b_ == 1
    lc = ctx.shape[1]
    depth = w_mod.shape[0]
    l = lc + t_
    assert lc % HG_BLOCK == 0 and l % HG_BLOCK == 0 and lc % RET_BLOCK == 0 and l % RET_BLOCK == 0
    n_seg = MOE_SEGMENTS
    assert l % n_seg == 0
    seg_tokens = l // n_seg
    nbs = pl.cdiv(2 * seg_tokens + N_EXPERTS * (MOE_BLOCK - 1), MOE_BLOCK)
    nbs = pl.cdiv(nbs, MOE_BLOCKS_PER_STEP) * MOE_BLOCKS_PER_STEP

    xc = jnp.concatenate([ctx[0], x[0]], axis=0)
    cc = jnp.zeros((8, d), F32).at[0].set(c[0]).at[1].set(c_ctx)
    mods_all = _modulation(cc, w_mod, b_mod)
    cos_t, sin_t = _rope_tables(lc, t_)
    w_in_bf = w_in.astype(BF16)
    w_gate_bf, w_up_bf, w_down_bf = (w.astype(BF16)
                                     for w in (expert_w_gate, expert_w_up, expert_w_down))

    for layer in range(depth):
        mods = mods_all[layer]
        p3, lf3 = _inproj(xc, mods, norm_mix_w[layer], w_in_bf, hgrn_lb_logits,
                          cos_t, sin_t, layer=layer, lc=lc)
        sb_hg, sb_ret = _mixer_bwd(ret_decay_logit[layer], p3, lf3, lc=lc)
        ohg, oret = _mixer_fwd(ret_decay_logit[layer], p3, lf3, sb_hg, sb_ret,
                               hgrn_norm_w[layer], ret_gn_w[layer])

        wr = jnp.concatenate([router_group_w[layer], router_expert_w[layer]], axis=1)
        wr = jnp.pad(wr, ((0, 0), (0, LANE - wr.shape[1])))
        wr_hi = wr.astype(BF16)
        wr_lo = (wr - wr_hi.astype(F32)).astype(BF16)
        rbias = jnp.pad(jnp.concatenate([router_group_b[layer], router_expert_b[layer]]),
                        (0, LANE - N_GROUPS - N_EXPERTS)).reshape(1, LANE)
        xc, hp, route, route_t, counts = _merge(
            ohg, oret, p3, xc, mods, w_o_hgrn[layer].astype(BF16), w_o_ret[layer].astype(BF16),
            w_out[layer].astype(BF16), norm_ffn_w[layer], wr_hi, wr_lo, rbias, lc=lc, n_seg=n_seg)

        slot, block_expert, n_used = _dispatch_plan(route_t, counts, nbs)
        ybuf = _experts(block_expert, n_used, slot, hp.reshape(n_seg, seg_tokens, d // 2),
                        w_gate_bf, w_up_bf, w_down_bf, nbs=nbs, layer=layer)
        xc = _combine(slot, ybuf, route, xc, mods, final_norm_w, lc=lc,
                      final=(layer == depth - 1))

    return xc[None]
```

```python
import functools

import jax
import jax.numpy as jnp
import numpy as np
from jax import lax
from jax.experimental import pallas as pl
from jax.experimental.pallas import tpu as pltpu

F32 = jnp.float32
BF16 = jnp.bfloat16
U32 = jnp.uint32

GRID_W = 64
HG_HEADS = 4
HG_DK = 128
HG_DV = 128
F_MIN = 1e-30
RET_HEADS = 4
RET_DK = 128
RET_DV = 256
ROPE_BASE = 10000.0
N_GROUPS = 4
EXPERTS_PER_GROUP = 8
N_EXPERTS = N_GROUPS * EXPERTS_PER_GROUP
N_MOD = 6
EPS = 1e-6

LANE = 128
COL_TILE = 512
SLABS_PER_TILE = COL_TILE // LANE
HG_CHUNK = 64
HG_SUB = 16
HG_BLOCK = 256
LOG2E = 1.4426950408889634
HG_FAST_SUB = 32
HG_CAP = 100.0
CUM_GROUP = 256
CUM_TERMS = 2
RET_BLOCK = 256
MOE_SEGMENTS = 5
MOE_BLOCK = 256
MOE_BLOCKS_PER_STEP = 1
MERGE_GROUPS = 1
MERGE_TILE = 832
ROUTE_W = 8
VMEM_LIMIT = 56 * 1024 * 1024

NT_DIMS = (((1,), (1,)), ((), ()))
TN_DIMS = (((0,), (0,)), ((), ()))

SEG_SLABS = (4, 4, 4, 4, 4, 4, 4, 8, 8, 8, 8)
SEG_ORDER = (7, 8, 9, 10, 0, 1, 2, 3, 4, 5, 6)
_starts = {}
_pos = 0
for _seg in SEG_ORDER:
    _starts[_seg] = _pos
    _pos += SEG_SLABS[_seg]
SEG_START = tuple(_starts[_seg] for _seg in range(len(SEG_SLABS)))
(S_HQ, S_KF, S_KB, S_HI, S_HGATE, S_RQ, S_RK, S_RV, S_RG, S_GA, S_GB) = SEG_START
assert SEG_ORDER == tuple(range(SEG_ORDER[0], len(SEG_SLABS))) + tuple(range(SEG_ORDER[0]))
TILE_ROTATION = sum(SEG_SLABS[:SEG_ORDER[0]]) // SLABS_PER_TILE
N_SLABS = sum(SEG_SLABS)


def _params(sem):
    return pltpu.CompilerParams(dimension_semantics=sem, vmem_limit_bytes=VMEM_LIMIT)


def _sigmoid(x):
    return 1.0 / (1.0 + jnp.exp(-x))


def _silu(x):
    return x * _sigmoid(x)


def _largest_divisor(n, cap, multiple):
    best = None
    for d in range(multiple, cap + 1, multiple):
        if n % d == 0:
            best = d
    assert best is not None, (n, cap, multiple)
    return best


def _row_select(row0, n_rows, lc, mods_ref, k, d):
    rows = row0 + lax.broadcasted_iota(jnp.int32, (n_rows, 1), 0)
    lat = mods_ref[0:1, k * d:(k + 1) * d]
    ctx = mods_ref[1:2, k * d:(k + 1) * d]
    return jnp.where(rows < lc, ctx, lat)


def _mod_kernel(cc_ref, w_ref, b_ref, o_ref):
    cc = cc_ref[...]
    s = _silu(cc)
    o_ref[0] = jnp.dot(s, w_ref[0], precision=lax.Precision.HIGHEST,
                       preferred_element_type=F32) + b_ref[0]


def _modulation(cc, w_mod, b_mod):
    depth, d, n = w_mod.shape
    tn = _largest_divisor(n, 1536, LANE)
    return pl.pallas_call(
        _mod_kernel,
        grid=(depth, n // tn),
        in_specs=[pl.BlockSpec((8, d), lambda l, j: (0, 0)),
                  pl.BlockSpec((1, d, tn), lambda l, j: (l, 0, j)),
                  pl.BlockSpec((1, 1, tn), lambda l, j: (l, 0, j))],
        out_specs=pl.BlockSpec((1, 8, tn), lambda l, j: (l, 0, j)),
        out_shape=jax.ShapeDtypeStruct((depth, 8, n), F32),
        compiler_params=_params(("arbitrary", "arbitrary")),
        name="modulation",
    )(cc, w_mod, b_mod.reshape(depth, 1, n))


def _inproj_kernel(x_ref, mods_ref, nw_ref, w_ref, lbl_ref, cos_ref, sin_ref,
                   p_ref, lf_ref, h_ref, *, layer, lc, tm, d):
    i = pl.program_id(0)
    j = pl.program_id(1)

    @pl.when(j == 0)
    def _():
        x = x_ref[...]
        xn = x * lax.rsqrt(jnp.mean(x * x, axis=-1, keepdims=True) + EPS) * nw_ref[...]
        shift = _row_select(i * tm, tm, lc, mods_ref, 0, d)
        scale = _row_select(i * tm, tm, lc, mods_ref, 1, d)
        h_ref[...] = (xn * (1.0 + scale) + shift).astype(BF16)

    def for_row_groups(epilogue):
        for r0 in range(0, tm, CUM_GROUP):
            rows = slice(r0, r0 + CUM_GROUP)
            acc = jnp.dot(h_ref[rows, :], w_ref[...], preferred_element_type=F32)
            epilogue(acc, rows)

    def put(val, rows):
        v = val.astype(BF16)
        for s in range(SLABS_PER_TILE):
            p_ref[s, rows, :] = v[:, s * LANE:(s + 1) * LANE]

    tiles = lambda seg: tuple(range(SEG_START[seg] // SLABS_PER_TILE,
                                    (SEG_START[seg] + SEG_SLABS[seg]) // SLABS_PER_TILE))
    in_tiles = lambda segs: functools.reduce(
        jnp.logical_or, [j == t for seg in segs for t in tiles(seg)])

    @pl.when(in_tiles((0,)))
    def _():
        for_row_groups(lambda acc, rows: put(_silu(acc) * (HG_DK ** -0.5), rows))

    @pl.when(in_tiles((1, 2)))
    def _():
        logits = lbl_ref[jnp.clip(j - tiles(1)[0], 0, 1)]
        e = jnp.exp(logits - jnp.max(logits, axis=0, keepdims=True))
        p = e / jnp.sum(e, axis=0, keepdims=True)
        lb = jnp.zeros_like(p[0:1])
        for r in range(1, layer + 1):
            lb = lb + p[r:r + 1]
        lb = jnp.clip(lb, 0.0, 1.0 - 1e-6)
        g = CUM_GROUP
        ri = lax.broadcasted_iota(jnp.int32, (g, g), 0)
        ci = lax.broadcasted_iota(jnp.int32, (g, g), 1)
        same_chunk = (ri // HG_CHUNK) == (ci // HG_CHUNK)
        before = jnp.where(same_chunk & (ci <= ri), 1.0, 0.0)
        after = jnp.where(same_chunk & (ci >= ri), 1.0, 0.0)
        tri = jnp.where(j == tiles(1)[0], before, after).astype(BF16)

        def epilogue(acc, rows):
            sig = _sigmoid(acc)
            put((1.0 - lb) * (1.0 - sig), rows)
            rest = jnp.log(jnp.maximum(lb + (1.0 - lb) * sig, F_MIN))
            cum = jnp.zeros((g, COL_TILE), F32)
            for _ in range(CUM_TERMS):
                term = rest.astype(BF16)
                cum = cum + jnp.dot(tri, term, preferred_element_type=F32)
                rest = rest - term.astype(F32)
            cum = cum * LOG2E
            for s in range(SLABS_PER_TILE):
                lf_ref[s, rows, :] = cum[:, s * LANE:(s + 1) * LANE]

        for_row_groups(epilogue)

    @pl.when(in_tiles((3, 7)))
    def _():
        for_row_groups(put)

    @pl.when(in_tiles((4, 8)))
    def _():
        for_row_groups(lambda acc, rows: put(_silu(acc), rows))

    @pl.when(in_tiles((5, 6)))
    def _():
        scale = jnp.where(j == tiles(5)[0], RET_DK ** -0.5, 1.0)

        def epilogue(acc, rows):
            xq = acc * scale
            n = xq.shape[1]
            lane = lax.broadcasted_iota(jnp.int32, xq.shape, 1)
            partner = jnp.where((lane & 32) == 0, pltpu.roll(xq, n - 32, axis=1),
                                pltpu.roll(xq, 32, axis=1))
            cos = jnp.concatenate([cos_ref[rows, :]] * SLABS_PER_TILE, axis=1)
            sin = jnp.concatenate([sin_ref[rows, :]] * SLABS_PER_TILE, axis=1)
            put(xq * cos + partner * sin, rows)

        for_row_groups(epilogue)

    @pl.when(in_tiles((9, 10)))
    def _():
        for_row_groups(lambda acc, rows: put(_sigmoid(acc), rows))


def _inproj(xc, mods, norm_w, w_in_bf, lb_logits, cos_t, sin_t, *, layer, lc):
    l, d = xc.shape
    d_in = w_in_bf.shape[2]
    assert d_in == N_SLABS * LANE
    tm = _largest_divisor(l, 1280, CUM_GROUP)
    n_col = d_in // COL_TILE
    kf_tile = S_KF // SLABS_PER_TILE
    kern = functools.partial(_inproj_kernel, layer=layer, lc=lc, tm=tm, d=d)
    return pl.pallas_call(
        kern,
        grid=(l // tm, n_col),
        in_specs=[pl.BlockSpec((tm, d), lambda i, j: (i, 0)),
                  pl.BlockSpec(mods.shape, lambda i, j: (0, 0)),
                  pl.BlockSpec((1, d), lambda i, j: (0, 0)),
                  pl.BlockSpec((None, d, COL_TILE),
                               lambda i, j: (layer, 0, (j + TILE_ROTATION) % n_col)),
                  pl.BlockSpec(lb_logits.shape, lambda i, j: (0, 0, 0)),
                  pl.BlockSpec((tm, LANE), lambda i, j: (i, 0)),
                  pl.BlockSpec((tm, LANE), lambda i, j: (i, 0))],
        out_specs=[pl.BlockSpec((SLABS_PER_TILE, tm, LANE), lambda i, j: (j, i, 0)),
                   pl.BlockSpec((SLABS_PER_TILE, tm, LANE),
                                lambda i, j: (jnp.clip(j - kf_tile, 0, 1), i, 0))],
        out_shape=[jax.ShapeDtypeStruct((N_SLABS, l, LANE), BF16),
                   jax.ShapeDtypeStruct((2 * HG_HEADS, l, LANE), F32)],
        scratch_shapes=[pltpu.VMEM((tm, d), BF16)],
        compiler_params=_params(("arbitrary", "arbitrary")),
        name="inproj",
    )(xc, mods, norm_w.reshape(1, d), w_in_bf, lb_logits, cos_t, sin_t)


def _hg_bwd_kernel(kb_ref, v_ref, bc_ref, sb_ref, s_ref, *, n_chunks):
    @pl.when(pl.program_id(0) == 0)
    def _():
        s_ref[...] = jnp.zeros_like(s_ref)

    for h in range(HG_HEADS):
        s = s_ref[h]
        for cc in reversed(range(n_chunks)):
            rows = slice(cc * HG_CHUNK, (cc + 1) * HG_CHUNK)
            bb = bc_ref[h, rows, :]
            k = kb_ref[h, rows, :].astype(F32)
            sb_ref[cc, h] = s.astype(BF16)
            kt = (k * jnp.exp2(bb[0:1, :] - bb)).astype(BF16)
            s = s * jnp.exp2(bb[0:1, :]) + lax.dot_general(
                v_ref[h, rows, :], kt, TN_DIMS, preferred_element_type=F32)
        s_ref[h] = s


def _bwd_block_order(i, n_ctx_blocks, n_blocks):
    return jnp.where(i < n_ctx_blocks, n_ctx_blocks - 1 - i, n_blocks - 1 - (i - n_ctx_blocks))


def _mixer_bwd_kernel(logit_ref, kb_ref, hv_ref, bc_ref, rk_ref, rv_ref,
                      hsb_ref, rsb_ref, hs_ref, rs_ref, *, n_chunks):
    _hg_bwd_kernel(kb_ref, hv_ref, bc_ref, hsb_ref, hs_ref, n_chunks=n_chunks)
    _ret_bwd_kernel(logit_ref, rk_ref, rv_ref, rsb_ref, rs_ref)


def _mixer_bwd(logit, p3, lf3, *, lc):
    l = p3.shape[1]
    tb = HG_BLOCK
    assert tb == RET_BLOCK
    nb, nbc = l // tb, lc // tb
    ncb = tb // HG_CHUNK
    order = lambda i: _bwd_block_order(i, nbc, nb)
    slabs = lambda n, start: pl.BlockSpec((n, tb, LANE), lambda i: (start // n, order(i), 0))
    kern = functools.partial(_mixer_bwd_kernel, n_chunks=ncb)
    return pl.pallas_call(
        kern,
        grid=(nb,),
        in_specs=[pl.BlockSpec(memory_space=pltpu.SMEM),
                  slabs(HG_HEADS, S_KB), slabs(HG_HEADS, S_HI),
                  pl.BlockSpec((HG_HEADS, tb, LANE), lambda i: (1, order(i), 0)),
                  slabs(RET_HEADS, S_RK), slabs(2 * RET_HEADS, S_RV)],
        out_specs=[pl.BlockSpec((ncb, HG_HEADS, HG_DV, HG_DK), lambda i: (order(i), 0, 0, 0)),
                   pl.BlockSpec((1, RET_HEADS, RET_DK, RET_DV), lambda i: (order(i), 0, 0, 0))],
        out_shape=[jax.ShapeDtypeStruct((l // HG_CHUNK, HG_HEADS, HG_DV, HG_DK), BF16),
                   jax.ShapeDtypeStruct((nb, RET_HEADS, RET_DK, RET_DV), BF16)],
        scratch_shapes=[pltpu.VMEM((HG_HEADS, HG_DV, HG_DK), F32),
                        pltpu.VMEM((RET_HEADS, RET_DK, RET_DV), F32)],
        compiler_params=_params(("arbitrary",)),
        name="mixer_bwd_state",
    )(logit, p3, p3, lf3, p3, p3)


def _hg_scores(q, q_edge, k32, b, fwd, sub, exact, k32_ref=None, b_ref=None):
    c = HG_CHUNK
    n_sub = c // sub
    cap = 0.0 if exact else HG_CAP
    lane = lax.broadcasted_iota(jnp.int32, (sub, c), 1)
    blocks = []
    for blk in range(n_sub):
        r = blk * sub
        q_blk = q[r:r + sub]
        b_blk = b[r:r + sub]
        edge = blk == 0 if fwd else blk == n_sub - 1
        if edge and exact:
            a = jnp.zeros((sub, c), F32)
        else:
            if edge:
                qt = q_edge[r:r + sub]
                ref_minus_b = -b
            else:
                ref = b[r - 1:r] if fwd else b[r + sub:r + sub + 1]
                qt = q_blk * jnp.exp2(b_blk - ref)
                ref_minus_b = ref - b
            kt = (k32 * jnp.exp2(jnp.minimum(ref_minus_b, cap))).astype(BF16)
            a = lax.dot_general(qt.astype(BF16), kt, NT_DIMS, preferred_element_type=F32)
        if exact:
            for jj in range(sub):
                s = r + jj
                e = jnp.exp2(b_blk - b_ref[s:s + 1, :])
                col = jnp.sum(q_blk * k32_ref[s:s + 1, :] * e, axis=1, keepdims=True)
                a = jnp.where(lane == s, col, a)
        blocks.append(a)
    a = jnp.concatenate(blocks, axis=0)
    ri = lax.broadcasted_iota(jnp.int32, (c, c), 0)
    ci = lax.broadcasted_iota(jnp.int32, (c, c), 1)
    return jnp.where((ci <= ri) if fwd else (ci >= ri), a, 0.0)


def _chunk_rows(cc):
    if isinstance(cc, int):
        return slice(cc * HG_CHUNK, (cc + 1) * HG_CHUNK)
    return pl.ds(pl.multiple_of(cc * HG_CHUNK, HG_CHUNK), HG_CHUNK)


def _hg_head_scores(refs, cc, h, sub, exact):
    (q_ref, kf_ref, kb_ref, v_ref, _, bc_ref, sb_ref, _, _, s_ref, k32_ref, b_ref) = refs
    rows = _chunk_rows(cc)
    q = q_ref[h, rows, :].astype(F32)
    v = v_ref[h, rows, :]
    bf = bc_ref[h, rows, :]
    bb = bc_ref[HG_HEADS + h, rows, :]
    kf = kf_ref[h, rows, :].astype(F32)
    kb = kb_ref[h, rows, :].astype(F32)
    qf = q * jnp.exp2(bf)
    qb = q * jnp.exp2(bb)
    s = s_ref[h]
    inter = lax.dot_general(jnp.concatenate([qf, qb], axis=1).astype(BF16),
                            jnp.concatenate([s.astype(BF16), sb_ref[cc, h]], axis=1),
                            NT_DIMS, preferred_element_type=F32)
    if exact:
        k32_ref[...] = kf
        b_ref[...] = bf
    a = _hg_scores(q, qf, kf, bf, True, sub, exact, k32_ref, b_ref)
    if exact:
        k32_ref[...] = kb
        b_ref[...] = bb
    a = a + _hg_scores(q, qb, kb, bb, False, sub, exact, k32_ref, b_ref)
    b_last = bf[HG_CHUNK - 1:HG_CHUNK, :]
    kt = (kf * jnp.exp2(b_last - bf)).astype(BF16)
    s_ref[h] = s * jnp.exp2(b_last) + lax.dot_general(v, kt, TN_DIMS,
                                                      preferred_element_type=F32)
    return inter, a, v


def _hg_head_finish(refs, cc, h, inter, a, v):
    gate_ref, nw_ref, o_ref = refs[4], refs[7], refs[8]
    rows = _chunk_rows(cc)
    o = inter + jnp.dot(a.astype(BF16), v, preferred_element_type=F32)
    o = o * lax.rsqrt(jnp.mean(o * o, axis=-1, keepdims=True) + EPS)
    o = o * nw_ref[h] * gate_ref[h, rows, :].astype(F32)
    o_ref[h, rows, :] = o.astype(BF16)


def _hg_min_block_decay(bc_ref):
    n = bc_ref.shape[1] // HG_FAST_SUB
    first = lax.broadcasted_iota(jnp.int32, (n, LANE), 0) % 2 == 0
    worst = None
    for h in range(HG_HEADS):
        ends = bc_ref.at[h][pl.ds(HG_FAST_SUB - 1, n, stride=HG_FAST_SUB), :]
        fwd = jnp.where(first, ends, ends - pltpu.roll(ends, 1, axis=0))
        starts = bc_ref.at[HG_HEADS + h][pl.ds(0, n, stride=HG_FAST_SUB), :]
        bwd = jnp.where(first, starts - pltpu.roll(starts, n - 1, axis=0), starts)
        m = jnp.minimum(fwd, bwd)
        worst = m if worst is None else jnp.minimum(worst, m)
    return jnp.min(worst)


def _hg_fwd_kernel(*refs, n_chunks):
    bc_ref, s_ref = refs[5], refs[9]

    @pl.when(pl.program_id(0) == 0)
    def _():
        s_ref[...] = jnp.zeros_like(s_ref)

    factored_ok = _hg_min_block_decay(bc_ref) >= -HG_CAP

    @pl.when(factored_ok)
    def _():
        staged_prev = None
        for cc in range(n_chunks):
            staged = [_hg_head_scores(refs, cc, h, HG_FAST_SUB, False) for h in range(HG_HEADS)]
            if staged_prev is not None:
                for h, parts in enumerate(staged_prev):
                    _hg_head_finish(refs, cc - 1, h, *parts)
            staged_prev = staged
        for h, parts in enumerate(staged_prev):
            _hg_head_finish(refs, n_chunks - 1, h, *parts)

    @pl.when(jnp.logical_not(factored_ok))
    def _():
        def body(n, carry):
            cc, h = n // HG_HEADS, n % HG_HEADS
            _hg_head_finish(refs, cc, h, *_hg_head_scores(refs, cc, h, HG_SUB, True))
            return carry
        lax.fori_loop(0, n_chunks * HG_HEADS, body, 0)


def _log_sigmoid(x):
    return jnp.minimum(x, 0.0) - jnp.log1p(jnp.exp(-jnp.abs(x)))


def _ret_log_gamma(logit_ref, direction, h, shape):
    return _log_sigmoid(jnp.full(shape, logit_ref[direction, h], F32))


def _ret_bwd_kernel(logit_ref, k_ref, v_ref, sb_ref, s_ref):
    c = RET_BLOCK

    @pl.when(pl.program_id(0) == 0)
    def _():
        s_ref[...] = jnp.zeros_like(s_ref)

    t = lax.broadcasted_iota(jnp.int32, (c, RET_DK), 0).astype(F32)
    for h in range(RET_HEADS):
        lg = _ret_log_gamma(logit_ref, 1, h, (c, RET_DK))
        s = s_ref[h]
        sb_ref[0, h] = s.astype(BF16)
        kt = (k_ref[h].astype(F32) * jnp.exp(t * lg)).astype(BF16)
        v = jnp.concatenate([v_ref[2 * h], v_ref[2 * h + 1]], axis=1)
        s_ref[h] = s * jnp.exp(c * lg[0:1, 0:1]) + lax.dot_general(
            kt, v, TN_DIMS, preferred_element_type=F32)


def _ret_fwd_kernel(logit_ref, q_ref, k_ref, v_ref, gate_ref, sb_ref, gnw_ref,
                    o_ref, s_ref, dmat_ref):
    c = RET_BLOCK

    @pl.when(pl.program_id(0) == 0)
    def _():
        s_ref[...] = jnp.zeros_like(s_ref)
        ri = lax.broadcasted_iota(jnp.int32, (c, c), 0)
        ci = lax.broadcasted_iota(jnp.int32, (c, c), 1)
        dist = (ri - ci).astype(F32)
        for h in range(RET_HEADS):
            lgf = _ret_log_gamma(logit_ref, 0, h, (c, c))
            lgb = _ret_log_gamma(logit_ref, 1, h, (c, c))
            dmat_ref[h] = (jnp.where(ci <= ri, jnp.exp(jnp.maximum(dist, 0.0) * lgf), 0.0)
                           + jnp.where(ci >= ri, jnp.exp(jnp.maximum(-dist, 0.0) * lgb), 0.0))

    t = lax.broadcasted_iota(jnp.int32, (c, RET_DK), 0).astype(F32)
    for h in range(RET_HEADS):
        lgf = _ret_log_gamma(logit_ref, 0, h, (c, RET_DK))
        lgb = _ret_log_gamma(logit_ref, 1, h, (c, RET_DK))
        q = q_ref[h]
        k = k_ref[h]
        q32 = q.astype(F32)
        v = jnp.concatenate([v_ref[2 * h], v_ref[2 * h + 1]], axis=1)
        s = s_ref[h]
        sc = lax.dot_general(q, k, NT_DIMS, preferred_element_type=F32) * dmat_ref[h]
        o = jnp.dot(sc.astype(BF16), v, preferred_element_type=F32)
        o = o + jnp.dot((q32 * jnp.exp((t + 1.0) * lgf)).astype(BF16), s.astype(BF16),
                        preferred_element_type=F32)
        o = o + jnp.dot((q32 * jnp.exp((c - t) * lgb)).astype(BF16), sb_ref[0, h],
                        preferred_element_type=F32)
        kt = (k.astype(F32) * jnp.exp((c - 1.0 - t) * lgf)).astype(BF16)
        s_ref[h] = s * jnp.exp(c * lgf[0:1, 0:1]) + lax.dot_general(
            kt, v, TN_DIMS, preferred_element_type=F32)

        mu = jnp.mean(o, axis=-1, keepdims=True)
        dev = o - mu
        var = jnp.mean(dev * dev, axis=-1, keepdims=True)
        o = dev * lax.rsqrt(var + EPS) * gnw_ref[h]
        gate = jnp.concatenate([gate_ref[2 * h], gate_ref[2 * h + 1]], axis=1).astype(F32)
        o_ref[:, h * RET_DV:(h + 1) * RET_DV] = (o * gate).astype(BF16)


N_HG_FWD_IN = 8
N_RET_FWD_IN = 7


def _mixer_fwd_kernel(*refs, n_chunks):
    hg_in = refs[:N_HG_FWD_IN]
    ret_in = refs[N_HG_FWD_IN:N_HG_FWD_IN + N_RET_FWD_IN]
    hg_out, ret_out, hs_ref, k32_ref, b_ref, rs_ref, dmat_ref = refs[N_HG_FWD_IN + N_RET_FWD_IN:]
    _hg_fwd_kernel(*hg_in, hg_out, hs_ref, k32_ref, b_ref, n_chunks=n_chunks)
    _ret_fwd_kernel(*ret_in, ret_out, rs_ref, dmat_ref)


def _mixer_fwd(logit, p3, lf3, hsb, rsb, hg_norm_w, gn_w):
    l = p3.shape[1]
    tb = HG_BLOCK
    assert tb == RET_BLOCK
    nb = l // tb
    ncb = tb // HG_CHUNK
    slabs = lambda n, start: pl.BlockSpec((n, tb, LANE), lambda i: (start // n, i, 0))
    kern = functools.partial(_mixer_fwd_kernel, n_chunks=ncb)
    return pl.pallas_call(
        kern,
        grid=(nb,),
        in_specs=[slabs(HG_HEADS, S_HQ), slabs(HG_HEADS, S_KF), slabs(HG_HEADS, S_KB),
                  slabs(HG_HEADS, S_HI), slabs(HG_HEADS, S_HGATE),
                  pl.BlockSpec((2 * HG_HEADS, tb, LANE), lambda i: (0, i, 0)),
                  pl.BlockSpec((ncb, HG_HEADS, HG_DV, HG_DK), lambda i: (i, 0, 0, 0)),
                  pl.BlockSpec((HG_HEADS, 1, HG_DV), lambda i: (0, 0, 0)),
                  pl.BlockSpec(memory_space=pltpu.SMEM),
                  slabs(RET_HEADS, S_RQ), slabs(RET_HEADS, S_RK),
                  slabs(2 * RET_HEADS, S_RV), slabs(2 * RET_HEADS, S_RG),
                  pl.BlockSpec((1, RET_HEADS, RET_DK, RET_DV), lambda i: (i, 0, 0, 0)),
                  pl.BlockSpec((RET_HEADS, 1, RET_DV), lambda i: (0, 0, 0))],
        out_specs=[pl.BlockSpec((HG_HEADS, tb, LANE), lambda i: (0, i, 0)),
                   pl.BlockSpec((tb, RET_HEADS * RET_DV), lambda i: (i, 0))],
        out_shape=[jax.ShapeDtypeStruct((HG_HEADS, l, LANE), BF16),
                   jax.ShapeDtypeStruct((l, RET_HEADS * RET_DV), BF16)],
        scratch_shapes=[pltpu.VMEM((HG_HEADS, HG_DV, HG_DK), F32),
                        pltpu.VMEM((HG_CHUNK, HG_DK), F32),
                        pltpu.VMEM((HG_CHUNK, HG_DK), F32),
                        pltpu.VMEM((RET_HEADS, RET_DK, RET_DV), F32),
                        pltpu.VMEM((RET_HEADS, tb, tb), F32)],
        compiler_params=_params(("arbitrary",)),
        name="mixer_fwd",
    )(p3, p3, p3, p3, p3, lf3, hsb, hg_norm_w.reshape(HG_HEADS, 1, HG_DV),
      logit, p3, p3, p3, p3, rsb, gn_w.reshape(RET_HEADS, 1, RET_DV))


def _pack_bf16_pairs(v):
    w = v.shape[1] // 2
    lo = pltpu.bitcast(v[:, :w].astype(BF16).astype(F32), U32)
    hi = pltpu.bitcast(v[:, w:].astype(BF16).astype(F32), U32)
    return (lo >> 16) | (hi & jnp.uint32(0xFFFF0000))


def _unpack_bf16_pairs(u):
    lo = pltpu.bitcast(u << 16, F32)
    hi = pltpu.bitcast(u & jnp.uint32(0xFFFF0000), F32)
    return lo, hi


def _route(logits):
    lane = lax.broadcasted_iota(jnp.int32, logits.shape, 1)
    big = jnp.int32(10 ** 6)
    neg = -jnp.inf
    gl = jnp.where(lane < N_GROUPS, logits, neg)
    gmax = jnp.max(gl, axis=1, keepdims=True)
    grp = jnp.min(jnp.where(gl == gmax, lane, big), axis=1, keepdims=True)
    g_val = 1.0 / jnp.sum(jnp.exp(gl - gmax), axis=1, keepdims=True)
    lo = N_GROUPS + EXPERTS_PER_GROUP * grp
    el = jnp.where((lane >= lo) & (lane < lo + EXPERTS_PER_GROUP), logits, neg)
    v1 = jnp.max(el, axis=1, keepdims=True)
    i1 = jnp.min(jnp.where(el == v1, lane, big), axis=1, keepdims=True)
    el2 = jnp.where(lane == i1, neg, el)
    v2 = jnp.max(el2, axis=1, keepdims=True)
    i2 = jnp.min(jnp.where(el2 == v2, lane, big), axis=1, keepdims=True)
    r = jnp.exp(v2 - v1)
    w1 = g_val / (1.0 + r)
    w2 = w1 * r
    return ((i1 - N_GROUPS).astype(F32), (i2 - N_GROUPS).astype(F32), w1, w2)


def _merge_kernel(ohg_ref, oret_ref, ga_ref, gb_ref, x_ref, mods_ref, wohg_ref, woret_ref,
                  wout_ref, nw_ref, wrh_ref, wrl_ref, rb_ref,
                  xo_ref, hp_ref, route_ref, route_t_ref, cnt_out_ref, cnt_ref,
                  *, lc, tm, d, tiles_per_seg):
    i = pl.program_id(0)
    tg = tm // MERGE_GROUPS
    routed = []
    for g in range(MERGE_GROUPS):
        rows = slice(g * tg, (g + 1) * tg)
        row0 = i * tm + g * tg
        ohg = jnp.concatenate([ohg_ref[s, rows, :] for s in range(HG_HEADS)], axis=1)
        ga = jnp.concatenate([ga_ref[s, rows, :] for s in range(d // LANE)], axis=1).astype(F32)
        gb = jnp.concatenate([gb_ref[s, rows, :] for s in range(d // LANE)], axis=1).astype(F32)
        y_hg = jnp.dot(ohg, wohg_ref[...], preferred_element_type=F32)
        y_ret = jnp.dot(oret_ref[rows, :], woret_ref[...], preferred_element_type=F32)
        m = (ga * y_hg + gb * y_ret).astype(BF16)
        y = jnp.dot(m, wout_ref[...], preferred_element_type=F32)
        x = x_ref[rows, :] + _row_select(row0, tg, lc, mods_ref, 2, d) * y
        xo_ref[rows, :] = x

        xn = x * lax.rsqrt(jnp.mean(x * x, axis=-1, keepdims=True) + EPS) * nw_ref[...]
        h = (xn * (1.0 + _row_select(row0, tg, lc, mods_ref, 4, d))
             + _row_select(row0, tg, lc, mods_ref, 3, d))
        hp_ref[rows, :] = _pack_bf16_pairs(h)

        h_hi = h.astype(BF16)
        h_lo = (h - h_hi.astype(F32)).astype(BF16)
        logits = (jnp.dot(h_hi, wrh_ref[...], preferred_element_type=F32)
                  + jnp.dot(h_lo, wrh_ref[...], preferred_element_type=F32)
                  + jnp.dot(h_hi, wrl_ref[...], preferred_element_type=F32)) + rb_ref[...]
        routed.append(_route(logits))
    e0, e1, w0, w1 = (jnp.concatenate(parts, axis=0) for parts in zip(*routed))

    @pl.when(i % tiles_per_seg == 0)
    def _():
        cnt_ref[...] = jnp.zeros_like(cnt_ref)

    lane_e = lax.broadcasted_iota(jnp.int32, (tm, LANE), 1).astype(F32)
    hot0 = lane_e == e0
    hot1 = lane_e == e1
    hot = jnp.where(hot0 | hot1, 1.0, 0.0)
    earlier = (lax.broadcasted_iota(jnp.int32, (tm, tm), 1)
               < lax.broadcasted_iota(jnp.int32, (tm, tm), 0))
    before = jnp.dot(jnp.where(earlier, 1.0, 0.0).astype(BF16), hot.astype(BF16),
                     preferred_element_type=F32) + cnt_ref[0:1, :]
    rank0 = jnp.sum(jnp.where(hot0, before, 0.0), axis=1, keepdims=True)
    rank1 = jnp.sum(jnp.where(hot1, before, 0.0), axis=1, keepdims=True)
    cnt_ref[...] = cnt_ref[...] + jnp.sum(hot, axis=0, keepdims=True)
    cnt_out_ref[0] = cnt_ref[...]

    lane = lax.broadcasted_iota(jnp.int32, (tm, LANE), 1)
    rec = jnp.zeros((tm, LANE), F32)
    for k, val in enumerate((e0, e1, w0, w1, rank0, rank1)):
        rec = jnp.where(lane == k, val, rec)
    route_ref[...] = rec[:, :ROUTE_W]
    eye = jnp.where(lax.broadcasted_iota(jnp.int32, (ROUTE_W, LANE), 0)
                    == lax.broadcasted_iota(jnp.int32, (ROUTE_W, LANE), 1), 1.0, 0.0)
    route_t_ref[0] = lax.dot_general(eye, rec, NT_DIMS, precision=lax.Precision.HIGHEST,
                                     preferred_element_type=F32)


def _merge(ohg, oret, p3, xc, mods, wohg, woret, wout, norm_w, wr_hi, wr_lo, rbias, *, lc, n_seg):
    l, d = xc.shape
    seg_tokens = l // n_seg
    tm = _largest_divisor(seg_tokens, MERGE_TILE, 16 * MERGE_GROUPS)
    tiles_per_seg = seg_tokens // tm
    n_slab = d // LANE
    full = lambda a: pl.BlockSpec(a.shape, lambda i: (0,) * a.ndim)
    kern = functools.partial(_merge_kernel, lc=lc, tm=tm, d=d, tiles_per_seg=tiles_per_seg)
    return pl.pallas_call(
        kern,
        grid=(l // tm,),
        in_specs=[pl.BlockSpec((HG_HEADS, tm, LANE), lambda i: (0, i, 0)),
                  pl.BlockSpec((tm, oret.shape[1]), lambda i: (i, 0)),
                  pl.BlockSpec((n_slab, tm, LANE), lambda i: (S_GA // n_slab, i, 0)),
                  pl.BlockSpec((n_slab, tm, LANE), lambda i: (S_GB // n_slab, i, 0)),
                  pl.BlockSpec((tm, d), lambda i: (i, 0)),
                  full(mods), full(wohg), full(woret), full(wout),
                  pl.BlockSpec((1, d), lambda i: (0, 0)),
                  full(wr_hi), full(wr_lo), full(rbias)],
        out_specs=[pl.BlockSpec((tm, d), lambda i: (i, 0)),
                   pl.BlockSpec((tm, d // 2), lambda i: (i, 0)),
                   pl.BlockSpec((tm, ROUTE_W), lambda i: (i, 0)),
                   pl.BlockSpec((1, ROUTE_W, tm), lambda i: (i, 0, 0)),
                   pl.BlockSpec((1, 8, LANE), lambda i: (i // tiles_per_seg, 0, 0))],
        out_shape=[jax.ShapeDtypeStruct((l, d), F32),
                   jax.ShapeDtypeStruct((l, d // 2), U32),
                   jax.ShapeDtypeStruct((l, ROUTE_W), F32),
                   jax.ShapeDtypeStruct((l // tm, ROUTE_W, tm), F32),
                   jax.ShapeDtypeStruct((n_seg, 8, LANE), F32)],
        scratch_shapes=[pltpu.VMEM((8, LANE), F32)],
        compiler_params=_params(("arbitrary",)),
        name="merge_router",
    )(ohg, oret, p3, p3, xc, mods, wohg, woret, wout, norm_w.reshape(1, d), wr_hi, wr_lo, rbias)


def _expert_kernel(be_ref, nused_ref, slot_ref, hp_ref, *rest, seg_tokens, n_tokens):
    w_refs, (y_ref, xs_ref) = rest[:3 * MOE_BLOCKS_PER_STEP], rest[3 * MOE_BLOCKS_PER_STEP:]
    s = pl.program_id(0)
    b = pl.program_id(1)

    @pl.when(b == 0)
    def _():
        xs_ref[...] = jnp.zeros_like(xs_ref)

        def scatter(t, carry):
            a = s * seg_tokens + t
            row = hp_ref[0, pl.ds(t, 1), :]
            xs_ref[pl.ds(slot_ref[a], 1), :] = row
            xs_ref[pl.ds(slot_ref[n_tokens + a], 1), :] = row
            return carry

        lax.fori_loop(0, seg_tokens, scatter, 0, unroll=4)

    first = b * MOE_BLOCKS_PER_STEP

    @pl.when(first < nused_ref[s])
    def _():
        for k in range(MOE_BLOCKS_PER_STEP):
            wg_ref, wu_ref, wd_ref = w_refs[3 * k:3 * k + 3]
            r0 = pl.multiple_of((first + k) * MOE_BLOCK, MOE_BLOCK)
            lo, hi = _unpack_bf16_pairs(xs_ref[pl.ds(r0, MOE_BLOCK), :])
            x = jnp.concatenate([lo, hi], axis=1).astype(BF16)
            g = jnp.dot(x, wg_ref[0], preferred_element_type=F32)
            u = jnp.dot(x, wu_ref[0], preferred_element_type=F32)
            a = (_silu(g) * u).astype(BF16)
            y_ref[0, k * MOE_BLOCK:(k + 1) * MOE_BLOCK, :] = _pack_bf16_pairs(
                jnp.dot(a, wd_ref[0], preferred_element_type=F32))


def _experts(block_expert, n_used, slot, hp_seg, wg, wu, wd, *, nbs, layer):
    n_seg, seg_tokens, half = hp_seg.shape
    d = 2 * half
    d_e = wg.shape[3]
    per = MOE_BLOCKS_PER_STEP
    kern = functools.partial(_expert_kernel, seg_tokens=seg_tokens, n_tokens=n_seg * seg_tokens)
    last_step = lambda s, nu: jnp.maximum(nu[s] - 1, 0) // per
    w_specs = []
    for k in range(per):
        w_idx = lambda s, b, be, nu, *_, k=k: (
            layer, be[s * nbs + jnp.minimum(b, last_step(s, nu)) * per + k], 0, 0)
        w_specs += [pl.BlockSpec((None, 1, d, d_e), w_idx), pl.BlockSpec((None, 1, d, d_e), w_idx),
                    pl.BlockSpec((None, 1, d_e, d), w_idx)]
    grid_spec = pltpu.PrefetchScalarGridSpec(
        num_scalar_prefetch=3,
        grid=(n_seg, nbs // per),
        in_specs=[pl.BlockSpec((1, seg_tokens, half), lambda s, b, *_: (s, 0, 0))] + w_specs,
        out_specs=pl.BlockSpec((1, per * MOE_BLOCK, half),
                               lambda s, b, be, nu, *_: (s, jnp.minimum(b, last_step(s, nu)), 0)),
        scratch_shapes=[pltpu.VMEM((nbs * MOE_BLOCK, half), U32)])
    return pl.pallas_call(
        kern,
        grid_spec=grid_spec,
        out_shape=jax.ShapeDtypeStruct((n_seg, nbs * MOE_BLOCK, half), U32),
        compiler_params=_params(("arbitrary", "arbitrary")),
        name="experts",
    )(block_expert, n_used, slot, hp_seg, *((wg, wu, wd) * per))


def _combine_kernel(slot_ref, y_ref, route_ref, x_ref, mods_ref, fw_ref, o_ref,
                    g0_ref, g1_ref, *, lc, tm, d, tiles_per_seg, n_tokens, final):
    s = pl.program_id(0)
    t = pl.program_id(1)
    row0 = (s * tiles_per_seg + t) * tm

    def gather(r, carry):
        a = row0 + r
        g0_ref[pl.ds(r, 1), :] = y_ref[0, pl.ds(slot_ref[a], 1), :]
        g1_ref[pl.ds(r, 1), :] = y_ref[0, pl.ds(slot_ref[n_tokens + a], 1), :]
        return carry

    lax.fori_loop(0, tm, gather, 0, unroll=8)
    lo0, hi0 = _unpack_bf16_pairs(g0_ref[...])
    lo1, hi1 = _unpack_bf16_pairs(g1_ref[...])
    w0 = route_ref[:, 2:3]
    w1 = route_ref[:, 3:4]
    y = jnp.concatenate([w0 * lo0 + w1 * lo1, w0 * hi0 + w1 * hi1], axis=1)
    x = x_ref[...] + _row_select(row0, tm, lc, mods_ref, 5, d) * y
    if final:
        x = x * lax.rsqrt(jnp.mean(x * x, axis=-1, keepdims=True) + EPS) * fw_ref[...]
    o_ref[...] = x


def _combine(slot, ybuf, route, xc, mods, final_w, *, lc, final):
    l, d = xc.shape
    n_seg, rows, half = ybuf.shape
    seg_tokens = l // n_seg
    tm = _largest_divisor(seg_tokens, 320, 8)
    tiles = seg_tokens // tm
    kern = functools.partial(_combine_kernel, lc=lc, tm=tm, d=d, tiles_per_seg=tiles, n_tokens=l,
                             final=final)
    row = lambda s, t, *_: (s * tiles + t, 0)
    if final:
        assert lc % tm == 0
        out_rows = l - lc
        out_row = lambda s, t, *_: (jnp.maximum(s * tiles + t - lc // tm, 0), 0)
    else:
        out_rows, out_row = l, row
    grid_spec = pltpu.PrefetchScalarGridSpec(
        num_scalar_prefetch=1,
        grid=(n_seg, tiles),
        in_specs=[pl.BlockSpec((1, rows, half), lambda s, t, *_: (s, 0, 0),
                               pipeline_mode=pl.Buffered(1)),
                  pl.BlockSpec((tm, ROUTE_W), row),
                  pl.BlockSpec((tm, d), row),
                  pl.BlockSpec(mods.shape, lambda s, t, *_: (0, 0)),
                  pl.BlockSpec((1, d), lambda s, t, *_: (0, 0))],
        out_specs=pl.BlockSpec((tm, d), out_row),
        scratch_shapes=[pltpu.VMEM((tm, half), U32), pltpu.VMEM((tm, half), U32)])
    return pl.pallas_call(
        kern,
        grid_spec=grid_spec,
        out_shape=jax.ShapeDtypeStruct((out_rows, d), F32),
        compiler_params=_params(("arbitrary", "arbitrary")),
        name="combine",
    )(slot, ybuf, route, xc, mods, final_w.reshape(1, d))


def _dispatch_plan(route_t, counts, nbs):
    n_seg = counts.shape[0]
    counts = counts[:, 0, :N_EXPERTS].astype(jnp.int32)
    padded = ((counts + MOE_BLOCK - 1) // MOE_BLOCK) * MOE_BLOCK
    pad_end = jnp.cumsum(padded, axis=1)
    pad_start = pad_end - padded
    starts = jnp.arange(nbs, dtype=jnp.int32) * MOE_BLOCK
    block_expert = jnp.minimum(
        jnp.sum((pad_end[:, None, :] <= starts[None, :, None]).astype(jnp.int32), axis=2),
        N_EXPERTS - 1)
    n_used = pad_end[:, -1] // MOE_BLOCK
    n_tiles, _, tm = route_t.shape
    tile_start = jnp.repeat(pad_start, n_tiles // n_seg, axis=0)
    slots = []
    for k in range(2):
        e = route_t[:, k, :].astype(jnp.int32)
        rank = route_t[:, 4 + k, :].astype(jnp.int32)
        start = jnp.zeros_like(rank)
        for x in range(N_EXPERTS):
            start = jnp.where(e == x, tile_start[:, x:x + 1], start)
        slots.append((rank + start).reshape(-1))
    return (jnp.concatenate(slots), block_expert.reshape(-1).astype(jnp.int32),
            n_used.astype(jnp.int32))


def _rope_tables(lc, t):
    quarter = RET_DK // 4
    n_rows = t // GRID_W
    inv = ROPE_BASE ** (-np.arange(0, 2 * quarter, 2, dtype=np.float64) / (2 * quarter))
    ang_r = np.arange(n_rows, dtype=np.float64)[:, None] * inv
    ang_c = np.arange(GRID_W, dtype=np.float64)[:, None] * inv

    def expand(row_part, col_part):
        r = np.broadcast_to(row_part[:, None, :], (n_rows, GRID_W, 2 * quarter))
        c = np.broadcast_to(col_part[None, :, :], (n_rows, GRID_W, 2 * quarter))
        return np.concatenate([r, c], axis=2).reshape(t, RET_DK)

    two = lambda a: np.concatenate([a, a], axis=1)
    cos = expand(two(np.cos(ang_r)), two(np.cos(ang_c)))
    sin = expand(np.concatenate([-np.sin(ang_r), np.sin(ang_r)], axis=1),
                 np.concatenate([-np.sin(ang_c), np.sin(ang_c)], axis=1))
    cos = np.concatenate([np.ones((lc, RET_DK)), cos], axis=0).astype(np.float32)
    sin = np.concatenate([np.zeros((lc, RET_DK)), sin], axis=0).astype(np.float32)
    return jnp.asarray(cos), jnp.asarray(sin)


def kernel(x, c, ctx, c_ctx, w_mod, b_mod, norm_mix_w, norm_ffn_w, w_in, hgrn_lb_logits, hgrn_norm_w,
           ret_decay_logit, ret_gn_w, w_o_hgrn, w_o_ret, w_out, router_group_w, router_group_b,
           router_expert_w, router_expert_b, expert_w_gate, expert_w_up, expert_w_down, final_norm_w):
    b_, t_, d = x.shape
    assert b_ == 1
    lc = ctx.shape[1]
    depth = w_mod.shape[0]
    l = lc + t_
    assert lc % HG_BLOCK == 0 and l % HG_BLOCK == 0 and lc % RET_BLOCK == 0 and l % RET_BLOCK == 0
    n_seg = MOE_SEGMENTS
    assert l % n_seg == 0
    seg_tokens = l // n_seg
    nbs = pl.cdiv(2 * seg_tokens + N_EXPERTS * (MOE_BLOCK - 1), MOE_BLOCK)
    nbs = pl.cdiv(nbs, MOE_BLOCKS_PER_STEP) * MOE_BLOCKS_PER_STEP

    xc = jnp.concatenate([ctx[0], x[0]], axis=0)
    cc = jnp.zeros((8, d), F32).at[0].set(c[0]).at[1].set(c_ctx)
    mods_all = _modulation(cc, w_mod, b_mod)
    cos_t, sin_t = _rope_tables(lc, t_)
    w_in_bf = w_in.astype(BF16)
    w_gate_bf, w_up_bf, w_down_bf = (w.astype(BF16)
                                     for w in (expert_w_gate, expert_w_up, expert_w_down))

    for layer in range(depth):
        mods = mods_all[layer]
        p3, lf3 = _inproj(xc, mods, norm_mix_w[layer], w_in_bf, hgrn_lb_logits,
                          cos_t, sin_t, layer=layer, lc=lc)
        sb_hg, sb_ret = _mixer_bwd(ret_decay_logit[layer], p3, lf3, lc=lc)
        ohg, oret = _mixer_fwd(ret_decay_logit[layer], p3, lf3, sb_hg, sb_ret,
                               hgrn_norm_w[layer], ret_gn_w[layer])

        wr = jnp.concatenate([router_group_w[layer], router_expert_w[layer]], axis=1)
        wr = jnp.pad(wr, ((0, 0), (0, LANE - wr.shape[1])))
        wr_hi = wr.astype(BF16)
        wr_lo = (wr - wr_hi.astype(F32)).astype(BF16)
        rbias = jnp.pad(jnp.concatenate([router_group_b[layer], router_expert_b[layer]]),
                        (0, LANE - N_GROUPS - N_EXPERTS)).reshape(1, LANE)
        xc, hp, route, route_t, counts = _merge(
            ohg, oret, p3, xc, mods, w_o_hgrn[layer].astype(BF16), w_o_ret[layer].astype(BF16),
            w_out[layer].astype(BF16), norm_ffn_w[layer], wr_hi, wr_lo, rbias, lc=lc, n_seg=n_seg)

        slot, block_expert, n_used = _dispatch_plan(route_t, counts, nbs)
        ybuf = _experts(block_expert, n_used, slot, hp.reshape(n_seg, seg_tokens, d // 2),
                        w_gate_bf, w_up_bf, w_down_bf, nbs=nbs, layer=layer)
        xc = _combine(slot, ybuf, route, xc, mods, final_norm_w, lc=lc,
                      final=(layer == depth - 1))

    return xc[None]
```

```python
import functools

import jax
import jax.numpy as jnp
import numpy as np
from jax import lax
from jax.experimental import pallas as pl
from jax.experimental.pallas import tpu as pltpu

F32 = jnp.float32
BF16 = jnp.bfloat16
U32 = jnp.uint32

GRID_W = 64
HG_HEADS = 4
HG_DK = 128
HG_DV = 128
F_MIN = 1e-30
RET_HEADS = 4
RET_DK = 128
RET_DV = 256
ROPE_BASE = 10000.0
N_GROUPS = 4
EXPERTS_PER_GROUP = 8
N_EXPERTS = N_GROUPS * EXPERTS_PER_GROUP
N_MOD = 6
EPS = 1e-6

LANE = 128
COL_TILE = 512
SLABS_PER_TILE = COL_TILE // LANE
HG_CHUNK = 64
HG_SUB = 16
HG_BLOCK = 256
LOG2E = 1.4426950408889634
HG_FAST_SUB = 32
HG_CAP = 100.0
CUM_GROUP = 256
CUM_TERMS = 2
RET_BLOCK = 256
MOE_SEGMENTS = 5
MOE_BLOCK = 256
MERGE_GROUPS = 1
MERGE_TILE = 832
ROUTE_W = 8
VMEM_LIMIT = 56 * 1024 * 1024

NT_DIMS = (((1,), (1,)), ((), ()))
TN_DIMS = (((0,), (0,)), ((), ()))

SEG_SLABS = (4, 4, 4, 4, 4, 4, 4, 8, 8, 8, 8)
SEG_ORDER = (7, 8, 9, 10, 0, 1, 2, 3, 4, 5, 6)
_starts = {}
_pos = 0
for _seg in SEG_ORDER:
    _starts[_seg] = _pos
    _pos += SEG_SLABS[_seg]
SEG_START = tuple(_starts[_seg] for _seg in range(len(SEG_SLABS)))
(S_HQ, S_KF, S_KB, S_HI, S_HGATE, S_RQ, S_RK, S_RV, S_RG, S_GA, S_GB) = SEG_START
assert SEG_ORDER == tuple(range(SEG_ORDER[0], len(SEG_SLABS))) + tuple(range(SEG_ORDER[0]))
TILE_ROTATION = sum(SEG_SLABS[:SEG_ORDER[0]]) // SLABS_PER_TILE
N_SLABS = sum(SEG_SLABS)


def _params(sem):
    return pltpu.CompilerParams(dimension_semantics=sem, vmem_limit_bytes=VMEM_LIMIT)


def _sigmoid(x):
    return 1.0 / (1.0 + jnp.exp(-x))


def _silu(x):
    return x * _sigmoid(x)


def _largest_divisor(n, cap, multiple):
    best = None
    for d in range(multiple, cap + 1, multiple):
        if n % d == 0:
            best = d
    assert best is not None, (n, cap, multiple)
    return best


def _row_select(row0, n_rows, lc, mods_ref, k, d):
    rows = row0 + lax.broadcasted_iota(jnp.int32, (n_rows, 1), 0)
    lat = mods_ref[0:1, k * d:(k + 1) * d]
    ctx = mods_ref[1:2, k * d:(k + 1) * d]
    return jnp.where(rows < lc, ctx, lat)


def _mod_kernel(cc_ref, w_ref, b_ref, o_ref):
    cc = cc_ref[...]
    s = _silu(cc)
    o_ref[0] = jnp.dot(s, w_ref[0], precision=lax.Precision.HIGHEST,
                       preferred_element_type=F32) + b_ref[0]


def _modulation(cc, w_mod, b_mod):
    depth, d, n = w_mod.shape
    tn = _largest_divisor(n, 1536, LANE)
    return pl.pallas_call(
        _mod_kernel,
        grid=(depth, n // tn),
        in_specs=[pl.BlockSpec((8, d), lambda l, j: (0, 0)),
                  pl.BlockSpec((1, d, tn), lambda l, j: (l, 0, j)),
                  pl.BlockSpec((1, 1, tn), lambda l, j: (l, 0, j))],
        out_specs=pl.BlockSpec((1, 8, tn), lambda l, j: (l, 0, j)),
        out_shape=jax.ShapeDtypeStruct((depth, 8, n), F32),
        compiler_params=_params(("arbitrary", "arbitrary")),
        name="modulation",
    )(cc, w_mod, b_mod.reshape(depth, 1, n))


def _inproj_kernel(x_ref, mods_ref, nw_ref, w_ref, lbl_ref, cos_ref, sin_ref,
                   p_ref, lf_ref, h_ref, *, layer, lc, tm, d):
    i = pl.program_id(0)
    j = pl.program_id(1)

    @pl.when(j == 0)
    def _():
        x = x_ref[...]
        xn = x * lax.rsqrt(jnp.mean(x * x, axis=-1, keepdims=True) + EPS) * nw_ref[...]
        shift = _row_select(i * tm, tm, lc, mods_ref, 0, d)
        scale = _row_select(i * tm, tm, lc, mods_ref, 1, d)
        h_ref[...] = (xn * (1.0 + scale) + shift).astype(BF16)

    def for_row_groups(epilogue):
        for r0 in range(0, tm, CUM_GROUP):
            rows = slice(r0, r0 + CUM_GROUP)
            acc = jnp.dot(h_ref[rows, :], w_ref[...], preferred_element_type=F32)
            epilogue(acc, rows)

    def put(val, rows):
        v = val.astype(BF16)
        for s in range(SLABS_PER_TILE):
            p_ref[s, rows, :] = v[:, s * LANE:(s + 1) * LANE]

    tiles = lambda seg: tuple(range(SEG_START[seg] // SLABS_PER_TILE,
                                    (SEG_START[seg] + SEG_SLABS[seg]) // SLABS_PER_TILE))
    in_tiles = lambda segs: functools.reduce(
        jnp.logical_or, [j == t for seg in segs for t in tiles(seg)])

    @pl.when(in_tiles((0,)))
    def _():
        for_row_groups(lambda acc, rows: put(_silu(acc) * (HG_DK ** -0.5), rows))

    @pl.when(in_tiles((1, 2)))
    def _():
        logits = lbl_ref[jnp.clip(j - tiles(1)[0], 0, 1)]
        e = jnp.exp(logits - jnp.max(logits, axis=0, keepdims=True))
        p = e / jnp.sum(e, axis=0, keepdims=True)
        lb = jnp.zeros_like(p[0:1])
        for r in range(1, layer + 1):
            lb = lb + p[r:r + 1]
        lb = jnp.clip(lb, 0.0, 1.0 - 1e-6)
        g = CUM_GROUP
        ri = lax.broadcasted_iota(jnp.int32, (g, g), 0)
        ci = lax.broadcasted_iota(jnp.int32, (g, g), 1)
        same_chunk = (ri // HG_CHUNK) == (ci // HG_CHUNK)
        before = jnp.where(same_chunk & (ci <= ri), 1.0, 0.0)
        after = jnp.where(same_chunk & (ci >= ri), 1.0, 0.0)
        tri = jnp.where(j == tiles(1)[0], before, after).astype(BF16)

        def epilogue(acc, rows):
            sig = _sigmoid(acc)
            put((1.0 - lb) * (1.0 - sig), rows)
            rest = jnp.log(jnp.maximum(lb + (1.0 - lb) * sig, F_MIN))
            cum = jnp.zeros((g, COL_TILE), F32)
            for _ in range(CUM_TERMS):
                term = rest.astype(BF16)
                cum = cum + jnp.dot(tri, term, preferred_element_type=F32)
                rest = rest - term.astype(F32)
            cum = cum * LOG2E
            for s in range(SLABS_PER_TILE):
                lf_ref[s, rows, :] = cum[:, s * LANE:(s + 1) * LANE]

        for_row_groups(epilogue)

    @pl.when(in_tiles((3, 7)))
    def _():
        for_row_groups(put)

    @pl.when(in_tiles((4, 8)))
    def _():
        for_row_groups(lambda acc, rows: put(_silu(acc), rows))

    @pl.when(in_tiles((5, 6)))
    def _():
        scale = jnp.where(j == tiles(5)[0], RET_DK ** -0.5, 1.0)

        def epilogue(acc, rows):
            xq = acc * scale
            n = xq.shape[1]
            lane = lax.broadcasted_iota(jnp.int32, xq.shape, 1)
            partner = jnp.where((lane & 32) == 0, pltpu.roll(xq, n - 32, axis=1),
                                pltpu.roll(xq, 32, axis=1))
            cos = jnp.concatenate([cos_ref[rows, :]] * SLABS_PER_TILE, axis=1)
            sin = jnp.concatenate([sin_ref[rows, :]] * SLABS_PER_TILE, axis=1)
            put(xq * cos + partner * sin, rows)

        for_row_groups(epilogue)

    @pl.when(in_tiles((9, 10)))
    def _():
        for_row_groups(lambda acc, rows: put(_sigmoid(acc), rows))


def _inproj(xc, mods, norm_w, w_in_bf, lb_logits, cos_t, sin_t, *, layer, lc):
    l, d = xc.shape
    d_in = w_in_bf.shape[2]
    assert d_in == N_SLABS * LANE
    tm = _largest_divisor(l, 1280, CUM_GROUP)
    n_col = d_in // COL_TILE
    kf_tile = S_KF // SLABS_PER_TILE
    kern = functools.partial(_inproj_kernel, layer=layer, lc=lc, tm=tm, d=d)
    return pl.pallas_call(
        kern,
        grid=(l // tm, n_col),
        in_specs=[pl.BlockSpec((tm, d), lambda i, j: (i, 0)),
                  pl.BlockSpec(mods.shape, lambda i, j: (0, 0)),
                  pl.BlockSpec((1, d), lambda i, j: (0, 0)),
                  pl.BlockSpec((None, d, COL_TILE),
                               lambda i, j: (layer, 0, (j + TILE_ROTATION) % n_col)),
                  pl.BlockSpec(lb_logits.shape, lambda i, j: (0, 0, 0)),
                  pl.BlockSpec((tm, LANE), lambda i, j: (i, 0)),
                  pl.BlockSpec((tm, LANE), lambda i, j: (i, 0))],
        out_specs=[pl.BlockSpec((SLABS_PER_TILE, tm, LANE), lambda i, j: (j, i, 0)),
                   pl.BlockSpec((SLABS_PER_TILE, tm, LANE),
                                lambda i, j: (jnp.clip(j - kf_tile, 0, 1), i, 0))],
        out_shape=[jax.ShapeDtypeStruct((N_SLABS, l, LANE), BF16),
                   jax.ShapeDtypeStruct((2 * HG_HEADS, l, LANE), F32)],
        scratch_shapes=[pltpu.VMEM((tm, d), BF16)],
        compiler_params=_params(("arbitrary", "arbitrary")),
        name="inproj",
    )(xc, mods, norm_w.reshape(1, d), w_in_bf, lb_logits, cos_t, sin_t)


def _hg_bwd_kernel(kb_ref, v_ref, bc_ref, sb_ref, s_ref, *, n_chunks):
    @pl.when(pl.program_id(0) == 0)
    def _():
        s_ref[...] = jnp.zeros_like(s_ref)

    for h in range(HG_HEADS):
        s = s_ref[h]
        for cc in reversed(range(n_chunks)):
            rows = slice(cc * HG_CHUNK, (cc + 1) * HG_CHUNK)
            bb = bc_ref[h, rows, :]
            k = kb_ref[h, rows, :].astype(F32)
            sb_ref[cc, h] = s.astype(BF16)
            kt = (k * jnp.exp2(bb[0:1, :] - bb)).astype(BF16)
            s = s * jnp.exp2(bb[0:1, :]) + lax.dot_general(
                v_ref[h, rows, :], kt, TN_DIMS, preferred_element_type=F32)
        s_ref[h] = s


def _bwd_block_order(i, n_ctx_blocks, n_blocks):
    return jnp.where(i < n_ctx_blocks, n_ctx_blocks - 1 - i, n_blocks - 1 - (i - n_ctx_blocks))


def _mixer_bwd_kernel(logit_ref, kb_ref, hv_ref, bc_ref, rk_ref, rv_ref,
                      hsb_ref, rsb_ref, hs_ref, rs_ref, *, n_chunks):
    _hg_bwd_kernel(kb_ref, hv_ref, bc_ref, hsb_ref, hs_ref, n_chunks=n_chunks)
    _ret_bwd_kernel(logit_ref, rk_ref, rv_ref, rsb_ref, rs_ref)


def _mixer_bwd(logit, p3, lf3, *, lc):
    l = p3.shape[1]
    tb = HG_BLOCK
    assert tb == RET_BLOCK
    nb, nbc = l // tb, lc // tb
    ncb = tb // HG_CHUNK
    order = lambda i: _bwd_block_order(i, nbc, nb)
    slabs = lambda n, start: pl.BlockSpec((n, tb, LANE), lambda i: (start // n, order(i), 0))
    kern = functools.partial(_mixer_bwd_kernel, n_chunks=ncb)
    return pl.pallas_call(
        kern,
        grid=(nb,),
        in_specs=[pl.BlockSpec(memory_space=pltpu.SMEM),
                  slabs(HG_HEADS, S_KB), slabs(HG_HEADS, S_HI),
                  pl.BlockSpec((HG_HEADS, tb, LANE), lambda i: (1, order(i), 0)),
                  slabs(RET_HEADS, S_RK), slabs(2 * RET_HEADS, S_RV)],
        out_specs=[pl.BlockSpec((ncb, HG_HEADS, HG_DV, HG_DK), lambda i: (order(i), 0, 0, 0)),
                   pl.BlockSpec((1, RET_HEADS, RET_DK, RET_DV), lambda i: (order(i), 0, 0, 0))],
        out_shape=[jax.ShapeDtypeStruct((l // HG_CHUNK, HG_HEADS, HG_DV, HG_DK), BF16),
                   jax.ShapeDtypeStruct((nb, RET_HEADS, RET_DK, RET_DV), BF16)],
        scratch_shapes=[pltpu.VMEM((HG_HEADS, HG_DV, HG_DK), F32),
                        pltpu.VMEM((RET_HEADS, RET_DK, RET_DV), F32)],
        compiler_params=_params(("arbitrary",)),
        name="mixer_bwd_state",
    )(logit, p3, p3, lf3, p3, p3)


def _hg_scores(q, q_edge, k32, b, fwd, sub, exact, k32_ref=None, b_ref=None):
    c = HG_CHUNK
    n_sub = c // sub
    cap = 0.0 if exact else HG_CAP
    lane = lax.broadcasted_iota(jnp.int32, (sub, c), 1)
    blocks = []
    for blk in range(n_sub):
        r = blk * sub
        q_blk = q[r:r + sub]
        b_blk = b[r:r + sub]
        edge = blk == 0 if fwd else blk == n_sub - 1
        if edge and exact:
            a = jnp.zeros((sub, c), F32)
        else:
            if edge:
                qt = q_edge[r:r + sub]
                ref_minus_b = -b
            else:
                ref = b[r - 1:r] if fwd else b[r + sub:r + sub + 1]
                qt = q_blk * jnp.exp2(b_blk - ref)
                ref_minus_b = ref - b
            kt = (k32 * jnp.exp2(jnp.minimum(ref_minus_b, cap))).astype(BF16)
            a = lax.dot_general(qt.astype(BF16), kt, NT_DIMS, preferred_element_type=F32)
        if exact:
            for jj in range(sub):
                s = r + jj
                e = jnp.exp2(b_blk - b_ref[s:s + 1, :])
                col = jnp.sum(q_blk * k32_ref[s:s + 1, :] * e, axis=1, keepdims=True)
                a = jnp.where(lane == s, col, a)
        blocks.append(a)
    a = jnp.concatenate(blocks, axis=0)
    ri = lax.broadcasted_iota(jnp.int32, (c, c), 0)
    ci = lax.broadcasted_iota(jnp.int32, (c, c), 1)
    return jnp.where((ci <= ri) if fwd else (ci >= ri), a, 0.0)


def _chunk_rows(cc):
    if isinstance(cc, int):
        return slice(cc * HG_CHUNK, (cc + 1) * HG_CHUNK)
    return pl.ds(pl.multiple_of(cc * HG_CHUNK, HG_CHUNK), HG_CHUNK)


def _hg_head_scores(refs, cc, h, sub, exact):
    (q_ref, kf_ref, kb_ref, v_ref, _, bc_ref, sb_ref, _, _, s_ref, k32_ref, b_ref) = refs
    rows = _chunk_rows(cc)
    q = q_ref[h, rows, :].astype(F32)
    v = v_ref[h, rows, :]
    bf = bc_ref[h, rows, :]
    bb = bc_ref[HG_HEADS + h, rows, :]
    kf = kf_ref[h, rows, :].astype(F32)
    kb = kb_ref[h, rows, :].astype(F32)
    qf = q * jnp.exp2(bf)
    qb = q * jnp.exp2(bb)
    s = s_ref[h]
    inter = lax.dot_general(jnp.concatenate([qf, qb], axis=1).astype(BF16),
                            jnp.concatenate([s.astype(BF16), sb_ref[cc, h]], axis=1),
                            NT_DIMS, preferred_element_type=F32)
    if exact:
        k32_ref[...] = kf
        b_ref[...] = bf
    a = _hg_scores(q, qf, kf, bf, True, sub, exact, k32_ref, b_ref)
    if exact:
        k32_ref[...] = kb
        b_ref[...] = bb
    a = a + _hg_scores(q, qb, kb, bb, False, sub, exact, k32_ref, b_ref)
    b_last = bf[HG_CHUNK - 1:HG_CHUNK, :]
    kt = (kf * jnp.exp2(b_last - bf)).astype(BF16)
    s_ref[h] = s * jnp.exp2(b_last) + lax.dot_general(v, kt, TN_DIMS,
                                                      preferred_element_type=F32)
    return inter, a, v


def _hg_head_finish(refs, cc, h, inter, a, v):
    gate_ref, nw_ref, o_ref = refs[4], refs[7], refs[8]
    rows = _chunk_rows(cc)
    o = inter + jnp.dot(a.astype(BF16), v, preferred_element_type=F32)
    o = o * lax.rsqrt(jnp.mean(o * o, axis=-1, keepdims=True) + EPS)
    o = o * nw_ref[h] * gate_ref[h, rows, :].astype(F32)
    o_ref[h, rows, :] = o.astype(BF16)


def _hg_min_block_decay(bc_ref):
    n = bc_ref.shape[1] // HG_FAST_SUB
    first = lax.broadcasted_iota(jnp.int32, (n, LANE), 0) % 2 == 0
    worst = None
    for h in range(HG_HEADS):
        ends = bc_ref.at[h][pl.ds(HG_FAST_SUB - 1, n, stride=HG_FAST_SUB), :]
        fwd = jnp.where(first, ends, ends - pltpu.roll(ends, 1, axis=0))
        starts = bc_ref.at[HG_HEADS + h][pl.ds(0, n, stride=HG_FAST_SUB), :]
        bwd = jnp.where(first, starts - pltpu.roll(starts, n - 1, axis=0), starts)
        m = jnp.minimum(fwd, bwd)
        worst = m if worst is None else jnp.minimum(worst, m)
    return jnp.min(worst)


def _hg_fwd_kernel(*refs, n_chunks):
    bc_ref, s_ref = refs[5], refs[9]

    @pl.when(pl.program_id(0) == 0)
    def _():
        s_ref[...] = jnp.zeros_like(s_ref)

    factored_ok = _hg_min_block_decay(bc_ref) >= -HG_CAP

    @pl.when(factored_ok)
    def _():
        staged_prev = None
        for cc in range(n_chunks):
            staged = [_hg_head_scores(refs, cc, h, HG_FAST_SUB, False) for h in range(HG_HEADS)]
            if staged_prev is not None:
                for h, parts in enumerate(staged_prev):
                    _hg_head_finish(refs, cc - 1, h, *parts)
            staged_prev = staged
        for h, parts in enumerate(staged_prev):
            _hg_head_finish(refs, n_chunks - 1, h, *parts)

    @pl.when(jnp.logical_not(factored_ok))
    def _():
        def body(n, carry):
            cc, h = n // HG_HEADS, n % HG_HEADS
            _hg_head_finish(refs, cc, h, *_hg_head_scores(refs, cc, h, HG_SUB, True))
            return carry
        lax.fori_loop(0, n_chunks * HG_HEADS, body, 0)


def _log_sigmoid(x):
    return jnp.minimum(x, 0.0) - jnp.log1p(jnp.exp(-jnp.abs(x)))


def _ret_log_gamma(logit_ref, direction, h, shape):
    return _log_sigmoid(jnp.full(shape, logit_ref[direction, h], F32))


def _ret_bwd_kernel(logit_ref, k_ref, v_ref, sb_ref, s_ref):
    c = RET_BLOCK

    @pl.when(pl.program_id(0) == 0)
    def _():
        s_ref[...] = jnp.zeros_like(s_ref)

    t = lax.broadcasted_iota(jnp.int32, (c, RET_DK), 0).astype(F32)
    for h in range(RET_HEADS):
        lg = _ret_log_gamma(logit_ref, 1, h, (c, RET_DK))
        s = s_ref[h]
        sb_ref[0, h] = s.astype(BF16)
        kt = (k_ref[h].astype(F32) * jnp.exp(t * lg)).astype(BF16)
        v = jnp.concatenate([v_ref[2 * h], v_ref[2 * h + 1]], axis=1)
        s_ref[h] = s * jnp.exp(c * lg[0:1, 0:1]) + lax.dot_general(
            kt, v, TN_DIMS, preferred_element_type=F32)


def _ret_fwd_kernel(logit_ref, q_ref, k_ref, v_ref, gate_ref, sb_ref, gnw_ref,
                    o_ref, s_ref, dmat_ref):
    c = RET_BLOCK

    @pl.when(pl.program_id(0) == 0)
    def _():
        s_ref[...] = jnp.zeros_like(s_ref)
        ri = lax.broadcasted_iota(jnp.int32, (c, c), 0)
        ci = lax.broadcasted_iota(jnp.int32, (c, c), 1)
        dist = (ri - ci).astype(F32)
        for h in range(RET_HEADS):
            lgf = _ret_log_gamma(logit_ref, 0, h, (c, c))
            lgb = _ret_log_gamma(logit_ref, 1, h, (c, c))
            dmat_ref[h] = (jnp.where(ci <= ri, jnp.exp(jnp.maximum(dist, 0.0) * lgf), 0.0)
                           + jnp.where(ci >= ri, jnp.exp(jnp.maximum(-dist, 0.0) * lgb), 0.0))

    t = lax.broadcasted_iota(jnp.int32, (c, RET_DK), 0).astype(F32)
    for h in range(RET_HEADS):
        lgf = _ret_log_gamma(logit_ref, 0, h, (c, RET_DK))
        lgb = _ret_log_gamma(logit_ref, 1, h, (c, RET_DK))
        q = q_ref[h]
        k = k_ref[h]
        q32 = q.astype(F32)
        v = jnp.concatenate([v_ref[2 * h], v_ref[2 * h + 1]], axis=1)
        s = s_ref[h]
        sc = lax.dot_general(q, k, NT_DIMS, preferred_element_type=F32) * dmat_ref[h]
        o = jnp.dot(sc.astype(BF16), v, preferred_element_type=F32)
        o = o + jnp.dot((q32 * jnp.exp((t + 1.0) * lgf)).astype(BF16), s.astype(BF16),
                        preferred_element_type=F32)
        o = o + jnp.dot((q32 * jnp.exp((c - t) * lgb)).astype(BF16), sb_ref[0, h],
                        preferred_element_type=F32)
        kt = (k.astype(F32) * jnp.exp((c - 1.0 - t) * lgf)).astype(BF16)
        s_ref[h] = s * jnp.exp(c * lgf[0:1, 0:1]) + lax.dot_general(
            kt, v, TN_DIMS, preferred_element_type=F32)

        mu = jnp.mean(o, axis=-1, keepdims=True)
        dev = o - mu
        var = jnp.mean(dev * dev, axis=-1, keepdims=True)
        o = dev * lax.rsqrt(var + EPS) * gnw_ref[h]
        gate = jnp.concatenate([gate_ref[2 * h], gate_ref[2 * h + 1]], axis=1).astype(F32)
        o_ref[:, h * RET_DV:(h + 1) * RET_DV] = (o * gate).astype(BF16)


N_HG_FWD_IN = 8
N_RET_FWD_IN = 7


def _mixer_fwd_kernel(*refs, n_chunks):
    hg_in = refs[:N_HG_FWD_IN]
    ret_in = refs[N_HG_FWD_IN:N_HG_FWD_IN + N_RET_FWD_IN]
    hg_out, ret_out, hs_ref, k32_ref, b_ref, rs_ref, dmat_ref = refs[N_HG_FWD_IN + N_RET_FWD_IN:]
    _hg_fwd_kernel(*hg_in, hg_out, hs_ref, k32_ref, b_ref, n_chunks=n_chunks)
    _ret_fwd_kernel(*ret_in, ret_out, rs_ref, dmat_ref)


def _mixer_fwd(logit, p3, lf3, hsb, rsb, hg_norm_w, gn_w):
    l = p3.shape[1]
    tb = HG_BLOCK
    assert tb == RET_BLOCK
    nb = l // tb
    ncb = tb // HG_CHUNK
    slabs = lambda n, start: pl.BlockSpec((n, tb, LANE), lambda i: (start // n, i, 0))
    kern = functools.partial(_mixer_fwd_kernel, n_chunks=ncb)
    return pl.pallas_call(
        kern,
        grid=(nb,),
        in_specs=[slabs(HG_HEADS, S_HQ), slabs(HG_HEADS, S_KF), slabs(HG_HEADS, S_KB),
                  slabs(HG_HEADS, S_HI), slabs(HG_HEADS, S_HGATE),
                  pl.BlockSpec((2 * HG_HEADS, tb, LANE), lambda i: (0, i, 0)),
                  pl.BlockSpec((ncb, HG_HEADS, HG_DV, HG_DK), lambda i: (i, 0, 0, 0)),
                  pl.BlockSpec((HG_HEADS, 1, HG_DV), lambda i: (0, 0, 0)),
                  pl.BlockSpec(memory_space=pltpu.SMEM),
                  slabs(RET_HEADS, S_RQ), slabs(RET_HEADS, S_RK),
                  slabs(2 * RET_HEADS, S_RV), slabs(2 * RET_HEADS, S_RG),
                  pl.BlockSpec((1, RET_HEADS, RET_DK, RET_DV), lambda i: (i, 0, 0, 0)),
                  pl.BlockSpec((RET_HEADS, 1, RET_DV), lambda i: (0, 0, 0))],
        out_specs=[pl.BlockSpec((HG_HEADS, tb, LANE), lambda i: (0, i, 0)),
                   pl.BlockSpec((tb, RET_HEADS * RET_DV), lambda i: (i, 0))],
        out_shape=[jax.ShapeDtypeStruct((HG_HEADS, l, LANE), BF16),
                   jax.ShapeDtypeStruct((l, RET_HEADS * RET_DV), BF16)],
        scratch_shapes=[pltpu.VMEM((HG_HEADS, HG_DV, HG_DK), F32),
                        pltpu.VMEM((HG_CHUNK, HG_DK), F32),
                        pltpu.VMEM((HG_CHUNK, HG_DK), F32),
                        pltpu.VMEM((RET_HEADS, RET_DK, RET_DV), F32),
                        pltpu.VMEM((RET_HEADS, tb, tb), F32)],
        compiler_params=_params(("arbitrary",)),
        name="mixer_fwd",
    )(p3, p3, p3, p3, p3, lf3, hsb, hg_norm_w.reshape(HG_HEADS, 1, HG_DV),
      logit, p3, p3, p3, p3, rsb, gn_w.reshape(RET_HEADS, 1, RET_DV))


def _pack_bf16_pairs(v):
    w = v.shape[1] // 2
    lo = pltpu.bitcast(v[:, :w].astype(BF16).astype(F32), U32)
    hi = pltpu.bitcast(v[:, w:].astype(BF16).astype(F32), U32)
    return (lo >> 16) | (hi & jnp.uint32(0xFFFF0000))


def _unpack_bf16_pairs(u):
    lo = pltpu.bitcast(u << 16, F32)
    hi = pltpu.bitcast(u & jnp.uint32(0xFFFF0000), F32)
    return lo, hi


def _route(logits):
    lane = lax.broadcasted_iota(jnp.int32, logits.shape, 1)
    big = jnp.int32(10 ** 6)
    neg = -jnp.inf
    gl = jnp.where(lane < N_GROUPS, logits, neg)
    gmax = jnp.max(gl, axis=1, keepdims=True)
    grp = jnp.min(jnp.where(gl == gmax, lane, big), axis=1, keepdims=True)
    g_val = 1.0 / jnp.sum(jnp.exp(gl - gmax), axis=1, keepdims=True)
    lo = N_GROUPS + EXPERTS_PER_GROUP * grp
    el = jnp.where((lane >= lo) & (lane < lo + EXPERTS_PER_GROUP), logits, neg)
    v1 = jnp.max(el, axis=1, keepdims=True)
    i1 = jnp.min(jnp.where(el == v1, lane, big), axis=1, keepdims=True)
    el2 = jnp.where(lane == i1, neg, el)
    v2 = jnp.max(el2, axis=1, keepdims=True)
    i2 = jnp.min(jnp.where(el2 == v2, lane, big), axis=1, keepdims=True)
    r = jnp.exp(v2 - v1)
    w1 = g_val / (1.0 + r)
    w2 = w1 * r
    return ((i1 - N_GROUPS).astype(F32), (i2 - N_GROUPS).astype(F32), w1, w2)


def _merge_kernel(ohg_ref, oret_ref, ga_ref, gb_ref, x_ref, mods_ref, wohg_ref, woret_ref,
                  wout_ref, nw_ref, wrh_ref, wrl_ref, rb_ref,
                  xo_ref, hp_ref, route_ref, route_t_ref, cnt_out_ref, cnt_ref,
                  *, lc, tm, d, tiles_per_seg):
    i = pl.program_id(0)
    tg = tm // MERGE_GROUPS
    routed = []
    for g in range(MERGE_GROUPS):
        rows = slice(g * tg, (g + 1) * tg)
        row0 = i * tm + g * tg
        ohg = jnp.concatenate([ohg_ref[s, rows, :] for s in range(HG_HEADS)], axis=1)
        ga = jnp.concatenate([ga_ref[s, rows, :] for s in range(d // LANE)], axis=1).astype(F32)
        gb = jnp.concatenate([gb_ref[s, rows, :] for s in range(d // LANE)], axis=1).astype(F32)
        y_hg = jnp.dot(ohg, wohg_ref[...], preferred_element_type=F32)
        y_ret = jnp.dot(oret_ref[rows, :], woret_ref[...], preferred_element_type=F32)
        m = (ga * y_hg + gb * y_ret).astype(BF16)
        y = jnp.dot(m, wout_ref[...], preferred_element_type=F32)
        x = x_ref[rows, :] + _row_select(row0, tg, lc, mods_ref, 2, d) * y
        xo_ref[rows, :] = x

        xn = x * lax.rsqrt(jnp.mean(x * x, axis=-1, keepdims=True) + EPS) * nw_ref[...]
        h = (xn * (1.0 + _row_select(row0, tg, lc, mods_ref, 4, d))
             + _row_select(row0, tg, lc, mods_ref, 3, d))
        hp_ref[rows, :] = _pack_bf16_pairs(h)

        h_hi = h.astype(BF16)
        h_lo = (h - h_hi.astype(F32)).astype(BF16)
        logits = (jnp.dot(h_hi, wrh_ref[...], preferred_element_type=F32)
                  + jnp.dot(h_lo, wrh_ref[...], preferred_element_type=F32)
                  + jnp.dot(h_hi, wrl_ref[...], preferred_element_type=F32)) + rb_ref[...]
        routed.append(_route(logits))
    e0, e1, w0, w1 = (jnp.concatenate(parts, axis=0) for parts in zip(*routed))

    @pl.when(i % tiles_per_seg == 0)
    def _():
        cnt_ref[...] = jnp.zeros_like(cnt_ref)

    lane_e = lax.broadcasted_iota(jnp.int32, (tm, LANE), 1).astype(F32)
    hot0 = lane_e == e0
    hot1 = lane_e == e1
    hot = jnp.where(hot0 | hot1, 1.0, 0.0)
    earlier = (lax.broadcasted_iota(jnp.int32, (tm, tm), 1)
               < lax.broadcasted_iota(jnp.int32, (tm, tm), 0))
    before = jnp.dot(jnp.where(earlier, 1.0, 0.0).astype(BF16), hot.astype(BF16),
                     preferred_element_type=F32) + cnt_ref[0:1, :]
    rank0 = jnp.sum(jnp.where(hot0, before, 0.0), axis=1, keepdims=True)
    rank1 = jnp.sum(jnp.where(hot1, before, 0.0), axis=1, keepdims=True)
    cnt_ref[...] = cnt_ref[...] + jnp.sum(hot, axis=0, keepdims=True)
    cnt_out_ref[0] = cnt_ref[...]

    lane = lax.broadcasted_iota(jnp.int32, (tm, LANE), 1)
    rec = jnp.zeros((tm, LANE), F32)
    for k, val in enumerate((e0, e1, w0, w1, rank0, rank1)):
        rec = jnp.where(lane == k, val, rec)
    route_ref[...] = rec[:, :ROUTE_W]
    eye = jnp.where(lax.broadcasted_iota(jnp.int32, (ROUTE_W, LANE), 0)
                    == lax.broadcasted_iota(jnp.int32, (ROUTE_W, LANE), 1), 1.0, 0.0)
    route_t_ref[0] = lax.dot_general(eye, rec, NT_DIMS, precision=lax.Precision.HIGHEST,
                                     preferred_element_type=F32)


def _merge(ohg, oret, p3, xc, mods, wohg, woret, wout, norm_w, wr_hi, wr_lo, rbias, *, lc, n_seg):
    l, d = xc.shape
    seg_tokens = l // n_seg
    tm = _largest_divisor(seg_tokens, MERGE_TILE, 16 * MERGE_GROUPS)
    tiles_per_seg = seg_tokens // tm
    n_slab = d // LANE
    full = lambda a: pl.BlockSpec(a.shape, lambda i: (0,) * a.ndim)
    kern = functools.partial(_merge_kernel, lc=lc, tm=tm, d=d, tiles_per_seg=tiles_per_seg)
    return pl.pallas_call(
        kern,
        grid=(l // tm,),
        in_specs=[pl.BlockSpec((HG_HEADS, tm, LANE), lambda i: (0, i, 0)),
                  pl.BlockSpec((tm, oret.shape[1]), lambda i: (i, 0)),
                  pl.BlockSpec((n_slab, tm, LANE), lambda i: (S_GA // n_slab, i, 0)),
                  pl.BlockSpec((n_slab, tm, LANE), lambda i: (S_GB // n_slab, i, 0)),
                  pl.BlockSpec((tm, d), lambda i: (i, 0)),
                  full(mods), full(wohg), full(woret), full(wout),
                  pl.BlockSpec((1, d), lambda i: (0, 0)),
                  full(wr_hi), full(wr_lo), full(rbias)],
        out_specs=[pl.BlockSpec((tm, d), lambda i: (i, 0)),
                   pl.BlockSpec((tm, d // 2), lambda i: (i, 0)),
                   pl.BlockSpec((tm, ROUTE_W), lambda i: (i, 0)),
                   pl.BlockSpec((1, ROUTE_W, tm), lambda i: (i, 0, 0)),
                   pl.BlockSpec((1, 8, LANE), lambda i: (i // tiles_per_seg, 0, 0))],
        out_shape=[jax.ShapeDtypeStruct((l, d), F32),
                   jax.ShapeDtypeStruct((l, d // 2), U32),
                   jax.ShapeDtypeStruct((l, ROUTE_W), F32),
                   jax.ShapeDtypeStruct((l // tm, ROUTE_W, tm), F32),
                   jax.ShapeDtypeStruct((n_seg, 8, LANE), F32)],
        scratch_shapes=[pltpu.VMEM((8, LANE), F32)],
        compiler_params=_params(("arbitrary",)),
        name="merge_router",
    )(ohg, oret, p3, p3, xc, mods, wohg, woret, wout, norm_w.reshape(1, d), wr_hi, wr_lo, rbias)


def _moe_kernel(be_ref, nused_ref, slot_ref, hp_ref, wg_ref, wu_ref, wd_ref, route_ref, x_ref,
                mods_ref, fw_ref, o_ref, xs_ref, g0_ref, g1_ref,
                *, lc, tm, d, nbs, tiles_per_seg, seg_tokens, n_tokens, final):
    s = pl.program_id(0)
    b = pl.program_id(1)

    @pl.when(b == 0)
    def _():
        xs_ref[...] = jnp.zeros_like(xs_ref)

        def scatter(t, carry):
            a = s * seg_tokens + t
            row = hp_ref[0, pl.ds(t, 1), :]
            xs_ref[pl.ds(slot_ref[a], 1), :] = row
            xs_ref[pl.ds(slot_ref[n_tokens + a], 1), :] = row
            return carry

        lax.fori_loop(0, seg_tokens, scatter, 0, unroll=4)

    @pl.when(b < nused_ref[s])
    def _():
        rows = pl.ds(pl.multiple_of(b * MOE_BLOCK, MOE_BLOCK), MOE_BLOCK)
        lo, hi = _unpack_bf16_pairs(xs_ref[rows, :])
        x = jnp.concatenate([lo, hi], axis=1).astype(BF16)
        g = jnp.dot(x, wg_ref[0], preferred_element_type=F32)
        u = jnp.dot(x, wu_ref[0], preferred_element_type=F32)
        a = (_silu(g) * u).astype(BF16)
        xs_ref[rows, :] = _pack_bf16_pairs(jnp.dot(a, wd_ref[0], preferred_element_type=F32))

    @pl.when(b >= nbs)
    def _():
        tile0 = s * seg_tokens + (b - nbs) * tm

        def gather(r, carry):
            a = tile0 + r
            g0_ref[pl.ds(r, 1), :] = xs_ref[pl.ds(slot_ref[a], 1), :]
            g1_ref[pl.ds(r, 1), :] = xs_ref[pl.ds(slot_ref[n_tokens + a], 1), :]
            return carry

        lax.fori_loop(0, tm, gather, 0, unroll=8)
        lo0, hi0 = _unpack_bf16_pairs(g0_ref[...])
        lo1, hi1 = _unpack_bf16_pairs(g1_ref[...])
        w0 = route_ref[:, 2:3]
        w1 = route_ref[:, 3:4]
        y = jnp.concatenate([w0 * lo0 + w1 * lo1, w0 * hi0 + w1 * hi1], axis=1)
        x = x_ref[...] + _row_select(tile0, tm, lc, mods_ref, 5, d) * y
        if final:
            x = x * lax.rsqrt(jnp.mean(x * x, axis=-1, keepdims=True) + EPS) * fw_ref[...]
        o_ref[...] = x


def _moe(block_expert, n_used, slot, hp_seg, wg, wu, wd, route, xc, mods, final_w,
         *, nbs, layer, lc, final):
    n_seg, seg_tokens, half = hp_seg.shape
    l, d = xc.shape
    d_e = wg.shape[3]
    tm = _largest_divisor(seg_tokens, 320, 8)
    tiles = seg_tokens // tm
    kern = functools.partial(_moe_kernel, lc=lc, tm=tm, d=d, nbs=nbs, tiles_per_seg=tiles,
                             seg_tokens=seg_tokens, n_tokens=l, final=final)
    w_idx = lambda s, b, be, nu, *_: (
        layer, be[s * nbs + jnp.minimum(b, jnp.maximum(nu[s] - 1, 0))], 0, 0)
    tile = lambda s, b: s * tiles + jnp.clip(b - nbs, 0, tiles - 1)
    row = lambda s, b, *_: (tile(s, b), 0)
    if final:
        assert lc % tm == 0
        out_rows = l - lc
        out_row = lambda s, b, *_: (jnp.maximum(tile(s, b) - lc // tm, 0), 0)
    else:
        out_rows, out_row = l, row
    grid_spec = pltpu.PrefetchScalarGridSpec(
        num_scalar_prefetch=3,
        grid=(n_seg, nbs + tiles),
        in_specs=[pl.BlockSpec((1, seg_tokens, half), lambda s, b, *_: (s, 0, 0),
                               pipeline_mode=pl.Buffered(1)),
                  pl.BlockSpec((None, 1, d, d_e), w_idx),
                  pl.BlockSpec((None, 1, d, d_e), w_idx),
                  pl.BlockSpec((None, 1, d_e, d), w_idx),
                  pl.BlockSpec((tm, ROUTE_W), row),
                  pl.BlockSpec((tm, d), row),
                  pl.BlockSpec(mods.shape, lambda s, b, *_: (0, 0)),
                  pl.BlockSpec((1, d), lambda s, b, *_: (0, 0))],
        out_specs=pl.BlockSpec((tm, d), out_row),
        scratch_shapes=[pltpu.VMEM((nbs * MOE_BLOCK, half), U32),
                        pltpu.VMEM((tm, half), U32), pltpu.VMEM((tm, half), U32)])
    return pl.pallas_call(
        kern,
        grid_spec=grid_spec,
        out_shape=jax.ShapeDtypeStruct((out_rows, d), F32),
        compiler_params=_params(("arbitrary", "arbitrary")),
        name="moe",
    )(block_expert, n_used, slot, hp_seg, wg, wu, wd, route, xc, mods, final_w.reshape(1, d))


def _dispatch_plan(route_t, counts, nbs):
    n_seg = counts.shape[0]
    counts = counts[:, 0, :N_EXPERTS].astype(jnp.int32)
    padded = ((counts + MOE_BLOCK - 1) // MOE_BLOCK) * MOE_BLOCK
    pad_end = jnp.cumsum(padded, axis=1)
    pad_start = pad_end - padded
    starts = jnp.arange(nbs, dtype=jnp.int32) * MOE_BLOCK
    block_expert = jnp.minimum(
        jnp.sum((pad_end[:, None, :] <= starts[None, :, None]).astype(jnp.int32), axis=2),
        N_EXPERTS - 1)
    n_used = pad_end[:, -1] // MOE_BLOCK
    n_tiles, _, tm = route_t.shape
    tile_start = jnp.repeat(pad_start, n_tiles // n_seg, axis=0)
    slots = []
    for k in range(2):
        e = route_t[:, k, :].astype(jnp.int32)
        rank = route_t[:, 4 + k, :].astype(jnp.int32)
        start = jnp.zeros_like(rank)
        for x in range(N_EXPERTS):
            start = jnp.where(e == x, tile_start[:, x:x + 1], start)
        slots.append((rank + start).reshape(-1))
    return (jnp.concatenate(slots), block_expert.reshape(-1).astype(jnp.int32),
            n_used.astype(jnp.int32))


def _rope_tables(lc, t):
    quarter = RET_DK // 4
    n_rows = t // GRID_W
    inv = ROPE_BASE ** (-np.arange(0, 2 * quarter, 2, dtype=np.float64) / (2 * quarter))
    ang_r = np.arange(n_rows, dtype=np.float64)[:, None] * inv
    ang_c = np.arange(GRID_W, dtype=np.float64)[:, None] * inv

    def expand(row_part, col_part):
        r = np.broadcast_to(row_part[:, None, :], (n_rows, GRID_W, 2 * quarter))
        c = np.broadcast_to(col_part[None, :, :], (n_rows, GRID_W, 2 * quarter))
        return np.concatenate([r, c], axis=2).reshape(t, RET_DK)

    two = lambda a: np.concatenate([a, a], axis=1)
    cos = expand(two(np.cos(ang_r)), two(np.cos(ang_c)))
    sin = expand(np.concatenate([-np.sin(ang_r), np.sin(ang_r)], axis=1),
                 np.concatenate([-np.sin(ang_c), np.sin(ang_c)], axis=1))
    cos = np.concatenate([np.ones((lc, RET_DK)), cos], axis=0).astype(np.float32)
    sin = np.concatenate([np.zeros((lc, RET_DK)), sin], axis=0).astype(np.float32)
    return jnp.asarray(cos), jnp.asarray(sin)


def kernel(x, c, ctx, c_ctx, w_mod, b_mod, norm_mix_w, norm_ffn_w, w_in, hgrn_lb_logits, hgrn_norm_w,
           ret_decay_logit, ret_gn_w, w_o_hgrn, w_o_ret, w_out, router_group_w, router_group_b,
           router_expert_w, router_expert_b, expert_w_gate, expert_w_up, expert_w_down, final_norm_w):
    b_, t_, d = x.shape
    assert b_ == 1
    lc = ctx.shape[1]
    depth = w_mod.shape[0]
    l = lc + t_
    assert lc % HG_BLOCK == 0 and l % HG_BLOCK == 0 and lc % RET_BLOCK == 0 and l % RET_BLOCK == 0
    n_seg = MOE_SEGMENTS
    assert l % n_seg == 0
    seg_tokens = l // n_seg
    nbs = pl.cdiv(2 * seg_tokens + N_EXPERTS * (MOE_BLOCK - 1), MOE_BLOCK)

    xc = jnp.concatenate([ctx[0], x[0]], axis=0)
    cc = jnp.zeros((8, d), F32).at[0].set(c[0]).at[1].set(c_ctx)
    mods_all = _modulation(cc, w_mod, b_mod)
    cos_t, sin_t = _rope_tables(lc, t_)
    w_in_bf = w_in.astype(BF16)
    w_gate_bf, w_up_bf, w_down_bf = (w.astype(BF16)
                                     for w in (expert_w_gate, expert_w_up, expert_w_down))

    for layer in range(depth):
        mods = mods_all[layer]
        p3, lf3 = _inproj(xc, mods, norm_mix_w[layer], w_in_bf, hgrn_lb_logits,
                          cos_t, sin_t, layer=layer, lc=lc)
        sb_hg, sb_ret = _mixer_bwd(ret_decay_logit[layer], p3, lf3, lc=lc)
        ohg, oret = _mixer_fwd(ret_decay_logit[layer], p3, lf3, sb_hg, sb_ret,
                               hgrn_norm_w[layer], ret_gn_w[layer])

        wr = jnp.concatenate([router_group_w[layer], router_expert_w[layer]], axis=1)
        wr = jnp.pad(wr, ((0, 0), (0, LANE - wr.shape[1])))
        wr_hi = wr.astype(BF16)
        wr_lo = (wr - wr_hi.astype(F32)).astype(BF16)
        rbias = jnp.pad(jnp.concatenate([router_group_b[layer], router_expert_b[layer]]),
                        (0, LANE - N_GROUPS - N_EXPERTS)).reshape(1, LANE)
        xc, hp, route, route_t, counts = _merge(
            ohg, oret, p3, xc, mods, w_o_hgrn[layer].astype(BF16), w_o_ret[layer].astype(BF16),
            w_out[layer].astype(BF16), norm_ffn_w[layer], wr_hi, wr_lo, rbias, lc=lc, n_seg=n_seg)

        slot, block_expert, n_used = _dispatch_plan(route_t, counts, nbs)
        xc = _moe(block_expert, n_used, slot, hp.reshape(n_seg, seg_tokens, d // 2),
                  w_gate_bf, w_up_bf, w_down_bf, route, xc, mods, final_norm_w,
                  nbs=nbs, layer=layer, lc=lc, final=(layer == depth - 1))

    return xc[None]
```

```python
import functools

import jax
import jax.numpy as jnp
import numpy as np
from jax import lax
from jax.experimental import pallas as pl
from jax.experimental.pallas import tpu as pltpu

F32 = jnp.float32
BF16 = jnp.bfloat16
U32 = jnp.uint32

GRID_W = 64
HG_HEADS = 4
HG_DK = 128
HG_DV = 128
F_MIN = 1e-30
RET_HEADS = 4
RET_DK = 128
RET_DV = 256
ROPE_BASE = 10000.0
N_GROUPS = 4
EXPERTS_PER_GROUP = 8
N_EXPERTS = N_GROUPS * EXPERTS_PER_GROUP
N_MOD = 6
EPS = 1e-6

LANE = 128
COL_TILE = 512
SLABS_PER_TILE = COL_TILE // LANE
HG_CHUNK = 64
HG_SUB = 16
HG_BLOCK = 256
LOG2E = 1.4426950408889634
HG_FAST_SUB = 32
HG_CAP = 100.0
CUM_GROUP = 256
CUM_TERMS = 2
RET_BLOCK = 256
MOE_SEGMENTS = 5
MOE_BLOCK = 256
ROW_GROUP = 16
MERGE_GROUPS = 1
MERGE_TILE = 832
ROUTE_W = 8
VMEM_LIMIT = 56 * 1024 * 1024

NT_DIMS = (((1,), (1,)), ((), ()))
TN_DIMS = (((0,), (0,)), ((), ()))

SEG_SLABS = (4, 4, 4, 4, 4, 4, 4, 8, 8, 8, 8)
SEG_ORDER = (7, 8, 9, 10, 0, 1, 2, 3, 4, 5, 6)
_starts = {}
_pos = 0
for _seg in SEG_ORDER:
    _starts[_seg] = _pos
    _pos += SEG_SLABS[_seg]
SEG_START = tuple(_starts[_seg] for _seg in range(len(SEG_SLABS)))
(S_HQ, S_KF, S_KB, S_HI, S_HGATE, S_RQ, S_RK, S_RV, S_RG, S_GA, S_GB) = SEG_START
assert SEG_ORDER == tuple(range(SEG_ORDER[0], len(SEG_SLABS))) + tuple(range(SEG_ORDER[0]))
TILE_ROTATION = sum(SEG_SLABS[:SEG_ORDER[0]]) // SLABS_PER_TILE
N_SLABS = sum(SEG_SLABS)


def _params(sem):
    return pltpu.CompilerParams(dimension_semantics=sem, vmem_limit_bytes=VMEM_LIMIT)


def _sigmoid(x):
    return 1.0 / (1.0 + jnp.exp(-x))


def _silu(x):
    return x * _sigmoid(x)


def _largest_divisor(n, cap, multiple):
    best = None
    for d in range(multiple, cap + 1, multiple):
        if n % d == 0:
            best = d
    assert best is not None, (n, cap, multiple)
    return best


def _row_select(row0, n_rows, lc, mods_ref, k, d):
    rows = row0 + lax.broadcasted_iota(jnp.int32, (n_rows, 1), 0)
    lat = mods_ref[0:1, k * d:(k + 1) * d]
    ctx = mods_ref[1:2, k * d:(k + 1) * d]
    return jnp.where(rows < lc, ctx, lat)


def _mod_kernel(cc_ref, w_ref, b_ref, o_ref):
    cc = cc_ref[...]
    s = _silu(cc)
    o_ref[0] = jnp.dot(s, w_ref[0], precision=lax.Precision.HIGHEST,
                       preferred_element_type=F32) + b_ref[0]


def _modulation(cc, w_mod, b_mod):
    depth, d, n = w_mod.shape
    tn = _largest_divisor(n, 1536, LANE)
    return pl.pallas_call(
        _mod_kernel,
        grid=(depth, n // tn),
        in_specs=[pl.BlockSpec((8, d), lambda l, j: (0, 0)),
                  pl.BlockSpec((1, d, tn), lambda l, j: (l, 0, j)),
                  pl.BlockSpec((1, 1, tn), lambda l, j: (l, 0, j))],
        out_specs=pl.BlockSpec((1, 8, tn), lambda l, j: (l, 0, j)),
        out_shape=jax.ShapeDtypeStruct((depth, 8, n), F32),
        compiler_params=_params(("arbitrary", "arbitrary")),
        name="modulation",
    )(cc, w_mod, b_mod.reshape(depth, 1, n))


def _inproj_kernel(x_ref, mods_ref, nw_ref, w_ref, lbl_ref, cos_ref, sin_ref,
                   p_ref, lf_ref, h_ref, *, layer, lc, tm, d):
    i = pl.program_id(0)
    j = pl.program_id(1)

    @pl.when(j == 0)
    def _():
        x = x_ref[...]
        xn = x * lax.rsqrt(jnp.mean(x * x, axis=-1, keepdims=True) + EPS) * nw_ref[...]
        shift = _row_select(i * tm, tm, lc, mods_ref, 0, d)
        scale = _row_select(i * tm, tm, lc, mods_ref, 1, d)
        h_ref[...] = (xn * (1.0 + scale) + shift).astype(BF16)

    def for_row_groups(epilogue):
        for r0 in range(0, tm, CUM_GROUP):
            rows = slice(r0, r0 + CUM_GROUP)
            acc = jnp.dot(h_ref[rows, :], w_ref[...], preferred_element_type=F32)
            epilogue(acc, rows)

    def put(val, rows):
        v = val.astype(BF16)
        for s in range(SLABS_PER_TILE):
            p_ref[s, rows, :] = v[:, s * LANE:(s + 1) * LANE]

    tiles = lambda seg: tuple(range(SEG_START[seg] // SLABS_PER_TILE,
                                    (SEG_START[seg] + SEG_SLABS[seg]) // SLABS_PER_TILE))
    in_tiles = lambda segs: functools.reduce(
        jnp.logical_or, [j == t for seg in segs for t in tiles(seg)])

    @pl.when(in_tiles((0,)))
    def _():
        for_row_groups(lambda acc, rows: put(_silu(acc) * (HG_DK ** -0.5), rows))

    @pl.when(in_tiles((1, 2)))
    def _():
        logits = lbl_ref[jnp.clip(j - tiles(1)[0], 0, 1)]
        e = jnp.exp(logits - jnp.max(logits, axis=0, keepdims=True))
        p = e / jnp.sum(e, axis=0, keepdims=True)
        lb = jnp.zeros_like(p[0:1])
        for r in range(1, layer + 1):
            lb = lb + p[r:r + 1]
        lb = jnp.clip(lb, 0.0, 1.0 - 1e-6)
        g = CUM_GROUP
        ri = lax.broadcasted_iota(jnp.int32, (g, g), 0)
        ci = lax.broadcasted_iota(jnp.int32, (g, g), 1)
        same_chunk = (ri // HG_CHUNK) == (ci // HG_CHUNK)
        before = jnp.where(same_chunk & (ci <= ri), 1.0, 0.0)
        after = jnp.where(same_chunk & (ci >= ri), 1.0, 0.0)
        tri = jnp.where(j == tiles(1)[0], before, after).astype(BF16)

        log_f_terms = []

        def epilogue(acc, rows):
            sig = _sigmoid(acc)
            put((1.0 - lb) * (1.0 - sig), rows)
            rest = jnp.log(jnp.maximum(lb + (1.0 - lb) * sig, F_MIN))
            terms = []
            for _ in range(CUM_TERMS):
                terms.append(rest.astype(BF16))
                rest = rest - terms[-1].astype(F32)
            log_f_terms.append((rows, terms))

        for_row_groups(epilogue)
        for rows, terms in log_f_terms:
            cum = jnp.zeros((g, COL_TILE), F32)
            for term in terms:
                cum = cum + jnp.dot(tri, term, preferred_element_type=F32)
            cum = cum * LOG2E
            for s in range(SLABS_PER_TILE):
                lf_ref[s, rows, :] = cum[:, s * LANE:(s + 1) * LANE]

    @pl.when(in_tiles((3, 7)))
    def _():
        for_row_groups(put)

    @pl.when(in_tiles((4, 8)))
    def _():
        for_row_groups(lambda acc, rows: put(_silu(acc), rows))

    @pl.when(in_tiles((5, 6)))
    def _():
        scale = jnp.where(j == tiles(5)[0], RET_DK ** -0.5, 1.0)

        def epilogue(acc, rows):
            xq = acc * scale
            n = xq.shape[1]
            lane = lax.broadcasted_iota(jnp.int32, xq.shape, 1)
            partner = jnp.where((lane & 32) == 0, pltpu.roll(xq, n - 32, axis=1),
                                pltpu.roll(xq, 32, axis=1))
            cos = jnp.concatenate([cos_ref[rows, :]] * SLABS_PER_TILE, axis=1)
            sin = jnp.concatenate([sin_ref[rows, :]] * SLABS_PER_TILE, axis=1)
            put(xq * cos + partner * sin, rows)

        for_row_groups(epilogue)

    @pl.when(in_tiles((9, 10)))
    def _():
        for_row_groups(lambda acc, rows: put(_sigmoid(acc), rows))


def _inproj(xc, mods, norm_w, w_in_bf, lb_logits, cos_t, sin_t, *, layer, lc):
    l, d = xc.shape
    d_in = w_in_bf.shape[2]
    assert d_in == N_SLABS * LANE
    tm = _largest_divisor(l, 1280, CUM_GROUP)
    n_col = d_in // COL_TILE
    kf_tile = S_KF // SLABS_PER_TILE
    kern = functools.partial(_inproj_kernel, layer=layer, lc=lc, tm=tm, d=d)
    return pl.pallas_call(
        kern,
        grid=(l // tm, n_col),
        in_specs=[pl.BlockSpec((tm, d), lambda i, j: (i, 0)),
                  pl.BlockSpec(mods.shape, lambda i, j: (0, 0)),
                  pl.BlockSpec((1, d), lambda i, j: (0, 0)),
                  pl.BlockSpec((None, d, COL_TILE),
                               lambda i, j: (layer, 0, (j + TILE_ROTATION) % n_col)),
                  pl.BlockSpec(lb_logits.shape, lambda i, j: (0, 0, 0)),
                  pl.BlockSpec((tm, LANE), lambda i, j: (i, 0)),
                  pl.BlockSpec((tm, LANE), lambda i, j: (i, 0))],
        out_specs=[pl.BlockSpec((SLABS_PER_TILE, tm, LANE), lambda i, j: (j, i, 0)),
                   pl.BlockSpec((SLABS_PER_TILE, tm, LANE),
                                lambda i, j: (jnp.clip(j - kf_tile, 0, 1), i, 0))],
        out_shape=[jax.ShapeDtypeStruct((N_SLABS, l, LANE), BF16),
                   jax.ShapeDtypeStruct((2 * HG_HEADS, l, LANE), F32)],
        scratch_shapes=[pltpu.VMEM((tm, d), BF16)],
        compiler_params=_params(("arbitrary", "arbitrary")),
        name="inproj",
    )(xc, mods, norm_w.reshape(1, d), w_in_bf, lb_logits, cos_t, sin_t)


def _hg_bwd_kernel(kb_ref, v_ref, bc_ref, sb_ref, s_ref, *, n_chunks):
    @pl.when(pl.program_id(0) == 0)
    def _():
        s_ref[...] = jnp.zeros_like(s_ref)

    for h in range(HG_HEADS):
        s = s_ref[h]
        for cc in reversed(range(n_chunks)):
            rows = slice(cc * HG_CHUNK, (cc + 1) * HG_CHUNK)
            bb = bc_ref[h, rows, :]
            k = kb_ref[h, rows, :].astype(F32)
            sb_ref[cc, h] = s.astype(BF16)
            kt = (k * jnp.exp2(bb[0:1, :] - bb)).astype(BF16)
            s = s * jnp.exp2(bb[0:1, :]) + lax.dot_general(
                v_ref[h, rows, :], kt, TN_DIMS, preferred_element_type=F32)
        s_ref[h] = s


def _bwd_block_order(i, n_ctx_blocks, n_blocks):
    return jnp.where(i < n_ctx_blocks, n_ctx_blocks - 1 - i, n_blocks - 1 - (i - n_ctx_blocks))


def _mixer_bwd_kernel(logit_ref, kb_ref, hv_ref, bc_ref, rk_ref, rv_ref,
                      hsb_ref, rsb_ref, hs_ref, rs_ref, *, n_chunks):
    _hg_bwd_kernel(kb_ref, hv_ref, bc_ref, hsb_ref, hs_ref, n_chunks=n_chunks)
    _ret_bwd_kernel(logit_ref, rk_ref, rv_ref, rsb_ref, rs_ref)


def _mixer_bwd(logit, p3, lf3, *, lc):
    l = p3.shape[1]
    tb = HG_BLOCK
    assert tb == RET_BLOCK
    nb, nbc = l // tb, lc // tb
    ncb = tb // HG_CHUNK
    order = lambda i: _bwd_block_order(i, nbc, nb)
    slabs = lambda n, start: pl.BlockSpec((n, tb, LANE), lambda i: (start // n, order(i), 0))
    kern = functools.partial(_mixer_bwd_kernel, n_chunks=ncb)
    return pl.pallas_call(
        kern,
        grid=(nb,),
        in_specs=[pl.BlockSpec(memory_space=pltpu.SMEM),
                  slabs(HG_HEADS, S_KB), slabs(HG_HEADS, S_HI),
                  pl.BlockSpec((HG_HEADS, tb, LANE), lambda i: (1, order(i), 0)),
                  slabs(RET_HEADS, S_RK), slabs(2 * RET_HEADS, S_RV)],
        out_specs=[pl.BlockSpec((ncb, HG_HEADS, HG_DV, HG_DK), lambda i: (order(i), 0, 0, 0)),
                   pl.BlockSpec((1, RET_HEADS, RET_DK, RET_DV), lambda i: (order(i), 0, 0, 0))],
        out_shape=[jax.ShapeDtypeStruct((l // HG_CHUNK, HG_HEADS, HG_DV, HG_DK), BF16),
                   jax.ShapeDtypeStruct((nb, RET_HEADS, RET_DK, RET_DV), BF16)],
        scratch_shapes=[pltpu.VMEM((HG_HEADS, HG_DV, HG_DK), F32),
                        pltpu.VMEM((RET_HEADS, RET_DK, RET_DV), F32)],
        compiler_params=_params(("arbitrary",)),
        name="mixer_bwd_state",
    )(logit, p3, p3, lf3, p3, p3)


def _hg_scores(q, q_edge, k32, b, fwd, sub, exact, k32_ref=None, b_ref=None):
    c = HG_CHUNK
    n_sub = c // sub
    cap = 0.0 if exact else HG_CAP
    lane = lax.broadcasted_iota(jnp.int32, (sub, c), 1)
    blocks = []
    for blk in range(n_sub):
        r = blk * sub
        q_blk = q[r:r + sub]
        b_blk = b[r:r + sub]
        edge = blk == 0 if fwd else blk == n_sub - 1
        if edge and exact:
            a = jnp.zeros((sub, c), F32)
        else:
            if edge:
                qt = q_edge[r:r + sub]
                ref_minus_b = -b
            else:
                ref = b[r - 1:r] if fwd else b[r + sub:r + sub + 1]
                qt = q_blk * jnp.exp2(b_blk - ref)
                ref_minus_b = ref - b
            kt = (k32 * jnp.exp2(jnp.minimum(ref_minus_b, cap))).astype(BF16)
            a = lax.dot_general(qt.astype(BF16), kt, NT_DIMS, preferred_element_type=F32)
        if exact:
            for jj in range(sub):
                s = r + jj
                e = jnp.exp2(b_blk - b_ref[s:s + 1, :])
                col = jnp.sum(q_blk * k32_ref[s:s + 1, :] * e, axis=1, keepdims=True)
                a = jnp.where(lane == s, col, a)
        blocks.append(a)
    a = jnp.concatenate(blocks, axis=0)
    ri = lax.broadcasted_iota(jnp.int32, (c, c), 0)
    ci = lax.broadcasted_iota(jnp.int32, (c, c), 1)
    return jnp.where((ci <= ri) if fwd else (ci >= ri), a, 0.0)


def _chunk_rows(cc):
    if isinstance(cc, int):
        return slice(cc * HG_CHUNK, (cc + 1) * HG_CHUNK)
    return pl.ds(pl.multiple_of(cc * HG_CHUNK, HG_CHUNK), HG_CHUNK)


def _hg_head_scores(refs, cc, h, sub, exact):
    (q_ref, kf_ref, kb_ref, v_ref, _, bc_ref, sb_ref, _, _, s_ref, k32_ref, b_ref) = refs
    rows = _chunk_rows(cc)
    q = q_ref[h, rows, :].astype(F32)
    v = v_ref[h, rows, :]
    bf = bc_ref[h, rows, :]
    bb = bc_ref[HG_HEADS + h, rows, :]
    kf = kf_ref[h, rows, :].astype(F32)
    kb = kb_ref[h, rows, :].astype(F32)
    qf = q * jnp.exp2(bf)
    qb = q * jnp.exp2(bb)
    s = s_ref[h]
    inter = lax.dot_general(jnp.concatenate([qf, qb], axis=1).astype(BF16),
                            jnp.concatenate([s.astype(BF16), sb_ref[cc, h]], axis=1),
                            NT_DIMS, preferred_element_type=F32)
    if exact:
        k32_ref[...] = kf
        b_ref[...] = bf
    a = _hg_scores(q, qf, kf, bf, True, sub, exact, k32_ref, b_ref)
    if exact:
        k32_ref[...] = kb
        b_ref[...] = bb
    a = a + _hg_scores(q, qb, kb, bb, False, sub, exact, k32_ref, b_ref)
    b_last = bf[HG_CHUNK - 1:HG_CHUNK, :]
    kt = (kf * jnp.exp2(b_last - bf)).astype(BF16)
    s_ref[h] = s * jnp.exp2(b_last) + lax.dot_general(v, kt, TN_DIMS,
                                                      preferred_element_type=F32)
    return inter, a, v


def _hg_head_finish(refs, cc, h, inter, a, v):
    gate_ref, nw_ref, o_ref = refs[4], refs[7], refs[8]
    rows = _chunk_rows(cc)
    o = inter + jnp.dot(a.astype(BF16), v, preferred_element_type=F32)
    o = o * lax.rsqrt(jnp.mean(o * o, axis=-1, keepdims=True) + EPS)
    o = o * nw_ref[h] * gate_ref[h, rows, :].astype(F32)
    o_ref[h, rows, :] = o.astype(BF16)


def _hg_min_block_decay(bc_ref):
    n = bc_ref.shape[1] // HG_FAST_SUB
    first = lax.broadcasted_iota(jnp.int32, (n, LANE), 0) % 2 == 0
    worst = None
    for h in range(HG_HEADS):
        ends = bc_ref.at[h][pl.ds(HG_FAST_SUB - 1, n, stride=HG_FAST_SUB), :]
        fwd = jnp.where(first, ends, ends - pltpu.roll(ends, 1, axis=0))
        starts = bc_ref.at[HG_HEADS + h][pl.ds(0, n, stride=HG_FAST_SUB), :]
        bwd = jnp.where(first, starts - pltpu.roll(starts, n - 1, axis=0), starts)
        m = jnp.minimum(fwd, bwd)
        worst = m if worst is None else jnp.minimum(worst, m)
    return jnp.min(worst)


def _hg_fwd_kernel(*refs, n_chunks):
    bc_ref, s_ref = refs[5], refs[9]

    @pl.when(pl.program_id(0) == 0)
    def _():
        s_ref[...] = jnp.zeros_like(s_ref)

    factored_ok = _hg_min_block_decay(bc_ref) >= -HG_CAP

    @pl.when(factored_ok)
    def _():
        staged_prev = None
        for cc in range(n_chunks):
            staged = [_hg_head_scores(refs, cc, h, HG_FAST_SUB, False) for h in range(HG_HEADS)]
            if staged_prev is not None:
                for h, parts in enumerate(staged_prev):
                    _hg_head_finish(refs, cc - 1, h, *parts)
            staged_prev = staged
        for h, parts in enumerate(staged_prev):
            _hg_head_finish(refs, n_chunks - 1, h, *parts)

    @pl.when(jnp.logical_not(factored_ok))
    def _():
        def body(n, carry):
            cc, h = n // HG_HEADS, n % HG_HEADS
            _hg_head_finish(refs, cc, h, *_hg_head_scores(refs, cc, h, HG_SUB, True))
            return carry
        lax.fori_loop(0, n_chunks * HG_HEADS, body, 0)


def _log_sigmoid(x):
    return jnp.minimum(x, 0.0) - jnp.log1p(jnp.exp(-jnp.abs(x)))


def _ret_log_gamma(logit_ref, direction, h, shape):
    return _log_sigmoid(jnp.full(shape, logit_ref[direction, h], F32))


def _ret_bwd_kernel(logit_ref, k_ref, v_ref, sb_ref, s_ref):
    c = RET_BLOCK

    @pl.when(pl.program_id(0) == 0)
    def _():
        s_ref[...] = jnp.zeros_like(s_ref)

    t = lax.broadcasted_iota(jnp.int32, (c, RET_DK), 0).astype(F32)
    for h in range(RET_HEADS):
        lg = _ret_log_gamma(logit_ref, 1, h, (c, RET_DK))
        s = s_ref[h]
        sb_ref[0, h] = s.astype(BF16)
        kt = (k_ref[h].astype(F32) * jnp.exp(t * lg)).astype(BF16)
        v = jnp.concatenate([v_ref[2 * h], v_ref[2 * h + 1]], axis=1)
        s_ref[h] = s * jnp.exp(c * lg[0:1, 0:1]) + lax.dot_general(
            kt, v, TN_DIMS, preferred_element_type=F32)


def _ret_fwd_kernel(logit_ref, q_ref, k_ref, v_ref, gate_ref, sb_ref, gnw_ref,
                    o_ref, s_ref, dmat_ref):
    c = RET_BLOCK

    @pl.when(pl.program_id(0) == 0)
    def _():
        s_ref[...] = jnp.zeros_like(s_ref)
        ri = lax.broadcasted_iota(jnp.int32, (c, c), 0)
        ci = lax.broadcasted_iota(jnp.int32, (c, c), 1)
        dist = (ri - ci).astype(F32)
        for h in range(RET_HEADS):
            lgf = _ret_log_gamma(logit_ref, 0, h, (c, c))
            lgb = _ret_log_gamma(logit_ref, 1, h, (c, c))
            dmat_ref[h] = (jnp.where(ci <= ri, jnp.exp(jnp.maximum(dist, 0.0) * lgf), 0.0)
                           + jnp.where(ci >= ri, jnp.exp(jnp.maximum(-dist, 0.0) * lgb), 0.0))

    t = lax.broadcasted_iota(jnp.int32, (c, RET_DK), 0).astype(F32)
    for h in range(RET_HEADS):
        lgf = _ret_log_gamma(logit_ref, 0, h, (c, RET_DK))
        lgb = _ret_log_gamma(logit_ref, 1, h, (c, RET_DK))
        q = q_ref[h]
        k = k_ref[h]
        q32 = q.astype(F32)
        v = jnp.concatenate([v_ref[2 * h], v_ref[2 * h + 1]], axis=1)
        s = s_ref[h]
        sc = lax.dot_general(q, k, NT_DIMS, preferred_element_type=F32) * dmat_ref[h]
        o = jnp.dot(sc.astype(BF16), v, preferred_element_type=F32)
        o = o + jnp.dot((q32 * jnp.exp((t + 1.0) * lgf)).astype(BF16), s.astype(BF16),
                        preferred_element_type=F32)
        o = o + jnp.dot((q32 * jnp.exp((c - t) * lgb)).astype(BF16), sb_ref[0, h],
                        preferred_element_type=F32)
        kt = (k.astype(F32) * jnp.exp((c - 1.0 - t) * lgf)).astype(BF16)
        s_ref[h] = s * jnp.exp(c * lgf[0:1, 0:1]) + lax.dot_general(
            kt, v, TN_DIMS, preferred_element_type=F32)

        mu = jnp.mean(o, axis=-1, keepdims=True)
        dev = o - mu
        var = jnp.mean(dev * dev, axis=-1, keepdims=True)
        o = dev * lax.rsqrt(var + EPS) * gnw_ref[h]
        gate = jnp.concatenate([gate_ref[2 * h], gate_ref[2 * h + 1]], axis=1).astype(F32)
        o_ref[:, h * RET_DV:(h + 1) * RET_DV] = (o * gate).astype(BF16)


N_HG_FWD_IN = 8
N_RET_FWD_IN = 7


def _mixer_fwd_kernel(*refs, n_chunks):
    hg_in = refs[:N_HG_FWD_IN]
    ret_in = refs[N_HG_FWD_IN:N_HG_FWD_IN + N_RET_FWD_IN]
    hg_out, ret_out, hs_ref, k32_ref, b_ref, rs_ref, dmat_ref = refs[N_HG_FWD_IN + N_RET_FWD_IN:]
    _hg_fwd_kernel(*hg_in, hg_out, hs_ref, k32_ref, b_ref, n_chunks=n_chunks)
    _ret_fwd_kernel(*ret_in, ret_out, rs_ref, dmat_ref)


def _mixer_fwd(logit, p3, lf3, hsb, rsb, hg_norm_w, gn_w):
    l = p3.shape[1]
    tb = HG_BLOCK
    assert tb == RET_BLOCK
    nb = l // tb
    ncb = tb // HG_CHUNK
    slabs = lambda n, start: pl.BlockSpec((n, tb, LANE), lambda i: (start // n, i, 0))
    kern = functools.partial(_mixer_fwd_kernel, n_chunks=ncb)
    return pl.pallas_call(
        kern,
        grid=(nb,),
        in_specs=[slabs(HG_HEADS, S_HQ), slabs(HG_HEADS, S_KF), slabs(HG_HEADS, S_KB),
                  slabs(HG_HEADS, S_HI), slabs(HG_HEADS, S_HGATE),
                  pl.BlockSpec((2 * HG_HEADS, tb, LANE), lambda i: (0, i, 0)),
                  pl.BlockSpec((ncb, HG_HEADS, HG_DV, HG_DK), lambda i: (i, 0, 0, 0)),
                  pl.BlockSpec((HG_HEADS, 1, HG_DV), lambda i: (0, 0, 0)),
                  pl.BlockSpec(memory_space=pltpu.SMEM),
                  slabs(RET_HEADS, S_RQ), slabs(RET_HEADS, S_RK),
                  slabs(2 * RET_HEADS, S_RV), slabs(2 * RET_HEADS, S_RG),
                  pl.BlockSpec((1, RET_HEADS, RET_DK, RET_DV), lambda i: (i, 0, 0, 0)),
                  pl.BlockSpec((RET_HEADS, 1, RET_DV), lambda i: (0, 0, 0))],
        out_specs=[pl.BlockSpec((HG_HEADS, tb, LANE), lambda i: (0, i, 0)),
                   pl.BlockSpec((tb, RET_HEADS * RET_DV), lambda i: (i, 0))],
        out_shape=[jax.ShapeDtypeStruct((HG_HEADS, l, LANE), BF16),
                   jax.ShapeDtypeStruct((l, RET_HEADS * RET_DV), BF16)],
        scratch_shapes=[pltpu.VMEM((HG_HEADS, HG_DV, HG_DK), F32),
                        pltpu.VMEM((HG_CHUNK, HG_DK), F32),
                        pltpu.VMEM((HG_CHUNK, HG_DK), F32),
                        pltpu.VMEM((RET_HEADS, RET_DK, RET_DV), F32),
                        pltpu.VMEM((RET_HEADS, tb, tb), F32)],
        compiler_params=_params(("arbitrary",)),
        name="mixer_fwd",
    )(p3, p3, p3, p3, p3, lf3, hsb, hg_norm_w.reshape(HG_HEADS, 1, HG_DV),
      logit, p3, p3, p3, p3, rsb, gn_w.reshape(RET_HEADS, 1, RET_DV))


def _pack_bf16_pairs(v):
    w = v.shape[1] // 2
    lo = pltpu.bitcast(v[:, :w].astype(BF16).astype(F32), U32)
    hi = pltpu.bitcast(v[:, w:].astype(BF16).astype(F32), U32)
    return (lo >> 16) | (hi & jnp.uint32(0xFFFF0000))


def _unpack_bf16_pairs(u):
    lo = pltpu.bitcast(u << 16, F32)
    hi = pltpu.bitcast(u & jnp.uint32(0xFFFF0000), F32)
    return lo, hi


def _store_planes(ref, rows, packed):
    for c in range(ref.shape[0]):
        ref[c, rows, :] = packed[:, c * LANE:(c + 1) * LANE]


def _load_planes(ref, rows):
    return jnp.concatenate([ref[c, rows, :] for c in range(ref.shape[0])], axis=1)


def _copy_row(dst_ref, dst_row, src_ref, src_row):
    for c in range(src_ref.shape[0]):
        dst_ref[c, pl.ds(dst_row, 1), :] = src_ref[c, pl.ds(src_row, 1), :]


def _route(logits):
    lane = lax.broadcasted_iota(jnp.int32, logits.shape, 1)
    big = jnp.int32(10 ** 6)
    neg = -jnp.inf
    gl = jnp.where(lane < N_GROUPS, logits, neg)
    gmax = jnp.max(gl, axis=1, keepdims=True)
    grp = jnp.min(jnp.where(gl == gmax, lane, big), axis=1, keepdims=True)
    g_val = 1.0 / jnp.sum(jnp.exp(gl - gmax), axis=1, keepdims=True)
    lo = N_GROUPS + EXPERTS_PER_GROUP * grp
    el = jnp.where((lane >= lo) & (lane < lo + EXPERTS_PER_GROUP), logits, neg)
    v1 = jnp.max(el, axis=1, keepdims=True)
    i1 = jnp.min(jnp.where(el == v1, lane, big), axis=1, keepdims=True)
    el2 = jnp.where(lane == i1, neg, el)
    v2 = jnp.max(el2, axis=1, keepdims=True)
    i2 = jnp.min(jnp.where(el2 == v2, lane, big), axis=1, keepdims=True)
    r = jnp.exp(v2 - v1)
    w1 = g_val / (1.0 + r)
    w2 = w1 * r
    return ((i1 - N_GROUPS).astype(F32), (i2 - N_GROUPS).astype(F32), w1, w2)


def _merge_kernel(ohg_ref, oret_ref, ga_ref, gb_ref, x_ref, mods_ref, wohg_ref, woret_ref,
                  wout_ref, nw_ref, wrh_ref, wrl_ref, rb_ref,
                  xo_ref, hp_ref, route_ref, route_t_ref, cnt_out_ref, cnt_ref,
                  *, lc, tm, d, tiles_per_seg):
    i = pl.program_id(0)
    tg = tm // MERGE_GROUPS
    routed = []
    for g in range(MERGE_GROUPS):
        rows = slice(g * tg, (g + 1) * tg)
        row0 = i * tm + g * tg
        ohg = jnp.concatenate([ohg_ref[s, rows, :] for s in range(HG_HEADS)], axis=1)
        ga = jnp.concatenate([ga_ref[s, rows, :] for s in range(d // LANE)], axis=1).astype(F32)
        gb = jnp.concatenate([gb_ref[s, rows, :] for s in range(d // LANE)], axis=1).astype(F32)
        y_hg = jnp.dot(ohg, wohg_ref[...], preferred_element_type=F32)
        y_ret = jnp.dot(oret_ref[rows, :], woret_ref[...], preferred_element_type=F32)
        m = (ga * y_hg + gb * y_ret).astype(BF16)
        y = jnp.dot(m, wout_ref[...], preferred_element_type=F32)
        x = x_ref[rows, :] + _row_select(row0, tg, lc, mods_ref, 2, d) * y
        xo_ref[rows, :] = x

        xn = x * lax.rsqrt(jnp.mean(x * x, axis=-1, keepdims=True) + EPS) * nw_ref[...]
        h = (xn * (1.0 + _row_select(row0, tg, lc, mods_ref, 4, d))
             + _row_select(row0, tg, lc, mods_ref, 3, d))
        _store_planes(hp_ref, rows, _pack_bf16_pairs(h))

        h_hi = h.astype(BF16)
        h_lo = (h - h_hi.astype(F32)).astype(BF16)
        logits = (jnp.dot(h_hi, wrh_ref[...], preferred_element_type=F32)
                  + jnp.dot(h_lo, wrh_ref[...], preferred_element_type=F32)
                  + jnp.dot(h_hi, wrl_ref[...], preferred_element_type=F32)) + rb_ref[...]
        routed.append(_route(logits))
    e0, e1, w0, w1 = (jnp.concatenate(parts, axis=0) for parts in zip(*routed))

    @pl.when(i % tiles_per_seg == 0)
    def _():
        cnt_ref[...] = jnp.zeros_like(cnt_ref)

    lane_e = lax.broadcasted_iota(jnp.int32, (tm, LANE), 1).astype(F32)
    hot0 = lane_e == e0
    hot1 = lane_e == e1
    hot = jnp.where(hot0 | hot1, 1.0, 0.0)
    earlier = (lax.broadcasted_iota(jnp.int32, (tm, tm), 1)
               < lax.broadcasted_iota(jnp.int32, (tm, tm), 0))
    before = jnp.dot(jnp.where(earlier, 1.0, 0.0).astype(BF16), hot.astype(BF16),
                     preferred_element_type=F32) + cnt_ref[0:1, :]
    rank0 = jnp.sum(jnp.where(hot0, before, 0.0), axis=1, keepdims=True)
    rank1 = jnp.sum(jnp.where(hot1, before, 0.0), axis=1, keepdims=True)
    cnt_ref[...] = cnt_ref[...] + jnp.sum(hot, axis=0, keepdims=True)
    cnt_out_ref[0] = cnt_ref[...]

    lane = lax.broadcasted_iota(jnp.int32, (tm, LANE), 1)
    rec = jnp.zeros((tm, LANE), F32)
    for k, val in enumerate((e0, e1, w0, w1, rank0, rank1)):
        rec = jnp.where(lane == k, val, rec)
    route_ref[...] = rec[:, :ROUTE_W]
    eye = jnp.where(lax.broadcasted_iota(jnp.int32, (ROUTE_W, LANE), 0)
                    == lax.broadcasted_iota(jnp.int32, (ROUTE_W, LANE), 1), 1.0, 0.0)
    route_t_ref[0] = lax.dot_general(eye, rec, NT_DIMS, precision=lax.Precision.HIGHEST,
                                     preferred_element_type=F32)


def _merge(ohg, oret, p3, xc, mods, wohg, woret, wout, norm_w, wr_hi, wr_lo, rbias, *, lc, n_seg):
    l, d = xc.shape
    seg_tokens = l // n_seg
    tm = _largest_divisor(seg_tokens, MERGE_TILE, 16 * MERGE_GROUPS)
    tiles_per_seg = seg_tokens // tm
    n_slab = d // LANE
    full = lambda a: pl.BlockSpec(a.shape, lambda i: (0,) * a.ndim)
    kern = functools.partial(_merge_kernel, lc=lc, tm=tm, d=d, tiles_per_seg=tiles_per_seg)
    return pl.pallas_call(
        kern,
        grid=(l // tm,),
        in_specs=[pl.BlockSpec((HG_HEADS, tm, LANE), lambda i: (0, i, 0)),
                  pl.BlockSpec((tm, oret.shape[1]), lambda i: (i, 0)),
                  pl.BlockSpec((n_slab, tm, LANE), lambda i: (S_GA // n_slab, i, 0)),
                  pl.BlockSpec((n_slab, tm, LANE), lambda i: (S_GB // n_slab, i, 0)),
                  pl.BlockSpec((tm, d), lambda i: (i, 0)),
                  full(mods), full(wohg), full(woret), full(wout),
                  pl.BlockSpec((1, d), lambda i: (0, 0)),
                  full(wr_hi), full(wr_lo), full(rbias)],
        out_specs=[pl.BlockSpec((tm, d), lambda i: (i, 0)),
                   pl.BlockSpec((d // 2 // LANE, tm, LANE), lambda i: (0, i, 0)),
                   pl.BlockSpec((tm, ROUTE_W), lambda i: (i, 0)),
                   pl.BlockSpec((1, ROUTE_W, tm), lambda i: (i, 0, 0)),
                   pl.BlockSpec((1, 8, LANE), lambda i: (i // tiles_per_seg, 0, 0))],
        out_shape=[jax.ShapeDtypeStruct((l, d), F32),
                   jax.ShapeDtypeStruct((d // 2 // LANE, l, LANE), U32),
                   jax.ShapeDtypeStruct((l, ROUTE_W), F32),
                   jax.ShapeDtypeStruct((l // tm, ROUTE_W, tm), F32),
                   jax.ShapeDtypeStruct((n_seg, 8, LANE), F32)],
        scratch_shapes=[pltpu.VMEM((8, LANE), F32)],
        compiler_params=_params(("arbitrary",)),
        name="merge_router",
    )(ohg, oret, p3, p3, xc, mods, wohg, woret, wout, norm_w.reshape(1, d), wr_hi, wr_lo, rbias)


def _moe_kernel(be_ref, nused_ref, slot_ref, hp_ref, wg_ref, wu_ref, wd_ref, route_ref, x_ref,
                mods_ref, fw_ref, o_ref, xs_ref, g0_ref, g1_ref,
                *, lc, tm, d, nbs, tiles_per_seg, seg_tokens, n_tokens, final):
    s = pl.program_id(0)
    b = pl.program_id(1)

    @pl.when(b == 0)
    def _():
        xs_ref[...] = jnp.zeros_like(xs_ref)

        def scatter(g, a0):
            t0 = pl.multiple_of(g * ROW_GROUP, ROW_GROUP)
            for k in range(ROW_GROUP):
                _copy_row(xs_ref, slot_ref[a0 + k], hp_ref, t0 + k)
                _copy_row(xs_ref, slot_ref[n_tokens + a0 + k], hp_ref, t0 + k)
            return a0 + ROW_GROUP

        lax.fori_loop(0, seg_tokens // ROW_GROUP, scatter, s * seg_tokens)

    @pl.when(b < nused_ref[s])
    def _():
        rows = pl.ds(pl.multiple_of(b * MOE_BLOCK, MOE_BLOCK), MOE_BLOCK)
        lo, hi = _unpack_bf16_pairs(_load_planes(xs_ref, rows))
        x = jnp.concatenate([lo, hi], axis=1).astype(BF16)
        g = jnp.dot(x, wg_ref[0], preferred_element_type=F32)
        u = jnp.dot(x, wu_ref[0], preferred_element_type=F32)
        a = (_silu(g) * u).astype(BF16)
        _store_planes(xs_ref, rows,
                      _pack_bf16_pairs(jnp.dot(a, wd_ref[0], preferred_element_type=F32)))

    @pl.when(b >= nbs)
    def _():
        tile0 = s * seg_tokens + (b - nbs) * tm

        def gather(g, a0):
            r0 = pl.multiple_of(g * ROW_GROUP, ROW_GROUP)
            for k in range(ROW_GROUP):
                _copy_row(g0_ref, r0 + k, xs_ref, slot_ref[a0 + k])
                _copy_row(g1_ref, r0 + k, xs_ref, slot_ref[n_tokens + a0 + k])
            return a0 + ROW_GROUP

        lax.fori_loop(0, tm // ROW_GROUP, gather, tile0)
        lo0, hi0 = _unpack_bf16_pairs(_load_planes(g0_ref, slice(None)))
        lo1, hi1 = _unpack_bf16_pairs(_load_planes(g1_ref, slice(None)))
        w0 = route_ref[:, 2:3]
        w1 = route_ref[:, 3:4]
        y = jnp.concatenate([w0 * lo0 + w1 * lo1, w0 * hi0 + w1 * hi1], axis=1)
        x = x_ref[...] + _row_select(tile0, tm, lc, mods_ref, 5, d) * y
        if final:
            x = x * lax.rsqrt(jnp.mean(x * x, axis=-1, keepdims=True) + EPS) * fw_ref[...]
        o_ref[...] = x


def _moe(block_expert, n_used, slot, hp, wg, wu, wd, route, xc, mods, final_w,
         *, n_seg, nbs, layer, lc, final):
    l, d = xc.shape
    n_planes = hp.shape[0]
    seg_tokens = l // n_seg
    d_e = wg.shape[3]
    tm = _largest_divisor(seg_tokens, 320, 8)
    tiles = seg_tokens // tm
    kern = functools.partial(_moe_kernel, lc=lc, tm=tm, d=d, nbs=nbs, tiles_per_seg=tiles,
                             seg_tokens=seg_tokens, n_tokens=l, final=final)
    w_idx = lambda s, b, be, nu, *_: (
        layer, be[s * nbs + jnp.minimum(b, jnp.maximum(nu[s] - 1, 0))], 0, 0)
    tile = lambda s, b: s * tiles + jnp.clip(b - nbs, 0, tiles - 1)
    row = lambda s, b, *_: (tile(s, b), 0)
    if final:
        assert lc % tm == 0
        out_rows = l - lc
        out_row = lambda s, b, *_: (jnp.maximum(tile(s, b) - lc // tm, 0), 0)
    else:
        out_rows, out_row = l, row
    grid_spec = pltpu.PrefetchScalarGridSpec(
        num_scalar_prefetch=3,
        grid=(n_seg, nbs + tiles),
        in_specs=[pl.BlockSpec((n_planes, seg_tokens, LANE), lambda s, b, *_: (0, s, 0)),
                  pl.BlockSpec((None, 1, d, d_e), w_idx),
                  pl.BlockSpec((None, 1, d, d_e), w_idx),
                  pl.BlockSpec((None, 1, d_e, d), w_idx),
                  pl.BlockSpec((tm, ROUTE_W), row),
                  pl.BlockSpec((tm, d), row),
                  pl.BlockSpec(mods.shape, lambda s, b, *_: (0, 0)),
                  pl.BlockSpec((1, d), lambda s, b, *_: (0, 0))],
        out_specs=pl.BlockSpec((tm, d), out_row),
        scratch_shapes=[pltpu.VMEM((n_planes, nbs * MOE_BLOCK, LANE), U32),
                        pltpu.VMEM((n_planes, tm, LANE), U32),
                        pltpu.VMEM((n_planes, tm, LANE), U32)])
    return pl.pallas_call(
        kern,
        grid_spec=grid_spec,
        out_shape=jax.ShapeDtypeStruct((out_rows, d), F32),
        compiler_params=_params(("arbitrary", "arbitrary")),
        name="moe",
    )(block_expert, n_used, slot, hp, wg, wu, wd, route, xc, mods, final_w.reshape(1, d))


def _dispatch_plan(route_t, counts, nbs):
    n_seg = counts.shape[0]
    counts = counts[:, 0, :N_EXPERTS].astype(jnp.int32)
    padded = ((counts + MOE_BLOCK - 1) // MOE_BLOCK) * MOE_BLOCK
    pad_end = jnp.cumsum(padded, axis=1)
    pad_start = pad_end - padded
    starts = jnp.arange(nbs, dtype=jnp.int32) * MOE_BLOCK
    block_expert = jnp.minimum(
        jnp.sum((pad_end[:, None, :] <= starts[None, :, None]).astype(jnp.int32), axis=2),
        N_EXPERTS - 1)
    n_used = pad_end[:, -1] // MOE_BLOCK
    n_tiles, _, tm = route_t.shape
    tile_start = jnp.repeat(pad_start, n_tiles // n_seg, axis=0)
    slots = []
    for k in range(2):
        e = route_t[:, k, :].astype(jnp.int32)
        rank = route_t[:, 4 + k, :].astype(jnp.int32)
        start = jnp.zeros_like(rank)
        for x in range(N_EXPERTS):
            start = jnp.where(e == x, tile_start[:, x:x + 1], start)
        slots.append((rank + start).reshape(-1))
    return (jnp.concatenate(slots), block_expert.reshape(-1).astype(jnp.int32),
            n_used.astype(jnp.int32))


def _rope_tables(lc, t):
    quarter = RET_DK // 4
    n_rows = t // GRID_W
    inv = ROPE_BASE ** (-np.arange(0, 2 * quarter, 2, dtype=np.float64) / (2 * quarter))
    ang_r = np.arange(n_rows, dtype=np.float64)[:, None] * inv
    ang_c = np.arange(GRID_W, dtype=np.float64)[:, None] * inv

    def expand(row_part, col_part):
        r = np.broadcast_to(row_part[:, None, :], (n_rows, GRID_W, 2 * quarter))
        c = np.broadcast_to(col_part[None, :, :], (n_rows, GRID_W, 2 * quarter))
        return np.concatenate([r, c], axis=2).reshape(t, RET_DK)

    two = lambda a: np.concatenate([a, a], axis=1)
    cos = expand(two(np.cos(ang_r)), two(np.cos(ang_c)))
    sin = expand(np.concatenate([-np.sin(ang_r), np.sin(ang_r)], axis=1),
                 np.concatenate([-np.sin(ang_c), np.sin(ang_c)], axis=1))
    cos = np.concatenate([np.ones((lc, RET_DK)), cos], axis=0).astype(np.float32)
    sin = np.concatenate([np.zeros((lc, RET_DK)), sin], axis=0).astype(np.float32)
    return jnp.asarray(cos), jnp.asarray(sin)


def kernel(x, c, ctx, c_ctx, w_mod, b_mod, norm_mix_w, norm_ffn_w, w_in, hgrn_lb_logits, hgrn_norm_w,
           ret_decay_logit, ret_gn_w, w_o_hgrn, w_o_ret, w_out, router_group_w, router_group_b,
           router_expert_w, router_expert_b, expert_w_gate, expert_w_up, expert_w_down, final_norm_w):
    b_, t_, d = x.shape
    assert b_ == 1
    lc = ctx.shape[1]
    depth = w_mod.shape[0]
    l = lc + t_
    assert lc % HG_BLOCK == 0 and l % HG_BLOCK == 0 and lc % RET_BLOCK == 0 and l % RET_BLOCK == 0
    n_seg = MOE_SEGMENTS
    assert l % n_seg == 0
    seg_tokens = l // n_seg
    nbs = pl.cdiv(2 * seg_tokens + N_EXPERTS * (MOE_BLOCK - 1), MOE_BLOCK)

    xc = jnp.concatenate([ctx[0], x[0]], axis=0)
    cc = jnp.zeros((8, d), F32).at[0].set(c[0]).at[1].set(c_ctx)
    mods_all = _modulation(cc, w_mod, b_mod)
    cos_t, sin_t = _rope_tables(lc, t_)
    w_in_bf = w_in.astype(BF16)
    w_gate_bf, w_up_bf, w_down_bf = (w.astype(BF16)
                                     for w in (expert_w_gate, expert_w_up, expert_w_down))

    for layer in range(depth):
        mods = mods_all[layer]
        p3, lf3 = _inproj(xc, mods, norm_mix_w[layer], w_in_bf, hgrn_lb_logits,
                          cos_t, sin_t, layer=layer, lc=lc)
        sb_hg, sb_ret = _mixer_bwd(ret_decay_logit[layer], p3, lf3, lc=lc)
        ohg, oret = _mixer_fwd(ret_decay_logit[layer], p3, lf3, sb_hg, sb_ret,
                               hgrn_norm_w[layer], ret_gn_w[layer])

        wr = jnp.concatenate([router_group_w[layer], router_expert_w[layer]], axis=1)
        wr = jnp.pad(wr, ((0, 0), (0, LANE - wr.shape[1])))
        wr_hi = wr.astype(BF16)
        wr_lo = (wr - wr_hi.astype(F32)).astype(BF16)
        rbias = jnp.pad(jnp.concatenate([router_group_b[layer], router_expert_b[layer]]),
                        (0, LANE - N_GROUPS - N_EXPERTS)).reshape(1, LANE)
        xc, hp, route, route_t, counts = _merge(
            ohg, oret, p3, xc, mods, w_o_hgrn[layer].astype(BF16), w_o_ret[layer].astype(BF16),
            w_out[layer].astype(BF16), norm_ffn_w[layer], wr_hi, wr_lo, rbias, lc=lc, n_seg=n_seg)

        slot, block_expert, n_used = _dispatch_plan(route_t, counts, nbs)
        xc = _moe(block_expert, n_used, slot, hp, w_gate_bf, w_up_bf, w_down_bf, route, xc, mods,
                  final_norm_w, n_seg=n_seg, nbs=nbs, layer=layer, lc=lc,
                  final=(layer == depth - 1))

    return xc[None]
```

```python
import functools

import jax
import jax.numpy as jnp
import numpy as np
from jax import lax
from jax.experimental import pallas as pl
from jax.experimental.pallas import tpu as pltpu

F32 = jnp.float32
BF16 = jnp.bfloat16
U32 = jnp.uint32

GRID_W = 64
HG_HEADS = 4
HG_DK = 128
HG_DV = 128
F_MIN = 1e-30
RET_HEADS = 4
RET_DK = 128
RET_DV = 256
ROPE_BASE = 10000.0
N_GROUPS = 4
EXPERTS_PER_GROUP = 8
N_EXPERTS = N_GROUPS * EXPERTS_PER_GROUP
N_MOD = 6
EPS = 1e-6

LANE = 128
COL_TILE = 512
SLABS_PER_TILE = COL_TILE // LANE
HG_CHUNK = 64
HG_SUB = 16
HG_BLOCK = 256
LOG2E = 1.4426950408889634
HG_FAST_SUB = 32
HG_CAP = 100.0
CUM_GROUP = 256
CUM_TERMS = 2
RET_BLOCK = 256
MOE_SEGMENTS = 5
MOE_BLOCK = 256
MOE_PER_STEP = 2
ROW_GROUP = 16
MERGE_GROUPS = 1
MERGE_TILE = 832
ROUTE_W = 8
VMEM_LIMIT = 56 * 1024 * 1024

NT_DIMS = (((1,), (1,)), ((), ()))
TN_DIMS = (((0,), (0,)), ((), ()))

SEG_SLABS = (4, 4, 4, 4, 4, 4, 4, 8, 8, 8, 8)
SEG_ORDER = (7, 8, 9, 10, 0, 1, 2, 3, 4, 5, 6)
_starts = {}
_pos = 0
for _seg in SEG_ORDER:
    _starts[_seg] = _pos
    _pos += SEG_SLABS[_seg]
SEG_START = tuple(_starts[_seg] for _seg in range(len(SEG_SLABS)))
(S_HQ, S_KF, S_KB, S_HI, S_HGATE, S_RQ, S_RK, S_RV, S_RG, S_GA, S_GB) = SEG_START
assert SEG_ORDER == tuple(range(SEG_ORDER[0], len(SEG_SLABS))) + tuple(range(SEG_ORDER[0]))
TILE_ROTATION = sum(SEG_SLABS[:SEG_ORDER[0]]) // SLABS_PER_TILE
N_SLABS = sum(SEG_SLABS)


def _params(sem):
    return pltpu.CompilerParams(dimension_semantics=sem, vmem_limit_bytes=VMEM_LIMIT)


def _sigmoid(x):
    return 1.0 / (1.0 + jnp.exp(-x))


def _silu(x):
    return x * _sigmoid(x)


def _largest_divisor(n, cap, multiple):
    best = None
    for d in range(multiple, cap + 1, multiple):
        if n % d == 0:
            best = d
    assert best is not None, (n, cap, multiple)
    return best


def _row_select(row0, n_rows, lc, mods_ref, k, d):
    rows = row0 + lax.broadcasted_iota(jnp.int32, (n_rows, 1), 0)
    lat = mods_ref[0:1, k * d:(k + 1) * d]
    ctx = mods_ref[1:2, k * d:(k + 1) * d]
    return jnp.where(rows < lc, ctx, lat)


def _mod_kernel(cc_ref, w_ref, b_ref, o_ref):
    cc = cc_ref[...]
    s = _silu(cc)
    o_ref[0] = jnp.dot(s, w_ref[0], precision=lax.Precision.HIGHEST,
                       preferred_element_type=F32) + b_ref[0]


def _modulation(cc, w_mod, b_mod):
    depth, d, n = w_mod.shape
    tn = _largest_divisor(n, 1536, LANE)
    return pl.pallas_call(
        _mod_kernel,
        grid=(depth, n // tn),
        in_specs=[pl.BlockSpec((8, d), lambda l, j: (0, 0)),
                  pl.BlockSpec((1, d, tn), lambda l, j: (l, 0, j)),
                  pl.BlockSpec((1, 1, tn), lambda l, j: (l, 0, j))],
        out_specs=pl.BlockSpec((1, 8, tn), lambda l, j: (l, 0, j)),
        out_shape=jax.ShapeDtypeStruct((depth, 8, n), F32),
        compiler_params=_params(("arbitrary", "arbitrary")),
        name="modulation",
    )(cc, w_mod, b_mod.reshape(depth, 1, n))


def _inproj_kernel(x_ref, mods_ref, nw_ref, w_ref, lbl_ref, cos_ref, sin_ref,
                   p_ref, lf_ref, h_ref, *, layer, lc, tm, d):
    i = pl.program_id(0)
    j = pl.program_id(1)

    @pl.when(j == 0)
    def _():
        x = x_ref[...]
        xn = x * lax.rsqrt(jnp.mean(x * x, axis=-1, keepdims=True) + EPS) * nw_ref[...]
        shift = _row_select(i * tm, tm, lc, mods_ref, 0, d)
        scale = _row_select(i * tm, tm, lc, mods_ref, 1, d)
        h_ref[...] = (xn * (1.0 + scale) + shift).astype(BF16)

    def for_row_groups(epilogue):
        for r0 in range(0, tm, CUM_GROUP):
            rows = slice(r0, r0 + CUM_GROUP)
            acc = jnp.dot(h_ref[rows, :], w_ref[...], preferred_element_type=F32)
            epilogue(acc, rows)

    def put(val, rows):
        v = val.astype(BF16)
        for s in range(SLABS_PER_TILE):
            p_ref[s, rows, :] = v[:, s * LANE:(s + 1) * LANE]

    tiles = lambda seg: tuple(range(SEG_START[seg] // SLABS_PER_TILE,
                                    (SEG_START[seg] + SEG_SLABS[seg]) // SLABS_PER_TILE))
    in_tiles = lambda segs: functools.reduce(
        jnp.logical_or, [j == t for seg in segs for t in tiles(seg)])

    @pl.when(in_tiles((0,)))
    def _():
        for_row_groups(lambda acc, rows: put(_silu(acc) * (HG_DK ** -0.5), rows))

    @pl.when(in_tiles((1, 2)))
    def _():
        logits = lbl_ref[jnp.clip(j - tiles(1)[0], 0, 1)]
        e = jnp.exp(logits - jnp.max(logits, axis=0, keepdims=True))
        p = e / jnp.sum(e, axis=0, keepdims=True)
        lb = jnp.zeros_like(p[0:1])
        for r in range(1, layer + 1):
            lb = lb + p[r:r + 1]
        lb = jnp.clip(lb, 0.0, 1.0 - 1e-6)
        g = CUM_GROUP
        ri = lax.broadcasted_iota(jnp.int32, (g, g), 0)
        ci = lax.broadcasted_iota(jnp.int32, (g, g), 1)
        same_chunk = (ri // HG_CHUNK) == (ci // HG_CHUNK)
        before = jnp.where(same_chunk & (ci <= ri), 1.0, 0.0)
        after = jnp.where(same_chunk & (ci >= ri), 1.0, 0.0)
        tri = jnp.where(j == tiles(1)[0], before, after).astype(BF16)

        log_f_terms = []

        def epilogue(acc, rows):
            sig = _sigmoid(acc)
            put((1.0 - lb) * (1.0 - sig), rows)
            rest = jnp.log(jnp.maximum(lb + (1.0 - lb) * sig, F_MIN))
            terms = []
            for _ in range(CUM_TERMS):
                terms.append(rest.astype(BF16))
                rest = rest - terms[-1].astype(F32)
            log_f_terms.append((rows, terms))

        for_row_groups(epilogue)
        for rows, terms in log_f_terms:
            cum = jnp.zeros((g, COL_TILE), F32)
            for term in terms:
                cum = cum + jnp.dot(tri, term, preferred_element_type=F32)
            cum = cum * LOG2E
            for s in range(SLABS_PER_TILE):
                lf_ref[s, rows, :] = cum[:, s * LANE:(s + 1) * LANE]

    @pl.when(in_tiles((3, 7)))
    def _():
        for_row_groups(put)

    @pl.when(in_tiles((4, 8)))
    def _():
        for_row_groups(lambda acc, rows: put(_silu(acc), rows))

    @pl.when(in_tiles((5, 6)))
    def _():
        scale = jnp.where(j == tiles(5)[0], RET_DK ** -0.5, 1.0)

        def epilogue(acc, rows):
            xq = acc * scale
            n = xq.shape[1]
            lane = lax.broadcasted_iota(jnp.int32, xq.shape, 1)
            partner = jnp.where((lane & 32) == 0, pltpu.roll(xq, n - 32, axis=1),
                                pltpu.roll(xq, 32, axis=1))
            cos = jnp.concatenate([cos_ref[rows, :]] * SLABS_PER_TILE, axis=1)
            sin = jnp.concatenate([sin_ref[rows, :]] * SLABS_PER_TILE, axis=1)
            put(xq * cos + partner * sin, rows)

        for_row_groups(epilogue)

    @pl.when(in_tiles((9, 10)))
    def _():
        for_row_groups(lambda acc, rows: put(_sigmoid(acc), rows))


def _inproj(xc, mods, norm_w, w_in_bf, lb_logits, cos_t, sin_t, *, layer, lc):
    l, d = xc.shape
    d_in = w_in_bf.shape[2]
    assert d_in == N_SLABS * LANE
    tm = _largest_divisor(l, 1280, CUM_GROUP)
    n_col = d_in // COL_TILE
    kf_tile = S_KF // SLABS_PER_TILE
    kern = functools.partial(_inproj_kernel, layer=layer, lc=lc, tm=tm, d=d)
    return pl.pallas_call(
        kern,
        grid=(l // tm, n_col),
        in_specs=[pl.BlockSpec((tm, d), lambda i, j: (i, 0)),
                  pl.BlockSpec(mods.shape, lambda i, j: (0, 0)),
                  pl.BlockSpec((1, d), lambda i, j: (0, 0)),
                  pl.BlockSpec((None, d, COL_TILE),
                               lambda i, j: (layer, 0, (j + TILE_ROTATION) % n_col)),
                  pl.BlockSpec(lb_logits.shape, lambda i, j: (0, 0, 0)),
                  pl.BlockSpec((tm, LANE), lambda i, j: (i, 0)),
                  pl.BlockSpec((tm, LANE), lambda i, j: (i, 0))],
        out_specs=[pl.BlockSpec((SLABS_PER_TILE, tm, LANE), lambda i, j: (j, i, 0)),
                   pl.BlockSpec((SLABS_PER_TILE, tm, LANE),
                                lambda i, j: (jnp.clip(j - kf_tile, 0, 1), i, 0))],
        out_shape=[jax.ShapeDtypeStruct((N_SLABS, l, LANE), BF16),
                   jax.ShapeDtypeStruct((2 * HG_HEADS, l, LANE), F32)],
        scratch_shapes=[pltpu.VMEM((tm, d), BF16)],
        compiler_params=_params(("arbitrary", "arbitrary")),
        name="inproj",
    )(xc, mods, norm_w.reshape(1, d), w_in_bf, lb_logits, cos_t, sin_t)


def _hg_bwd_kernel(kb_ref, v_ref, bc_ref, sb_ref, s_ref, *, n_chunks):
    @pl.when(pl.program_id(0) == 0)
    def _():
        s_ref[...] = jnp.zeros_like(s_ref)

    for h in range(HG_HEADS):
        s = s_ref[h]
        for cc in reversed(range(n_chunks)):
            rows = slice(cc * HG_CHUNK, (cc + 1) * HG_CHUNK)
            bb = bc_ref[h, rows, :]
            k = kb_ref[h, rows, :].astype(F32)
            sb_ref[cc, h] = s.astype(BF16)
            kt = (k * jnp.exp2(bb[0:1, :] - bb)).astype(BF16)
            s = s * jnp.exp2(bb[0:1, :]) + lax.dot_general(
                v_ref[h, rows, :], kt, TN_DIMS, preferred_element_type=F32)
        s_ref[h] = s


def _bwd_block_order(i, n_ctx_blocks, n_blocks):
    return jnp.where(i < n_ctx_blocks, n_ctx_blocks - 1 - i, n_blocks - 1 - (i - n_ctx_blocks))


def _mixer_bwd_kernel(logit_ref, kb_ref, hv_ref, bc_ref, rk_ref, rv_ref,
                      hsb_ref, rsb_ref, hs_ref, rs_ref, *, n_chunks):
    _hg_bwd_kernel(kb_ref, hv_ref, bc_ref, hsb_ref, hs_ref, n_chunks=n_chunks)
    _ret_bwd_kernel(logit_ref, rk_ref, rv_ref, rsb_ref, rs_ref)


def _mixer_bwd(logit, p3, lf3, *, lc):
    l = p3.shape[1]
    tb = HG_BLOCK
    assert tb == RET_BLOCK
    nb, nbc = l // tb, lc // tb
    ncb = tb // HG_CHUNK
    order = lambda i: _bwd_block_order(i, nbc, nb)
    slabs = lambda n, start: pl.BlockSpec((n, tb, LANE), lambda i: (start // n, order(i), 0))
    kern = functools.partial(_mixer_bwd_kernel, n_chunks=ncb)
    return pl.pallas_call(
        kern,
        grid=(nb,),
        in_specs=[pl.BlockSpec(memory_space=pltpu.SMEM),
                  slabs(HG_HEADS, S_KB), slabs(HG_HEADS, S_HI),
                  pl.BlockSpec((HG_HEADS, tb, LANE), lambda i: (1, order(i), 0)),
                  slabs(RET_HEADS, S_RK), slabs(2 * RET_HEADS, S_RV)],
        out_specs=[pl.BlockSpec((ncb, HG_HEADS, HG_DV, HG_DK), lambda i: (order(i), 0, 0, 0)),
                   pl.BlockSpec((1, RET_HEADS, RET_DK, RET_DV), lambda i: (order(i), 0, 0, 0))],
        out_shape=[jax.ShapeDtypeStruct((l // HG_CHUNK, HG_HEADS, HG_DV, HG_DK), BF16),
                   jax.ShapeDtypeStruct((nb, RET_HEADS, RET_DK, RET_DV), BF16)],
        scratch_shapes=[pltpu.VMEM((HG_HEADS, HG_DV, HG_DK), F32),
                        pltpu.VMEM((RET_HEADS, RET_DK, RET_DV), F32)],
        compiler_params=_params(("arbitrary",)),
        name="mixer_bwd_state",
    )(logit, p3, p3, lf3, p3, p3)


def _hg_scores(q, q_edge, k32, b, fwd, sub, exact, k32_ref=None, b_ref=None):
    c = HG_CHUNK
    n_sub = c // sub
    cap = 0.0 if exact else HG_CAP
    lane = lax.broadcasted_iota(jnp.int32, (sub, c), 1)
    blocks = []
    for blk in range(n_sub):
        r = blk * sub
        q_blk = q[r:r + sub]
        b_blk = b[r:r + sub]
        edge = blk == 0 if fwd else blk == n_sub - 1
        if edge and exact:
            a = jnp.zeros((sub, c), F32)
        else:
            if edge:
                qt = q_edge[r:r + sub]
                ref_minus_b = -b
            else:
                ref = b[r - 1:r] if fwd else b[r + sub:r + sub + 1]
                qt = q_blk * jnp.exp2(b_blk - ref)
                ref_minus_b = ref - b
            kt = (k32 * jnp.exp2(jnp.minimum(ref_minus_b, cap))).astype(BF16)
            a = lax.dot_general(qt.astype(BF16), kt, NT_DIMS, preferred_element_type=F32)
        if exact:
            for jj in range(sub):
                s = r + jj
                e = jnp.exp2(b_blk - b_ref[s:s + 1, :])
                col = jnp.sum(q_blk * k32_ref[s:s + 1, :] * e, axis=1, keepdims=True)
                a = jnp.where(lane == s, col, a)
        blocks.append(a)
    a = jnp.concatenate(blocks, axis=0)
    ri = lax.broadcasted_iota(jnp.int32, (c, c), 0)
    ci = lax.broadcasted_iota(jnp.int32, (c, c), 1)
    return jnp.where((ci <= ri) if fwd else (ci >= ri), a, 0.0)


def _chunk_rows(cc):
    if isinstance(cc, int):
        return slice(cc * HG_CHUNK, (cc + 1) * HG_CHUNK)
    return pl.ds(pl.multiple_of(cc * HG_CHUNK, HG_CHUNK), HG_CHUNK)


def _hg_head_scores(refs, cc, h, sub, exact):
    (q_ref, kf_ref, kb_ref, v_ref, _, bc_ref, sb_ref, _, _, s_ref, k32_ref, b_ref) = refs
    rows = _chunk_rows(cc)
    q = q_ref[h, rows, :].astype(F32)
    v = v_ref[h, rows, :]
    bf = bc_ref[h, rows, :]
    bb = bc_ref[HG_HEADS + h, rows, :]
    kf = kf_ref[h, rows, :].astype(F32)
    kb = kb_ref[h, rows, :].astype(F32)
    qf = q * jnp.exp2(bf)
    qb = q * jnp.exp2(bb)
    s = s_ref[h]
    inter = lax.dot_general(jnp.concatenate([qf, qb], axis=1).astype(BF16),
                            jnp.concatenate([s.astype(BF16), sb_ref[cc, h]], axis=1),
                            NT_DIMS, preferred_element_type=F32)
    if exact:
        k32_ref[...] = kf
        b_ref[...] = bf
    a = _hg_scores(q, qf, kf, bf, True, sub, exact, k32_ref, b_ref)
    if exact:
        k32_ref[...] = kb
        b_ref[...] = bb
    a = a + _hg_scores(q, qb, kb, bb, False, sub, exact, k32_ref, b_ref)
    b_last = bf[HG_CHUNK - 1:HG_CHUNK, :]
    kt = (kf * jnp.exp2(b_last - bf)).astype(BF16)
    s_ref[h] = s * jnp.exp2(b_last) + lax.dot_general(v, kt, TN_DIMS,
                                                      preferred_element_type=F32)
    return inter, a, v


def _hg_head_finish(refs, cc, h, inter, a, v):
    gate_ref, nw_ref, o_ref = refs[4], refs[7], refs[8]
    rows = _chunk_rows(cc)
    o = inter + jnp.dot(a.astype(BF16), v, preferred_element_type=F32)
    o = o * lax.rsqrt(jnp.mean(o * o, axis=-1, keepdims=True) + EPS)
    o = o * nw_ref[h] * gate_ref[h, rows, :].astype(F32)
    o_ref[h, rows, :] = o.astype(BF16)


def _hg_min_block_decay(bc_ref):
    n = bc_ref.shape[1] // HG_FAST_SUB
    first = lax.broadcasted_iota(jnp.int32, (n, LANE), 0) % 2 == 0
    worst = None
    for h in range(HG_HEADS):
        ends = bc_ref.at[h][pl.ds(HG_FAST_SUB - 1, n, stride=HG_FAST_SUB), :]
        fwd = jnp.where(first, ends, ends - pltpu.roll(ends, 1, axis=0))
        starts = bc_ref.at[HG_HEADS + h][pl.ds(0, n, stride=HG_FAST_SUB), :]
        bwd = jnp.where(first, starts - pltpu.roll(starts, n - 1, axis=0), starts)
        m = jnp.minimum(fwd, bwd)
        worst = m if worst is None else jnp.minimum(worst, m)
    return jnp.min(worst)


def _hg_fwd_kernel(*refs, n_chunks):
    bc_ref, s_ref = refs[5], refs[9]

    @pl.when(pl.program_id(0) == 0)
    def _():
        s_ref[...] = jnp.zeros_like(s_ref)

    factored_ok = _hg_min_block_decay(bc_ref) >= -HG_CAP

    @pl.when(factored_ok)
    def _():
        staged_prev = None
        for cc in range(n_chunks):
            staged = [_hg_head_scores(refs, cc, h, HG_FAST_SUB, False) for h in range(HG_HEADS)]
            if staged_prev is not None:
                for h, parts in enumerate(staged_prev):
                    _hg_head_finish(refs, cc - 1, h, *parts)
            staged_prev = staged
        for h, parts in enumerate(staged_prev):
            _hg_head_finish(refs, n_chunks - 1, h, *parts)

    @pl.when(jnp.logical_not(factored_ok))
    def _():
        def body(n, carry):
            cc, h = n // HG_HEADS, n % HG_HEADS
            _hg_head_finish(refs, cc, h, *_hg_head_scores(refs, cc, h, HG_SUB, True))
            return carry
        lax.fori_loop(0, n_chunks * HG_HEADS, body, 0)


def _log_sigmoid(x):
    return jnp.minimum(x, 0.0) - jnp.log1p(jnp.exp(-jnp.abs(x)))


def _ret_log_gamma(logit_ref, direction, h, shape):
    return _log_sigmoid(jnp.full(shape, logit_ref[direction, h], F32))


def _ret_bwd_kernel(logit_ref, k_ref, v_ref, sb_ref, s_ref):
    c = RET_BLOCK

    @pl.when(pl.program_id(0) == 0)
    def _():
        s_ref[...] = jnp.zeros_like(s_ref)

    t = lax.broadcasted_iota(jnp.int32, (c, RET_DK), 0).astype(F32)
    for h in range(RET_HEADS):
        lg = _ret_log_gamma(logit_ref, 1, h, (c, RET_DK))
        s = s_ref[h]
        sb_ref[0, h] = s.astype(BF16)
        kt = (k_ref[h].astype(F32) * jnp.exp(t * lg)).astype(BF16)
        v = jnp.concatenate([v_ref[2 * h], v_ref[2 * h + 1]], axis=1)
        s_ref[h] = s * jnp.exp(c * lg[0:1, 0:1]) + lax.dot_general(
            kt, v, TN_DIMS, preferred_element_type=F32)


def _ret_fwd_kernel(logit_ref, q_ref, k_ref, v_ref, gate_ref, sb_ref, gnw_ref,
                    o_ref, s_ref, dmat_ref):
    c = RET_BLOCK

    @pl.when(pl.program_id(0) == 0)
    def _():
        s_ref[...] = jnp.zeros_like(s_ref)
        ri = lax.broadcasted_iota(jnp.int32, (c, c), 0)
        ci = lax.broadcasted_iota(jnp.int32, (c, c), 1)
        dist = (ri - ci).astype(F32)
        for h in range(RET_HEADS):
            lgf = _ret_log_gamma(logit_ref, 0, h, (c, c))
            lgb = _ret_log_gamma(logit_ref, 1, h, (c, c))
            dmat_ref[h] = (jnp.where(ci <= ri, jnp.exp(jnp.maximum(dist, 0.0) * lgf), 0.0)
                           + jnp.where(ci >= ri, jnp.exp(jnp.maximum(-dist, 0.0) * lgb), 0.0))

    t = lax.broadcasted_iota(jnp.int32, (c, RET_DK), 0).astype(F32)
    for h in range(RET_HEADS):
        lgf = _ret_log_gamma(logit_ref, 0, h, (c, RET_DK))
        lgb = _ret_log_gamma(logit_ref, 1, h, (c, RET_DK))
        q = q_ref[h]
        k = k_ref[h]
        q32 = q.astype(F32)
        v = jnp.concatenate([v_ref[2 * h], v_ref[2 * h + 1]], axis=1)
        s = s_ref[h]
        sc = lax.dot_general(q, k, NT_DIMS, preferred_element_type=F32) * dmat_ref[h]
        o = jnp.dot(sc.astype(BF16), v, preferred_element_type=F32)
        o = o + jnp.dot((q32 * jnp.exp((t + 1.0) * lgf)).astype(BF16), s.astype(BF16),
                        preferred_element_type=F32)
        o = o + jnp.dot((q32 * jnp.exp((c - t) * lgb)).astype(BF16), sb_ref[0, h],
                        preferred_element_type=F32)
        kt = (k.astype(F32) * jnp.exp((c - 1.0 - t) * lgf)).astype(BF16)
        s_ref[h] = s * jnp.exp(c * lgf[0:1, 0:1]) + lax.dot_general(
            kt, v, TN_DIMS, preferred_element_type=F32)

        mu = jnp.mean(o, axis=-1, keepdims=True)
        dev = o - mu
        var = jnp.mean(dev * dev, axis=-1, keepdims=True)
        o = dev * lax.rsqrt(var + EPS) * gnw_ref[h]
        gate = jnp.concatenate([gate_ref[2 * h], gate_ref[2 * h + 1]], axis=1).astype(F32)
        o_ref[:, h * RET_DV:(h + 1) * RET_DV] = (o * gate).astype(BF16)


N_HG_FWD_IN = 8
N_RET_FWD_IN = 7


def _mixer_fwd_kernel(*refs, n_chunks):
    hg_in = refs[:N_HG_FWD_IN]
    ret_in = refs[N_HG_FWD_IN:N_HG_FWD_IN + N_RET_FWD_IN]
    hg_out, ret_out, hs_ref, k32_ref, b_ref, rs_ref, dmat_ref = refs[N_HG_FWD_IN + N_RET_FWD_IN:]
    _hg_fwd_kernel(*hg_in, hg_out, hs_ref, k32_ref, b_ref, n_chunks=n_chunks)
    _ret_fwd_kernel(*ret_in, ret_out, rs_ref, dmat_ref)


def _mixer_fwd(logit, p3, lf3, hsb, rsb, hg_norm_w, gn_w):
    l = p3.shape[1]
    tb = HG_BLOCK
    assert tb == RET_BLOCK
    nb = l // tb
    ncb = tb // HG_CHUNK
    slabs = lambda n, start: pl.BlockSpec((n, tb, LANE), lambda i: (start // n, i, 0))
    kern = functools.partial(_mixer_fwd_kernel, n_chunks=ncb)
    return pl.pallas_call(
        kern,
        grid=(nb,),
        in_specs=[slabs(HG_HEADS, S_HQ), slabs(HG_HEADS, S_KF), slabs(HG_HEADS, S_KB),
                  slabs(HG_HEADS, S_HI), slabs(HG_HEADS, S_HGATE),
                  pl.BlockSpec((2 * HG_HEADS, tb, LANE), lambda i: (0, i, 0)),
                  pl.BlockSpec((ncb, HG_HEADS, HG_DV, HG_DK), lambda i: (i, 0, 0, 0)),
                  pl.BlockSpec((HG_HEADS, 1, HG_DV), lambda i: (0, 0, 0)),
                  pl.BlockSpec(memory_space=pltpu.SMEM),
                  slabs(RET_HEADS, S_RQ), slabs(RET_HEADS, S_RK),
                  slabs(2 * RET_HEADS, S_RV), slabs(2 * RET_HEADS, S_RG),
                  pl.BlockSpec((1, RET_HEADS, RET_DK, RET_DV), lambda i: (i, 0, 0, 0)),
                  pl.BlockSpec((RET_HEADS, 1, RET_DV), lambda i: (0, 0, 0))],
        out_specs=[pl.BlockSpec((HG_HEADS, tb, LANE), lambda i: (0, i, 0)),
                   pl.BlockSpec((tb, RET_HEADS * RET_DV), lambda i: (i, 0))],
        out_shape=[jax.ShapeDtypeStruct((HG_HEADS, l, LANE), BF16),
                   jax.ShapeDtypeStruct((l, RET_HEADS * RET_DV), BF16)],
        scratch_shapes=[pltpu.VMEM((HG_HEADS, HG_DV, HG_DK), F32),
                        pltpu.VMEM((HG_CHUNK, HG_DK), F32),
                        pltpu.VMEM((HG_CHUNK, HG_DK), F32),
                        pltpu.VMEM((RET_HEADS, RET_DK, RET_DV), F32),
                        pltpu.VMEM((RET_HEADS, tb, tb), F32)],
        compiler_params=_params(("arbitrary",)),
        name="mixer_fwd",
    )(p3, p3, p3, p3, p3, lf3, hsb, hg_norm_w.reshape(HG_HEADS, 1, HG_DV),
      logit, p3, p3, p3, p3, rsb, gn_w.reshape(RET_HEADS, 1, RET_DV))


def _pack_bf16_pairs(v):
    w = v.shape[1] // 2
    lo = pltpu.bitcast(v[:, :w].astype(BF16).astype(F32), U32)
    hi = pltpu.bitcast(v[:, w:].astype(BF16).astype(F32), U32)
    return (lo >> 16) | (hi & jnp.uint32(0xFFFF0000))


def _unpack_bf16_pairs(u):
    lo = pltpu.bitcast(u << 16, F32)
    hi = pltpu.bitcast(u & jnp.uint32(0xFFFF0000), F32)
    return lo, hi


def _store_planes(ref, rows, packed):
    for c in range(ref.shape[0]):
        ref[c, rows, :] = packed[:, c * LANE:(c + 1) * LANE]


def _load_planes(ref, rows):
    return jnp.concatenate([ref[c, rows, :] for c in range(ref.shape[0])], axis=1)


def _copy_row(dst_ref, dst_row, src_ref, src_row):
    for c in range(src_ref.shape[0]):
        dst_ref[c, pl.ds(dst_row, 1), :] = src_ref[c, pl.ds(src_row, 1), :]


def _route(logits):
    lane = lax.broadcasted_iota(jnp.int32, logits.shape, 1)
    big = jnp.int32(10 ** 6)
    neg = -jnp.inf
    gl = jnp.where(lane < N_GROUPS, logits, neg)
    gmax = jnp.max(gl, axis=1, keepdims=True)
    grp = jnp.min(jnp.where(gl == gmax, lane, big), axis=1, keepdims=True)
    g_val = 1.0 / jnp.sum(jnp.exp(gl - gmax), axis=1, keepdims=True)
    lo = N_GROUPS + EXPERTS_PER_GROUP * grp
    el = jnp.where((lane >= lo) & (lane < lo + EXPERTS_PER_GROUP), logits, neg)
    v1 = jnp.max(el, axis=1, keepdims=True)
    i1 = jnp.min(jnp.where(el == v1, lane, big), axis=1, keepdims=True)
    el2 = jnp.where(lane == i1, neg, el)
    v2 = jnp.max(el2, axis=1, keepdims=True)
    i2 = jnp.min(jnp.where(el2 == v2, lane, big), axis=1, keepdims=True)
    r = jnp.exp(v2 - v1)
    w1 = g_val / (1.0 + r)
    w2 = w1 * r
    return ((i1 - N_GROUPS).astype(F32), (i2 - N_GROUPS).astype(F32), w1, w2)


def _merge_kernel(ohg_ref, oret_ref, ga_ref, gb_ref, x_ref, mods_ref, wohg_ref, woret_ref,
                  wout_ref, nw_ref, wrh_ref, wrl_ref, rb_ref,
                  xo_ref, hp_ref, route_ref, route_t_ref, cnt_out_ref, cnt_ref,
                  *, lc, tm, d, tiles_per_seg):
    i = pl.program_id(0)
    tg = tm // MERGE_GROUPS
    routed = []
    for g in range(MERGE_GROUPS):
        rows = slice(g * tg, (g + 1) * tg)
        row0 = i * tm + g * tg
        ohg = jnp.concatenate([ohg_ref[s, rows, :] for s in range(HG_HEADS)], axis=1)
        ga = jnp.concatenate([ga_ref[s, rows, :] for s in range(d // LANE)], axis=1).astype(F32)
        gb = jnp.concatenate([gb_ref[s, rows, :] for s in range(d // LANE)], axis=1).astype(F32)
        y_hg = jnp.dot(ohg, wohg_ref[...], preferred_element_type=F32)
        y_ret = jnp.dot(oret_ref[rows, :], woret_ref[...], preferred_element_type=F32)
        m = (ga * y_hg + gb * y_ret).astype(BF16)
        y = jnp.dot(m, wout_ref[...], preferred_element_type=F32)
        x = x_ref[rows, :] + _row_select(row0, tg, lc, mods_ref, 2, d) * y
        xo_ref[rows, :] = x

        xn = x * lax.rsqrt(jnp.mean(x * x, axis=-1, keepdims=True) + EPS) * nw_ref[...]
        h = (xn * (1.0 + _row_select(row0, tg, lc, mods_ref, 4, d))
             + _row_select(row0, tg, lc, mods_ref, 3, d))
        _store_planes(hp_ref, rows, _pack_bf16_pairs(h))

        h_hi = h.astype(BF16)
        h_lo = (h - h_hi.astype(F32)).astype(BF16)
        logits = (jnp.dot(h_hi, wrh_ref[...], preferred_element_type=F32)
                  + jnp.dot(h_lo, wrh_ref[...], preferred_element_type=F32)
                  + jnp.dot(h_hi, wrl_ref[...], preferred_element_type=F32)) + rb_ref[...]
        routed.append(_route(logits))
    e0, e1, w0, w1 = (jnp.concatenate(parts, axis=0) for parts in zip(*routed))

    @pl.when(i % tiles_per_seg == 0)
    def _():
        cnt_ref[...] = jnp.zeros_like(cnt_ref)

    lane_e = lax.broadcasted_iota(jnp.int32, (tm, LANE), 1).astype(F32)
    hot0 = lane_e == e0
    hot1 = lane_e == e1
    hot = jnp.where(hot0 | hot1, 1.0, 0.0)
    earlier = (lax.broadcasted_iota(jnp.int32, (tm, tm), 1)
               < lax.broadcasted_iota(jnp.int32, (tm, tm), 0))
    before = jnp.dot(jnp.where(earlier, 1.0, 0.0).astype(BF16), hot.astype(BF16),
                     preferred_element_type=F32) + cnt_ref[0:1, :]
    rank0 = jnp.sum(jnp.where(hot0, before, 0.0), axis=1, keepdims=True)
    rank1 = jnp.sum(jnp.where(hot1, before, 0.0), axis=1, keepdims=True)
    cnt_ref[...] = cnt_ref[...] + jnp.sum(hot, axis=0, keepdims=True)
    cnt_out_ref[0] = cnt_ref[...]

    lane = lax.broadcasted_iota(jnp.int32, (tm, LANE), 1)
    rec = jnp.zeros((tm, LANE), F32)
    for k, val in enumerate((e0, e1, w0, w1, rank0, rank1)):
        rec = jnp.where(lane == k, val, rec)
    route_ref[...] = rec[:, :ROUTE_W]
    eye = jnp.where(lax.broadcasted_iota(jnp.int32, (ROUTE_W, LANE), 0)
                    == lax.broadcasted_iota(jnp.int32, (ROUTE_W, LANE), 1), 1.0, 0.0)
    route_t_ref[0] = lax.dot_general(eye, rec, NT_DIMS, precision=lax.Precision.HIGHEST,
                                     preferred_element_type=F32)


def _merge(ohg, oret, p3, xc, mods, wohg, woret, wout, norm_w, wr_hi, wr_lo, rbias, *, lc, n_seg):
    l, d = xc.shape
    seg_tokens = l // n_seg
    tm = _largest_divisor(seg_tokens, MERGE_TILE, 16 * MERGE_GROUPS)
    tiles_per_seg = seg_tokens // tm
    n_slab = d // LANE
    full = lambda a: pl.BlockSpec(a.shape, lambda i: (0,) * a.ndim)
    kern = functools.partial(_merge_kernel, lc=lc, tm=tm, d=d, tiles_per_seg=tiles_per_seg)
    return pl.pallas_call(
        kern,
        grid=(l // tm,),
        in_specs=[pl.BlockSpec((HG_HEADS, tm, LANE), lambda i: (0, i, 0)),
                  pl.BlockSpec((tm, oret.shape[1]), lambda i: (i, 0)),
                  pl.BlockSpec((n_slab, tm, LANE), lambda i: (S_GA // n_slab, i, 0)),
                  pl.BlockSpec((n_slab, tm, LANE), lambda i: (S_GB // n_slab, i, 0)),
                  pl.BlockSpec((tm, d), lambda i: (i, 0)),
                  full(mods), full(wohg), full(woret), full(wout),
                  pl.BlockSpec((1, d), lambda i: (0, 0)),
                  full(wr_hi), full(wr_lo), full(rbias)],
        out_specs=[pl.BlockSpec((tm, d), lambda i: (i, 0)),
                   pl.BlockSpec((d // 2 // LANE, tm, LANE), lambda i: (0, i, 0)),
                   pl.BlockSpec((tm, ROUTE_W), lambda i: (i, 0)),
                   pl.BlockSpec((1, ROUTE_W, tm), lambda i: (i, 0, 0)),
                   pl.BlockSpec((1, 8, LANE), lambda i: (i // tiles_per_seg, 0, 0))],
        out_shape=[jax.ShapeDtypeStruct((l, d), F32),
                   jax.ShapeDtypeStruct((d // 2 // LANE, l, LANE), U32),
                   jax.ShapeDtypeStruct((l, ROUTE_W), F32),
                   jax.ShapeDtypeStruct((l // tm, ROUTE_W, tm), F32),
                   jax.ShapeDtypeStruct((n_seg, 8, LANE), F32)],
        scratch_shapes=[pltpu.VMEM((8, LANE), F32)],
        compiler_params=_params(("arbitrary",)),
        name="merge_router",
    )(ohg, oret, p3, p3, xc, mods, wohg, woret, wout, norm_w.reshape(1, d), wr_hi, wr_lo, rbias)


def _moe_kernel(be_ref, nused_ref, slot_ref, hp_ref, *rest,
                lc, tm, d, expert_steps, seg_tokens, n_tokens, final):
    w_refs = rest[:3 * MOE_PER_STEP]
    route_ref, x_ref, mods_ref, fw_ref, o_ref, xs_ref, g0_ref, g1_ref = rest[3 * MOE_PER_STEP:]
    s = pl.program_id(0)
    b = pl.program_id(1)

    @pl.when(b == 0)
    def _():
        xs_ref[...] = jnp.zeros_like(xs_ref)

        def scatter(g, a0):
            t0 = pl.multiple_of(g * ROW_GROUP, ROW_GROUP)
            for k in range(ROW_GROUP):
                _copy_row(xs_ref, slot_ref[a0 + k], hp_ref, t0 + k)
                _copy_row(xs_ref, slot_ref[n_tokens + a0 + k], hp_ref, t0 + k)
            return a0 + ROW_GROUP

        lax.fori_loop(0, seg_tokens // ROW_GROUP, scatter, s * seg_tokens)

    def expert_blocks(count):
        for k in range(count):
            wg_ref, wu_ref, wd_ref = w_refs[3 * k:3 * k + 3]
            rows = pl.ds(pl.multiple_of((b * MOE_PER_STEP + k) * MOE_BLOCK, MOE_BLOCK), MOE_BLOCK)
            lo, hi = _unpack_bf16_pairs(_load_planes(xs_ref, rows))
            x = jnp.concatenate([lo, hi], axis=1).astype(BF16)
            g = jnp.dot(x, wg_ref[0], preferred_element_type=F32)
            u = jnp.dot(x, wu_ref[0], preferred_element_type=F32)
            a = (_silu(g) * u).astype(BF16)
            _store_planes(xs_ref, rows,
                          _pack_bf16_pairs(jnp.dot(a, wd_ref[0], preferred_element_type=F32)))

    n_here = jnp.clip(nused_ref[s] - b * MOE_PER_STEP, 0, MOE_PER_STEP)
    for count in range(1, MOE_PER_STEP + 1):
        pl.when(jnp.logical_and(b < expert_steps, n_here == count))(
            functools.partial(expert_blocks, count))

    @pl.when(b >= expert_steps)
    def _():
        tile0 = s * seg_tokens + (b - expert_steps) * tm

        def gather(g, a0):
            r0 = pl.multiple_of(g * ROW_GROUP, ROW_GROUP)
            for k in range(ROW_GROUP):
                _copy_row(g0_ref, r0 + k, xs_ref, slot_ref[a0 + k])
                _copy_row(g1_ref, r0 + k, xs_ref, slot_ref[n_tokens + a0 + k])
            return a0 + ROW_GROUP

        lax.fori_loop(0, tm // ROW_GROUP, gather, tile0)
        lo0, hi0 = _unpack_bf16_pairs(_load_planes(g0_ref, slice(None)))
        lo1, hi1 = _unpack_bf16_pairs(_load_planes(g1_ref, slice(None)))
        w0 = route_ref[:, 2:3]
        w1 = route_ref[:, 3:4]
        y = jnp.concatenate([w0 * lo0 + w1 * lo1, w0 * hi0 + w1 * hi1], axis=1)
        x = x_ref[...] + _row_select(tile0, tm, lc, mods_ref, 5, d) * y
        if final:
            x = x * lax.rsqrt(jnp.mean(x * x, axis=-1, keepdims=True) + EPS) * fw_ref[...]
        o_ref[...] = x


def _moe(block_expert, n_used, slot, hp, wg, wu, wd, route, xc, mods, final_w,
         *, n_seg, nbs, layer, lc, final):
    l, d = xc.shape
    n_planes = hp.shape[0]
    seg_tokens = l // n_seg
    d_e = wg.shape[3]
    tm = _largest_divisor(seg_tokens, 320, 8)
    tiles = seg_tokens // tm
    per = MOE_PER_STEP
    assert nbs % per == 0
    expert_steps = nbs // per
    kern = functools.partial(_moe_kernel, lc=lc, tm=tm, d=d, expert_steps=expert_steps,
                             seg_tokens=seg_tokens, n_tokens=l, final=final)
    w_specs = []
    for k in range(per):
        w_idx = lambda s, b, be, nu, *_, k=k: (
            layer, be[s * nbs + jnp.minimum(b * per + k, jnp.maximum(nu[s] - 1, 0))], 0, 0)
        w_specs += [pl.BlockSpec((None, 1, d, d_e), w_idx), pl.BlockSpec((None, 1, d, d_e), w_idx),
                    pl.BlockSpec((None, 1, d_e, d), w_idx)]
    tile = lambda s, b: s * tiles + jnp.clip(b - expert_steps, 0, tiles - 1)
    row = lambda s, b, *_: (tile(s, b), 0)
    if final:
        assert lc % tm == 0
        out_rows = l - lc
        out_row = lambda s, b, *_: (jnp.maximum(tile(s, b) - lc // tm, 0), 0)
    else:
        out_rows, out_row = l, row
    grid_spec = pltpu.PrefetchScalarGridSpec(
        num_scalar_prefetch=3,
        grid=(n_seg, expert_steps + tiles),
        in_specs=[pl.BlockSpec((n_planes, seg_tokens, LANE), lambda s, b, *_: (0, s, 0))]
        + w_specs + [
                  pl.BlockSpec((tm, ROUTE_W), row),
                  pl.BlockSpec((tm, d), row),
                  pl.BlockSpec(mods.shape, lambda s, b, *_: (0, 0)),
                  pl.BlockSpec((1, d), lambda s, b, *_: (0, 0))],
        out_specs=pl.BlockSpec((tm, d), out_row),
        scratch_shapes=[pltpu.VMEM((n_planes, nbs * MOE_BLOCK, LANE), U32),
                        pltpu.VMEM((n_planes, tm, LANE), U32),
                        pltpu.VMEM((n_planes, tm, LANE), U32)])
    return pl.pallas_call(
        kern,
        grid_spec=grid_spec,
        out_shape=jax.ShapeDtypeStruct((out_rows, d), F32),
        compiler_params=_params(("arbitrary", "arbitrary")),
        name="moe",
    )(block_expert, n_used, slot, hp, *((wg, wu, wd) * per), route, xc, mods,
      final_w.reshape(1, d))


def _dispatch_plan(route_t, counts, nbs):
    n_seg = counts.shape[0]
    counts = counts[:, 0, :N_EXPERTS].astype(jnp.int32)
    padded = ((counts + MOE_BLOCK - 1) // MOE_BLOCK) * MOE_BLOCK
    pad_end = jnp.cumsum(padded, axis=1)
    pad_start = pad_end - padded
    starts = jnp.arange(nbs, dtype=jnp.int32) * MOE_BLOCK
    block_expert = jnp.minimum(
        jnp.sum((pad_end[:, None, :] <= starts[None, :, None]).astype(jnp.int32), axis=2),
        N_EXPERTS - 1)
    n_used = pad_end[:, -1] // MOE_BLOCK
    n_tiles, _, tm = route_t.shape
    tile_start = jnp.repeat(pad_start, n_tiles // n_seg, axis=0)
    slots = []
    for k in range(2):
        e = route_t[:, k, :].astype(jnp.int32)
        rank = route_t[:, 4 + k, :].astype(jnp.int32)
        start = jnp.zeros_like(rank)
        for x in range(N_EXPERTS):
            start = jnp.where(e == x, tile_start[:, x:x + 1], start)
        slots.append((rank + start).reshape(-1))
    return (jnp.concatenate(slots), block_expert.reshape(-1).astype(jnp.int32),
            n_used.astype(jnp.int32))


def _rope_tables(lc, t):
    quarter = RET_DK // 4
    n_rows = t // GRID_W
    inv = ROPE_BASE ** (-np.arange(0, 2 * quarter, 2, dtype=np.float64) / (2 * quarter))
    ang_r = np.arange(n_rows, dtype=np.float64)[:, None] * inv
    ang_c = np.arange(GRID_W, dtype=np.float64)[:, None] * inv

    def expand(row_part, col_part):
        r = np.broadcast_to(row_part[:, None, :], (n_rows, GRID_W, 2 * quarter))
        c = np.broadcast_to(col_part[None, :, :], (n_rows, GRID_W, 2 * quarter))
        return np.concatenate([r, c], axis=2).reshape(t, RET_DK)

    two = lambda a: np.concatenate([a, a], axis=1)
    cos = expand(two(np.cos(ang_r)), two(np.cos(ang_c)))
    sin = expand(np.concatenate([-np.sin(ang_r), np.sin(ang_r)], axis=1),
                 np.concatenate([-np.sin(ang_c), np.sin(ang_c)], axis=1))
    cos = np.concatenate([np.ones((lc, RET_DK)), cos], axis=0).astype(np.float32)
    sin = np.concatenate([np.zeros((lc, RET_DK)), sin], axis=0).astype(np.float32)
    return jnp.asarray(cos), jnp.asarray(sin)


def kernel(x, c, ctx, c_ctx, w_mod, b_mod, norm_mix_w, norm_ffn_w, w_in, hgrn_lb_logits, hgrn_norm_w,
           ret_decay_logit, ret_gn_w, w_o_hgrn, w_o_ret, w_out, router_group_w, router_group_b,
           router_expert_w, router_expert_b, expert_w_gate, expert_w_up, expert_w_down, final_norm_w):
    b_, t_, d = x.shape
    assert b_ == 1
    lc = ctx.shape[1]
    depth = w_mod.shape[0]
    l = lc + t_
    assert lc % HG_BLOCK == 0 and l % HG_BLOCK == 0 and lc % RET_BLOCK == 0 and l % RET_BLOCK == 0
    n_seg = MOE_SEGMENTS
    assert l % n_seg == 0
    seg_tokens = l // n_seg
    nbs = pl.cdiv(2 * seg_tokens + N_EXPERTS * (MOE_BLOCK - 1), MOE_BLOCK)
    nbs = pl.cdiv(nbs, MOE_PER_STEP) * MOE_PER_STEP

    xc = jnp.concatenate([ctx[0], x[0]], axis=0)
    cc = jnp.zeros((8, d), F32).at[0].set(c[0]).at[1].set(c_ctx)
    mods_all = _modulation(cc, w_mod, b_mod)
    cos_t, sin_t = _rope_tables(lc, t_)
    w_in_bf = w_in.astype(BF16)
    w_gate_bf, w_up_bf, w_down_bf = (w.astype(BF16)
                                     for w in (expert_w_gate, expert_w_up, expert_w_down))

    for layer in range(depth):
        mods = mods_all[layer]
        p3, lf3 = _inproj(xc, mods, norm_mix_w[layer], w_in_bf, hgrn_lb_logits,
                          cos_t, sin_t, layer=layer, lc=lc)
        sb_hg, sb_ret = _mixer_bwd(ret_decay_logit[layer], p3, lf3, lc=lc)
        ohg, oret = _mixer_fwd(ret_decay_logit[layer], p3, lf3, sb_hg, sb_ret,
                               hgrn_norm_w[layer], ret_gn_w[layer])

        wr = jnp.concatenate([router_group_w[layer], router_expert_w[layer]], axis=1)
        wr = jnp.pad(wr, ((0, 0), (0, LANE - wr.shape[1])))
        wr_hi = wr.astype(BF16)
        wr_lo = (wr - wr_hi.astype(F32)).astype(BF16)
        rbias = jnp.pad(jnp.concatenate([router_group_b[layer], router_expert_b[layer]]),
                        (0, LANE - N_GROUPS - N_EXPERTS)).reshape(1, LANE)
        xc, hp, route, route_t, counts = _merge(
            ohg, oret, p3, xc, mods, w_o_hgrn[layer].astype(BF16), w_o_ret[layer].astype(BF16),
            w_out[layer].astype(BF16), norm_ffn_w[layer], wr_hi, wr_lo, rbias, lc=lc, n_seg=n_seg)

        slot, block_expert, n_used = _dispatch_plan(route_t, counts, nbs)
        xc = _moe(block_expert, n_used, slot, hp, w_gate_bf, w_up_bf, w_down_bf, route, xc, mods,
                  final_norm_w, n_seg=n_seg, nbs=nbs, layer=layer, lc=lc,
                  final=(layer == depth - 1))

    return xc[None]
```

```python
import functools

import jax
import jax.numpy as jnp
import numpy as np
from jax import lax
from jax.experimental import pallas as pl
from jax.experimental.pallas import tpu as pltpu

F32 = jnp.float32
BF16 = jnp.bfloat16
U32 = jnp.uint32

GRID_W = 64
HG_HEADS = 4
HG_DK = 128
HG_DV = 128
F_MIN = 1e-30
RET_HEADS = 4
RET_DK = 128
RET_DV = 256
ROPE_BASE = 10000.0
N_GROUPS = 4
EXPERTS_PER_GROUP = 8
N_EXPERTS = N_GROUPS * EXPERTS_PER_GROUP
N_MOD = 6
EPS = 1e-6

LANE = 128
COL_TILE = 512
SLABS_PER_TILE = COL_TILE // LANE
HG_CHUNK = 64
HG_SUB = 16
HG_BLOCK = 256
LOG2E = 1.4426950408889634
HG_FAST_SUB = 32
HG_CAP = 100.0
CUM_GROUP = 256
CUM_TERMS = 2
RET_BLOCK = 256
MOE_SEGMENTS = 5
MOE_BLOCK = 256
MOE_PER_STEP = 2
ROW_GROUP = 16
MERGE_GROUPS = 1
MERGE_TILE = 832
ROUTE_W = 8
VMEM_LIMIT = 56 * 1024 * 1024

NT_DIMS = (((1,), (1,)), ((), ()))
TN_DIMS = (((0,), (0,)), ((), ()))

SEG_SLABS = (4, 4, 4, 4, 4, 4, 4, 8, 8, 8, 8)
SEG_ORDER = (7, 8, 9, 10, 0, 1, 2, 3, 4, 5, 6)
_starts = {}
_pos = 0
for _seg in SEG_ORDER:
    _starts[_seg] = _pos
    _pos += SEG_SLABS[_seg]
SEG_START = tuple(_starts[_seg] for _seg in range(len(SEG_SLABS)))
(S_HQ, S_KF, S_KB, S_HI, S_HGATE, S_RQ, S_RK, S_RV, S_RG, S_GA, S_GB) = SEG_START
assert SEG_ORDER == tuple(range(SEG_ORDER[0], len(SEG_SLABS))) + tuple(range(SEG_ORDER[0]))
TILE_ROTATION = sum(SEG_SLABS[:SEG_ORDER[0]]) // SLABS_PER_TILE
N_SLABS = sum(SEG_SLABS)


def _params(sem):
    return pltpu.CompilerParams(dimension_semantics=sem, vmem_limit_bytes=VMEM_LIMIT)


def _sigmoid(x):
    return 1.0 / (1.0 + jnp.exp(-x))


def _silu(x):
    return x * _sigmoid(x)


def _largest_divisor(n, cap, multiple):
    best = None
    for d in range(multiple, cap + 1, multiple):
        if n % d == 0:
            best = d
    assert best is not None, (n, cap, multiple)
    return best


def _row_select(row0, n_rows, lc, mods_ref, k, d):
    rows = row0 + lax.broadcasted_iota(jnp.int32, (n_rows, 1), 0)
    lat = mods_ref[0:1, k * d:(k + 1) * d]
    ctx = mods_ref[1:2, k * d:(k + 1) * d]
    return jnp.where(rows < lc, ctx, lat)


def _mod_kernel(cc_ref, w_ref, b_ref, o_ref):
    cc = cc_ref[...]
    s = _silu(cc)
    o_ref[0] = jnp.dot(s, w_ref[0], precision=lax.Precision.HIGHEST,
                       preferred_element_type=F32) + b_ref[0]


def _modulation(cc, w_mod, b_mod):
    depth, d, n = w_mod.shape
    tn = _largest_divisor(n, 1536, LANE)
    return pl.pallas_call(
        _mod_kernel,
        grid=(depth, n // tn),
        in_specs=[pl.BlockSpec((8, d), lambda l, j: (0, 0)),
                  pl.BlockSpec((1, d, tn), lambda l, j: (l, 0, j)),
                  pl.BlockSpec((1, 1, tn), lambda l, j: (l, 0, j))],
        out_specs=pl.BlockSpec((1, 8, tn), lambda l, j: (l, 0, j)),
        out_shape=jax.ShapeDtypeStruct((depth, 8, n), F32),
        compiler_params=_params(("arbitrary", "arbitrary")),
        name="modulation",
    )(cc, w_mod, b_mod.reshape(depth, 1, n))


def _inproj_kernel(x_ref, mods_ref, nw_ref, w_ref, lbl_ref, cos_ref, sin_ref,
                   p_ref, lf_ref, h_ref, *, layer, lc, tm, d):
    i = pl.program_id(0)
    j = pl.program_id(1)

    def normalise(rows):
        x = x_ref[rows, :]
        xn = x * lax.rsqrt(jnp.mean(x * x, axis=-1, keepdims=True) + EPS) * nw_ref[...]
        shift = _row_select(i * tm + rows.start, CUM_GROUP, lc, mods_ref, 0, d)
        scale = _row_select(i * tm + rows.start, CUM_GROUP, lc, mods_ref, 1, d)
        h_ref[rows, :] = (xn * (1.0 + scale) + shift).astype(BF16)

    def for_row_groups(epilogue, first_tile=False):
        for r0 in range(0, tm, CUM_GROUP):
            rows = slice(r0, r0 + CUM_GROUP)
            if first_tile:
                normalise(rows)
            acc = jnp.dot(h_ref[rows, :], w_ref[...], preferred_element_type=F32)
            epilogue(acc, rows)

    def put(val, rows):
        v = val.astype(BF16)
        for s in range(SLABS_PER_TILE):
            p_ref[s, rows, :] = v[:, s * LANE:(s + 1) * LANE]

    tiles = lambda seg: tuple(range(SEG_START[seg] // SLABS_PER_TILE,
                                    (SEG_START[seg] + SEG_SLABS[seg]) // SLABS_PER_TILE))
    in_tiles = lambda segs: functools.reduce(
        jnp.logical_or, [j == t for seg in segs for t in tiles(seg)])

    @pl.when(in_tiles((0,)))
    def _():
        for_row_groups(lambda acc, rows: put(_silu(acc) * (HG_DK ** -0.5), rows))

    @pl.when(in_tiles((1, 2)))
    def _():
        logits = lbl_ref[jnp.clip(j - tiles(1)[0], 0, 1)]
        e = jnp.exp(logits - jnp.max(logits, axis=0, keepdims=True))
        p = e / jnp.sum(e, axis=0, keepdims=True)
        lb = jnp.zeros_like(p[0:1])
        for r in range(1, layer + 1):
            lb = lb + p[r:r + 1]
        lb = jnp.clip(lb, 0.0, 1.0 - 1e-6)
        g = CUM_GROUP
        ri = lax.broadcasted_iota(jnp.int32, (g, g), 0)
        ci = lax.broadcasted_iota(jnp.int32, (g, g), 1)
        same_chunk = (ri // HG_CHUNK) == (ci // HG_CHUNK)
        before = jnp.where(same_chunk & (ci <= ri), 1.0, 0.0)
        after = jnp.where(same_chunk & (ci >= ri), 1.0, 0.0)
        tri = jnp.where(j == tiles(1)[0], before, after).astype(BF16)

        log_f_terms = []

        def epilogue(acc, rows):
            sig = _sigmoid(acc)
            put((1.0 - lb) * (1.0 - sig), rows)
            rest = jnp.log(jnp.maximum(lb + (1.0 - lb) * sig, F_MIN))
            terms = []
            for _ in range(CUM_TERMS):
                terms.append(rest.astype(BF16))
                rest = rest - terms[-1].astype(F32)
            log_f_terms.append((rows, terms))

        for_row_groups(epilogue)
        for rows, terms in log_f_terms:
            cum = jnp.zeros((g, COL_TILE), F32)
            for term in terms:
                cum = cum + jnp.dot(tri, term, preferred_element_type=F32)
            cum = cum * LOG2E
            for s in range(SLABS_PER_TILE):
                lf_ref[s, rows, :] = cum[:, s * LANE:(s + 1) * LANE]

    assert 0 in tiles(7)

    @pl.when(j == 0)
    def _():
        for_row_groups(put, first_tile=True)

    @pl.when(jnp.logical_and(in_tiles((3, 7)), j != 0))
    def _():
        for_row_groups(put)

    @pl.when(in_tiles((4, 8)))
    def _():
        for_row_groups(lambda acc, rows: put(_silu(acc), rows))

    @pl.when(in_tiles((5, 6)))
    def _():
        scale = jnp.where(j == tiles(5)[0], RET_DK ** -0.5, 1.0)

        def epilogue(acc, rows):
            xq = acc * scale
            n = xq.shape[1]
            lane = lax.broadcasted_iota(jnp.int32, xq.shape, 1)
            partner = jnp.where((lane & 32) == 0, pltpu.roll(xq, n - 32, axis=1),
                                pltpu.roll(xq, 32, axis=1))
            cos = jnp.concatenate([cos_ref[rows, :]] * SLABS_PER_TILE, axis=1)
            sin = jnp.concatenate([sin_ref[rows, :]] * SLABS_PER_TILE, axis=1)
            put(xq * cos + partner * sin, rows)

        for_row_groups(epilogue)

    @pl.when(in_tiles((9, 10)))
    def _():
        for_row_groups(lambda acc, rows: put(_sigmoid(acc), rows))


def _inproj(xc, mods, norm_w, w_in_bf, lb_logits, cos_t, sin_t, *, layer, lc):
    l, d = xc.shape
    d_in = w_in_bf.shape[2]
    assert d_in == N_SLABS * LANE
    tm = _largest_divisor(l, 1280, CUM_GROUP)
    n_col = d_in // COL_TILE
    kf_tile = S_KF // SLABS_PER_TILE
    kern = functools.partial(_inproj_kernel, layer=layer, lc=lc, tm=tm, d=d)
    return pl.pallas_call(
        kern,
        grid=(l // tm, n_col),
        in_specs=[pl.BlockSpec((tm, d), lambda i, j: (i, 0)),
                  pl.BlockSpec(mods.shape, lambda i, j: (0, 0)),
                  pl.BlockSpec((1, d), lambda i, j: (0, 0)),
                  pl.BlockSpec((None, d, COL_TILE),
                               lambda i, j: (layer, 0, (j + TILE_ROTATION) % n_col)),
                  pl.BlockSpec(lb_logits.shape, lambda i, j: (0, 0, 0)),
                  pl.BlockSpec((tm, LANE), lambda i, j: (i, 0)),
                  pl.BlockSpec((tm, LANE), lambda i, j: (i, 0))],
        out_specs=[pl.BlockSpec((SLABS_PER_TILE, tm, LANE), lambda i, j: (j, i, 0)),
                   pl.BlockSpec((SLABS_PER_TILE, tm, LANE),
                                lambda i, j: (jnp.clip(j - kf_tile, 0, 1), i, 0))],
        out_shape=[jax.ShapeDtypeStruct((N_SLABS, l, LANE), BF16),
                   jax.ShapeDtypeStruct((2 * HG_HEADS, l, LANE), F32)],
        scratch_shapes=[pltpu.VMEM((tm, d), BF16)],
        compiler_params=_params(("arbitrary", "arbitrary")),
        name="inproj",
    )(xc, mods, norm_w.reshape(1, d), w_in_bf, lb_logits, cos_t, sin_t)


def _hg_bwd_kernel(kb_ref, v_ref, bc_ref, sb_ref, s_ref, *, n_chunks):
    @pl.when(pl.program_id(0) == 0)
    def _():
        s_ref[...] = jnp.zeros_like(s_ref)

    for h in range(HG_HEADS):
        s = s_ref[h]
        for cc in reversed(range(n_chunks)):
            rows = slice(cc * HG_CHUNK, (cc + 1) * HG_CHUNK)
            bb = bc_ref[h, rows, :]
            k = kb_ref[h, rows, :].astype(F32)
            sb_ref[cc, h] = s.astype(BF16)
            kt = (k * jnp.exp2(bb[0:1, :] - bb)).astype(BF16)
            s = s * jnp.exp2(bb[0:1, :]) + lax.dot_general(
                v_ref[h, rows, :], kt, TN_DIMS, preferred_element_type=F32)
        s_ref[h] = s


def _bwd_block_order(i, n_ctx_blocks, n_blocks):
    return jnp.where(i < n_ctx_blocks, n_ctx_blocks - 1 - i, n_blocks - 1 - (i - n_ctx_blocks))


def _mixer_bwd_kernel(logit_ref, kb_ref, hv_ref, bc_ref, rk_ref, rv_ref,
                      hsb_ref, rsb_ref, hs_ref, rs_ref, *, n_chunks):
    _hg_bwd_kernel(kb_ref, hv_ref, bc_ref, hsb_ref, hs_ref, n_chunks=n_chunks)
    _ret_bwd_kernel(logit_ref, rk_ref, rv_ref, rsb_ref, rs_ref)


def _mixer_bwd(logit, p3, lf3, *, lc):
    l = p3.shape[1]
    tb = HG_BLOCK
    assert tb == RET_BLOCK
    nb, nbc = l // tb, lc // tb
    ncb = tb // HG_CHUNK
    order = lambda i: _bwd_block_order(i, nbc, nb)
    slabs = lambda n, start: pl.BlockSpec((n, tb, LANE), lambda i: (start // n, order(i), 0))
    kern = functools.partial(_mixer_bwd_kernel, n_chunks=ncb)
    return pl.pallas_call(
        kern,
        grid=(nb,),
        in_specs=[pl.BlockSpec(memory_space=pltpu.SMEM),
                  slabs(HG_HEADS, S_KB), slabs(HG_HEADS, S_HI),
                  pl.BlockSpec((HG_HEADS, tb, LANE), lambda i: (1, order(i), 0)),
                  slabs(RET_HEADS, S_RK), slabs(2 * RET_HEADS, S_RV)],
        out_specs=[pl.BlockSpec((ncb, HG_HEADS, HG_DV, HG_DK), lambda i: (order(i), 0, 0, 0)),
                   pl.BlockSpec((1, RET_HEADS, RET_DK, RET_DV), lambda i: (order(i), 0, 0, 0))],
        out_shape=[jax.ShapeDtypeStruct((l // HG_CHUNK, HG_HEADS, HG_DV, HG_DK), BF16),
                   jax.ShapeDtypeStruct((nb, RET_HEADS, RET_DK, RET_DV), BF16)],
        scratch_shapes=[pltpu.VMEM((HG_HEADS, HG_DV, HG_DK), F32),
                        pltpu.VMEM((RET_HEADS, RET_DK, RET_DV), F32)],
        compiler_params=_params(("arbitrary",)),
        name="mixer_bwd_state",
    )(logit, p3, p3, lf3, p3, p3)


def _hg_scores(q, q_edge, k32, b, fwd, sub, exact, k32_ref=None, b_ref=None):
    c = HG_CHUNK
    n_sub = c // sub
    cap = 0.0 if exact else HG_CAP
    lane = lax.broadcasted_iota(jnp.int32, (sub, c), 1)
    blocks = []
    for blk in range(n_sub):
        r = blk * sub
        q_blk = q[r:r + sub]
        b_blk = b[r:r + sub]
        edge = blk == 0 if fwd else blk == n_sub - 1
        if edge and exact:
            a = jnp.zeros((sub, c), F32)
        else:
            if edge:
                qt = q_edge[r:r + sub]
                ref_minus_b = -b
            else:
                ref = b[r - 1:r] if fwd else b[r + sub:r + sub + 1]
                qt = q_blk * jnp.exp2(b_blk - ref)
                ref_minus_b = ref - b
            kt = (k32 * jnp.exp2(jnp.minimum(ref_minus_b, cap))).astype(BF16)
            a = lax.dot_general(qt.astype(BF16), kt, NT_DIMS, preferred_element_type=F32)
        if exact:
            for jj in range(sub):
                s = r + jj
                e = jnp.exp2(b_blk - b_ref[s:s + 1, :])
                col = jnp.sum(q_blk * k32_ref[s:s + 1, :] * e, axis=1, keepdims=True)
                a = jnp.where(lane == s, col, a)
        blocks.append(a)
    a = jnp.concatenate(blocks, axis=0)
    ri = lax.broadcasted_iota(jnp.int32, (c, c), 0)
    ci = lax.broadcasted_iota(jnp.int32, (c, c), 1)
    return jnp.where((ci <= ri) if fwd else (ci >= ri), a, 0.0)


def _chunk_rows(cc):
    if isinstance(cc, int):
        return slice(cc * HG_CHUNK, (cc + 1) * HG_CHUNK)
    return pl.ds(pl.multiple_of(cc * HG_CHUNK, HG_CHUNK), HG_CHUNK)


def _hg_head_scores(refs, cc, h, sub, exact):
    (q_ref, kf_ref, kb_ref, v_ref, _, bc_ref, sb_ref, _, _, s_ref, k32_ref, b_ref) = refs
    rows = _chunk_rows(cc)
    q = q_ref[h, rows, :].astype(F32)
    v = v_ref[h, rows, :]
    bf = bc_ref[h, rows, :]
    bb = bc_ref[HG_HEADS + h, rows, :]
    kf = kf_ref[h, rows, :].astype(F32)
    kb = kb_ref[h, rows, :].astype(F32)
    qf = q * jnp.exp2(bf)
    qb = q * jnp.exp2(bb)
    s = s_ref[h]
    inter = lax.dot_general(jnp.concatenate([qf, qb], axis=1).astype(BF16),
                            jnp.concatenate([s.astype(BF16), sb_ref[cc, h]], axis=1),
                            NT_DIMS, preferred_element_type=F32)
    if exact:
        k32_ref[...] = kf
        b_ref[...] = bf
    a = _hg_scores(q, qf, kf, bf, True, sub, exact, k32_ref, b_ref)
    if exact:
        k32_ref[...] = kb
        b_ref[...] = bb
    a = a + _hg_scores(q, qb, kb, bb, False, sub, exact, k32_ref, b_ref)
    b_last = bf[HG_CHUNK - 1:HG_CHUNK, :]
    kt = (kf * jnp.exp2(b_last - bf)).astype(BF16)
    s_ref[h] = s * jnp.exp2(b_last) + lax.dot_general(v, kt, TN_DIMS,
                                                      preferred_element_type=F32)
    return inter, a, v


def _hg_head_finish(refs, cc, h, inter, a, v):
    gate_ref, nw_ref, o_ref = refs[4], refs[7], refs[8]
    rows = _chunk_rows(cc)
    o = inter + jnp.dot(a.astype(BF16), v, preferred_element_type=F32)
    o = o * lax.rsqrt(jnp.mean(o * o, axis=-1, keepdims=True) + EPS)
    o = o * nw_ref[h] * gate_ref[h, rows, :].astype(F32)
    o_ref[h, rows, :] = o.astype(BF16)


def _hg_min_block_decay(bc_ref):
    n = bc_ref.shape[1] // HG_FAST_SUB
    first = lax.broadcasted_iota(jnp.int32, (n, LANE), 0) % 2 == 0
    worst = None
    for h in range(HG_HEADS):
        ends = bc_ref.at[h][pl.ds(HG_FAST_SUB - 1, n, stride=HG_FAST_SUB), :]
        fwd = jnp.where(first, ends, ends - pltpu.roll(ends, 1, axis=0))
        starts = bc_ref.at[HG_HEADS + h][pl.ds(0, n, stride=HG_FAST_SUB), :]
        bwd = jnp.where(first, starts - pltpu.roll(starts, n - 1, axis=0), starts)
        m = jnp.minimum(fwd, bwd)
        worst = m if worst is None else jnp.minimum(worst, m)
    return jnp.min(worst)


def _hg_fwd_kernel(*refs, n_chunks):
    bc_ref, s_ref = refs[5], refs[9]

    @pl.when(pl.program_id(0) == 0)
    def _():
        s_ref[...] = jnp.zeros_like(s_ref)

    factored_ok = _hg_min_block_decay(bc_ref) >= -HG_CAP

    @pl.when(factored_ok)
    def _():
        staged_prev = None
        for cc in range(n_chunks):
            staged = [_hg_head_scores(refs, cc, h, HG_FAST_SUB, False) for h in range(HG_HEADS)]
            if staged_prev is not None:
                for h, parts in enumerate(staged_prev):
                    _hg_head_finish(refs, cc - 1, h, *parts)
            staged_prev = staged
        for h, parts in enumerate(staged_prev):
            _hg_head_finish(refs, n_chunks - 1, h, *parts)

    @pl.when(jnp.logical_not(factored_ok))
    def _():
        def body(n, carry):
            cc, h = n // HG_HEADS, n % HG_HEADS
            _hg_head_finish(refs, cc, h, *_hg_head_scores(refs, cc, h, HG_SUB, True))
            return carry
        lax.fori_loop(0, n_chunks * HG_HEADS, body, 0)


def _log_sigmoid(x):
    return jnp.minimum(x, 0.0) - jnp.log1p(jnp.exp(-jnp.abs(x)))


def _ret_log_gamma(logit_ref, direction, h, shape):
    return _log_sigmoid(jnp.full(shape, logit_ref[direction, h], F32))


def _ret_bwd_kernel(logit_ref, k_ref, v_ref, sb_ref, s_ref):
    c = RET_BLOCK

    @pl.when(pl.program_id(0) == 0)
    def _():
        s_ref[...] = jnp.zeros_like(s_ref)

    t = lax.broadcasted_iota(jnp.int32, (c, RET_DK), 0).astype(F32)
    for h in range(RET_HEADS):
        lg = _ret_log_gamma(logit_ref, 1, h, (c, RET_DK))
        s = s_ref[h]
        sb_ref[0, h] = s.astype(BF16)
        kt = (k_ref[h].astype(F32) * jnp.exp(t * lg)).astype(BF16)
        v = jnp.concatenate([v_ref[2 * h], v_ref[2 * h + 1]], axis=1)
        s_ref[h] = s * jnp.exp(c * lg[0:1, 0:1]) + lax.dot_general(
            kt, v, TN_DIMS, preferred_element_type=F32)


def _ret_fwd_kernel(logit_ref, q_ref, k_ref, v_ref, gate_ref, sb_ref, gnw_ref,
                    o_ref, s_ref, dmat_ref):
    c = RET_BLOCK

    @pl.when(pl.program_id(0) == 0)
    def _():
        s_ref[...] = jnp.zeros_like(s_ref)
        ri = lax.broadcasted_iota(jnp.int32, (c, c), 0)
        ci = lax.broadcasted_iota(jnp.int32, (c, c), 1)
        dist = (ri - ci).astype(F32)
        for h in range(RET_HEADS):
            lgf = _ret_log_gamma(logit_ref, 0, h, (c, c))
            lgb = _ret_log_gamma(logit_ref, 1, h, (c, c))
            dmat_ref[h] = (jnp.where(ci <= ri, jnp.exp(jnp.maximum(dist, 0.0) * lgf), 0.0)
                           + jnp.where(ci >= ri, jnp.exp(jnp.maximum(-dist, 0.0) * lgb), 0.0))

    t = lax.broadcasted_iota(jnp.int32, (c, RET_DK), 0).astype(F32)
    for h in range(RET_HEADS):
        lgf = _ret_log_gamma(logit_ref, 0, h, (c, RET_DK))
        lgb = _ret_log_gamma(logit_ref, 1, h, (c, RET_DK))
        q = q_ref[h]
        k = k_ref[h]
        q32 = q.astype(F32)
        v = jnp.concatenate([v_ref[2 * h], v_ref[2 * h + 1]], axis=1)
        s = s_ref[h]
        sc = lax.dot_general(q, k, NT_DIMS, preferred_element_type=F32) * dmat_ref[h]
        o = jnp.dot(sc.astype(BF16), v, preferred_element_type=F32)
        o = o + jnp.dot((q32 * jnp.exp((t + 1.0) * lgf)).astype(BF16), s.astype(BF16),
                        preferred_element_type=F32)
        o = o + jnp.dot((q32 * jnp.exp((c - t) * lgb)).astype(BF16), sb_ref[0, h],
                        preferred_element_type=F32)
        kt = (k.astype(F32) * jnp.exp((c - 1.0 - t) * lgf)).astype(BF16)
        s_ref[h] = s * jnp.exp(c * lgf[0:1, 0:1]) + lax.dot_general(
            kt, v, TN_DIMS, preferred_element_type=F32)

        mu = jnp.mean(o, axis=-1, keepdims=True)
        dev = o - mu
        var = jnp.mean(dev * dev, axis=-1, keepdims=True)
        o = dev * lax.rsqrt(var + EPS) * gnw_ref[h]
        gate = jnp.concatenate([gate_ref[2 * h], gate_ref[2 * h + 1]], axis=1).astype(F32)
        o_ref[:, h * RET_DV:(h + 1) * RET_DV] = (o * gate).astype(BF16)


N_HG_FWD_IN = 8
N_RET_FWD_IN = 7


def _mixer_fwd_kernel(*refs, n_chunks):
    hg_in = refs[:N_HG_FWD_IN]
    ret_in = refs[N_HG_FWD_IN:N_HG_FWD_IN + N_RET_FWD_IN]
    hg_out, ret_out, hs_ref, k32_ref, b_ref, rs_ref, dmat_ref = refs[N_HG_FWD_IN + N_RET_FWD_IN:]
    _hg_fwd_kernel(*hg_in, hg_out, hs_ref, k32_ref, b_ref, n_chunks=n_chunks)
    _ret_fwd_kernel(*ret_in, ret_out, rs_ref, dmat_ref)


def _mixer_fwd(logit, p3, lf3, hsb, rsb, hg_norm_w, gn_w):
    l = p3.shape[1]
    tb = HG_BLOCK
    assert tb == RET_BLOCK
    nb = l // tb
    ncb = tb // HG_CHUNK
    slabs = lambda n, start: pl.BlockSpec((n, tb, LANE), lambda i: (start // n, i, 0))
    kern = functools.partial(_mixer_fwd_kernel, n_chunks=ncb)
    return pl.pallas_call(
        kern,
        grid=(nb,),
        in_specs=[slabs(HG_HEADS, S_HQ), slabs(HG_HEADS, S_KF), slabs(HG_HEADS, S_KB),
                  slabs(HG_HEADS, S_HI), slabs(HG_HEADS, S_HGATE),
                  pl.BlockSpec((2 * HG_HEADS, tb, LANE), lambda i: (0, i, 0)),
                  pl.BlockSpec((ncb, HG_HEADS, HG_DV, HG_DK), lambda i: (i, 0, 0, 0)),
                  pl.BlockSpec((HG_HEADS, 1, HG_DV), lambda i: (0, 0, 0)),
                  pl.BlockSpec(memory_space=pltpu.SMEM),
                  slabs(RET_HEADS, S_RQ), slabs(RET_HEADS, S_RK),
                  slabs(2 * RET_HEADS, S_RV), slabs(2 * RET_HEADS, S_RG),
                  pl.BlockSpec((1, RET_HEADS, RET_DK, RET_DV), lambda i: (i, 0, 0, 0)),
                  pl.BlockSpec((RET_HEADS, 1, RET_DV), lambda i: (0, 0, 0))],
        out_specs=[pl.BlockSpec((HG_HEADS, tb, LANE), lambda i: (0, i, 0)),
                   pl.BlockSpec((tb, RET_HEADS * RET_DV), lambda i: (i, 0))],
        out_shape=[jax.ShapeDtypeStruct((HG_HEADS, l, LANE), BF16),
                   jax.ShapeDtypeStruct((l, RET_HEADS * RET_DV), BF16)],
        scratch_shapes=[pltpu.VMEM((HG_HEADS, HG_DV, HG_DK), F32),
                        pltpu.VMEM((HG_CHUNK, HG_DK), F32),
                        pltpu.VMEM((HG_CHUNK, HG_DK), F32),
                        pltpu.VMEM((RET_HEADS, RET_DK, RET_DV), F32),
                        pltpu.VMEM((RET_HEADS, tb, tb), F32)],
        compiler_params=_params(("arbitrary",)),
        name="mixer_fwd",
    )(p3, p3, p3, p3, p3, lf3, hsb, hg_norm_w.reshape(HG_HEADS, 1, HG_DV),
      logit, p3, p3, p3, p3, rsb, gn_w.reshape(RET_HEADS, 1, RET_DV))


def _pack_bf16_pairs(v):
    w = v.shape[1] // 2
    lo = pltpu.bitcast(v[:, :w].astype(BF16).astype(F32), U32)
    hi = pltpu.bitcast(v[:, w:].astype(BF16).astype(F32), U32)
    return (lo >> 16) | (hi & jnp.uint32(0xFFFF0000))


def _unpack_bf16_pairs(u):
    lo = pltpu.bitcast(u << 16, F32)
    hi = pltpu.bitcast(u & jnp.uint32(0xFFFF0000), F32)
    return lo, hi


def _store_planes(ref, rows, packed):
    for c in range(ref.shape[0]):
        ref[c, rows, :] = packed[:, c * LANE:(c + 1) * LANE]


def _load_planes(ref, rows):
    return jnp.concatenate([ref[c, rows, :] for c in range(ref.shape[0])], axis=1)


def _copy_row(dst_ref, dst_row, src_ref, src_row):
    for c in range(src_ref.shape[0]):
        dst_ref[c, pl.ds(dst_row, 1), :] = src_ref[c, pl.ds(src_row, 1), :]


def _route(logits):
    lane = lax.broadcasted_iota(jnp.int32, logits.shape, 1)
    big = jnp.int32(10 ** 6)
    neg = -jnp.inf
    gl = jnp.where(lane < N_GROUPS, logits, neg)
    gmax = jnp.max(gl, axis=1, keepdims=True)
    grp = jnp.min(jnp.where(gl == gmax, lane, big), axis=1, keepdims=True)
    g_val = 1.0 / jnp.sum(jnp.exp(gl - gmax), axis=1, keepdims=True)
    lo = N_GROUPS + EXPERTS_PER_GROUP * grp
    el = jnp.where((lane >= lo) & (lane < lo + EXPERTS_PER_GROUP), logits, neg)
    v1 = jnp.max(el, axis=1, keepdims=True)
    i1 = jnp.min(jnp.where(el == v1, lane, big), axis=1, keepdims=True)
    el2 = jnp.where(lane == i1, neg, el)
    v2 = jnp.max(el2, axis=1, keepdims=True)
    i2 = jnp.min(jnp.where(el2 == v2, lane, big), axis=1, keepdims=True)
    r = jnp.exp(v2 - v1)
    w1 = g_val / (1.0 + r)
    w2 = w1 * r
    return ((i1 - N_GROUPS).astype(F32), (i2 - N_GROUPS).astype(F32), w1, w2)


def _merge_kernel(ohg_ref, oret_ref, ga_ref, gb_ref, x_ref, mods_ref, wohg_ref, woret_ref,
                  wout_ref, nw_ref, wrh_ref, wrl_ref, rb_ref,
                  xo_ref, hp_ref, route_ref, route_t_ref, cnt_out_ref, cnt_ref,
                  *, lc, tm, d, tiles_per_seg):
    i = pl.program_id(0)
    tg = tm // MERGE_GROUPS
    routed = []
    for g in range(MERGE_GROUPS):
        rows = slice(g * tg, (g + 1) * tg)
        row0 = i * tm + g * tg
        ohg = jnp.concatenate([ohg_ref[s, rows, :] for s in range(HG_HEADS)], axis=1)
        ga = jnp.concatenate([ga_ref[s, rows, :] for s in range(d // LANE)], axis=1).astype(F32)
        gb = jnp.concatenate([gb_ref[s, rows, :] for s in range(d // LANE)], axis=1).astype(F32)
        y_hg = jnp.dot(ohg, wohg_ref[...], preferred_element_type=F32)
        y_ret = jnp.dot(oret_ref[rows, :], woret_ref[...], preferred_element_type=F32)
        m = (ga * y_hg + gb * y_ret).astype(BF16)
        y = jnp.dot(m, wout_ref[...], preferred_element_type=F32)
        x = x_ref[rows, :] + _row_select(row0, tg, lc, mods_ref, 2, d) * y
        xo_ref[rows, :] = x

        xn = x * lax.rsqrt(jnp.mean(x * x, axis=-1, keepdims=True) + EPS) * nw_ref[...]
        h = (xn * (1.0 + _row_select(row0, tg, lc, mods_ref, 4, d))
             + _row_select(row0, tg, lc, mods_ref, 3, d))
        _store_planes(hp_ref, rows, _pack_bf16_pairs(h))

        h_hi = h.astype(BF16)
        h_lo = (h - h_hi.astype(F32)).astype(BF16)
        logits = (jnp.dot(h_hi, wrh_ref[...], preferred_element_type=F32)
                  + jnp.dot(h_lo, wrh_ref[...], preferred_element_type=F32)
                  + jnp.dot(h_hi, wrl_ref[...], preferred_element_type=F32)) + rb_ref[...]
        routed.append(_route(logits))
    e0, e1, w0, w1 = (jnp.concatenate(parts, axis=0) for parts in zip(*routed))

    @pl.when(i % tiles_per_seg == 0)
    def _():
        cnt_ref[...] = jnp.zeros_like(cnt_ref)

    lane_e = lax.broadcasted_iota(jnp.int32, (tm, LANE), 1).astype(F32)
    hot0 = lane_e == e0
    hot1 = lane_e == e1
    hot = jnp.where(hot0 | hot1, 1.0, 0.0)
    earlier = (lax.broadcasted_iota(jnp.int32, (tm, tm), 1)
               < lax.broadcasted_iota(jnp.int32, (tm, tm), 0))
    before = jnp.dot(jnp.where(earlier, 1.0, 0.0).astype(BF16), hot.astype(BF16),
                     preferred_element_type=F32) + cnt_ref[0:1, :]
    rank0 = jnp.sum(jnp.where(hot0, before, 0.0), axis=1, keepdims=True)
    rank1 = jnp.sum(jnp.where(hot1, before, 0.0), axis=1, keepdims=True)
    cnt_ref[...] = cnt_ref[...] + jnp.sum(hot, axis=0, keepdims=True)
    cnt_out_ref[0] = cnt_ref[...]

    lane = lax.broadcasted_iota(jnp.int32, (tm, LANE), 1)
    rec = jnp.zeros((tm, LANE), F32)
    for k, val in enumerate((e0, e1, w0, w1, rank0, rank1)):
        rec = jnp.where(lane == k, val, rec)
    route_ref[...] = rec[:, :ROUTE_W]
    eye = jnp.where(lax.broadcasted_iota(jnp.int32, (ROUTE_W, LANE), 0)
                    == lax.broadcasted_iota(jnp.int32, (ROUTE_W, LANE), 1), 1.0, 0.0)
    route_t_ref[0] = lax.dot_general(eye, rec, NT_DIMS, precision=lax.Precision.HIGHEST,
                                     preferred_element_type=F32)


def _merge(ohg, oret, p3, xc, mods, wohg, woret, wout, norm_w, wr_hi, wr_lo, rbias, *, lc, n_seg):
    l, d = xc.shape
    seg_tokens = l // n_seg
    tm = _largest_divisor(seg_tokens, MERGE_TILE, 16 * MERGE_GROUPS)
    tiles_per_seg = seg_tokens // tm
    n_slab = d // LANE
    full = lambda a: pl.BlockSpec(a.shape, lambda i: (0,) * a.ndim)
    kern = functools.partial(_merge_kernel, lc=lc, tm=tm, d=d, tiles_per_seg=tiles_per_seg)
    return pl.pallas_call(
        kern,
        grid=(l // tm,),
        in_specs=[pl.BlockSpec((HG_HEADS, tm, LANE), lambda i: (0, i, 0)),
                  pl.BlockSpec((tm, oret.shape[1]), lambda i: (i, 0)),
                  pl.BlockSpec((n_slab, tm, LANE), lambda i: (S_GA // n_slab, i, 0)),
                  pl.BlockSpec((n_slab, tm, LANE), lambda i: (S_GB // n_slab, i, 0)),
                  pl.BlockSpec((tm, d), lambda i: (i, 0)),
                  full(mods), full(wohg), full(woret), full(wout),
                  pl.BlockSpec((1, d), lambda i: (0, 0)),
                  full(wr_hi), full(wr_lo), full(rbias)],
        out_specs=[pl.BlockSpec((tm, d), lambda i: (i, 0)),
                   pl.BlockSpec((d // 2 // LANE, tm, LANE), lambda i: (0, i, 0)),
                   pl.BlockSpec((tm, ROUTE_W), lambda i: (i, 0)),
                   pl.BlockSpec((1, ROUTE_W, tm), lambda i: (i, 0, 0)),
                   pl.BlockSpec((1, 8, LANE), lambda i: (i // tiles_per_seg, 0, 0))],
        out_shape=[jax.ShapeDtypeStruct((l, d), F32),
                   jax.ShapeDtypeStruct((d // 2 // LANE, l, LANE), U32),
                   jax.ShapeDtypeStruct((l, ROUTE_W), F32),
                   jax.ShapeDtypeStruct((l // tm, ROUTE_W, tm), F32),
                   jax.ShapeDtypeStruct((n_seg, 8, LANE), F32)],
        scratch_shapes=[pltpu.VMEM((8, LANE), F32)],
        compiler_params=_params(("arbitrary",)),
        name="merge_router",
    )(ohg, oret, p3, p3, xc, mods, wohg, woret, wout, norm_w.reshape(1, d), wr_hi, wr_lo, rbias)


def _moe_kernel(be_ref, nused_ref, slot_ref, hp_ref, *rest,
                lc, tm, d, expert_steps, seg_tokens, n_tokens, final):
    w_refs = rest[:3 * MOE_PER_STEP]
    route_ref, x_ref, mods_ref, fw_ref, o_ref, xs_ref, g0_ref, g1_ref = rest[3 * MOE_PER_STEP:]
    s = pl.program_id(0)
    b = pl.program_id(1)

    @pl.when(b == 0)
    def _():
        xs_ref[...] = jnp.zeros_like(xs_ref)

        def scatter(g, a0):
            t0 = pl.multiple_of(g * ROW_GROUP, ROW_GROUP)
            for k in range(ROW_GROUP):
                _copy_row(xs_ref, slot_ref[a0 + k], hp_ref, t0 + k)
                _copy_row(xs_ref, slot_ref[n_tokens + a0 + k], hp_ref, t0 + k)
            return a0 + ROW_GROUP

        lax.fori_loop(0, seg_tokens // ROW_GROUP, scatter, s * seg_tokens)

    def expert_blocks(count):
        for k in range(count):
            wg_ref, wu_ref, wd_ref = w_refs[3 * k:3 * k + 3]
            rows = pl.ds(pl.multiple_of((b * MOE_PER_STEP + k) * MOE_BLOCK, MOE_BLOCK), MOE_BLOCK)
            lo, hi = _unpack_bf16_pairs(_load_planes(xs_ref, rows))
            x = jnp.concatenate([lo, hi], axis=1).astype(BF16)
            g = jnp.dot(x, wg_ref[0], preferred_element_type=F32)
            u = jnp.dot(x, wu_ref[0], preferred_element_type=F32)
            a = (_silu(g) * u).astype(BF16)
            _store_planes(xs_ref, rows,
                          _pack_bf16_pairs(jnp.dot(a, wd_ref[0], preferred_element_type=F32)))

    n_here = jnp.clip(nused_ref[s] - b * MOE_PER_STEP, 0, MOE_PER_STEP)
    for count in range(1, MOE_PER_STEP + 1):
        pl.when(jnp.logical_and(b < expert_steps, n_here == count))(
            functools.partial(expert_blocks, count))

    @pl.when(b >= expert_steps)
    def _():
        tile0 = s * seg_tokens + (b - expert_steps) * tm

        def gather(g, a0):
            r0 = pl.multiple_of(g * ROW_GROUP, ROW_GROUP)
            for k in range(ROW_GROUP):
                _copy_row(g0_ref, r0 + k, xs_ref, slot_ref[a0 + k])
                _copy_row(g1_ref, r0 + k, xs_ref, slot_ref[n_tokens + a0 + k])
            return a0 + ROW_GROUP

        lax.fori_loop(0, tm // ROW_GROUP, gather, tile0)
        lo0, hi0 = _unpack_bf16_pairs(_load_planes(g0_ref, slice(None)))
        lo1, hi1 = _unpack_bf16_pairs(_load_planes(g1_ref, slice(None)))
        w0 = route_ref[:, 2:3]
        w1 = route_ref[:, 3:4]
        y = jnp.concatenate([w0 * lo0 + w1 * lo1, w0 * hi0 + w1 * hi1], axis=1)
        x = x_ref[...] + _row_select(tile0, tm, lc, mods_ref, 5, d) * y
        if final:
            x = x * lax.rsqrt(jnp.mean(x * x, axis=-1, keepdims=True) + EPS) * fw_ref[...]
        o_ref[...] = x


def _moe(block_expert, n_used, slot, hp, wg, wu, wd, route, xc, mods, final_w,
         *, n_seg, nbs, layer, lc, final):
    l, d = xc.shape
    n_planes = hp.shape[0]
    seg_tokens = l // n_seg
    d_e = wg.shape[3]
    tm = _largest_divisor(seg_tokens, 320, 8)
    tiles = seg_tokens // tm
    per = MOE_PER_STEP
    assert nbs % per == 0
    expert_steps = nbs // per
    kern = functools.partial(_moe_kernel, lc=lc, tm=tm, d=d, expert_steps=expert_steps,
                             seg_tokens=seg_tokens, n_tokens=l, final=final)
    w_specs = []
    for k in range(per):
        w_idx = lambda s, b, be, nu, *_, k=k: (
            layer, be[s * nbs + jnp.minimum(b * per + k, jnp.maximum(nu[s] - 1, 0))], 0, 0)
        w_specs += [pl.BlockSpec((None, 1, d, d_e), w_idx), pl.BlockSpec((None, 1, d, d_e), w_idx),
                    pl.BlockSpec((None, 1, d_e, d), w_idx)]
    tile = lambda s, b: s * tiles + jnp.clip(b - expert_steps, 0, tiles - 1)
    row = lambda s, b, *_: (tile(s, b), 0)
    if final:
        assert lc % tm == 0
        out_rows = l - lc
        out_row = lambda s, b, *_: (jnp.maximum(tile(s, b) - lc // tm, 0), 0)
    else:
        out_rows, out_row = l, row
    grid_spec = pltpu.PrefetchScalarGridSpec(
        num_scalar_prefetch=3,
        grid=(n_seg, expert_steps + tiles),
        in_specs=[pl.BlockSpec((n_planes, seg_tokens, LANE), lambda s, b, *_: (0, s, 0))]
        + w_specs + [
                  pl.BlockSpec((tm, ROUTE_W), row),
                  pl.BlockSpec((tm, d), row),
                  pl.BlockSpec(mods.shape, lambda s, b, *_: (0, 0)),
                  pl.BlockSpec((1, d), lambda s, b, *_: (0, 0))],
        out_specs=pl.BlockSpec((tm, d), out_row),
        scratch_shapes=[pltpu.VMEM((n_planes, nbs * MOE_BLOCK, LANE), U32),
                        pltpu.VMEM((n_planes, tm, LANE), U32),
                        pltpu.VMEM((n_planes, tm, LANE), U32)])
    return pl.pallas_call(
        kern,
        grid_spec=grid_spec,
        out_shape=jax.ShapeDtypeStruct((out_rows, d), F32),
        compiler_params=_params(("arbitrary", "arbitrary")),
        name="moe",
    )(block_expert, n_used, slot, hp, *((wg, wu, wd) * per), route, xc, mods,
      final_w.reshape(1, d))


def _dispatch_plan(route_t, counts, nbs):
    counts_f32 = counts
    counts = counts[:, 0, :N_EXPERTS].astype(jnp.int32)
    padded = ((counts + MOE_BLOCK - 1) // MOE_BLOCK) * MOE_BLOCK
    pad_end = jnp.cumsum(padded, axis=1)
    starts = jnp.arange(nbs, dtype=jnp.int32) * MOE_BLOCK
    block_expert = jnp.minimum(
        jnp.sum((pad_end[:, None, :] <= starts[None, :, None]).astype(jnp.int32), axis=2),
        N_EXPERTS - 1)
    n_used = pad_end[:, -1] // MOE_BLOCK
    slot = _slots(route_t, counts_f32)
    return (jnp.swapaxes(slot, 0, 1).reshape(-1), block_expert.reshape(-1).astype(jnp.int32),
            n_used.astype(jnp.int32))


def _slot_kernel(rt_ref, cnt_ref, slot_ref, *, tiles_per_seg):
    s = pl.program_id(0) // tiles_per_seg
    cnt = cnt_ref[s]
    blocks = jnp.floor((cnt + (MOE_BLOCK - 1)) * (1.0 / MOE_BLOCK))
    before = jnp.where(lax.broadcasted_iota(jnp.int32, (LANE, LANE), 0)
                       < lax.broadcasted_iota(jnp.int32, (LANE, LANE), 1), 1.0, 0.0)
    first_block = jnp.dot(blocks.astype(BF16), before.astype(BF16),
                          preferred_element_type=F32)
    tm = rt_ref.shape[2]
    expert_id = lax.broadcasted_iota(jnp.int32, (LANE, tm), 0).astype(F32)
    for k in range(2):
        hot = jnp.where(expert_id == rt_ref[0, k:k + 1, :], 1.0, 0.0)
        first = jnp.dot(first_block.astype(BF16), hot.astype(BF16),
                        preferred_element_type=F32)[0:1] * MOE_BLOCK
        slot_ref[0, k:k + 1, :] = (first + rt_ref[0, 4 + k:5 + k, :]).astype(jnp.int32)


def _slots(route_t, counts):
    n_tiles, _, tm = route_t.shape
    n_seg = counts.shape[0]
    kern = functools.partial(_slot_kernel, tiles_per_seg=n_tiles // n_seg)
    return pl.pallas_call(
        kern,
        grid=(n_tiles,),
        in_specs=[pl.BlockSpec((1, ROUTE_W, tm), lambda i: (i, 0, 0)),
                  pl.BlockSpec(counts.shape, lambda i: (0, 0, 0))],
        out_specs=pl.BlockSpec((1, 2, tm), lambda i: (i, 0, 0)),
        out_shape=jax.ShapeDtypeStruct((n_tiles, 2, tm), jnp.int32),
        compiler_params=_params(("arbitrary",)),
        name="moe_slots",
    )(route_t, counts)


def _rope_tables(lc, t):
    quarter = RET_DK // 4
    n_rows = t // GRID_W
    inv = ROPE_BASE ** (-np.arange(0, 2 * quarter, 2, dtype=np.float64) / (2 * quarter))
    ang_r = np.arange(n_rows, dtype=np.float64)[:, None] * inv
    ang_c = np.arange(GRID_W, dtype=np.float64)[:, None] * inv

    def expand(row_part, col_part):
        r = np.broadcast_to(row_part[:, None, :], (n_rows, GRID_W, 2 * quarter))
        c = np.broadcast_to(col_part[None, :, :], (n_rows, GRID_W, 2 * quarter))
        return np.concatenate([r, c], axis=2).reshape(t, RET_DK)

    two = lambda a: np.concatenate([a, a], axis=1)
    cos = expand(two(np.cos(ang_r)), two(np.cos(ang_c)))
    sin = expand(np.concatenate([-np.sin(ang_r), np.sin(ang_r)], axis=1),
                 np.concatenate([-np.sin(ang_c), np.sin(ang_c)], axis=1))
    cos = np.concatenate([np.ones((lc, RET_DK)), cos], axis=0).astype(np.float32)
    sin = np.concatenate([np.zeros((lc, RET_DK)), sin], axis=0).astype(np.float32)
    return jnp.asarray(cos), jnp.asarray(sin)


def kernel(x, c, ctx, c_ctx, w_mod, b_mod, norm_mix_w, norm_ffn_w, w_in, hgrn_lb_logits, hgrn_norm_w,
           ret_decay_logit, ret_gn_w, w_o_hgrn, w_o_ret, w_out, router_group_w, router_group_b,
           router_expert_w, router_expert_b, expert_w_gate, expert_w_up, expert_w_down, final_norm_w):
    b_, t_, d = x.shape
    assert b_ == 1
    lc = ctx.shape[1]
    depth = w_mod.shape[0]
    l = lc + t_
    assert lc % HG_BLOCK == 0 and l % HG_BLOCK == 0 and lc % RET_BLOCK == 0 and l % RET_BLOCK == 0
    n_seg = MOE_SEGMENTS
    assert l % n_seg == 0
    seg_tokens = l // n_seg
    nbs = pl.cdiv(2 * seg_tokens + N_EXPERTS * (MOE_BLOCK - 1), MOE_BLOCK)
    nbs = pl.cdiv(nbs, MOE_PER_STEP) * MOE_PER_STEP
    assert nbs <= 256

    xc = jnp.concatenate([ctx[0], x[0]], axis=0)
    cc = jnp.zeros((8, d), F32).at[0].set(c[0]).at[1].set(c_ctx)
    mods_all = _modulation(cc, w_mod, b_mod)
    cos_t, sin_t = _rope_tables(lc, t_)
    w_in_bf = w_in.astype(BF16)
    w_gate_bf, w_up_bf, w_down_bf = (w.astype(BF16)
                                     for w in (expert_w_gate, expert_w_up, expert_w_down))

    for layer in range(depth):
        mods = mods_all[layer]
        p3, lf3 = _inproj(xc, mods, norm_mix_w[layer], w_in_bf, hgrn_lb_logits,
                          cos_t, sin_t, layer=layer, lc=lc)
        sb_hg, sb_ret = _mixer_bwd(ret_decay_logit[layer], p3, lf3, lc=lc)
        ohg, oret = _mixer_fwd(ret_decay_logit[layer], p3, lf3, sb_hg, sb_ret,
                               hgrn_norm_w[layer], ret_gn_w[layer])

        wr = jnp.concatenate([router_group_w[layer], router_expert_w[layer]], axis=1)
        wr = jnp.pad(wr, ((0, 0), (0, LANE - wr.shape[1])))
        wr_hi = wr.astype(BF16)
        wr_lo = (wr - wr_hi.astype(F32)).astype(BF16)
        rbias = jnp.pad(jnp.concatenate([router_group_b[layer], router_expert_b[layer]]),
                        (0, LANE - N_GROUPS - N_EXPERTS)).reshape(1, LANE)
        xc, hp, route, route_t, counts = _merge(
            ohg, oret, p3, xc, mods, w_o_hgrn[layer].astype(BF16), w_o_ret[layer].astype(BF16),
            w_out[layer].astype(BF16), norm_ffn_w[layer], wr_hi, wr_lo, rbias, lc=lc, n_seg=n_seg)

        slot, block_expert, n_used = _dispatch_plan(route_t, counts, nbs)
        xc = _moe(block_expert, n_used, slot, hp, w_gate_bf, w_up_bf, w_down_bf, route, xc, mods,
                  final_norm_w, n_seg=n_seg, nbs=nbs, layer=layer, lc=lc,
                  final=(layer == depth - 1))

    return xc[None]
```

```python
import functools

import jax
import jax.numpy as jnp
import numpy as np
from jax import lax
from jax.experimental import pallas as pl
from jax.experimental.pallas import tpu as pltpu

F32 = jnp.float32
BF16 = jnp.bfloat16
U32 = jnp.uint32

GRID_W = 64
HG_HEADS = 4
HG_DK = 128
HG_DV = 128
F_MIN = 1e-30
RET_HEADS = 4
RET_DK = 128
RET_DV = 256
ROPE_BASE = 10000.0
N_GROUPS = 4
EXPERTS_PER_GROUP = 8
N_EXPERTS = N_GROUPS * EXPERTS_PER_GROUP
N_MOD = 6
EPS = 1e-6

LANE = 128
COL_TILE = 512
SLABS_PER_TILE = COL_TILE // LANE
HG_CHUNK = 64
HG_SUB = 16
HG_BLOCK = 256
LOG2E = 1.4426950408889634
HG_FAST_SUB = 32
HG_CAP = 100.0
CUM_GROUP = 256
CUM_TERMS = 2
RET_BLOCK = 256
MOE_SEGMENTS = 5
MOE_BLOCK = 256
MOE_PER_STEP = 2
ROW_GROUP = 16
MERGE_GROUPS = 1
MERGE_TILE = 832
ROUTE_W = 8
VMEM_LIMIT = 56 * 1024 * 1024

NT_DIMS = (((1,), (1,)), ((), ()))
TN_DIMS = (((0,), (0,)), ((), ()))

SEG_SLABS = (4, 4, 4, 4, 4, 4, 4, 8, 8, 8, 8)
SEG_ORDER = (7, 8, 9, 10, 0, 1, 2, 3, 4, 5, 6)
_starts = {}
_pos = 0
for _seg in SEG_ORDER:
    _starts[_seg] = _pos
    _pos += SEG_SLABS[_seg]
SEG_START = tuple(_starts[_seg] for _seg in range(len(SEG_SLABS)))
(S_HQ, S_KF, S_KB, S_HI, S_HGATE, S_RQ, S_RK, S_RV, S_RG, S_GA, S_GB) = SEG_START
assert SEG_ORDER == tuple(range(SEG_ORDER[0], len(SEG_SLABS))) + tuple(range(SEG_ORDER[0]))
TILE_ROTATION = sum(SEG_SLABS[:SEG_ORDER[0]]) // SLABS_PER_TILE
N_SLABS = sum(SEG_SLABS)


def _params(sem):
    return pltpu.CompilerParams(dimension_semantics=sem, vmem_limit_bytes=VMEM_LIMIT)


def _sigmoid(x):
    return 1.0 / (1.0 + jnp.exp(-x))


def _silu(x):
    return x * _sigmoid(x)


def _largest_divisor(n, cap, multiple):
    best = None
    for d in range(multiple, cap + 1, multiple):
        if n % d == 0:
            best = d
    assert best is not None, (n, cap, multiple)
    return best


def _row_select(row0, n_rows, lc, mods_ref, k, d):
    rows = row0 + lax.broadcasted_iota(jnp.int32, (n_rows, 1), 0)
    lat = mods_ref[0:1, k * d:(k + 1) * d]
    ctx = mods_ref[1:2, k * d:(k + 1) * d]
    return jnp.where(rows < lc, ctx, lat)


def _mod_kernel(cc_ref, w_ref, b_ref, o_ref):
    cc = cc_ref[...]
    s = _silu(cc)
    o_ref[0] = jnp.dot(s, w_ref[0], precision=lax.Precision.HIGHEST,
                       preferred_element_type=F32) + b_ref[0]


def _modulation(cc, w_mod, b_mod):
    depth, d, n = w_mod.shape
    tn = _largest_divisor(n, 1536, LANE)
    return pl.pallas_call(
        _mod_kernel,
        grid=(depth, n // tn),
        in_specs=[pl.BlockSpec((8, d), lambda l, j: (0, 0)),
                  pl.BlockSpec((1, d, tn), lambda l, j: (l, 0, j)),
                  pl.BlockSpec((1, 1, tn), lambda l, j: (l, 0, j))],
        out_specs=pl.BlockSpec((1, 8, tn), lambda l, j: (l, 0, j)),
        out_shape=jax.ShapeDtypeStruct((depth, 8, n), F32),
        compiler_params=_params(("arbitrary", "arbitrary")),
        name="modulation",
    )(cc, w_mod, b_mod.reshape(depth, 1, n))


def _inproj_kernel(x_ref, mods_ref, nw_ref, w_ref, lbl_ref, cos_ref, sin_ref,
                   p_ref, lf_ref, h_ref, *, layer, lc, tm, d):
    i = pl.program_id(0)
    j = pl.program_id(1)

    def normalise(rows):
        x = x_ref[rows, :]
        xn = x * lax.rsqrt(jnp.mean(x * x, axis=-1, keepdims=True) + EPS) * nw_ref[...]
        shift = _row_select(i * tm + rows.start, CUM_GROUP, lc, mods_ref, 0, d)
        scale = _row_select(i * tm + rows.start, CUM_GROUP, lc, mods_ref, 1, d)
        h_ref[rows, :] = (xn * (1.0 + scale) + shift).astype(BF16)

    def for_row_groups(epilogue, first_tile=False):
        for r0 in range(0, tm, CUM_GROUP):
            rows = slice(r0, r0 + CUM_GROUP)
            if first_tile:
                normalise(rows)
            acc = jnp.dot(h_ref[rows, :], w_ref[...], preferred_element_type=F32)
            epilogue(acc, rows)

    def put(val, rows):
        v = val.astype(BF16)
        for s in range(SLABS_PER_TILE):
            p_ref[s, rows, :] = v[:, s * LANE:(s + 1) * LANE]

    tiles = lambda seg: tuple(range(SEG_START[seg] // SLABS_PER_TILE,
                                    (SEG_START[seg] + SEG_SLABS[seg]) // SLABS_PER_TILE))
    in_tiles = lambda segs: functools.reduce(
        jnp.logical_or, [j == t for seg in segs for t in tiles(seg)])

    @pl.when(in_tiles((0,)))
    def _():
        for_row_groups(lambda acc, rows: put(_silu(acc) * (HG_DK ** -0.5), rows))

    @pl.when(in_tiles((1, 2)))
    def _():
        logits = lbl_ref[jnp.clip(j - tiles(1)[0], 0, 1)]
        e = jnp.exp(logits - jnp.max(logits, axis=0, keepdims=True))
        p = e / jnp.sum(e, axis=0, keepdims=True)
        lb = jnp.zeros_like(p[0:1])
        for r in range(1, layer + 1):
            lb = lb + p[r:r + 1]
        lb = jnp.clip(lb, 0.0, 1.0 - 1e-6)
        g = CUM_GROUP
        ri = lax.broadcasted_iota(jnp.int32, (g, g), 0)
        ci = lax.broadcasted_iota(jnp.int32, (g, g), 1)
        same_chunk = (ri // HG_CHUNK) == (ci // HG_CHUNK)
        before = jnp.where(same_chunk & (ci <= ri), 1.0, 0.0)
        after = jnp.where(same_chunk & (ci >= ri), 1.0, 0.0)
        tri = jnp.where(j == tiles(1)[0], before, after).astype(BF16)

        log_f_terms = []

        def epilogue(acc, rows):
            sig = _sigmoid(acc)
            put((1.0 - lb) * (1.0 - sig), rows)
            rest = jnp.log(jnp.maximum(lb + (1.0 - lb) * sig, F_MIN))
            terms = []
            for _ in range(CUM_TERMS):
                terms.append(rest.astype(BF16))
                rest = rest - terms[-1].astype(F32)
            log_f_terms.append((rows, terms))

        for_row_groups(epilogue)
        for rows, terms in log_f_terms:
            cum = jnp.zeros((g, COL_TILE), F32)
            for term in terms:
                cum = cum + jnp.dot(tri, term, preferred_element_type=F32)
            cum = cum * LOG2E
            for s in range(SLABS_PER_TILE):
                lf_ref[s, rows, :] = cum[:, s * LANE:(s + 1) * LANE]

    assert 0 in tiles(7)

    @pl.when(j == 0)
    def _():
        for_row_groups(put, first_tile=True)

    @pl.when(jnp.logical_and(in_tiles((3, 7)), j != 0))
    def _():
        for_row_groups(put)

    @pl.when(in_tiles((4, 8)))
    def _():
        for_row_groups(lambda acc, rows: put(_silu(acc), rows))

    @pl.when(in_tiles((5, 6)))
    def _():
        scale = jnp.where(j == tiles(5)[0], RET_DK ** -0.5, 1.0)

        def epilogue(acc, rows):
            xq = acc * scale
            n = xq.shape[1]
            lane = lax.broadcasted_iota(jnp.int32, xq.shape, 1)
            partner = jnp.where((lane & 32) == 0, pltpu.roll(xq, n - 32, axis=1),
                                pltpu.roll(xq, 32, axis=1))
            cos = jnp.concatenate([cos_ref[rows, :]] * SLABS_PER_TILE, axis=1)
            sin = jnp.concatenate([sin_ref[rows, :]] * SLABS_PER_TILE, axis=1)
            put(xq * cos + partner * sin, rows)

        for_row_groups(epilogue)

    @pl.when(in_tiles((9, 10)))
    def _():
        for_row_groups(lambda acc, rows: put(_sigmoid(acc), rows))


def _inproj(xc, mods, norm_w, w_in_bf, lb_logits, cos_t, sin_t, *, layer, lc):
    l, d = xc.shape
    d_in = w_in_bf.shape[2]
    assert d_in == N_SLABS * LANE
    tm = _largest_divisor(l, 1280, CUM_GROUP)
    n_col = d_in // COL_TILE
    kf_tile = S_KF // SLABS_PER_TILE
    kern = functools.partial(_inproj_kernel, layer=layer, lc=lc, tm=tm, d=d)
    return pl.pallas_call(
        kern,
        grid=(l // tm, n_col),
        in_specs=[pl.BlockSpec((tm, d), lambda i, j: (i, 0)),
                  pl.BlockSpec(mods.shape, lambda i, j: (0, 0)),
                  pl.BlockSpec((1, d), lambda i, j: (0, 0)),
                  pl.BlockSpec((None, d, COL_TILE),
                               lambda i, j: (layer, 0, (j + TILE_ROTATION) % n_col)),
                  pl.BlockSpec(lb_logits.shape, lambda i, j: (0, 0, 0)),
                  pl.BlockSpec((tm, LANE), lambda i, j: (i, 0)),
                  pl.BlockSpec((tm, LANE), lambda i, j: (i, 0))],
        out_specs=[pl.BlockSpec((SLABS_PER_TILE, tm, LANE), lambda i, j: (j, i, 0)),
                   pl.BlockSpec((SLABS_PER_TILE, tm, LANE),
                                lambda i, j: (jnp.clip(j - kf_tile, 0, 1), i, 0))],
        out_shape=[jax.ShapeDtypeStruct((N_SLABS, l, LANE), BF16),
                   jax.ShapeDtypeStruct((2 * HG_HEADS, l, LANE), F32)],
        scratch_shapes=[pltpu.VMEM((tm, d), BF16)],
        compiler_params=_params(("arbitrary", "arbitrary")),
        name="inproj",
    )(xc, mods, norm_w.reshape(1, d), w_in_bf, lb_logits, cos_t, sin_t)


def _hg_bwd_kernel(kb_ref, v_ref, bc_ref, sb_ref, s_ref, *, n_chunks):
    for h in range(HG_HEADS):
        s = s_ref[h]
        for cc in reversed(range(n_chunks)):
            rows = slice(cc * HG_CHUNK, (cc + 1) * HG_CHUNK)
            bb = bc_ref[h, rows, :]
            k = kb_ref[h, rows, :].astype(F32)
            sb_ref[cc, h] = s.astype(BF16)
            kt = (k * jnp.exp2(bb[0:1, :] - bb)).astype(BF16)
            s = s * jnp.exp2(bb[0:1, :]) + lax.dot_general(
                v_ref[h, rows, :], kt, TN_DIMS, preferred_element_type=F32)
        s_ref[h] = s


def _bwd_block_order(i, n_ctx_blocks, n_blocks):
    return jnp.where(i < n_ctx_blocks, n_ctx_blocks - 1 - i, n_blocks - 1 - (i - n_ctx_blocks))


def _mixer_bwd_kernel(logit_ref, kb_ref, hv_ref, bc_ref, rk_ref, rv_ref,
                      hsb_ref, rsb_ref, hs_ref, rs_ref, *, n_chunks):
    @pl.when(pl.program_id(0) == 0)
    def _():
        hs_ref[...] = jnp.zeros_like(hs_ref)
        rs_ref[...] = jnp.zeros_like(rs_ref)

    _hg_bwd_kernel(kb_ref, hv_ref, bc_ref, hsb_ref, hs_ref, n_chunks=n_chunks)
    _ret_bwd_kernel(logit_ref, rk_ref, rv_ref, rsb_ref, rs_ref)


def _mixer_bwd(logit, p3, lf3, *, lc):
    l = p3.shape[1]
    tb = HG_BLOCK
    assert tb == RET_BLOCK
    nb, nbc = l // tb, lc // tb
    ncb = tb // HG_CHUNK
    order = lambda i: _bwd_block_order(i, nbc, nb)
    slabs = lambda n, start: pl.BlockSpec((n, tb, LANE), lambda i: (start // n, order(i), 0))
    kern = functools.partial(_mixer_bwd_kernel, n_chunks=ncb)
    return pl.pallas_call(
        kern,
        grid=(nb,),
        in_specs=[pl.BlockSpec(memory_space=pltpu.SMEM),
                  slabs(HG_HEADS, S_KB), slabs(HG_HEADS, S_HI),
                  pl.BlockSpec((HG_HEADS, tb, LANE), lambda i: (1, order(i), 0)),
                  slabs(RET_HEADS, S_RK), slabs(2 * RET_HEADS, S_RV)],
        out_specs=[pl.BlockSpec((ncb, HG_HEADS, HG_DV, HG_DK), lambda i: (order(i), 0, 0, 0)),
                   pl.BlockSpec((1, RET_HEADS, RET_DK, RET_DV), lambda i: (order(i), 0, 0, 0))],
        out_shape=[jax.ShapeDtypeStruct((l // HG_CHUNK, HG_HEADS, HG_DV, HG_DK), BF16),
                   jax.ShapeDtypeStruct((nb, RET_HEADS, RET_DK, RET_DV), BF16)],
        scratch_shapes=[pltpu.VMEM((HG_HEADS, HG_DV, HG_DK), F32),
                        pltpu.VMEM((RET_HEADS, RET_DK, RET_DV), F32)],
        compiler_params=_params(("arbitrary",)),
        name="mixer_bwd_state",
    )(logit, p3, p3, lf3, p3, p3)


def _hg_scores(q, q_edge, k32, b, fwd, sub, exact, k32_ref=None, b_ref=None):
    c = HG_CHUNK
    n_sub = c // sub
    cap = 0.0 if exact else HG_CAP
    lane = lax.broadcasted_iota(jnp.int32, (sub, c), 1)
    blocks = []
    for blk in range(n_sub):
        r = blk * sub
        q_blk = q[r:r + sub]
        b_blk = b[r:r + sub]
        edge = blk == 0 if fwd else blk == n_sub - 1
        if edge and exact:
            a = jnp.zeros((sub, c), F32)
        else:
            if edge:
                qt = q_edge[r:r + sub]
                ref_minus_b = -b
            else:
                ref = b[r - 1:r] if fwd else b[r + sub:r + sub + 1]
                qt = q_blk * jnp.exp2(b_blk - ref)
                ref_minus_b = ref - b
            kt = (k32 * jnp.exp2(jnp.minimum(ref_minus_b, cap))).astype(BF16)
            a = lax.dot_general(qt.astype(BF16), kt, NT_DIMS, preferred_element_type=F32)
        if exact:
            for jj in range(sub):
                s = r + jj
                e = jnp.exp2(b_blk - b_ref[s:s + 1, :])
                col = jnp.sum(q_blk * k32_ref[s:s + 1, :] * e, axis=1, keepdims=True)
                a = jnp.where(lane == s, col, a)
        blocks.append(a)
    a = jnp.concatenate(blocks, axis=0)
    ri = lax.broadcasted_iota(jnp.int32, (c, c), 0)
    ci = lax.broadcasted_iota(jnp.int32, (c, c), 1)
    return jnp.where((ci <= ri) if fwd else (ci >= ri), a, 0.0)


def _chunk_rows(cc):
    if isinstance(cc, int):
        return slice(cc * HG_CHUNK, (cc + 1) * HG_CHUNK)
    return pl.ds(pl.multiple_of(cc * HG_CHUNK, HG_CHUNK), HG_CHUNK)


def _hg_head_scores(refs, cc, h, sub, exact):
    (q_ref, kf_ref, kb_ref, v_ref, _, bc_ref, sb_ref, _, _, s_ref, k32_ref, b_ref) = refs
    rows = _chunk_rows(cc)
    q = q_ref[h, rows, :].astype(F32)
    v = v_ref[h, rows, :]
    bf = bc_ref[h, rows, :]
    bb = bc_ref[HG_HEADS + h, rows, :]
    kf = kf_ref[h, rows, :].astype(F32)
    kb = kb_ref[h, rows, :].astype(F32)
    qf = q * jnp.exp2(bf)
    qb = q * jnp.exp2(bb)
    s = s_ref[h]
    inter = lax.dot_general(jnp.concatenate([qf, qb], axis=1).astype(BF16),
                            jnp.concatenate([s.astype(BF16), sb_ref[cc, h]], axis=1),
                            NT_DIMS, preferred_element_type=F32)
    if exact:
        k32_ref[...] = kf
        b_ref[...] = bf
    a = _hg_scores(q, qf, kf, bf, True, sub, exact, k32_ref, b_ref)
    if exact:
        k32_ref[...] = kb
        b_ref[...] = bb
    a = a + _hg_scores(q, qb, kb, bb, False, sub, exact, k32_ref, b_ref)
    b_last = bf[HG_CHUNK - 1:HG_CHUNK, :]
    kt = (kf * jnp.exp2(b_last - bf)).astype(BF16)
    s_ref[h] = s * jnp.exp2(b_last) + lax.dot_general(v, kt, TN_DIMS,
                                                      preferred_element_type=F32)
    return inter, a, v


def _hg_head_finish(refs, cc, h, inter, a, v):
    gate_ref, nw_ref, o_ref = refs[4], refs[7], refs[8]
    rows = _chunk_rows(cc)
    o = inter + jnp.dot(a.astype(BF16), v, preferred_element_type=F32)
    o = o * lax.rsqrt(jnp.mean(o * o, axis=-1, keepdims=True) + EPS)
    o = o * nw_ref[h] * gate_ref[h, rows, :].astype(F32)
    o_ref[h, rows, :] = o.astype(BF16)


def _hg_min_block_decay(bc_ref):
    n = bc_ref.shape[1] // HG_FAST_SUB
    first = lax.broadcasted_iota(jnp.int32, (n, LANE), 0) % 2 == 0
    worst = None
    for h in range(HG_HEADS):
        ends = bc_ref.at[h][pl.ds(HG_FAST_SUB - 1, n, stride=HG_FAST_SUB), :]
        fwd = jnp.where(first, ends, ends - pltpu.roll(ends, 1, axis=0))
        starts = bc_ref.at[HG_HEADS + h][pl.ds(0, n, stride=HG_FAST_SUB), :]
        bwd = jnp.where(first, starts - pltpu.roll(starts, n - 1, axis=0), starts)
        m = jnp.minimum(fwd, bwd)
        worst = m if worst is None else jnp.minimum(worst, m)
    return jnp.min(worst)


def _hg_fwd_kernel(*refs, n_chunks, alongside=lambda: None):
    bc_ref, s_ref = refs[5], refs[9]

    @pl.when(pl.program_id(0) == 0)
    def _():
        s_ref[...] = jnp.zeros_like(s_ref)

    factored_ok = _hg_min_block_decay(bc_ref) >= -HG_CAP

    @pl.when(factored_ok)
    def _():
        staged_prev = None
        for cc in range(n_chunks):
            staged = [_hg_head_scores(refs, cc, h, HG_FAST_SUB, False) for h in range(HG_HEADS)]
            if staged_prev is not None:
                for h, parts in enumerate(staged_prev):
                    _hg_head_finish(refs, cc - 1, h, *parts)
            staged_prev = staged
        alongside()
        for h, parts in enumerate(staged_prev):
            _hg_head_finish(refs, n_chunks - 1, h, *parts)

    @pl.when(jnp.logical_not(factored_ok))
    def _():
        def body(n, carry):
            cc, h = n // HG_HEADS, n % HG_HEADS
            _hg_head_finish(refs, cc, h, *_hg_head_scores(refs, cc, h, HG_SUB, True))
            return carry
        lax.fori_loop(0, n_chunks * HG_HEADS, body, 0)
        alongside()


def _log_sigmoid(x):
    return jnp.minimum(x, 0.0) - jnp.log1p(jnp.exp(-jnp.abs(x)))


def _ret_log_gamma(logit_ref, direction, h, shape):
    return _log_sigmoid(jnp.full(shape, logit_ref[direction, h], F32))


def _ret_bwd_kernel(logit_ref, k_ref, v_ref, sb_ref, s_ref):
    c = RET_BLOCK
    t = lax.broadcasted_iota(jnp.int32, (c, RET_DK), 0).astype(F32)
    for h in range(RET_HEADS):
        lg = _ret_log_gamma(logit_ref, 1, h, (c, RET_DK))
        s = s_ref[h]
        sb_ref[0, h] = s.astype(BF16)
        kt = (k_ref[h].astype(F32) * jnp.exp(t * lg)).astype(BF16)
        v = jnp.concatenate([v_ref[2 * h], v_ref[2 * h + 1]], axis=1)
        s_ref[h] = s * jnp.exp(c * lg[0:1, 0:1]) + lax.dot_general(
            kt, v, TN_DIMS, preferred_element_type=F32)


def _ret_fwd_init(logit_ref, s_ref, dmat_ref, dvec_ref):
    c = RET_BLOCK

    @pl.when(pl.program_id(0) == 0)
    def _():
        s_ref[...] = jnp.zeros_like(s_ref)
        ri = lax.broadcasted_iota(jnp.int32, (c, c), 0)
        ci = lax.broadcasted_iota(jnp.int32, (c, c), 1)
        dist = (ri - ci).astype(F32)
        t = lax.broadcasted_iota(jnp.int32, (c, RET_DK), 0).astype(F32)
        for h in range(RET_HEADS):
            lgf = _ret_log_gamma(logit_ref, 0, h, (c, c))
            lgb = _ret_log_gamma(logit_ref, 1, h, (c, c))
            dmat_ref[h] = (jnp.where(ci <= ri, jnp.exp(jnp.maximum(dist, 0.0) * lgf), 0.0)
                           + jnp.where(ci >= ri, jnp.exp(jnp.maximum(-dist, 0.0) * lgb), 0.0))
            lgf = _ret_log_gamma(logit_ref, 0, h, (c, RET_DK))
            lgb = _ret_log_gamma(logit_ref, 1, h, (c, RET_DK))
            dvec_ref[3 * h] = jnp.exp((t + 1.0) * lgf)
            dvec_ref[3 * h + 1] = jnp.exp((c - t) * lgb)
            dvec_ref[3 * h + 2] = jnp.exp((c - 1.0 - t) * lgf)


def _ret_fwd_body(logit_ref, q_ref, k_ref, v_ref, gate_ref, sb_ref, gnw_ref,
                  o_ref, s_ref, dmat_ref, dvec_ref):
    c = RET_BLOCK
    for h in range(RET_HEADS):
        lgf = _ret_log_gamma(logit_ref, 0, h, (c, RET_DK))[0:1, 0:1]
        q = q_ref[h]
        k = k_ref[h]
        q32 = q.astype(F32)
        v = jnp.concatenate([v_ref[2 * h], v_ref[2 * h + 1]], axis=1)
        s = s_ref[h]
        sc = lax.dot_general(q, k, NT_DIMS, preferred_element_type=F32) * dmat_ref[h]
        o = jnp.dot(sc.astype(BF16), v, preferred_element_type=F32)
        o = o + jnp.dot((q32 * dvec_ref[3 * h]).astype(BF16), s.astype(BF16),
                        preferred_element_type=F32)
        o = o + jnp.dot((q32 * dvec_ref[3 * h + 1]).astype(BF16), sb_ref[0, h],
                        preferred_element_type=F32)
        kt = (k.astype(F32) * dvec_ref[3 * h + 2]).astype(BF16)
        s_ref[h] = s * jnp.exp(c * lgf) + lax.dot_general(
            kt, v, TN_DIMS, preferred_element_type=F32)

        mu = jnp.mean(o, axis=-1, keepdims=True)
        dev = o - mu
        var = jnp.mean(dev * dev, axis=-1, keepdims=True)
        o = dev * lax.rsqrt(var + EPS) * gnw_ref[h]
        gate = jnp.concatenate([gate_ref[2 * h], gate_ref[2 * h + 1]], axis=1).astype(F32)
        o_ref[:, h * RET_DV:(h + 1) * RET_DV] = (o * gate).astype(BF16)


N_HG_FWD_IN = 8
N_RET_FWD_IN = 7


def _mixer_fwd_kernel(*refs, n_chunks):
    hg_in = refs[:N_HG_FWD_IN]
    ret_in = refs[N_HG_FWD_IN:N_HG_FWD_IN + N_RET_FWD_IN]
    (hg_out, ret_out, hs_ref, k32_ref, b_ref, rs_ref, dmat_ref,
     dvec_ref) = refs[N_HG_FWD_IN + N_RET_FWD_IN:]
    _ret_fwd_init(ret_in[0], rs_ref, dmat_ref, dvec_ref)
    retention = functools.partial(_ret_fwd_body, *ret_in, ret_out, rs_ref, dmat_ref, dvec_ref)
    _hg_fwd_kernel(*hg_in, hg_out, hs_ref, k32_ref, b_ref, n_chunks=n_chunks,
                   alongside=retention)


def _mixer_fwd(logit, p3, lf3, hsb, rsb, hg_norm_w, gn_w):
    l = p3.shape[1]
    tb = HG_BLOCK
    assert tb == RET_BLOCK
    nb = l // tb
    ncb = tb // HG_CHUNK
    slabs = lambda n, start: pl.BlockSpec((n, tb, LANE), lambda i: (start // n, i, 0))
    kern = functools.partial(_mixer_fwd_kernel, n_chunks=ncb)
    return pl.pallas_call(
        kern,
        grid=(nb,),
        in_specs=[slabs(HG_HEADS, S_HQ), slabs(HG_HEADS, S_KF), slabs(HG_HEADS, S_KB),
                  slabs(HG_HEADS, S_HI), slabs(HG_HEADS, S_HGATE),
                  pl.BlockSpec((2 * HG_HEADS, tb, LANE), lambda i: (0, i, 0)),
                  pl.BlockSpec((ncb, HG_HEADS, HG_DV, HG_DK), lambda i: (i, 0, 0, 0)),
                  pl.BlockSpec((HG_HEADS, 1, HG_DV), lambda i: (0, 0, 0)),
                  pl.BlockSpec(memory_space=pltpu.SMEM),
                  slabs(RET_HEADS, S_RQ), slabs(RET_HEADS, S_RK),
                  slabs(2 * RET_HEADS, S_RV), slabs(2 * RET_HEADS, S_RG),
                  pl.BlockSpec((1, RET_HEADS, RET_DK, RET_DV), lambda i: (i, 0, 0, 0)),
                  pl.BlockSpec((RET_HEADS, 1, RET_DV), lambda i: (0, 0, 0))],
        out_specs=[pl.BlockSpec((HG_HEADS, tb, LANE), lambda i: (0, i, 0)),
                   pl.BlockSpec((tb, RET_HEADS * RET_DV), lambda i: (i, 0))],
        out_shape=[jax.ShapeDtypeStruct((HG_HEADS, l, LANE), BF16),
                   jax.ShapeDtypeStruct((l, RET_HEADS * RET_DV), BF16)],
        scratch_shapes=[pltpu.VMEM((HG_HEADS, HG_DV, HG_DK), F32),
                        pltpu.VMEM((HG_CHUNK, HG_DK), F32),
                        pltpu.VMEM((HG_CHUNK, HG_DK), F32),
                        pltpu.VMEM((RET_HEADS, RET_DK, RET_DV), F32),
                        pltpu.VMEM((RET_HEADS, tb, tb), F32),
                        pltpu.VMEM((3 * RET_HEADS, tb, RET_DK), F32)],
        compiler_params=_params(("arbitrary",)),
        name="mixer_fwd",
    )(p3, p3, p3, p3, p3, lf3, hsb, hg_norm_w.reshape(HG_HEADS, 1, HG_DV),
      logit, p3, p3, p3, p3, rsb, gn_w.reshape(RET_HEADS, 1, RET_DV))


def _pack_bf16_pairs(v):
    w = v.shape[1] // 2
    lo = pltpu.bitcast(v[:, :w].astype(BF16).astype(F32), U32)
    hi = pltpu.bitcast(v[:, w:].astype(BF16).astype(F32), U32)
    return (lo >> 16) | (hi & jnp.uint32(0xFFFF0000))


def _unpack_bf16_pairs(u):
    lo = pltpu.bitcast(u << 16, F32)
    hi = pltpu.bitcast(u & jnp.uint32(0xFFFF0000), F32)
    return lo, hi


def _store_planes(ref, rows, packed):
    for c in range(ref.shape[0]):
        ref[c, rows, :] = packed[:, c * LANE:(c + 1) * LANE]


def _load_planes(ref, rows):
    return jnp.concatenate([ref[c, rows, :] for c in range(ref.shape[0])], axis=1)


def _copy_row(dst_ref, dst_row, src_ref, src_row):
    for c in range(src_ref.shape[0]):
        dst_ref[c, pl.ds(dst_row, 1), :] = src_ref[c, pl.ds(src_row, 1), :]


def _route(logits):
    lane = lax.broadcasted_iota(jnp.int32, logits.shape, 1)
    big = jnp.int32(10 ** 6)
    neg = -jnp.inf
    gl = jnp.where(lane < N_GROUPS, logits, neg)
    gmax = jnp.max(gl, axis=1, keepdims=True)
    grp = jnp.min(jnp.where(gl == gmax, lane, big), axis=1, keepdims=True)
    g_val = 1.0 / jnp.sum(jnp.exp(gl - gmax), axis=1, keepdims=True)
    lo = N_GROUPS + EXPERTS_PER_GROUP * grp
    el = jnp.where((lane >= lo) & (lane < lo + EXPERTS_PER_GROUP), logits, neg)
    v1 = jnp.max(el, axis=1, keepdims=True)
    i1 = jnp.min(jnp.where(el == v1, lane, big), axis=1, keepdims=True)
    el2 = jnp.where(lane == i1, neg, el)
    v2 = jnp.max(el2, axis=1, keepdims=True)
    i2 = jnp.min(jnp.where(el2 == v2, lane, big), axis=1, keepdims=True)
    r = jnp.exp(v2 - v1)
    w1 = g_val / (1.0 + r)
    w2 = w1 * r
    return ((i1 - N_GROUPS).astype(F32), (i2 - N_GROUPS).astype(F32), w1, w2)


def _merge_kernel(ohg_ref, oret_ref, ga_ref, gb_ref, x_ref, mods_ref, wohg_ref, woret_ref,
                  wout_ref, nw_ref, wrh_ref, wrl_ref, rb_ref,
                  xo_ref, hp_ref, route_ref, route_t_ref, cnt_out_ref, cnt_ref,
                  *, lc, tm, d, tiles_per_seg):
    i = pl.program_id(0)
    tg = tm // MERGE_GROUPS
    routed = []
    for g in range(MERGE_GROUPS):
        rows = slice(g * tg, (g + 1) * tg)
        row0 = i * tm + g * tg
        ohg = jnp.concatenate([ohg_ref[s, rows, :] for s in range(HG_HEADS)], axis=1)
        ga = jnp.concatenate([ga_ref[s, rows, :] for s in range(d // LANE)], axis=1).astype(F32)
        gb = jnp.concatenate([gb_ref[s, rows, :] for s in range(d // LANE)], axis=1).astype(F32)
        y_hg = jnp.dot(ohg, wohg_ref[...], preferred_element_type=F32)
        y_ret = jnp.dot(oret_ref[rows, :], woret_ref[...], preferred_element_type=F32)
        m = (ga * y_hg + gb * y_ret).astype(BF16)
        y = jnp.dot(m, wout_ref[...], preferred_element_type=F32)
        x = x_ref[rows, :] + _row_select(row0, tg, lc, mods_ref, 2, d) * y
        xo_ref[rows, :] = x

        xn = x * lax.rsqrt(jnp.mean(x * x, axis=-1, keepdims=True) + EPS) * nw_ref[...]
        h = (xn * (1.0 + _row_select(row0, tg, lc, mods_ref, 4, d))
             + _row_select(row0, tg, lc, mods_ref, 3, d))
        _store_planes(hp_ref, rows, _pack_bf16_pairs(h))

        h_hi = h.astype(BF16)
        h_lo = (h - h_hi.astype(F32)).astype(BF16)
        logits = (jnp.dot(h_hi, wrh_ref[...], preferred_element_type=F32)
                  + jnp.dot(h_lo, wrh_ref[...], preferred_element_type=F32)
                  + jnp.dot(h_hi, wrl_ref[...], preferred_element_type=F32)) + rb_ref[...]
        routed.append(_route(logits))
    e0, e1, w0, w1 = (jnp.concatenate(parts, axis=0) for parts in zip(*routed))

    @pl.when(i % tiles_per_seg == 0)
    def _():
        cnt_ref[...] = jnp.zeros_like(cnt_ref)

    lane_e = lax.broadcasted_iota(jnp.int32, (tm, LANE), 1).astype(F32)
    hot0 = lane_e == e0
    hot1 = lane_e == e1
    hot = jnp.where(hot0 | hot1, 1.0, 0.0)
    earlier = (lax.broadcasted_iota(jnp.int32, (tm, tm), 1)
               < lax.broadcasted_iota(jnp.int32, (tm, tm), 0))
    before = jnp.dot(jnp.where(earlier, 1.0, 0.0).astype(BF16), hot.astype(BF16),
                     preferred_element_type=F32) + cnt_ref[0:1, :]
    rank0 = jnp.sum(jnp.where(hot0, before, 0.0), axis=1, keepdims=True)
    rank1 = jnp.sum(jnp.where(hot1, before, 0.0), axis=1, keepdims=True)
    cnt_ref[...] = cnt_ref[...] + jnp.sum(hot, axis=0, keepdims=True)
    cnt_out_ref[0] = cnt_ref[...]

    lane = lax.broadcasted_iota(jnp.int32, (tm, LANE), 1)
    rec = jnp.zeros((tm, LANE), F32)
    for k, val in enumerate((e0, e1, w0, w1, rank0, rank1)):
        rec = jnp.where(lane == k, val, rec)
    route_ref[...] = rec[:, :ROUTE_W]
    eye = jnp.where(lax.broadcasted_iota(jnp.int32, (ROUTE_W, LANE), 0)
                    == lax.broadcasted_iota(jnp.int32, (ROUTE_W, LANE), 1), 1.0, 0.0)
    route_t_ref[0] = lax.dot_general(eye, rec, NT_DIMS, precision=lax.Precision.HIGHEST,
                                     preferred_element_type=F32)


def _merge(ohg, oret, p3, xc, mods, wohg, woret, wout, norm_w, wr_hi, wr_lo, rbias, *, lc, n_seg):
    l, d = xc.shape
    seg_tokens = l // n_seg
    tm = _largest_divisor(seg_tokens, MERGE_TILE, 16 * MERGE_GROUPS)
    tiles_per_seg = seg_tokens // tm
    n_slab = d // LANE
    full = lambda a: pl.BlockSpec(a.shape, lambda i: (0,) * a.ndim)
    kern = functools.partial(_merge_kernel, lc=lc, tm=tm, d=d, tiles_per_seg=tiles_per_seg)
    return pl.pallas_call(
        kern,
        grid=(l // tm,),
        in_specs=[pl.BlockSpec((HG_HEADS, tm, LANE), lambda i: (0, i, 0)),
                  pl.BlockSpec((tm, oret.shape[1]), lambda i: (i, 0)),
                  pl.BlockSpec((n_slab, tm, LANE), lambda i: (S_GA // n_slab, i, 0)),
                  pl.BlockSpec((n_slab, tm, LANE), lambda i: (S_GB // n_slab, i, 0)),
                  pl.BlockSpec((tm, d), lambda i: (i, 0)),
                  full(mods), full(wohg), full(woret), full(wout),
                  pl.BlockSpec((1, d), lambda i: (0, 0)),
                  full(wr_hi), full(wr_lo), full(rbias)],
        out_specs=[pl.BlockSpec((tm, d), lambda i: (i, 0)),
                   pl.BlockSpec((d // 2 // LANE, tm, LANE), lambda i: (0, i, 0)),
                   pl.BlockSpec((tm, ROUTE_W), lambda i: (i, 0)),
                   pl.BlockSpec((1, ROUTE_W, tm), lambda i: (i, 0, 0)),
                   pl.BlockSpec((1, 8, LANE), lambda i: (i // tiles_per_seg, 0, 0))],
        out_shape=[jax.ShapeDtypeStruct((l, d), F32),
                   jax.ShapeDtypeStruct((d // 2 // LANE, l, LANE), U32),
                   jax.ShapeDtypeStruct((l, ROUTE_W), F32),
                   jax.ShapeDtypeStruct((l // tm, ROUTE_W, tm), F32),
                   jax.ShapeDtypeStruct((n_seg, 8, LANE), F32)],
        scratch_shapes=[pltpu.VMEM((8, LANE), F32)],
        compiler_params=_params(("arbitrary",)),
        name="merge_router",
    )(ohg, oret, p3, p3, xc, mods, wohg, woret, wout, norm_w.reshape(1, d), wr_hi, wr_lo, rbias)


def _moe_kernel(be_ref, nused_ref, slot_ref, hp_ref, *rest,
                lc, tm, d, expert_steps, seg_tokens, n_tokens, final):
    w_refs = rest[:3 * MOE_PER_STEP]
    route_ref, x_ref, mods_ref, fw_ref, o_ref, xs_ref, g0_ref, g1_ref = rest[3 * MOE_PER_STEP:]
    s = pl.program_id(0)
    b = pl.program_id(1)

    @pl.when(b == 0)
    def _():
        xs_ref[...] = jnp.zeros_like(xs_ref)

        def scatter(g, a0):
            t0 = pl.multiple_of(g * ROW_GROUP, ROW_GROUP)
            for k in range(ROW_GROUP):
                _copy_row(xs_ref, slot_ref[a0 + k], hp_ref, t0 + k)
                _copy_row(xs_ref, slot_ref[n_tokens + a0 + k], hp_ref, t0 + k)
            return a0 + ROW_GROUP

        lax.fori_loop(0, seg_tokens // ROW_GROUP, scatter, s * seg_tokens)

    def expert_blocks(count):
        for k in range(count):
            wg_ref, wu_ref, wd_ref = w_refs[3 * k:3 * k + 3]
            rows = pl.ds(pl.multiple_of((b * MOE_PER_STEP + k) * MOE_BLOCK, MOE_BLOCK), MOE_BLOCK)
            lo, hi = _unpack_bf16_pairs(_load_planes(xs_ref, rows))
            x = jnp.concatenate([lo, hi], axis=1).astype(BF16)
            g = jnp.dot(x, wg_ref[0], preferred_element_type=F32)
            u = jnp.dot(x, wu_ref[0], preferred_element_type=F32)
            a = (_silu(g) * u).astype(BF16)
            _store_planes(xs_ref, rows,
                          _pack_bf16_pairs(jnp.dot(a, wd_ref[0], preferred_element_type=F32)))

    n_here = jnp.clip(nused_ref[s] - b * MOE_PER_STEP, 0, MOE_PER_STEP)
    for count in range(1, MOE_PER_STEP + 1):
        pl.when(jnp.logical_and(b < expert_steps, n_here == count))(
            functools.partial(expert_blocks, count))

    @pl.when(b >= expert_steps)
    def _():
        tile0 = s * seg_tokens + (b - expert_steps) * tm

        def gather(g, a0):
            r0 = pl.multiple_of(g * ROW_GROUP, ROW_GROUP)
            for k in range(ROW_GROUP):
                _copy_row(g0_ref, r0 + k, xs_ref, slot_ref[a0 + k])
                _copy_row(g1_ref, r0 + k, xs_ref, slot_ref[n_tokens + a0 + k])
            return a0 + ROW_GROUP

        lax.fori_loop(0, tm // ROW_GROUP, gather, tile0)
        lo0, hi0 = _unpack_bf16_pairs(_load_planes(g0_ref, slice(None)))
        lo1, hi1 = _unpack_bf16_pairs(_load_planes(g1_ref, slice(None)))
        w0 = route_ref[:, 2:3]
        w1 = route_ref[:, 3:4]
        y = jnp.concatenate([w0 * lo0 + w1 * lo1, w0 * hi0 + w1 * hi1], axis=1)
        x = x_ref[...] + _row_select(tile0, tm, lc, mods_ref, 5, d) * y
        if final:
            x = x * lax.rsqrt(jnp.mean(x * x, axis=-1, keepdims=True) + EPS) * fw_ref[...]
        o_ref[...] = x


def _moe(block_expert, n_used, slot, hp, wg, wu, wd, route, xc, mods, final_w,
         *, n_seg, nbs, layer, lc, final):
    l, d = xc.shape
    n_planes = hp.shape[0]
    seg_tokens = l // n_seg
    d_e = wg.shape[3]
    tm = _largest_divisor(seg_tokens, 320, 8)
    tiles = seg_tokens // tm
    per = MOE_PER_STEP
    assert nbs % per == 0
    expert_steps = nbs // per
    kern = functools.partial(_moe_kernel, lc=lc, tm=tm, d=d, expert_steps=expert_steps,
                             seg_tokens=seg_tokens, n_tokens=l, final=final)
    w_specs = []
    for k in range(per):
        w_idx = lambda s, b, be, nu, *_, k=k: (
            layer, be[s * nbs + jnp.minimum(b * per + k, jnp.maximum(nu[s] - 1, 0))], 0, 0)
        w_specs += [pl.BlockSpec((None, 1, d, d_e), w_idx), pl.BlockSpec((None, 1, d, d_e), w_idx),
                    pl.BlockSpec((None, 1, d_e, d), w_idx)]
    tile = lambda s, b: s * tiles + jnp.clip(b - expert_steps, 0, tiles - 1)
    row = lambda s, b, *_: (tile(s, b), 0)
    if final:
        assert lc % tm == 0
        out_rows = l - lc
        out_row = lambda s, b, *_: (jnp.maximum(tile(s, b) - lc // tm, 0), 0)
    else:
        out_rows, out_row = l, row
    grid_spec = pltpu.PrefetchScalarGridSpec(
        num_scalar_prefetch=3,
        grid=(n_seg, expert_steps + tiles),
        in_specs=[pl.BlockSpec((n_planes, seg_tokens, LANE), lambda s, b, *_: (0, s, 0))]
        + w_specs + [
                  pl.BlockSpec((tm, ROUTE_W), row),
                  pl.BlockSpec((tm, d), row),
                  pl.BlockSpec(mods.shape, lambda s, b, *_: (0, 0)),
                  pl.BlockSpec((1, d), lambda s, b, *_: (0, 0))],
        out_specs=pl.BlockSpec((tm, d), out_row),
        scratch_shapes=[pltpu.VMEM((n_planes, nbs * MOE_BLOCK, LANE), U32),
                        pltpu.VMEM((n_planes, tm, LANE), U32),
                        pltpu.VMEM((n_planes, tm, LANE), U32)])
    return pl.pallas_call(
        kern,
        grid_spec=grid_spec,
        out_shape=jax.ShapeDtypeStruct((out_rows, d), F32),
        compiler_params=_params(("arbitrary", "arbitrary")),
        name="moe",
    )(block_expert, n_used, slot, hp, *((wg, wu, wd) * per), route, xc, mods,
      final_w.reshape(1, d))


def _dispatch_plan(route_t, counts, nbs):
    counts_f32 = counts
    counts = counts[:, 0, :N_EXPERTS].astype(jnp.int32)
    padded = ((counts + MOE_BLOCK - 1) // MOE_BLOCK) * MOE_BLOCK
    pad_end = jnp.cumsum(padded, axis=1)
    starts = jnp.arange(nbs, dtype=jnp.int32) * MOE_BLOCK
    block_expert = jnp.minimum(
        jnp.sum((pad_end[:, None, :] <= starts[None, :, None]).astype(jnp.int32), axis=2),
        N_EXPERTS - 1)
    n_used = pad_end[:, -1] // MOE_BLOCK
    slot = _slots(route_t, counts_f32)
    return (jnp.swapaxes(slot, 0, 1).reshape(-1), block_expert.reshape(-1).astype(jnp.int32),
            n_used.astype(jnp.int32))


def _slot_kernel(rt_ref, cnt_ref, slot_ref, *, tiles_per_seg):
    s = pl.program_id(0) // tiles_per_seg
    cnt = cnt_ref[s]
    blocks = jnp.floor((cnt + (MOE_BLOCK - 1)) * (1.0 / MOE_BLOCK))
    before = jnp.where(lax.broadcasted_iota(jnp.int32, (LANE, LANE), 0)
                       < lax.broadcasted_iota(jnp.int32, (LANE, LANE), 1), 1.0, 0.0)
    first_block = jnp.dot(blocks.astype(BF16), before.astype(BF16),
                          preferred_element_type=F32)
    tm = rt_ref.shape[2]
    expert_id = lax.broadcasted_iota(jnp.int32, (LANE, tm), 0).astype(F32)
    for k in range(2):
        hot = jnp.where(expert_id == rt_ref[0, k:k + 1, :], 1.0, 0.0)
        first = jnp.dot(first_block.astype(BF16), hot.astype(BF16),
                        preferred_element_type=F32)[0:1] * MOE_BLOCK
        slot_ref[0, k:k + 1, :] = (first + rt_ref[0, 4 + k:5 + k, :]).astype(jnp.int32)


def _slots(route_t, counts):
    n_tiles, _, tm = route_t.shape
    n_seg = counts.shape[0]
    kern = functools.partial(_slot_kernel, tiles_per_seg=n_tiles // n_seg)
    return pl.pallas_call(
        kern,
        grid=(n_tiles,),
        in_specs=[pl.BlockSpec((1, ROUTE_W, tm), lambda i: (i, 0, 0)),
                  pl.BlockSpec(counts.shape, lambda i: (0, 0, 0))],
        out_specs=pl.BlockSpec((1, 2, tm), lambda i: (i, 0, 0)),
        out_shape=jax.ShapeDtypeStruct((n_tiles, 2, tm), jnp.int32),
        compiler_params=_params(("arbitrary",)),
        name="moe_slots",
    )(route_t, counts)


def _rope_tables(lc, t):
    quarter = RET_DK // 4
    n_rows = t // GRID_W
    inv = ROPE_BASE ** (-np.arange(0, 2 * quarter, 2, dtype=np.float64) / (2 * quarter))
    ang_r = np.arange(n_rows, dtype=np.float64)[:, None] * inv
    ang_c = np.arange(GRID_W, dtype=np.float64)[:, None] * inv

    def expand(row_part, col_part):
        r = np.broadcast_to(row_part[:, None, :], (n_rows, GRID_W, 2 * quarter))
        c = np.broadcast_to(col_part[None, :, :], (n_rows, GRID_W, 2 * quarter))
        return np.concatenate([r, c], axis=2).reshape(t, RET_DK)

    two = lambda a: np.concatenate([a, a], axis=1)
    cos = expand(two(np.cos(ang_r)), two(np.cos(ang_c)))
    sin = expand(np.concatenate([-np.sin(ang_r), np.sin(ang_r)], axis=1),
                 np.concatenate([-np.sin(ang_c), np.sin(ang_c)], axis=1))
    cos = np.concatenate([np.ones((lc, RET_DK)), cos], axis=0).astype(np.float32)
    sin = np.concatenate([np.zeros((lc, RET_DK)), sin], axis=0).astype(np.float32)
    return jnp.asarray(cos), jnp.asarray(sin)


def kernel(x, c, ctx, c_ctx, w_mod, b_mod, norm_mix_w, norm_ffn_w, w_in, hgrn_lb_logits, hgrn_norm_w,
           ret_decay_logit, ret_gn_w, w_o_hgrn, w_o_ret, w_out, router_group_w, router_group_b,
           router_expert_w, router_expert_b, expert_w_gate, expert_w_up, expert_w_down, final_norm_w):
    b_, t_, d = x.shape
    assert b_ == 1
    lc = ctx.shape[1]
    depth = w_mod.shape[0]
    l = lc + t_
    assert lc % HG_BLOCK == 0 and l % HG_BLOCK == 0 and lc % RET_BLOCK == 0 and l % RET_BLOCK == 0
    n_seg = MOE_SEGMENTS
    assert l % n_seg == 0
    seg_tokens = l // n_seg
    nbs = pl.cdiv(2 * seg_tokens + N_EXPERTS * (MOE_BLOCK - 1), MOE_BLOCK)
    nbs = pl.cdiv(nbs, MOE_PER_STEP) * MOE_PER_STEP
    assert nbs <= 256

    xc = jnp.concatenate([ctx[0], x[0]], axis=0)
    cc = jnp.zeros((8, d), F32).at[0].set(c[0]).at[1].set(c_ctx)
    mods_all = _modulation(cc, w_mod, b_mod)
    cos_t, sin_t = _rope_tables(lc, t_)
    w_in_bf = w_in.astype(BF16)
    w_gate_bf, w_up_bf, w_down_bf = (w.astype(BF16)
                                     for w in (expert_w_gate, expert_w_up, expert_w_down))

    for layer in range(depth):
        mods = mods_all[layer]
        p3, lf3 = _inproj(xc, mods, norm_mix_w[layer], w_in_bf, hgrn_lb_logits,
                          cos_t, sin_t, layer=layer, lc=lc)
        sb_hg, sb_ret = _mixer_bwd(ret_decay_logit[layer], p3, lf3, lc=lc)
        ohg, oret = _mixer_fwd(ret_decay_logit[layer], p3, lf3, sb_hg, sb_ret,
                               hgrn_norm_w[layer], ret_gn_w[layer])

        wr = jnp.concatenate([router_group_w[layer], router_expert_w[layer]], axis=1)
        wr = jnp.pad(wr, ((0, 0), (0, LANE - wr.shape[1])))
        wr_hi = wr.astype(BF16)
        wr_lo = (wr - wr_hi.astype(F32)).astype(BF16)
        rbias = jnp.pad(jnp.concatenate([router_group_b[layer], router_expert_b[layer]]),
                        (0, LANE - N_GROUPS - N_EXPERTS)).reshape(1, LANE)
        xc, hp, route, route_t, counts = _merge(
            ohg, oret, p3, xc, mods, w_o_hgrn[layer].astype(BF16), w_o_ret[layer].astype(BF16),
            w_out[layer].astype(BF16), norm_ffn_w[layer], wr_hi, wr_lo, rbias, lc=lc, n_seg=n_seg)

        slot, block_expert, n_used = _dispatch_plan(route_t, counts, nbs)
        xc = _moe(block_expert, n_used, slot, hp, w_gate_bf, w_up_bf, w_down_bf, route, xc, mods,
                  final_norm_w, n_seg=n_seg, nbs=nbs, layer=layer, lc=lc,
                  final=(layer == depth - 1))

    return xc[None]
```

```python
import functools

import jax
import jax.numpy as jnp
import numpy as np
from jax import lax
from jax.experimental import pallas as pl
from jax.experimental.pallas import tpu as pltpu

F32 = jnp.float32
BF16 = jnp.bfloat16
U32 = jnp.uint32

GRID_W = 64
HG_HEADS = 4
HG_DK = 128
HG_DV = 128
F_MIN = 1e-30
RET_HEADS = 4
RET_DK = 128
RET_DV = 256
ROPE_BASE = 10000.0
N_GROUPS = 4
EXPERTS_PER_GROUP = 8
N_EXPERTS = N_GROUPS * EXPERTS_PER_GROUP
N_MOD = 6
EPS = 1e-6

LANE = 128
COL_TILE = 512
SLABS_PER_TILE = COL_TILE // LANE
HG_CHUNK = 64
HG_SUB = 16
HG_BLOCK = 256
LOG2E = 1.4426950408889634
HG_FAST_SUB = 32
HG_CAP = 100.0
CUM_GROUP = 256
CUM_TERMS = 2
RET_BLOCK = 256
MOE_SEGMENTS = 5
MOE_BLOCK = 256
MOE_PER_STEP = 3
ROW_GROUP = 16
MERGE_GROUPS = 1
MERGE_TILE = 832
ROUTE_W = 8
VMEM_LIMIT = 60 * 1024 * 1024

NT_DIMS = (((1,), (1,)), ((), ()))
TN_DIMS = (((0,), (0,)), ((), ()))

SEG_SLABS = (4, 4, 4, 4, 4, 4, 4, 8, 8, 8, 8)
SEG_ORDER = (7, 8, 9, 10, 0, 1, 2, 3, 4, 5, 6)
_starts = {}
_pos = 0
for _seg in SEG_ORDER:
    _starts[_seg] = _pos
    _pos += SEG_SLABS[_seg]
SEG_START = tuple(_starts[_seg] for _seg in range(len(SEG_SLABS)))
(S_HQ, S_KF, S_KB, S_HI, S_HGATE, S_RQ, S_RK, S_RV, S_RG, S_GA, S_GB) = SEG_START
assert SEG_ORDER == tuple(range(SEG_ORDER[0], len(SEG_SLABS))) + tuple(range(SEG_ORDER[0]))
TILE_ROTATION = sum(SEG_SLABS[:SEG_ORDER[0]]) // SLABS_PER_TILE
N_SLABS = sum(SEG_SLABS)


def _params(sem):
    return pltpu.CompilerParams(dimension_semantics=sem, vmem_limit_bytes=VMEM_LIMIT)


def _sigmoid(x):
    return 1.0 / (1.0 + jnp.exp(-x))


def _silu(x):
    return x * _sigmoid(x)


def _largest_divisor(n, cap, multiple):
    best = None
    for d in range(multiple, cap + 1, multiple):
        if n % d == 0:
            best = d
    assert best is not None, (n, cap, multiple)
    return best


def _row_select(row0, n_rows, lc, mods_ref, k, d):
    rows = row0 + lax.broadcasted_iota(jnp.int32, (n_rows, 1), 0)
    lat = mods_ref[0:1, k * d:(k + 1) * d]
    ctx = mods_ref[1:2, k * d:(k + 1) * d]
    return jnp.where(rows < lc, ctx, lat)


def _mod_kernel(cc_ref, w_ref, b_ref, o_ref):
    cc = cc_ref[...]
    s = _silu(cc)
    o_ref[0] = jnp.dot(s, w_ref[0], precision=lax.Precision.HIGHEST,
                       preferred_element_type=F32) + b_ref[0]


def _modulation(cc, w_mod, b_mod):
    depth, d, n = w_mod.shape
    tn = _largest_divisor(n, 1536, LANE)
    return pl.pallas_call(
        _mod_kernel,
        grid=(depth, n // tn),
        in_specs=[pl.BlockSpec((8, d), lambda l, j: (0, 0)),
                  pl.BlockSpec((1, d, tn), lambda l, j: (l, 0, j)),
                  pl.BlockSpec((1, 1, tn), lambda l, j: (l, 0, j))],
        out_specs=pl.BlockSpec((1, 8, tn), lambda l, j: (l, 0, j)),
        out_shape=jax.ShapeDtypeStruct((depth, 8, n), F32),
        compiler_params=_params(("arbitrary", "arbitrary")),
        name="modulation",
    )(cc, w_mod, b_mod.reshape(depth, 1, n))


def _inproj_kernel(x_ref, mods_ref, nw_ref, w_ref, lbl_ref, cos_ref, sin_ref,
                   p_ref, lf_ref, h_ref, *, layer, lc, tm, d):
    i = pl.program_id(0)
    j = pl.program_id(1)

    def normalise(rows):
        x = x_ref[rows, :]
        xn = x * lax.rsqrt(jnp.mean(x * x, axis=-1, keepdims=True) + EPS) * nw_ref[...]
        shift = _row_select(i * tm + rows.start, CUM_GROUP, lc, mods_ref, 0, d)
        scale = _row_select(i * tm + rows.start, CUM_GROUP, lc, mods_ref, 1, d)
        h_ref[rows, :] = (xn * (1.0 + scale) + shift).astype(BF16)

    def for_row_groups(epilogue, first_tile=False):
        for r0 in range(0, tm, CUM_GROUP):
            rows = slice(r0, r0 + CUM_GROUP)
            if first_tile:
                normalise(rows)
            acc = jnp.dot(h_ref[rows, :], w_ref[...], preferred_element_type=F32)
            epilogue(acc, rows)

    def put(val, rows):
        v = val.astype(BF16)
        for s in range(SLABS_PER_TILE):
            p_ref[s, rows, :] = v[:, s * LANE:(s + 1) * LANE]

    tiles = lambda seg: tuple(range(SEG_START[seg] // SLABS_PER_TILE,
                                    (SEG_START[seg] + SEG_SLABS[seg]) // SLABS_PER_TILE))
    in_tiles = lambda segs: functools.reduce(
        jnp.logical_or, [j == t for seg in segs for t in tiles(seg)])

    @pl.when(in_tiles((0,)))
    def _():
        for_row_groups(lambda acc, rows: put(_silu(acc) * (HG_DK ** -0.5), rows))

    @pl.when(in_tiles((1, 2)))
    def _():
        logits = lbl_ref[jnp.clip(j - tiles(1)[0], 0, 1)]
        e = jnp.exp(logits - jnp.max(logits, axis=0, keepdims=True))
        p = e / jnp.sum(e, axis=0, keepdims=True)
        lb = jnp.zeros_like(p[0:1])
        for r in range(1, layer + 1):
            lb = lb + p[r:r + 1]
        lb = jnp.clip(lb, 0.0, 1.0 - 1e-6)
        g = CUM_GROUP
        ri = lax.broadcasted_iota(jnp.int32, (g, g), 0)
        ci = lax.broadcasted_iota(jnp.int32, (g, g), 1)
        same_chunk = (ri // HG_CHUNK) == (ci // HG_CHUNK)
        before = jnp.where(same_chunk & (ci <= ri), 1.0, 0.0)
        after = jnp.where(same_chunk & (ci >= ri), 1.0, 0.0)
        tri = jnp.where(j == tiles(1)[0], before, after).astype(BF16)

        log_f_terms = []

        def epilogue(acc, rows):
            sig = _sigmoid(acc)
            put((1.0 - lb) * (1.0 - sig), rows)
            rest = jnp.log(jnp.maximum(lb + (1.0 - lb) * sig, F_MIN))
            terms = []
            for _ in range(CUM_TERMS):
                terms.append(rest.astype(BF16))
                rest = rest - terms[-1].astype(F32)
            log_f_terms.append((rows, terms))

        for_row_groups(epilogue)
        for rows, terms in log_f_terms:
            cum = jnp.zeros((g, COL_TILE), F32)
            for term in terms:
                cum = cum + jnp.dot(tri, term, preferred_element_type=F32)
            cum = cum * LOG2E
            for s in range(SLABS_PER_TILE):
                lf_ref[s, rows, :] = cum[:, s * LANE:(s + 1) * LANE]

    assert 0 in tiles(7)

    @pl.when(j == 0)
    def _():
        for_row_groups(put, first_tile=True)

    @pl.when(jnp.logical_and(in_tiles((3, 7)), j != 0))
    def _():
        for_row_groups(put)

    @pl.when(in_tiles((4, 8)))
    def _():
        for_row_groups(lambda acc, rows: put(_silu(acc), rows))

    @pl.when(in_tiles((5, 6)))
    def _():
        scale = jnp.where(j == tiles(5)[0], RET_DK ** -0.5, 1.0)

        def epilogue(acc, rows):
            xq = acc * scale
            n = xq.shape[1]
            lane = lax.broadcasted_iota(jnp.int32, xq.shape, 1)
            partner = jnp.where((lane & 32) == 0, pltpu.roll(xq, n - 32, axis=1),
                                pltpu.roll(xq, 32, axis=1))
            cos = jnp.concatenate([cos_ref[rows, :]] * SLABS_PER_TILE, axis=1)
            sin = jnp.concatenate([sin_ref[rows, :]] * SLABS_PER_TILE, axis=1)
            put(xq * cos + partner * sin, rows)

        for_row_groups(epilogue)

    @pl.when(in_tiles((9, 10)))
    def _():
        for_row_groups(lambda acc, rows: put(_sigmoid(acc), rows))


def _inproj(xc, mods, norm_w, w_in_bf, lb_logits, cos_t, sin_t, *, layer, lc):
    l, d = xc.shape
    d_in = w_in_bf.shape[2]
    assert d_in == N_SLABS * LANE
    tm = _largest_divisor(l, 1280, CUM_GROUP)
    n_col = d_in // COL_TILE
    kf_tile = S_KF // SLABS_PER_TILE
    kern = functools.partial(_inproj_kernel, layer=layer, lc=lc, tm=tm, d=d)
    return pl.pallas_call(
        kern,
        grid=(l // tm, n_col),
        in_specs=[pl.BlockSpec((tm, d), lambda i, j: (i, 0)),
                  pl.BlockSpec(mods.shape, lambda i, j: (0, 0)),
                  pl.BlockSpec((1, d), lambda i, j: (0, 0)),
                  pl.BlockSpec((None, d, COL_TILE),
                               lambda i, j: (layer, 0, (j + TILE_ROTATION) % n_col)),
                  pl.BlockSpec(lb_logits.shape, lambda i, j: (0, 0, 0)),
                  pl.BlockSpec((tm, LANE), lambda i, j: (i, 0)),
                  pl.BlockSpec((tm, LANE), lambda i, j: (i, 0))],
        out_specs=[pl.BlockSpec((SLABS_PER_TILE, tm, LANE), lambda i, j: (j, i, 0)),
                   pl.BlockSpec((SLABS_PER_TILE, tm, LANE),
                                lambda i, j: (jnp.clip(j - kf_tile, 0, 1), i, 0))],
        out_shape=[jax.ShapeDtypeStruct((N_SLABS, l, LANE), BF16),
                   jax.ShapeDtypeStruct((2 * HG_HEADS, l, LANE), F32)],
        scratch_shapes=[pltpu.VMEM((tm, d), BF16)],
        compiler_params=_params(("arbitrary", "arbitrary")),
        name="inproj",
    )(xc, mods, norm_w.reshape(1, d), w_in_bf, lb_logits, cos_t, sin_t)


def _hg_bwd_kernel(kb_ref, v_ref, bc_ref, sb_ref, s_ref, *, n_chunks):
    for h in range(HG_HEADS):
        s = s_ref[h]
        for cc in reversed(range(n_chunks)):
            rows = slice(cc * HG_CHUNK, (cc + 1) * HG_CHUNK)
            bb = bc_ref[h, rows, :]
            k = kb_ref[h, rows, :].astype(F32)
            sb_ref[cc, h] = s.astype(BF16)
            kt = (k * jnp.exp2(bb[0:1, :] - bb)).astype(BF16)
            s = s * jnp.exp2(bb[0:1, :]) + lax.dot_general(
                v_ref[h, rows, :], kt, TN_DIMS, preferred_element_type=F32)
        s_ref[h] = s


def _bwd_block_order(i, n_ctx_blocks, n_blocks):
    return jnp.where(i < n_ctx_blocks, n_ctx_blocks - 1 - i, n_blocks - 1 - (i - n_ctx_blocks))


def _mixer_bwd_kernel(logit_ref, kb_ref, hv_ref, bc_ref, rk_ref, rv_ref,
                      hsb_ref, rsb_ref, hs_ref, rs_ref, *, n_chunks):
    @pl.when(pl.program_id(0) == 0)
    def _():
        hs_ref[...] = jnp.zeros_like(hs_ref)
        rs_ref[...] = jnp.zeros_like(rs_ref)

    _hg_bwd_kernel(kb_ref, hv_ref, bc_ref, hsb_ref, hs_ref, n_chunks=n_chunks)
    _ret_bwd_kernel(logit_ref, rk_ref, rv_ref, rsb_ref, rs_ref)


def _mixer_bwd(logit, p3, lf3, *, lc):
    l = p3.shape[1]
    tb = HG_BLOCK
    assert tb == RET_BLOCK
    nb, nbc = l // tb, lc // tb
    ncb = tb // HG_CHUNK
    order = lambda i: _bwd_block_order(i, nbc, nb)
    slabs = lambda n, start: pl.BlockSpec((n, tb, LANE), lambda i: (start // n, order(i), 0))
    kern = functools.partial(_mixer_bwd_kernel, n_chunks=ncb)
    return pl.pallas_call(
        kern,
        grid=(nb,),
        in_specs=[pl.BlockSpec(memory_space=pltpu.SMEM),
                  slabs(HG_HEADS, S_KB), slabs(HG_HEADS, S_HI),
                  pl.BlockSpec((HG_HEADS, tb, LANE), lambda i: (1, order(i), 0)),
                  slabs(RET_HEADS, S_RK), slabs(2 * RET_HEADS, S_RV)],
        out_specs=[pl.BlockSpec((ncb, HG_HEADS, HG_DV, HG_DK), lambda i: (order(i), 0, 0, 0)),
                   pl.BlockSpec((1, RET_HEADS, RET_DK, RET_DV), lambda i: (order(i), 0, 0, 0))],
        out_shape=[jax.ShapeDtypeStruct((l // HG_CHUNK, HG_HEADS, HG_DV, HG_DK), BF16),
                   jax.ShapeDtypeStruct((nb, RET_HEADS, RET_DK, RET_DV), BF16)],
        scratch_shapes=[pltpu.VMEM((HG_HEADS, HG_DV, HG_DK), F32),
                        pltpu.VMEM((RET_HEADS, RET_DK, RET_DV), F32)],
        compiler_params=_params(("arbitrary",)),
        name="mixer_bwd_state",
    )(logit, p3, p3, lf3, p3, p3)


def _hg_scores(q, q_edge, k32, b, fwd, sub, exact, k32_ref=None, b_ref=None):
    c = HG_CHUNK
    n_sub = c // sub
    cap = 0.0 if exact else HG_CAP
    lane = lax.broadcasted_iota(jnp.int32, (sub, c), 1)
    blocks = []
    for blk in range(n_sub):
        r = blk * sub
        q_blk = q[r:r + sub]
        b_blk = b[r:r + sub]
        edge = blk == 0 if fwd else blk == n_sub - 1
        if edge and exact:
            a = jnp.zeros((sub, c), F32)
        else:
            if edge:
                qt = q_edge[r:r + sub]
                ref_minus_b = -b
            else:
                ref = b[r - 1:r] if fwd else b[r + sub:r + sub + 1]
                qt = q_blk * jnp.exp2(b_blk - ref)
                ref_minus_b = ref - b
            kt = (k32 * jnp.exp2(jnp.minimum(ref_minus_b, cap))).astype(BF16)
            a = lax.dot_general(qt.astype(BF16), kt, NT_DIMS, preferred_element_type=F32)
        if exact:
            for jj in range(sub):
                s = r + jj
                e = jnp.exp2(b_blk - b_ref[s:s + 1, :])
                col = jnp.sum(q_blk * k32_ref[s:s + 1, :] * e, axis=1, keepdims=True)
                a = jnp.where(lane == s, col, a)
        blocks.append(a)
    a = jnp.concatenate(blocks, axis=0)
    ri = lax.broadcasted_iota(jnp.int32, (c, c), 0)
    ci = lax.broadcasted_iota(jnp.int32, (c, c), 1)
    return jnp.where((ci <= ri) if fwd else (ci >= ri), a, 0.0)


def _chunk_rows(cc):
    if isinstance(cc, int):
        return slice(cc * HG_CHUNK, (cc + 1) * HG_CHUNK)
    return pl.ds(pl.multiple_of(cc * HG_CHUNK, HG_CHUNK), HG_CHUNK)


def _hg_head_scores(refs, cc, h, sub, exact):
    (q_ref, kf_ref, kb_ref, v_ref, _, bc_ref, sb_ref, _, _, s_ref, k32_ref, b_ref) = refs
    rows = _chunk_rows(cc)
    q = q_ref[h, rows, :].astype(F32)
    v = v_ref[h, rows, :]
    bf = bc_ref[h, rows, :]
    bb = bc_ref[HG_HEADS + h, rows, :]
    kf = kf_ref[h, rows, :].astype(F32)
    kb = kb_ref[h, rows, :].astype(F32)
    qf = q * jnp.exp2(bf)
    qb = q * jnp.exp2(bb)
    s = s_ref[h]
    inter = lax.dot_general(jnp.concatenate([qf, qb], axis=1).astype(BF16),
                            jnp.concatenate([s.astype(BF16), sb_ref[cc, h]], axis=1),
                            NT_DIMS, preferred_element_type=F32)
    if exact:
        k32_ref[...] = kf
        b_ref[...] = bf
    a = _hg_scores(q, qf, kf, bf, True, sub, exact, k32_ref, b_ref)
    if exact:
        k32_ref[...] = kb
        b_ref[...] = bb
    a = a + _hg_scores(q, qb, kb, bb, False, sub, exact, k32_ref, b_ref)
    b_last = bf[HG_CHUNK - 1:HG_CHUNK, :]
    kt = (kf * jnp.exp2(b_last - bf)).astype(BF16)
    s_ref[h] = s * jnp.exp2(b_last) + lax.dot_general(v, kt, TN_DIMS,
                                                      preferred_element_type=F32)
    return inter, a, v


def _hg_head_finish(refs, cc, h, inter, a, v):
    gate_ref, nw_ref, o_ref = refs[4], refs[7], refs[8]
    rows = _chunk_rows(cc)
    o = inter + jnp.dot(a.astype(BF16), v, preferred_element_type=F32)
    o = o * lax.rsqrt(jnp.mean(o * o, axis=-1, keepdims=True) + EPS)
    o = o * nw_ref[h] * gate_ref[h, rows, :].astype(F32)
    o_ref[h, rows, :] = o.astype(BF16)


def _hg_min_block_decay(bc_ref):
    n = bc_ref.shape[1] // HG_FAST_SUB
    first = lax.broadcasted_iota(jnp.int32, (n, LANE), 0) % 2 == 0
    worst = None
    for h in range(HG_HEADS):
        ends = bc_ref.at[h][pl.ds(HG_FAST_SUB - 1, n, stride=HG_FAST_SUB), :]
        fwd = jnp.where(first, ends, ends - pltpu.roll(ends, 1, axis=0))
        starts = bc_ref.at[HG_HEADS + h][pl.ds(0, n, stride=HG_FAST_SUB), :]
        bwd = jnp.where(first, starts - pltpu.roll(starts, n - 1, axis=0), starts)
        m = jnp.minimum(fwd, bwd)
        worst = m if worst is None else jnp.minimum(worst, m)
    return jnp.min(worst)


def _hg_fwd_kernel(*refs, n_chunks, alongside=lambda: None):
    bc_ref, s_ref = refs[5], refs[9]

    @pl.when(pl.program_id(0) == 0)
    def _():
        s_ref[...] = jnp.zeros_like(s_ref)

    factored_ok = _hg_min_block_decay(bc_ref) >= -HG_CAP

    @pl.when(factored_ok)
    def _():
        staged_prev = None
        for cc in range(n_chunks):
            staged = [_hg_head_scores(refs, cc, h, HG_FAST_SUB, False) for h in range(HG_HEADS)]
            if staged_prev is not None:
                for h, parts in enumerate(staged_prev):
                    _hg_head_finish(refs, cc - 1, h, *parts)
            staged_prev = staged
        alongside()
        for h, parts in enumerate(staged_prev):
            _hg_head_finish(refs, n_chunks - 1, h, *parts)

    @pl.when(jnp.logical_not(factored_ok))
    def _():
        def body(n, carry):
            cc, h = n // HG_HEADS, n % HG_HEADS
            _hg_head_finish(refs, cc, h, *_hg_head_scores(refs, cc, h, HG_SUB, True))
            return carry
        lax.fori_loop(0, n_chunks * HG_HEADS, body, 0)
        alongside()


def _log_sigmoid(x):
    return jnp.minimum(x, 0.0) - jnp.log1p(jnp.exp(-jnp.abs(x)))


def _ret_log_gamma(logit_ref, direction, h, shape):
    return _log_sigmoid(jnp.full(shape, logit_ref[direction, h], F32))


def _ret_bwd_kernel(logit_ref, k_ref, v_ref, sb_ref, s_ref):
    c = RET_BLOCK
    t = lax.broadcasted_iota(jnp.int32, (c, RET_DK), 0).astype(F32)
    for h in range(RET_HEADS):
        lg = _ret_log_gamma(logit_ref, 1, h, (c, RET_DK))
        s = s_ref[h]
        sb_ref[0, h] = s.astype(BF16)
        kt = (k_ref[h].astype(F32) * jnp.exp(t * lg)).astype(BF16)
        v = jnp.concatenate([v_ref[2 * h], v_ref[2 * h + 1]], axis=1)
        s_ref[h] = s * jnp.exp(c * lg[0:1, 0:1]) + lax.dot_general(
            kt, v, TN_DIMS, preferred_element_type=F32)


def _ret_fwd_init(logit_ref, s_ref, dmat_ref, dvec_ref):
    c = RET_BLOCK

    @pl.when(pl.program_id(0) == 0)
    def _():
        s_ref[...] = jnp.zeros_like(s_ref)
        ri = lax.broadcasted_iota(jnp.int32, (c, c), 0)
        ci = lax.broadcasted_iota(jnp.int32, (c, c), 1)
        dist = (ri - ci).astype(F32)
        t = lax.broadcasted_iota(jnp.int32, (c, RET_DK), 0).astype(F32)
        for h in range(RET_HEADS):
            lgf = _ret_log_gamma(logit_ref, 0, h, (c, c))
            lgb = _ret_log_gamma(logit_ref, 1, h, (c, c))
            dmat_ref[h] = (jnp.where(ci <= ri, jnp.exp(jnp.maximum(dist, 0.0) * lgf), 0.0)
                           + jnp.where(ci >= ri, jnp.exp(jnp.maximum(-dist, 0.0) * lgb), 0.0))
            lgf = _ret_log_gamma(logit_ref, 0, h, (c, RET_DK))
            lgb = _ret_log_gamma(logit_ref, 1, h, (c, RET_DK))
            dvec_ref[3 * h] = jnp.exp((t + 1.0) * lgf)
            dvec_ref[3 * h + 1] = jnp.exp((c - t) * lgb)
            dvec_ref[3 * h + 2] = jnp.exp((c - 1.0 - t) * lgf)


def _ret_fwd_body(logit_ref, q_ref, k_ref, v_ref, gate_ref, sb_ref, gnw_ref,
                  o_ref, s_ref, dmat_ref, dvec_ref):
    c = RET_BLOCK
    for h in range(RET_HEADS):
        lgf = _ret_log_gamma(logit_ref, 0, h, (c, RET_DK))[0:1, 0:1]
        q = q_ref[h]
        k = k_ref[h]
        q32 = q.astype(F32)
        v = jnp.concatenate([v_ref[2 * h], v_ref[2 * h + 1]], axis=1)
        s = s_ref[h]
        sc = lax.dot_general(q, k, NT_DIMS, preferred_element_type=F32) * dmat_ref[h]
        o = jnp.dot(sc.astype(BF16), v, preferred_element_type=F32)
        o = o + jnp.dot((q32 * dvec_ref[3 * h]).astype(BF16), s.astype(BF16),
                        preferred_element_type=F32)
        o = o + jnp.dot((q32 * dvec_ref[3 * h + 1]).astype(BF16), sb_ref[0, h],
                        preferred_element_type=F32)
        kt = (k.astype(F32) * dvec_ref[3 * h + 2]).astype(BF16)
        s_ref[h] = s * jnp.exp(c * lgf) + lax.dot_general(
            kt, v, TN_DIMS, preferred_element_type=F32)

        mu = jnp.mean(o, axis=-1, keepdims=True)
        dev = o - mu
        var = jnp.mean(dev * dev, axis=-1, keepdims=True)
        o = dev * lax.rsqrt(var + EPS) * gnw_ref[h]
        gate = jnp.concatenate([gate_ref[2 * h], gate_ref[2 * h + 1]], axis=1).astype(F32)
        o_ref[:, h * RET_DV:(h + 1) * RET_DV] = (o * gate).astype(BF16)


N_HG_FWD_IN = 8
N_RET_FWD_IN = 7


def _mixer_fwd_kernel(*refs, n_chunks):
    hg_in = refs[:N_HG_FWD_IN]
    ret_in = refs[N_HG_FWD_IN:N_HG_FWD_IN + N_RET_FWD_IN]
    (hg_out, ret_out, hs_ref, k32_ref, b_ref, rs_ref, dmat_ref,
     dvec_ref) = refs[N_HG_FWD_IN + N_RET_FWD_IN:]
    _ret_fwd_init(ret_in[0], rs_ref, dmat_ref, dvec_ref)
    retention = functools.partial(_ret_fwd_body, *ret_in, ret_out, rs_ref, dmat_ref, dvec_ref)
    _hg_fwd_kernel(*hg_in, hg_out, hs_ref, k32_ref, b_ref, n_chunks=n_chunks,
                   alongside=retention)


def _mixer_fwd(logit, p3, lf3, hsb, rsb, hg_norm_w, gn_w):
    l = p3.shape[1]
    tb = HG_BLOCK
    assert tb == RET_BLOCK
    nb = l // tb
    ncb = tb // HG_CHUNK
    slabs = lambda n, start: pl.BlockSpec((n, tb, LANE), lambda i: (start // n, i, 0))
    kern = functools.partial(_mixer_fwd_kernel, n_chunks=ncb)
    return pl.pallas_call(
        kern,
        grid=(nb,),
        in_specs=[slabs(HG_HEADS, S_HQ), slabs(HG_HEADS, S_KF), slabs(HG_HEADS, S_KB),
                  slabs(HG_HEADS, S_HI), slabs(HG_HEADS, S_HGATE),
                  pl.BlockSpec((2 * HG_HEADS, tb, LANE), lambda i: (0, i, 0)),
                  pl.BlockSpec((ncb, HG_HEADS, HG_DV, HG_DK), lambda i: (i, 0, 0, 0)),
                  pl.BlockSpec((HG_HEADS, 1, HG_DV), lambda i: (0, 0, 0)),
                  pl.BlockSpec(memory_space=pltpu.SMEM),
                  slabs(RET_HEADS, S_RQ), slabs(RET_HEADS, S_RK),
                  slabs(2 * RET_HEADS, S_RV), slabs(2 * RET_HEADS, S_RG),
                  pl.BlockSpec((1, RET_HEADS, RET_DK, RET_DV), lambda i: (i, 0, 0, 0)),
                  pl.BlockSpec((RET_HEADS, 1, RET_DV), lambda i: (0, 0, 0))],
        out_specs=[pl.BlockSpec((HG_HEADS, tb, LANE), lambda i: (0, i, 0)),
                   pl.BlockSpec((tb, RET_HEADS * RET_DV), lambda i: (i, 0))],
        out_shape=[jax.ShapeDtypeStruct((HG_HEADS, l, LANE), BF16),
                   jax.ShapeDtypeStruct((l, RET_HEADS * RET_DV), BF16)],
        scratch_shapes=[pltpu.VMEM((HG_HEADS, HG_DV, HG_DK), F32),
                        pltpu.VMEM((HG_CHUNK, HG_DK), F32),
                        pltpu.VMEM((HG_CHUNK, HG_DK), F32),
                        pltpu.VMEM((RET_HEADS, RET_DK, RET_DV), F32),
                        pltpu.VMEM((RET_HEADS, tb, tb), F32),
                        pltpu.VMEM((3 * RET_HEADS, tb, RET_DK), F32)],
        compiler_params=_params(("arbitrary",)),
        name="mixer_fwd",
    )(p3, p3, p3, p3, p3, lf3, hsb, hg_norm_w.reshape(HG_HEADS, 1, HG_DV),
      logit, p3, p3, p3, p3, rsb, gn_w.reshape(RET_HEADS, 1, RET_DV))


def _pack_bf16_pairs(v):
    w = v.shape[1] // 2
    lo = pltpu.bitcast(v[:, :w].astype(BF16).astype(F32), U32)
    hi = pltpu.bitcast(v[:, w:].astype(BF16).astype(F32), U32)
    return (lo >> 16) | (hi & jnp.uint32(0xFFFF0000))


def _unpack_bf16_pairs(u):
    lo = pltpu.bitcast(u << 16, F32)
    hi = pltpu.bitcast(u & jnp.uint32(0xFFFF0000), F32)
    return lo, hi


def _store_planes(ref, rows, packed):
    for c in range(ref.shape[0]):
        ref[c, rows, :] = packed[:, c * LANE:(c + 1) * LANE]


def _load_planes(ref, rows):
    return jnp.concatenate([ref[c, rows, :] for c in range(ref.shape[0])], axis=1)


def _copy_row(dst_ref, dst_row, src_ref, src_row):
    for c in range(src_ref.shape[0]):
        dst_ref[c, pl.ds(dst_row, 1), :] = src_ref[c, pl.ds(src_row, 1), :]


def _route(logits):
    lane = lax.broadcasted_iota(jnp.int32, logits.shape, 1)
    big = jnp.int32(10 ** 6)
    neg = -jnp.inf
    gl = jnp.where(lane < N_GROUPS, logits, neg)
    gmax = jnp.max(gl, axis=1, keepdims=True)
    grp = jnp.min(jnp.where(gl == gmax, lane, big), axis=1, keepdims=True)
    g_val = 1.0 / jnp.sum(jnp.exp(gl - gmax), axis=1, keepdims=True)
    lo = N_GROUPS + EXPERTS_PER_GROUP * grp
    el = jnp.where((lane >= lo) & (lane < lo + EXPERTS_PER_GROUP), logits, neg)
    v1 = jnp.max(el, axis=1, keepdims=True)
    i1 = jnp.min(jnp.where(el == v1, lane, big), axis=1, keepdims=True)
    el2 = jnp.where(lane == i1, neg, el)
    v2 = jnp.max(el2, axis=1, keepdims=True)
    i2 = jnp.min(jnp.where(el2 == v2, lane, big), axis=1, keepdims=True)
    r = jnp.exp(v2 - v1)
    w1 = g_val / (1.0 + r)
    w2 = w1 * r
    return ((i1 - N_GROUPS).astype(F32), (i2 - N_GROUPS).astype(F32), w1, w2)


def _merge_kernel(ohg_ref, oret_ref, ga_ref, gb_ref, x_ref, mods_ref, wohg_ref, woret_ref,
                  wout_ref, nw_ref, wrh_ref, wrl_ref, rb_ref,
                  xo_ref, hp_ref, route_ref, route_t_ref, cnt_out_ref, cnt_ref,
                  *, lc, tm, d, tiles_per_seg):
    i = pl.program_id(0)
    tg = tm // MERGE_GROUPS
    routed = []
    for g in range(MERGE_GROUPS):
        rows = slice(g * tg, (g + 1) * tg)
        row0 = i * tm + g * tg
        ohg = jnp.concatenate([ohg_ref[s, rows, :] for s in range(HG_HEADS)], axis=1)
        ga = jnp.concatenate([ga_ref[s, rows, :] for s in range(d // LANE)], axis=1).astype(F32)
        gb = jnp.concatenate([gb_ref[s, rows, :] for s in range(d // LANE)], axis=1).astype(F32)
        y_hg = jnp.dot(ohg, wohg_ref[...], preferred_element_type=F32)
        y_ret = jnp.dot(oret_ref[rows, :], woret_ref[...], preferred_element_type=F32)
        m = (ga * y_hg + gb * y_ret).astype(BF16)
        y = jnp.dot(m, wout_ref[...], preferred_element_type=F32)
        x = x_ref[rows, :] + _row_select(row0, tg, lc, mods_ref, 2, d) * y
        xo_ref[rows, :] = x

        xn = x * lax.rsqrt(jnp.mean(x * x, axis=-1, keepdims=True) + EPS) * nw_ref[...]
        h = (xn * (1.0 + _row_select(row0, tg, lc, mods_ref, 4, d))
             + _row_select(row0, tg, lc, mods_ref, 3, d))
        _store_planes(hp_ref, rows, _pack_bf16_pairs(h))

        h_hi = h.astype(BF16)
        h_lo = (h - h_hi.astype(F32)).astype(BF16)
        logits = (jnp.dot(h_hi, wrh_ref[...], preferred_element_type=F32)
                  + jnp.dot(h_lo, wrh_ref[...], preferred_element_type=F32)
                  + jnp.dot(h_hi, wrl_ref[...], preferred_element_type=F32)) + rb_ref[...]
        routed.append(_route(logits))
    e0, e1, w0, w1 = (jnp.concatenate(parts, axis=0) for parts in zip(*routed))

    @pl.when(i % tiles_per_seg == 0)
    def _():
        cnt_ref[...] = jnp.zeros_like(cnt_ref)

    lane_e = lax.broadcasted_iota(jnp.int32, (tm, LANE), 1).astype(F32)
    hot0 = lane_e == e0
    hot1 = lane_e == e1
    hot = jnp.where(hot0 | hot1, 1.0, 0.0)
    earlier = (lax.broadcasted_iota(jnp.int32, (tm, tm), 1)
               < lax.broadcasted_iota(jnp.int32, (tm, tm), 0))
    before = jnp.dot(jnp.where(earlier, 1.0, 0.0).astype(BF16), hot.astype(BF16),
                     preferred_element_type=F32) + cnt_ref[0:1, :]
    rank0 = jnp.sum(jnp.where(hot0, before, 0.0), axis=1, keepdims=True)
    rank1 = jnp.sum(jnp.where(hot1, before, 0.0), axis=1, keepdims=True)
    cnt_ref[...] = cnt_ref[...] + jnp.sum(hot, axis=0, keepdims=True)
    cnt_out_ref[0] = cnt_ref[...]

    lane = lax.broadcasted_iota(jnp.int32, (tm, LANE), 1)
    rec = jnp.zeros((tm, LANE), F32)
    for k, val in enumerate((e0, e1, w0, w1, rank0, rank1)):
        rec = jnp.where(lane == k, val, rec)
    route_ref[...] = rec[:, :ROUTE_W]
    eye = jnp.where(lax.broadcasted_iota(jnp.int32, (ROUTE_W, LANE), 0)
                    == lax.broadcasted_iota(jnp.int32, (ROUTE_W, LANE), 1), 1.0, 0.0)
    route_t_ref[0] = lax.dot_general(eye, rec, NT_DIMS, precision=lax.Precision.HIGHEST,
                                     preferred_element_type=F32)


def _merge(ohg, oret, p3, xc, mods, wohg, woret, wout, norm_w, wr_hi, wr_lo, rbias, *, lc, n_seg):
    l, d = xc.shape
    seg_tokens = l // n_seg
    tm = _largest_divisor(seg_tokens, MERGE_TILE, 16 * MERGE_GROUPS)
    tiles_per_seg = seg_tokens // tm
    n_slab = d // LANE
    full = lambda a: pl.BlockSpec(a.shape, lambda i: (0,) * a.ndim)
    kern = functools.partial(_merge_kernel, lc=lc, tm=tm, d=d, tiles_per_seg=tiles_per_seg)
    return pl.pallas_call(
        kern,
        grid=(l // tm,),
        in_specs=[pl.BlockSpec((HG_HEADS, tm, LANE), lambda i: (0, i, 0)),
                  pl.BlockSpec((tm, oret.shape[1]), lambda i: (i, 0)),
                  pl.BlockSpec((n_slab, tm, LANE), lambda i: (S_GA // n_slab, i, 0)),
                  pl.BlockSpec((n_slab, tm, LANE), lambda i: (S_GB // n_slab, i, 0)),
                  pl.BlockSpec((tm, d), lambda i: (i, 0)),
                  full(mods), full(wohg), full(woret), full(wout),
                  pl.BlockSpec((1, d), lambda i: (0, 0)),
                  full(wr_hi), full(wr_lo), full(rbias)],
        out_specs=[pl.BlockSpec((tm, d), lambda i: (i, 0)),
                   pl.BlockSpec((d // 2 // LANE, tm, LANE), lambda i: (0, i, 0)),
                   pl.BlockSpec((tm, ROUTE_W), lambda i: (i, 0)),
                   pl.BlockSpec((1, ROUTE_W, tm), lambda i: (i, 0, 0)),
                   pl.BlockSpec((1, 8, LANE), lambda i: (i // tiles_per_seg, 0, 0))],
        out_shape=[jax.ShapeDtypeStruct((l, d), F32),
                   jax.ShapeDtypeStruct((d // 2 // LANE, l, LANE), U32),
                   jax.ShapeDtypeStruct((l, ROUTE_W), F32),
                   jax.ShapeDtypeStruct((l // tm, ROUTE_W, tm), F32),
                   jax.ShapeDtypeStruct((n_seg, 8, LANE), F32)],
        scratch_shapes=[pltpu.VMEM((8, LANE), F32)],
        compiler_params=_params(("arbitrary",)),
        name="merge_router",
    )(ohg, oret, p3, p3, xc, mods, wohg, woret, wout, norm_w.reshape(1, d), wr_hi, wr_lo, rbias)


def _moe_kernel(be_ref, nused_ref, slot_ref, hp_ref, *rest,
                lc, tm, d, expert_steps, seg_tokens, n_tokens, final):
    w_refs = rest[:3 * MOE_PER_STEP]
    route_ref, x_ref, mods_ref, fw_ref, o_ref, xs_ref, g0_ref, g1_ref = rest[3 * MOE_PER_STEP:]
    s = pl.program_id(0)
    b = pl.program_id(1)

    @pl.when(b == 0)
    def _():
        xs_ref[...] = jnp.zeros_like(xs_ref)

        def scatter(g, a0):
            t0 = pl.multiple_of(g * ROW_GROUP, ROW_GROUP)
            for k in range(ROW_GROUP):
                _copy_row(xs_ref, slot_ref[a0 + k], hp_ref, t0 + k)
                _copy_row(xs_ref, slot_ref[n_tokens + a0 + k], hp_ref, t0 + k)
            return a0 + ROW_GROUP

        lax.fori_loop(0, seg_tokens // ROW_GROUP, scatter, s * seg_tokens)

    def expert_blocks(count):
        for k in range(count):
            wg_ref, wu_ref, wd_ref = w_refs[3 * k:3 * k + 3]
            rows = pl.ds(pl.multiple_of((b * MOE_PER_STEP + k) * MOE_BLOCK, MOE_BLOCK), MOE_BLOCK)
            lo, hi = _unpack_bf16_pairs(_load_planes(xs_ref, rows))
            x = jnp.concatenate([lo, hi], axis=1).astype(BF16)
            g = jnp.dot(x, wg_ref[0], preferred_element_type=F32)
            u = jnp.dot(x, wu_ref[0], preferred_element_type=F32)
            a = (_silu(g) * u).astype(BF16)
            _store_planes(xs_ref, rows,
                          _pack_bf16_pairs(jnp.dot(a, wd_ref[0], preferred_element_type=F32)))

    n_here = jnp.clip(nused_ref[s] - b * MOE_PER_STEP, 0, MOE_PER_STEP)
    for count in range(1, MOE_PER_STEP + 1):
        pl.when(jnp.logical_and(b < expert_steps, n_here == count))(
            functools.partial(expert_blocks, count))

    @pl.when(b >= expert_steps)
    def _():
        tile0 = s * seg_tokens + (b - expert_steps) * tm

        def gather(g, a0):
            r0 = pl.multiple_of(g * ROW_GROUP, ROW_GROUP)
            for k in range(ROW_GROUP):
                _copy_row(g0_ref, r0 + k, xs_ref, slot_ref[a0 + k])
                _copy_row(g1_ref, r0 + k, xs_ref, slot_ref[n_tokens + a0 + k])
            return a0 + ROW_GROUP

        lax.fori_loop(0, tm // ROW_GROUP, gather, tile0)
        lo0, hi0 = _unpack_bf16_pairs(_load_planes(g0_ref, slice(None)))
        lo1, hi1 = _unpack_bf16_pairs(_load_planes(g1_ref, slice(None)))
        w0 = route_ref[:, 2:3]
        w1 = route_ref[:, 3:4]
        y = jnp.concatenate([w0 * lo0 + w1 * lo1, w0 * hi0 + w1 * hi1], axis=1)
        x = x_ref[...] + _row_select(tile0, tm, lc, mods_ref, 5, d) * y
        if final:
            x = x * lax.rsqrt(jnp.mean(x * x, axis=-1, keepdims=True) + EPS) * fw_ref[...]
        o_ref[...] = x


def _moe(block_expert, n_used, slot, hp, wg, wu, wd, route, xc, mods, final_w,
         *, n_seg, nbs, layer, lc, final):
    l, d = xc.shape
    n_planes = hp.shape[0]
    seg_tokens = l // n_seg
    d_e = wg.shape[3]
    tm = _largest_divisor(seg_tokens, 320, 8)
    tiles = seg_tokens // tm
    per = MOE_PER_STEP
    assert nbs % per == 0
    expert_steps = nbs // per
    kern = functools.partial(_moe_kernel, lc=lc, tm=tm, d=d, expert_steps=expert_steps,
                             seg_tokens=seg_tokens, n_tokens=l, final=final)
    w_specs = []
    for k in range(per):
        w_idx = lambda s, b, be, nu, *_, k=k: (
            layer, be[s * nbs + jnp.minimum(b * per + k, jnp.maximum(nu[s] - 1, 0))], 0, 0)
        w_specs += [pl.BlockSpec((None, 1, d, d_e), w_idx), pl.BlockSpec((None, 1, d, d_e), w_idx),
                    pl.BlockSpec((None, 1, d_e, d), w_idx)]
    tile = lambda s, b: s * tiles + jnp.clip(b - expert_steps, 0, tiles - 1)
    row = lambda s, b, *_: (tile(s, b), 0)
    if final:
        assert lc % tm == 0
        out_rows = l - lc
        out_row = lambda s, b, *_: (jnp.maximum(tile(s, b) - lc // tm, 0), 0)
    else:
        out_rows, out_row = l, row
    grid_spec = pltpu.PrefetchScalarGridSpec(
        num_scalar_prefetch=3,
        grid=(n_seg, expert_steps + tiles),
        in_specs=[pl.BlockSpec((n_planes, seg_tokens, LANE), lambda s, b, *_: (0, s, 0))]
        + w_specs + [
                  pl.BlockSpec((tm, ROUTE_W), row),
                  pl.BlockSpec((tm, d), row),
                  pl.BlockSpec(mods.shape, lambda s, b, *_: (0, 0)),
                  pl.BlockSpec((1, d), lambda s, b, *_: (0, 0))],
        out_specs=pl.BlockSpec((tm, d), out_row),
        scratch_shapes=[pltpu.VMEM((n_planes, nbs * MOE_BLOCK, LANE), U32),
                        pltpu.VMEM((n_planes, tm, LANE), U32),
                        pltpu.VMEM((n_planes, tm, LANE), U32)])
    return pl.pallas_call(
        kern,
        grid_spec=grid_spec,
        out_shape=jax.ShapeDtypeStruct((out_rows, d), F32),
        compiler_params=_params(("arbitrary", "arbitrary")),
        name="moe",
    )(block_expert, n_used, slot, hp, *((wg, wu, wd) * per), route, xc, mods,
      final_w.reshape(1, d))


def _dispatch_plan(route_t, counts, nbs):
    counts_f32 = counts
    counts = counts[:, 0, :N_EXPERTS].astype(jnp.int32)
    padded = ((counts + MOE_BLOCK - 1) // MOE_BLOCK) * MOE_BLOCK
    pad_end = jnp.cumsum(padded, axis=1)
    starts = jnp.arange(nbs, dtype=jnp.int32) * MOE_BLOCK
    block_expert = jnp.minimum(
        jnp.sum((pad_end[:, None, :] <= starts[None, :, None]).astype(jnp.int32), axis=2),
        N_EXPERTS - 1)
    n_used = pad_end[:, -1] // MOE_BLOCK
    slot = _slots(route_t, counts_f32)
    return (jnp.swapaxes(slot, 0, 1).reshape(-1), block_expert.reshape(-1).astype(jnp.int32),
            n_used.astype(jnp.int32))


def _slot_kernel(rt_ref, cnt_ref, slot_ref, *, tiles_per_seg):
    s = pl.program_id(0) // tiles_per_seg
    cnt = cnt_ref[s]
    blocks = jnp.floor((cnt + (MOE_BLOCK - 1)) * (1.0 / MOE_BLOCK))
    before = jnp.where(lax.broadcasted_iota(jnp.int32, (LANE, LANE), 0)
                       < lax.broadcasted_iota(jnp.int32, (LANE, LANE), 1), 1.0, 0.0)
    first_block = jnp.dot(blocks.astype(BF16), before.astype(BF16),
                          preferred_element_type=F32)
    tm = rt_ref.shape[2]
    expert_id = lax.broadcasted_iota(jnp.int32, (LANE, tm), 0).astype(F32)
    for k in range(2):
        hot = jnp.where(expert_id == rt_ref[0, k:k + 1, :], 1.0, 0.0)
        first = jnp.dot(first_block.astype(BF16), hot.astype(BF16),
                        preferred_element_type=F32)[0:1] * MOE_BLOCK
        slot_ref[0, k:k + 1, :] = (first + rt_ref[0, 4 + k:5 + k, :]).astype(jnp.int32)


def _slots(route_t, counts):
    n_tiles, _, tm = route_t.shape
    n_seg = counts.shape[0]
    kern = functools.partial(_slot_kernel, tiles_per_seg=n_tiles // n_seg)
    return pl.pallas_call(
        kern,
        grid=(n_tiles,),
        in_specs=[pl.BlockSpec((1, ROUTE_W, tm), lambda i: (i, 0, 0)),
                  pl.BlockSpec(counts.shape, lambda i: (0, 0, 0))],
        out_specs=pl.BlockSpec((1, 2, tm), lambda i: (i, 0, 0)),
        out_shape=jax.ShapeDtypeStruct((n_tiles, 2, tm), jnp.int32),
        compiler_params=_params(("arbitrary",)),
        name="moe_slots",
    )(route_t, counts)


def _rope_tables(lc, t):
    quarter = RET_DK // 4
    n_rows = t // GRID_W
    inv = ROPE_BASE ** (-np.arange(0, 2 * quarter, 2, dtype=np.float64) / (2 * quarter))
    ang_r = np.arange(n_rows, dtype=np.float64)[:, None] * inv
    ang_c = np.arange(GRID_W, dtype=np.float64)[:, None] * inv

    def expand(row_part, col_part):
        r = np.broadcast_to(row_part[:, None, :], (n_rows, GRID_W, 2 * quarter))
        c = np.broadcast_to(col_part[None, :, :], (n_rows, GRID_W, 2 * quarter))
        return np.concatenate([r, c], axis=2).reshape(t, RET_DK)

    two = lambda a: np.concatenate([a, a], axis=1)
    cos = expand(two(np.cos(ang_r)), two(np.cos(ang_c)))
    sin = expand(np.concatenate([-np.sin(ang_r), np.sin(ang_r)], axis=1),
                 np.concatenate([-np.sin(ang_c), np.sin(ang_c)], axis=1))
    cos = np.concatenate([np.ones((lc, RET_DK)), cos], axis=0).astype(np.float32)
    sin = np.concatenate([np.zeros((lc, RET_DK)), sin], axis=0).astype(np.float32)
    return jnp.asarray(cos), jnp.asarray(sin)


def kernel(x, c, ctx, c_ctx, w_mod, b_mod, norm_mix_w, norm_ffn_w, w_in, hgrn_lb_logits, hgrn_norm_w,
           ret_decay_logit, ret_gn_w, w_o_hgrn, w_o_ret, w_out, router_group_w, router_group_b,
           router_expert_w, router_expert_b, expert_w_gate, expert_w_up, expert_w_down, final_norm_w):
    b_, t_, d = x.shape
    assert b_ == 1
    lc = ctx.shape[1]
    depth = w_mod.shape[0]
    l = lc + t_
    assert lc % HG_BLOCK == 0 and l % HG_BLOCK == 0 and lc % RET_BLOCK == 0 and l % RET_BLOCK == 0
    n_seg = MOE_SEGMENTS
    assert l % n_seg == 0
    seg_tokens = l // n_seg
    nbs = pl.cdiv(2 * seg_tokens + N_EXPERTS * (MOE_BLOCK - 1), MOE_BLOCK)
    nbs = pl.cdiv(nbs, MOE_PER_STEP) * MOE_PER_STEP
    assert nbs <= 256

    xc = jnp.concatenate([ctx[0], x[0]], axis=0)
    cc = jnp.zeros((8, d), F32).at[0].set(c[0]).at[1].set(c_ctx)
    mods_all = _modulation(cc, w_mod, b_mod)
    cos_t, sin_t = _rope_tables(lc, t_)
    w_in_bf = w_in.astype(BF16)
    w_gate_bf, w_up_bf, w_down_bf = (w.astype(BF16)
                                     for w in (expert_w_gate, expert_w_up, expert_w_down))

    for layer in range(depth):
        mods = mods_all[layer]
        p3, lf3 = _inproj(xc, mods, norm_mix_w[layer], w_in_bf, hgrn_lb_logits,
                          cos_t, sin_t, layer=layer, lc=lc)
        sb_hg, sb_ret = _mixer_bwd(ret_decay_logit[layer], p3, lf3, lc=lc)
        ohg, oret = _mixer_fwd(ret_decay_logit[layer], p3, lf3, sb_hg, sb_ret,
                               hgrn_norm_w[layer], ret_gn_w[layer])

        wr = jnp.concatenate([router_group_w[layer], router_expert_w[layer]], axis=1)
        wr = jnp.pad(wr, ((0, 0), (0, LANE - wr.shape[1])))
        wr_hi = wr.astype(BF16)
        wr_lo = (wr - wr_hi.astype(F32)).astype(BF16)
        rbias = jnp.pad(jnp.concatenate([router_group_b[layer], router_expert_b[layer]]),
                        (0, LANE - N_GROUPS - N_EXPERTS)).reshape(1, LANE)
        xc, hp, route, route_t, counts = _merge(
            ohg, oret, p3, xc, mods, w_o_hgrn[layer].astype(BF16), w_o_ret[layer].astype(BF16),
            w_out[layer].astype(BF16), norm_ffn_w[layer], wr_hi, wr_lo, rbias, lc=lc, n_seg=n_seg)

        slot, block_expert, n_used = _dispatch_plan(route_t, counts, nbs)
        xc = _moe(block_expert, n_used, slot, hp, w_gate_bf, w_up_bf, w_down_bf, route, xc, mods,
                  final_norm_w, n_seg=n_seg, nbs=nbs, layer=layer, lc=lc,
                  final=(layer == depth - 1))

    return xc[None]
```

```python
import functools

import jax
import jax.numpy as jnp
import numpy as np
from jax import lax
from jax.experimental import pallas as pl
from jax.experimental.pallas import tpu as pltpu

F32 = jnp.float32
BF16 = jnp.bfloat16
U32 = jnp.uint32

GRID_W = 64
HG_HEADS = 4
HG_DK = 128
HG_DV = 128
F_MIN = 1e-30
RET_HEADS = 4
RET_DK = 128
RET_DV = 256
ROPE_BASE = 10000.0
N_GROUPS = 4
EXPERTS_PER_GROUP = 8
N_EXPERTS = N_GROUPS * EXPERTS_PER_GROUP
N_MOD = 6
EPS = 1e-6

LANE = 128
COL_TILE = 512
SLABS_PER_TILE = COL_TILE // LANE
HG_CHUNK = 64
HG_SUB = 16
HG_BLOCK = 256
LOG2E = 1.4426950408889634
HG_FAST_SUB = 32
HG_CAP = 100.0
CUM_GROUP = 256
CUM_TERMS = 2
RET_BLOCK = 256
MOE_SEGMENTS = 5
MOE_BLOCK = 256
MOE_PER_STEP = 3
ROW_GROUP = 16
MERGE_GROUPS = 1
MERGE_TILE = 832
ROUTE_W = 8
VMEM_LIMIT = 60 * 1024 * 1024

NT_DIMS = (((1,), (1,)), ((), ()))
TN_DIMS = (((0,), (0,)), ((), ()))

SEG_SLABS = (4, 4, 4, 4, 4, 4, 4, 8, 8, 8, 8)
SEG_ORDER = (7, 8, 9, 10, 0, 1, 2, 3, 4, 5, 6)
_starts = {}
_pos = 0
for _seg in SEG_ORDER:
    _starts[_seg] = _pos
    _pos += SEG_SLABS[_seg]
SEG_START = tuple(_starts[_seg] for _seg in range(len(SEG_SLABS)))
(S_HQ, S_KF, S_KB, S_HI, S_HGATE, S_RQ, S_RK, S_RV, S_RG, S_GA, S_GB) = SEG_START
assert SEG_ORDER == tuple(range(SEG_ORDER[0], len(SEG_SLABS))) + tuple(range(SEG_ORDER[0]))
TILE_ROTATION = sum(SEG_SLABS[:SEG_ORDER[0]]) // SLABS_PER_TILE
N_SLABS = sum(SEG_SLABS)


def _params(sem):
    return pltpu.CompilerParams(dimension_semantics=sem, vmem_limit_bytes=VMEM_LIMIT)


def _sigmoid(x):
    return 1.0 / (1.0 + jnp.exp(-x))


def _silu(x):
    return x * _sigmoid(x)


def _largest_divisor(n, cap, multiple):
    best = None
    for d in range(multiple, cap + 1, multiple):
        if n % d == 0:
            best = d
    assert best is not None, (n, cap, multiple)
    return best


def _row_select(row0, n_rows, lc, mods_ref, k, d):
    rows = row0 + lax.broadcasted_iota(jnp.int32, (n_rows, 1), 0)
    lat = mods_ref[0:1, k * d:(k + 1) * d]
    ctx = mods_ref[1:2, k * d:(k + 1) * d]
    return jnp.where(rows < lc, ctx, lat)


def _mod_kernel(cc_ref, w_ref, b_ref, o_ref):
    cc = cc_ref[...]
    s = _silu(cc)
    o_ref[0] = jnp.dot(s, w_ref[0], precision=lax.Precision.HIGHEST,
                       preferred_element_type=F32) + b_ref[0]


def _modulation(cc, w_mod, b_mod):
    depth, d, n = w_mod.shape
    tn = _largest_divisor(n, 1536, LANE)
    return pl.pallas_call(
        _mod_kernel,
        grid=(depth, n // tn),
        in_specs=[pl.BlockSpec((8, d), lambda l, j: (0, 0)),
                  pl.BlockSpec((1, d, tn), lambda l, j: (l, 0, j)),
                  pl.BlockSpec((1, 1, tn), lambda l, j: (l, 0, j))],
        out_specs=pl.BlockSpec((1, 8, tn), lambda l, j: (l, 0, j)),
        out_shape=jax.ShapeDtypeStruct((depth, 8, n), F32),
        compiler_params=_params(("arbitrary", "arbitrary")),
        name="modulation",
    )(cc, w_mod, b_mod.reshape(depth, 1, n))


def _inproj_kernel(x_ref, mods_ref, nw_ref, w_ref, lbl_ref, cos_ref, sin_ref,
                   p_ref, lf_ref, h_ref, *, layer, lc, tm, d):
    i = pl.program_id(0)
    j = pl.program_id(1)

    def normalise(rows):
        x = x_ref[rows, :]
        xn = x * lax.rsqrt(jnp.mean(x * x, axis=-1, keepdims=True) + EPS) * nw_ref[...]
        shift = _row_select(i * tm + rows.start, CUM_GROUP, lc, mods_ref, 0, d)
        scale = _row_select(i * tm + rows.start, CUM_GROUP, lc, mods_ref, 1, d)
        h_ref[rows, :] = (xn * (1.0 + scale) + shift).astype(BF16)

    def for_row_groups(epilogue, first_tile=False):
        for r0 in range(0, tm, CUM_GROUP):
            rows = slice(r0, r0 + CUM_GROUP)
            if first_tile:
                normalise(rows)
            acc = jnp.dot(h_ref[rows, :], w_ref[...], preferred_element_type=F32)
            epilogue(acc, rows)

    def put(val, rows):
        v = val.astype(BF16)
        for s in range(SLABS_PER_TILE):
            p_ref[s, rows, :] = v[:, s * LANE:(s + 1) * LANE]

    tiles = lambda seg: tuple(range(SEG_START[seg] // SLABS_PER_TILE,
                                    (SEG_START[seg] + SEG_SLABS[seg]) // SLABS_PER_TILE))
    in_tiles = lambda segs: functools.reduce(
        jnp.logical_or, [j == t for seg in segs for t in tiles(seg)])

    @pl.when(in_tiles((0,)))
    def _():
        for_row_groups(lambda acc, rows: put(_silu(acc) * (HG_DK ** -0.5), rows))

    @pl.when(in_tiles((1, 2)))
    def _():
        logits = lbl_ref[jnp.clip(j - tiles(1)[0], 0, 1)]
        e = jnp.exp(logits - jnp.max(logits, axis=0, keepdims=True))
        p = e / jnp.sum(e, axis=0, keepdims=True)
        lb = jnp.zeros_like(p[0:1])
        for r in range(1, layer + 1):
            lb = lb + p[r:r + 1]
        lb = jnp.clip(lb, 0.0, 1.0 - 1e-6)
        g = CUM_GROUP
        ri = lax.broadcasted_iota(jnp.int32, (g, g), 0)
        ci = lax.broadcasted_iota(jnp.int32, (g, g), 1)
        same_chunk = (ri // HG_CHUNK) == (ci // HG_CHUNK)
        before = jnp.where(same_chunk & (ci <= ri), 1.0, 0.0)
        after = jnp.where(same_chunk & (ci >= ri), 1.0, 0.0)
        tri = jnp.where(j == tiles(1)[0], before, after).astype(BF16)

        log_f_terms = []

        def epilogue(acc, rows):
            sig = _sigmoid(acc)
            put((1.0 - lb) * (1.0 - sig), rows)
            rest = jnp.log(jnp.maximum(lb + (1.0 - lb) * sig, F_MIN))
            terms = []
            for _ in range(CUM_TERMS):
                terms.append(rest.astype(BF16))
                rest = rest - terms[-1].astype(F32)
            log_f_terms.append((rows, terms))

        for_row_groups(epilogue)
        for rows, terms in log_f_terms:
            cum = jnp.zeros((g, COL_TILE), F32)
            for term in terms:
                cum = cum + jnp.dot(tri, term, preferred_element_type=F32)
            cum = cum * LOG2E
            for s in range(SLABS_PER_TILE):
                lf_ref[s, rows, :] = cum[:, s * LANE:(s + 1) * LANE]

    assert 0 in tiles(7)

    @pl.when(j == 0)
    def _():
        for_row_groups(put, first_tile=True)

    @pl.when(jnp.logical_and(in_tiles((3, 7)), j != 0))
    def _():
        for_row_groups(put)

    @pl.when(in_tiles((4, 8)))
    def _():
        for_row_groups(lambda acc, rows: put(_silu(acc), rows))

    @pl.when(in_tiles((5, 6)))
    def _():
        scale = jnp.where(j == tiles(5)[0], RET_DK ** -0.5, 1.0)

        def epilogue(acc, rows):
            xq = acc * scale
            n = xq.shape[1]
            lane = lax.broadcasted_iota(jnp.int32, xq.shape, 1)
            partner = jnp.where((lane & 32) == 0, pltpu.roll(xq, n - 32, axis=1),
                                pltpu.roll(xq, 32, axis=1))
            cos = jnp.concatenate([cos_ref[rows, :]] * SLABS_PER_TILE, axis=1)
            sin = jnp.concatenate([sin_ref[rows, :]] * SLABS_PER_TILE, axis=1)
            put(xq * cos + partner * sin, rows)

        for_row_groups(epilogue)

    @pl.when(in_tiles((9, 10)))
    def _():
        for_row_groups(lambda acc, rows: put(_sigmoid(acc), rows))


def _inproj(xc, mods, norm_w, w_in_bf, lb_logits, cos_t, sin_t, *, layer, lc):
    l, d = xc.shape
    d_in = w_in_bf.shape[2]
    assert d_in == N_SLABS * LANE
    tm = _largest_divisor(l, 1280, CUM_GROUP)
    n_col = d_in // COL_TILE
    kf_tile = S_KF // SLABS_PER_TILE
    kern = functools.partial(_inproj_kernel, layer=layer, lc=lc, tm=tm, d=d)
    return pl.pallas_call(
        kern,
        grid=(l // tm, n_col),
        in_specs=[pl.BlockSpec((tm, d), lambda i, j: (i, 0)),
                  pl.BlockSpec(mods.shape, lambda i, j: (0, 0)),
                  pl.BlockSpec((1, d), lambda i, j: (0, 0)),
                  pl.BlockSpec((None, d, COL_TILE),
                               lambda i, j: (layer, 0, (j + TILE_ROTATION) % n_col)),
                  pl.BlockSpec(lb_logits.shape, lambda i, j: (0, 0, 0)),
                  pl.BlockSpec((tm, LANE), lambda i, j: (i, 0)),
                  pl.BlockSpec((tm, LANE), lambda i, j: (i, 0))],
        out_specs=[pl.BlockSpec((SLABS_PER_TILE, tm, LANE), lambda i, j: (j, i, 0)),
                   pl.BlockSpec((SLABS_PER_TILE, tm, LANE),
                                lambda i, j: (jnp.clip(j - kf_tile, 0, 1), i, 0))],
        out_shape=[jax.ShapeDtypeStruct((N_SLABS, l, LANE), BF16),
                   jax.ShapeDtypeStruct((2 * HG_HEADS, l, LANE), F32)],
        scratch_shapes=[pltpu.VMEM((tm, d), BF16)],
        compiler_params=pltpu.CompilerParams(
            dimension_semantics=("arbitrary", "arbitrary"), vmem_limit_bytes=VMEM_LIMIT,
            allow_input_fusion=[False, False, False, True, False, False, False]),
        name="inproj",
    )(xc, mods, norm_w.reshape(1, d), w_in_bf, lb_logits, cos_t, sin_t)


def _hg_bwd_kernel(kb_ref, v_ref, bc_ref, sb_ref, s_ref, *, n_chunks):
    for h in range(HG_HEADS):
        s = s_ref[h]
        for cc in reversed(range(n_chunks)):
            rows = slice(cc * HG_CHUNK, (cc + 1) * HG_CHUNK)
            bb = bc_ref[h, rows, :]
            k = kb_ref[h, rows, :].astype(F32)
            sb_ref[cc, h] = s.astype(BF16)
            kt = (k * jnp.exp2(bb[0:1, :] - bb)).astype(BF16)
            s = s * jnp.exp2(bb[0:1, :]) + lax.dot_general(
                v_ref[h, rows, :], kt, TN_DIMS, preferred_element_type=F32)
        s_ref[h] = s


def _bwd_block_order(i, n_ctx_blocks, n_blocks):
    return jnp.where(i < n_ctx_blocks, n_ctx_blocks - 1 - i, n_blocks - 1 - (i - n_ctx_blocks))


def _mixer_bwd_kernel(logit_ref, kb_ref, hv_ref, bc_ref, rk_ref, rv_ref,
                      hsb_ref, rsb_ref, hs_ref, rs_ref, *, n_chunks):
    @pl.when(pl.program_id(0) == 0)
    def _():
        hs_ref[...] = jnp.zeros_like(hs_ref)
        rs_ref[...] = jnp.zeros_like(rs_ref)

    _hg_bwd_kernel(kb_ref, hv_ref, bc_ref, hsb_ref, hs_ref, n_chunks=n_chunks)
    _ret_bwd_kernel(logit_ref, rk_ref, rv_ref, rsb_ref, rs_ref)


def _mixer_bwd(logit, p3, lf3, *, lc):
    l = p3.shape[1]
    tb = HG_BLOCK
    assert tb == RET_BLOCK
    nb, nbc = l // tb, lc // tb
    ncb = tb // HG_CHUNK
    order = lambda i: _bwd_block_order(i, nbc, nb)
    slabs = lambda n, start: pl.BlockSpec((n, tb, LANE), lambda i: (start // n, order(i), 0))
    kern = functools.partial(_mixer_bwd_kernel, n_chunks=ncb)
    return pl.pallas_call(
        kern,
        grid=(nb,),
        in_specs=[pl.BlockSpec(memory_space=pltpu.SMEM),
                  slabs(HG_HEADS, S_KB), slabs(HG_HEADS, S_HI),
                  pl.BlockSpec((HG_HEADS, tb, LANE), lambda i: (1, order(i), 0)),
                  slabs(RET_HEADS, S_RK), slabs(2 * RET_HEADS, S_RV)],
        out_specs=[pl.BlockSpec((ncb, HG_HEADS, HG_DV, HG_DK), lambda i: (order(i), 0, 0, 0)),
                   pl.BlockSpec((1, RET_HEADS, RET_DK, RET_DV), lambda i: (order(i), 0, 0, 0))],
        out_shape=[jax.ShapeDtypeStruct((l // HG_CHUNK, HG_HEADS, HG_DV, HG_DK), BF16),
                   jax.ShapeDtypeStruct((nb, RET_HEADS, RET_DK, RET_DV), BF16)],
        scratch_shapes=[pltpu.VMEM((HG_HEADS, HG_DV, HG_DK), F32),
                        pltpu.VMEM((RET_HEADS, RET_DK, RET_DV), F32)],
        compiler_params=_params(("arbitrary",)),
        name="mixer_bwd_state",
    )(logit, p3, p3, lf3, p3, p3)


def _hg_scores(q, q_edge, k32, b, fwd, sub, exact, k32_ref=None, b_ref=None):
    c = HG_CHUNK
    n_sub = c // sub
    cap = 0.0 if exact else HG_CAP
    lane = lax.broadcasted_iota(jnp.int32, (sub, c), 1)
    blocks = []
    for blk in range(n_sub):
        r = blk * sub
        q_blk = q[r:r + sub]
        b_blk = b[r:r + sub]
        edge = blk == 0 if fwd else blk == n_sub - 1
        if edge and exact:
            a = jnp.zeros((sub, c), F32)
        else:
            if edge:
                qt = q_edge[r:r + sub]
                ref_minus_b = -b
            else:
                ref = b[r - 1:r] if fwd else b[r + sub:r + sub + 1]
                qt = q_blk * jnp.exp2(b_blk - ref)
                ref_minus_b = ref - b
            kt = (k32 * jnp.exp2(jnp.minimum(ref_minus_b, cap))).astype(BF16)
            a = lax.dot_general(qt.astype(BF16), kt, NT_DIMS, preferred_element_type=F32)
        if exact:
            for jj in range(sub):
                s = r + jj
                e = jnp.exp2(b_blk - b_ref[s:s + 1, :])
                col = jnp.sum(q_blk * k32_ref[s:s + 1, :] * e, axis=1, keepdims=True)
                a = jnp.where(lane == s, col, a)
        blocks.append(a)
    a = jnp.concatenate(blocks, axis=0)
    ri = lax.broadcasted_iota(jnp.int32, (c, c), 0)
    ci = lax.broadcasted_iota(jnp.int32, (c, c), 1)
    return jnp.where((ci <= ri) if fwd else (ci >= ri), a, 0.0)


def _chunk_rows(cc):
    if isinstance(cc, int):
        return slice(cc * HG_CHUNK, (cc + 1) * HG_CHUNK)
    return pl.ds(pl.multiple_of(cc * HG_CHUNK, HG_CHUNK), HG_CHUNK)


def _hg_head_scores(refs, cc, h, sub, exact):
    (q_ref, kf_ref, kb_ref, v_ref, _, bc_ref, sb_ref, _, _, s_ref, k32_ref, b_ref) = refs
    rows = _chunk_rows(cc)
    q = q_ref[h, rows, :].astype(F32)
    v = v_ref[h, rows, :]
    bf = bc_ref[h, rows, :]
    bb = bc_ref[HG_HEADS + h, rows, :]
    kf = kf_ref[h, rows, :].astype(F32)
    kb = kb_ref[h, rows, :].astype(F32)
    qf = q * jnp.exp2(bf)
    qb = q * jnp.exp2(bb)
    s = s_ref[h]
    inter = lax.dot_general(jnp.concatenate([qf, qb], axis=1).astype(BF16),
                            jnp.concatenate([s.astype(BF16), sb_ref[cc, h]], axis=1),
                            NT_DIMS, preferred_element_type=F32)
    if exact:
        k32_ref[...] = kf
        b_ref[...] = bf
    a = _hg_scores(q, qf, kf, bf, True, sub, exact, k32_ref, b_ref)
    if exact:
        k32_ref[...] = kb
        b_ref[...] = bb
    a = a + _hg_scores(q, qb, kb, bb, False, sub, exact, k32_ref, b_ref)
    b_last = bf[HG_CHUNK - 1:HG_CHUNK, :]
    kt = (kf * jnp.exp2(b_last - bf)).astype(BF16)
    s_ref[h] = s * jnp.exp2(b_last) + lax.dot_general(v, kt, TN_DIMS,
                                                      preferred_element_type=F32)
    return inter, a, v


def _hg_head_finish(refs, cc, h, inter, a, v):
    gate_ref, nw_ref, o_ref = refs[4], refs[7], refs[8]
    rows = _chunk_rows(cc)
    o = inter + jnp.dot(a.astype(BF16), v, preferred_element_type=F32)
    o = o * lax.rsqrt(jnp.mean(o * o, axis=-1, keepdims=True) + EPS)
    o = o * nw_ref[h] * gate_ref[h, rows, :].astype(F32)
    o_ref[h, rows, :] = o.astype(BF16)


def _hg_min_block_decay(bc_ref):
    n = bc_ref.shape[1] // HG_FAST_SUB
    first = lax.broadcasted_iota(jnp.int32, (n, LANE), 0) % 2 == 0
    worst = None
    for h in range(HG_HEADS):
        ends = bc_ref.at[h][pl.ds(HG_FAST_SUB - 1, n, stride=HG_FAST_SUB), :]
        fwd = jnp.where(first, ends, ends - pltpu.roll(ends, 1, axis=0))
        starts = bc_ref.at[HG_HEADS + h][pl.ds(0, n, stride=HG_FAST_SUB), :]
        bwd = jnp.where(first, starts - pltpu.roll(starts, n - 1, axis=0), starts)
        m = jnp.minimum(fwd, bwd)
        worst = m if worst is None else jnp.minimum(worst, m)
    return jnp.min(worst)


def _hg_fwd_kernel(*refs, n_chunks, alongside=lambda: None):
    bc_ref, s_ref = refs[5], refs[9]

    @pl.when(pl.program_id(0) == 0)
    def _():
        s_ref[...] = jnp.zeros_like(s_ref)

    factored_ok = _hg_min_block_decay(bc_ref) >= -HG_CAP

    @pl.when(factored_ok)
    def _():
        staged_prev = None
        for cc in range(n_chunks):
            staged = [_hg_head_scores(refs, cc, h, HG_FAST_SUB, False) for h in range(HG_HEADS)]
            if staged_prev is not None:
                for h, parts in enumerate(staged_prev):
                    _hg_head_finish(refs, cc - 1, h, *parts)
            staged_prev = staged
        alongside()
        for h, parts in enumerate(staged_prev):
            _hg_head_finish(refs, n_chunks - 1, h, *parts)

    @pl.when(jnp.logical_not(factored_ok))
    def _():
        def body(n, carry):
            cc, h = n // HG_HEADS, n % HG_HEADS
            _hg_head_finish(refs, cc, h, *_hg_head_scores(refs, cc, h, HG_SUB, True))
            return carry
        lax.fori_loop(0, n_chunks * HG_HEADS, body, 0)
        alongside()


def _log_sigmoid(x):
    return jnp.minimum(x, 0.0) - jnp.log1p(jnp.exp(-jnp.abs(x)))


def _ret_log_gamma(logit_ref, direction, h, shape):
    return _log_sigmoid(jnp.full(shape, logit_ref[direction, h], F32))


def _ret_bwd_kernel(logit_ref, k_ref, v_ref, sb_ref, s_ref):
    c = RET_BLOCK
    t = lax.broadcasted_iota(jnp.int32, (c, RET_DK), 0).astype(F32)
    for h in range(RET_HEADS):
        lg = _ret_log_gamma(logit_ref, 1, h, (c, RET_DK))
        s = s_ref[h]
        sb_ref[0, h] = s.astype(BF16)
        kt = (k_ref[h].astype(F32) * jnp.exp(t * lg)).astype(BF16)
        v = jnp.concatenate([v_ref[2 * h], v_ref[2 * h + 1]], axis=1)
        s_ref[h] = s * jnp.exp(c * lg[0:1, 0:1]) + lax.dot_general(
            kt, v, TN_DIMS, preferred_element_type=F32)


def _ret_fwd_init(logit_ref, s_ref, dmat_ref, dvec_ref):
    c = RET_BLOCK

    @pl.when(pl.program_id(0) == 0)
    def _():
        s_ref[...] = jnp.zeros_like(s_ref)
        ri = lax.broadcasted_iota(jnp.int32, (c, c), 0)
        ci = lax.broadcasted_iota(jnp.int32, (c, c), 1)
        dist = (ri - ci).astype(F32)
        t = lax.broadcasted_iota(jnp.int32, (c, RET_DK), 0).astype(F32)
        for h in range(RET_HEADS):
            lgf = _ret_log_gamma(logit_ref, 0, h, (c, c))
            lgb = _ret_log_gamma(logit_ref, 1, h, (c, c))
            dmat_ref[h] = (jnp.where(ci <= ri, jnp.exp(jnp.maximum(dist, 0.0) * lgf), 0.0)
                           + jnp.where(ci >= ri, jnp.exp(jnp.maximum(-dist, 0.0) * lgb), 0.0))
            lgf = _ret_log_gamma(logit_ref, 0, h, (c, RET_DK))
            lgb = _ret_log_gamma(logit_ref, 1, h, (c, RET_DK))
            dvec_ref[3 * h] = jnp.exp((t + 1.0) * lgf)
            dvec_ref[3 * h + 1] = jnp.exp((c - t) * lgb)
            dvec_ref[3 * h + 2] = jnp.exp((c - 1.0 - t) * lgf)


def _ret_fwd_body(logit_ref, q_ref, k_ref, v_ref, gate_ref, sb_ref, gnw_ref,
                  o_ref, s_ref, dmat_ref, dvec_ref):
    c = RET_BLOCK
    for h in range(RET_HEADS):
        lgf = _ret_log_gamma(logit_ref, 0, h, (c, RET_DK))[0:1, 0:1]
        q = q_ref[h]
        k = k_ref[h]
        q32 = q.astype(F32)
        v = jnp.concatenate([v_ref[2 * h], v_ref[2 * h + 1]], axis=1)
        s = s_ref[h]
        sc = lax.dot_general(q, k, NT_DIMS, preferred_element_type=F32) * dmat_ref[h]
        o = jnp.dot(sc.astype(BF16), v, preferred_element_type=F32)
        o = o + jnp.dot((q32 * dvec_ref[3 * h]).astype(BF16), s.astype(BF16),
                        preferred_element_type=F32)
        o = o + jnp.dot((q32 * dvec_ref[3 * h + 1]).astype(BF16), sb_ref[0, h],
                        preferred_element_type=F32)
        kt = (k.astype(F32) * dvec_ref[3 * h + 2]).astype(BF16)
        s_ref[h] = s * jnp.exp(c * lgf) + lax.dot_general(
            kt, v, TN_DIMS, preferred_element_type=F32)

        mu = jnp.mean(o, axis=-1, keepdims=True)
        dev = o - mu
        var = jnp.mean(dev * dev, axis=-1, keepdims=True)
        o = dev * lax.rsqrt(var + EPS) * gnw_ref[h]
        gate = jnp.concatenate([gate_ref[2 * h], gate_ref[2 * h + 1]], axis=1).astype(F32)
        o_ref[:, h * RET_DV:(h + 1) * RET_DV] = (o * gate).astype(BF16)


N_HG_FWD_IN = 8
N_RET_FWD_IN = 7


def _mixer_fwd_kernel(*refs, n_chunks):
    hg_in = refs[:N_HG_FWD_IN]
    ret_in = refs[N_HG_FWD_IN:N_HG_FWD_IN + N_RET_FWD_IN]
    (hg_out, ret_out, hs_ref, k32_ref, b_ref, rs_ref, dmat_ref,
     dvec_ref) = refs[N_HG_FWD_IN + N_RET_FWD_IN:]
    _ret_fwd_init(ret_in[0], rs_ref, dmat_ref, dvec_ref)
    retention = functools.partial(_ret_fwd_body, *ret_in, ret_out, rs_ref, dmat_ref, dvec_ref)
    _hg_fwd_kernel(*hg_in, hg_out, hs_ref, k32_ref, b_ref, n_chunks=n_chunks,
                   alongside=retention)


def _mixer_fwd(logit, p3, lf3, hsb, rsb, hg_norm_w, gn_w):
    l = p3.shape[1]
    tb = HG_BLOCK
    assert tb == RET_BLOCK
    nb = l // tb
    ncb = tb // HG_CHUNK
    slabs = lambda n, start: pl.BlockSpec((n, tb, LANE), lambda i: (start // n, i, 0))
    kern = functools.partial(_mixer_fwd_kernel, n_chunks=ncb)
    return pl.pallas_call(
        kern,
        grid=(nb,),
        in_specs=[slabs(HG_HEADS, S_HQ), slabs(HG_HEADS, S_KF), slabs(HG_HEADS, S_KB),
                  slabs(HG_HEADS, S_HI), slabs(HG_HEADS, S_HGATE),
                  pl.BlockSpec((2 * HG_HEADS, tb, LANE), lambda i: (0, i, 0)),
                  pl.BlockSpec((ncb, HG_HEADS, HG_DV, HG_DK), lambda i: (i, 0, 0, 0)),
                  pl.BlockSpec((HG_HEADS, 1, HG_DV), lambda i: (0, 0, 0)),
                  pl.BlockSpec(memory_space=pltpu.SMEM),
                  slabs(RET_HEADS, S_RQ), slabs(RET_HEADS, S_RK),
                  slabs(2 * RET_HEADS, S_RV), slabs(2 * RET_HEADS, S_RG),
                  pl.BlockSpec((1, RET_HEADS, RET_DK, RET_DV), lambda i: (i, 0, 0, 0)),
                  pl.BlockSpec((RET_HEADS, 1, RET_DV), lambda i: (0, 0, 0))],
        out_specs=[pl.BlockSpec((HG_HEADS, tb, LANE), lambda i: (0, i, 0)),
                   pl.BlockSpec((tb, RET_HEADS * RET_DV), lambda i: (i, 0))],
        out_shape=[jax.ShapeDtypeStruct((HG_HEADS, l, LANE), BF16),
                   jax.ShapeDtypeStruct((l, RET_HEADS * RET_DV), BF16)],
        scratch_shapes=[pltpu.VMEM((HG_HEADS, HG_DV, HG_DK), F32),
                        pltpu.VMEM((HG_CHUNK, HG_DK), F32),
                        pltpu.VMEM((HG_CHUNK, HG_DK), F32),
                        pltpu.VMEM((RET_HEADS, RET_DK, RET_DV), F32),
                        pltpu.VMEM((RET_HEADS, tb, tb), F32),
                        pltpu.VMEM((3 * RET_HEADS, tb, RET_DK), F32)],
        compiler_params=_params(("arbitrary",)),
        name="mixer_fwd",
    )(p3, p3, p3, p3, p3, lf3, hsb, hg_norm_w.reshape(HG_HEADS, 1, HG_DV),
      logit, p3, p3, p3, p3, rsb, gn_w.reshape(RET_HEADS, 1, RET_DV))


def _pack_bf16_pairs(v):
    w = v.shape[1] // 2
    lo = pltpu.bitcast(v[:, :w].astype(BF16).astype(F32), U32)
    hi = pltpu.bitcast(v[:, w:].astype(BF16).astype(F32), U32)
    return (lo >> 16) | (hi & jnp.uint32(0xFFFF0000))


def _unpack_bf16_pairs(u):
    lo = pltpu.bitcast(u << 16, F32)
    hi = pltpu.bitcast(u & jnp.uint32(0xFFFF0000), F32)
    return lo, hi


def _store_planes(ref, rows, packed):
    for c in range(ref.shape[0]):
        ref[c, rows, :] = packed[:, c * LANE:(c + 1) * LANE]


def _load_planes(ref, rows):
    return jnp.concatenate([ref[c, rows, :] for c in range(ref.shape[0])], axis=1)


def _copy_row(dst_ref, dst_row, src_ref, src_row):
    for c in range(src_ref.shape[0]):
        dst_ref[c, pl.ds(dst_row, 1), :] = src_ref[c, pl.ds(src_row, 1), :]


def _route(logits):
    lane = lax.broadcasted_iota(jnp.int32, logits.shape, 1)
    big = jnp.int32(10 ** 6)
    neg = -jnp.inf
    gl = jnp.where(lane < N_GROUPS, logits, neg)
    gmax = jnp.max(gl, axis=1, keepdims=True)
    grp = jnp.min(jnp.where(gl == gmax, lane, big), axis=1, keepdims=True)
    g_val = 1.0 / jnp.sum(jnp.exp(gl - gmax), axis=1, keepdims=True)
    lo = N_GROUPS + EXPERTS_PER_GROUP * grp
    el = jnp.where((lane >= lo) & (lane < lo + EXPERTS_PER_GROUP), logits, neg)
    v1 = jnp.max(el, axis=1, keepdims=True)
    i1 = jnp.min(jnp.where(el == v1, lane, big), axis=1, keepdims=True)
    el2 = jnp.where(lane == i1, neg, el)
    v2 = jnp.max(el2, axis=1, keepdims=True)
    i2 = jnp.min(jnp.where(el2 == v2, lane, big), axis=1, keepdims=True)
    r = jnp.exp(v2 - v1)
    w1 = g_val / (1.0 + r)
    w2 = w1 * r
    return ((i1 - N_GROUPS).astype(F32), (i2 - N_GROUPS).astype(F32), w1, w2)


def _merge_kernel(ohg_ref, oret_ref, ga_ref, gb_ref, x_ref, mods_ref, wohg_ref, woret_ref,
                  wout_ref, nw_ref, wrh_ref, wrl_ref, rb_ref,
                  xo_ref, hp_ref, route_ref, route_t_ref, cnt_out_ref, cnt_ref,
                  *, lc, tm, d, tiles_per_seg):
    i = pl.program_id(0)
    tg = tm // MERGE_GROUPS
    routed = []
    for g in range(MERGE_GROUPS):
        rows = slice(g * tg, (g + 1) * tg)
        row0 = i * tm + g * tg
        ohg = jnp.concatenate([ohg_ref[s, rows, :] for s in range(HG_HEADS)], axis=1)
        ga = jnp.concatenate([ga_ref[s, rows, :] for s in range(d // LANE)], axis=1).astype(F32)
        gb = jnp.concatenate([gb_ref[s, rows, :] for s in range(d // LANE)], axis=1).astype(F32)
        y_hg = jnp.dot(ohg, wohg_ref[...], preferred_element_type=F32)
        y_ret = jnp.dot(oret_ref[rows, :], woret_ref[...], preferred_element_type=F32)
        m = (ga * y_hg + gb * y_ret).astype(BF16)
        y = jnp.dot(m, wout_ref[...], preferred_element_type=F32)
        x = x_ref[rows, :] + _row_select(row0, tg, lc, mods_ref, 2, d) * y
        xo_ref[rows, :] = x

        xn = x * lax.rsqrt(jnp.mean(x * x, axis=-1, keepdims=True) + EPS) * nw_ref[...]
        h = (xn * (1.0 + _row_select(row0, tg, lc, mods_ref, 4, d))
             + _row_select(row0, tg, lc, mods_ref, 3, d))
        _store_planes(hp_ref, rows, _pack_bf16_pairs(h))

        h_hi = h.astype(BF16)
        h_lo = (h - h_hi.astype(F32)).astype(BF16)
        logits = (jnp.dot(h_hi, wrh_ref[...], preferred_element_type=F32)
                  + jnp.dot(h_lo, wrh_ref[...], preferred_element_type=F32)
                  + jnp.dot(h_hi, wrl_ref[...], preferred_element_type=F32)) + rb_ref[...]
        routed.append(_route(logits))
    e0, e1, w0, w1 = (jnp.concatenate(parts, axis=0) for parts in zip(*routed))

    @pl.when(i % tiles_per_seg == 0)
    def _():
        cnt_ref[...] = jnp.zeros_like(cnt_ref)

    lane_e = lax.broadcasted_iota(jnp.int32, (tm, LANE), 1).astype(F32)
    hot0 = lane_e == e0
    hot1 = lane_e == e1
    hot = jnp.where(hot0 | hot1, 1.0, 0.0)
    earlier = (lax.broadcasted_iota(jnp.int32, (tm, tm), 1)
               < lax.broadcasted_iota(jnp.int32, (tm, tm), 0))
    before = jnp.dot(jnp.where(earlier, 1.0, 0.0).astype(BF16), hot.astype(BF16),
                     preferred_element_type=F32) + cnt_ref[0:1, :]
    rank0 = jnp.sum(jnp.where(hot0, before, 0.0), axis=1, keepdims=True)
    rank1 = jnp.sum(jnp.where(hot1, before, 0.0), axis=1, keepdims=True)
    cnt_ref[...] = cnt_ref[...] + jnp.sum(hot, axis=0, keepdims=True)
    cnt_out_ref[0] = cnt_ref[...]

    lane = lax.broadcasted_iota(jnp.int32, (tm, LANE), 1)
    rec = jnp.zeros((tm, LANE), F32)
    for k, val in enumerate((e0, e1, w0, w1, rank0, rank1)):
        rec = jnp.where(lane == k, val, rec)
    route_ref[...] = rec[:, :ROUTE_W]
    eye = jnp.where(lax.broadcasted_iota(jnp.int32, (ROUTE_W, LANE), 0)
                    == lax.broadcasted_iota(jnp.int32, (ROUTE_W, LANE), 1), 1.0, 0.0)
    route_t_ref[0] = lax.dot_general(eye, rec, NT_DIMS, precision=lax.Precision.HIGHEST,
                                     preferred_element_type=F32)


def _merge(ohg, oret, p3, xc, mods, wohg, woret, wout, norm_w, wr_hi, wr_lo, rbias, *, lc, n_seg):
    l, d = xc.shape
    seg_tokens = l // n_seg
    tm = _largest_divisor(seg_tokens, MERGE_TILE, 16 * MERGE_GROUPS)
    tiles_per_seg = seg_tokens // tm
    n_slab = d // LANE
    full = lambda a: pl.BlockSpec(a.shape, lambda i: (0,) * a.ndim)
    kern = functools.partial(_merge_kernel, lc=lc, tm=tm, d=d, tiles_per_seg=tiles_per_seg)
    return pl.pallas_call(
        kern,
        grid=(l // tm,),
        in_specs=[pl.BlockSpec((HG_HEADS, tm, LANE), lambda i: (0, i, 0)),
                  pl.BlockSpec((tm, oret.shape[1]), lambda i: (i, 0)),
                  pl.BlockSpec((n_slab, tm, LANE), lambda i: (S_GA // n_slab, i, 0)),
                  pl.BlockSpec((n_slab, tm, LANE), lambda i: (S_GB // n_slab, i, 0)),
                  pl.BlockSpec((tm, d), lambda i: (i, 0)),
                  full(mods), full(wohg), full(woret), full(wout),
                  pl.BlockSpec((1, d), lambda i: (0, 0)),
                  full(wr_hi), full(wr_lo), full(rbias)],
        out_specs=[pl.BlockSpec((tm, d), lambda i: (i, 0)),
                   pl.BlockSpec((d // 2 // LANE, tm, LANE), lambda i: (0, i, 0)),
                   pl.BlockSpec((tm, ROUTE_W), lambda i: (i, 0)),
                   pl.BlockSpec((1, ROUTE_W, tm), lambda i: (i, 0, 0)),
                   pl.BlockSpec((1, 8, LANE), lambda i: (i // tiles_per_seg, 0, 0))],
        out_shape=[jax.ShapeDtypeStruct((l, d), F32),
                   jax.ShapeDtypeStruct((d // 2 // LANE, l, LANE), U32),
                   jax.ShapeDtypeStruct((l, ROUTE_W), F32),
                   jax.ShapeDtypeStruct((l // tm, ROUTE_W, tm), F32),
                   jax.ShapeDtypeStruct((n_seg, 8, LANE), F32)],
        scratch_shapes=[pltpu.VMEM((8, LANE), F32)],
        compiler_params=_params(("arbitrary",)),
        name="merge_router",
    )(ohg, oret, p3, p3, xc, mods, wohg, woret, wout, norm_w.reshape(1, d), wr_hi, wr_lo, rbias)


def _moe_kernel(be_ref, nused_ref, slot_ref, hp_ref, *rest,
                lc, tm, d, expert_steps, seg_tokens, n_tokens, final):
    w_refs = rest[:3 * MOE_PER_STEP]
    route_ref, x_ref, mods_ref, fw_ref, o_ref, xs_ref, g0_ref, g1_ref = rest[3 * MOE_PER_STEP:]
    s = pl.program_id(0)
    b = pl.program_id(1)

    @pl.when(b == 0)
    def _():
        xs_ref[...] = jnp.zeros_like(xs_ref)

        def scatter(g, a0):
            t0 = pl.multiple_of(g * ROW_GROUP, ROW_GROUP)
            for k in range(ROW_GROUP):
                _copy_row(xs_ref, slot_ref[a0 + k], hp_ref, t0 + k)
                _copy_row(xs_ref, slot_ref[n_tokens + a0 + k], hp_ref, t0 + k)
            return a0 + ROW_GROUP

        lax.fori_loop(0, seg_tokens // ROW_GROUP, scatter, s * seg_tokens)

    def expert_blocks(count):
        for k in range(count):
            wg_ref, wu_ref, wd_ref = w_refs[3 * k:3 * k + 3]
            rows = pl.ds(pl.multiple_of((b * MOE_PER_STEP + k) * MOE_BLOCK, MOE_BLOCK), MOE_BLOCK)
            lo, hi = _unpack_bf16_pairs(_load_planes(xs_ref, rows))
            x = jnp.concatenate([lo, hi], axis=1).astype(BF16)
            g = jnp.dot(x, wg_ref[0], preferred_element_type=F32)
            u = jnp.dot(x, wu_ref[0], preferred_element_type=F32)
            a = (_silu(g) * u).astype(BF16)
            _store_planes(xs_ref, rows,
                          _pack_bf16_pairs(jnp.dot(a, wd_ref[0], preferred_element_type=F32)))

    n_here = jnp.clip(nused_ref[s] - b * MOE_PER_STEP, 0, MOE_PER_STEP)
    for count in range(1, MOE_PER_STEP + 1):
        pl.when(jnp.logical_and(b < expert_steps, n_here == count))(
            functools.partial(expert_blocks, count))

    @pl.when(b >= expert_steps)
    def _():
        tile0 = s * seg_tokens + (b - expert_steps) * tm

        def gather(g, a0):
            r0 = pl.multiple_of(g * ROW_GROUP, ROW_GROUP)
            for k in range(ROW_GROUP):
                _copy_row(g0_ref, r0 + k, xs_ref, slot_ref[a0 + k])
                _copy_row(g1_ref, r0 + k, xs_ref, slot_ref[n_tokens + a0 + k])
            return a0 + ROW_GROUP

        lax.fori_loop(0, tm // ROW_GROUP, gather, tile0)
        lo0, hi0 = _unpack_bf16_pairs(_load_planes(g0_ref, slice(None)))
        lo1, hi1 = _unpack_bf16_pairs(_load_planes(g1_ref, slice(None)))
        w0 = route_ref[:, 2:3]
        w1 = route_ref[:, 3:4]
        y = jnp.concatenate([w0 * lo0 + w1 * lo1, w0 * hi0 + w1 * hi1], axis=1)
        x = x_ref[...] + _row_select(tile0, tm, lc, mods_ref, 5, d) * y
        if final:
            x = x * lax.rsqrt(jnp.mean(x * x, axis=-1, keepdims=True) + EPS) * fw_ref[...]
        o_ref[...] = x


def _moe(block_expert, n_used, slot, hp, wg, wu, wd, route, xc, mods, final_w,
         *, n_seg, nbs, layer, lc, final):
    l, d = xc.shape
    n_planes = hp.shape[0]
    seg_tokens = l // n_seg
    d_e = wg.shape[3]
    tm = _largest_divisor(seg_tokens, 320, 8)
    tiles = seg_tokens // tm
    per = MOE_PER_STEP
    assert nbs % per == 0
    expert_steps = nbs // per
    kern = functools.partial(_moe_kernel, lc=lc, tm=tm, d=d, expert_steps=expert_steps,
                             seg_tokens=seg_tokens, n_tokens=l, final=final)
    w_specs = []
    for k in range(per):
        w_idx = lambda s, b, be, nu, *_, k=k: (
            layer, be[s * nbs + jnp.minimum(b * per + k, jnp.maximum(nu[s] - 1, 0))], 0, 0)
        w_specs += [pl.BlockSpec((None, 1, d, d_e), w_idx), pl.BlockSpec((None, 1, d, d_e), w_idx),
                    pl.BlockSpec((None, 1, d_e, d), w_idx)]
    tile = lambda s, b: s * tiles + jnp.clip(b - expert_steps, 0, tiles - 1)
    row = lambda s, b, *_: (tile(s, b), 0)
    if final:
        assert lc % tm == 0
        out_rows = l - lc
        out_row = lambda s, b, *_: (jnp.maximum(tile(s, b) - lc // tm, 0), 0)
    else:
        out_rows, out_row = l, row
    grid_spec = pltpu.PrefetchScalarGridSpec(
        num_scalar_prefetch=3,
        grid=(n_seg, expert_steps + tiles),
        in_specs=[pl.BlockSpec((n_planes, seg_tokens, LANE), lambda s, b, *_: (0, s, 0))]
        + w_specs + [
                  pl.BlockSpec((tm, ROUTE_W), row),
                  pl.BlockSpec((tm, d), row),
                  pl.BlockSpec(mods.shape, lambda s, b, *_: (0, 0)),
                  pl.BlockSpec((1, d), lambda s, b, *_: (0, 0))],
        out_specs=pl.BlockSpec((tm, d), out_row),
        scratch_shapes=[pltpu.VMEM((n_planes, nbs * MOE_BLOCK, LANE), U32),
                        pltpu.VMEM((n_planes, tm, LANE), U32),
                        pltpu.VMEM((n_planes, tm, LANE), U32)])
    return pl.pallas_call(
        kern,
        grid_spec=grid_spec,
        out_shape=jax.ShapeDtypeStruct((out_rows, d), F32),
        compiler_params=_params(("arbitrary", "arbitrary")),
        name="moe",
    )(block_expert, n_used, slot, hp, *((wg, wu, wd) * per), route, xc, mods,
      final_w.reshape(1, d))


def _dispatch_plan(route_t, counts, nbs):
    counts_f32 = counts
    counts = counts[:, 0, :N_EXPERTS].astype(jnp.int32)
    padded = ((counts + MOE_BLOCK - 1) // MOE_BLOCK) * MOE_BLOCK
    pad_end = jnp.cumsum(padded, axis=1)
    starts = jnp.arange(nbs, dtype=jnp.int32) * MOE_BLOCK
    block_expert = jnp.minimum(
        jnp.sum((pad_end[:, None, :] <= starts[None, :, None]).astype(jnp.int32), axis=2),
        N_EXPERTS - 1)
    n_used = pad_end[:, -1] // MOE_BLOCK
    slot = _slots(route_t, counts_f32)
    return (jnp.swapaxes(slot, 0, 1).reshape(-1), block_expert.reshape(-1).astype(jnp.int32),
            n_used.astype(jnp.int32))


def _slot_kernel(rt_ref, cnt_ref, slot_ref, *, tiles_per_seg):
    s = pl.program_id(0) // tiles_per_seg
    cnt = cnt_ref[s]
    blocks = jnp.floor((cnt + (MOE_BLOCK - 1)) * (1.0 / MOE_BLOCK))
    before = jnp.where(lax.broadcasted_iota(jnp.int32, (LANE, LANE), 0)
                       < lax.broadcasted_iota(jnp.int32, (LANE, LANE), 1), 1.0, 0.0)
    first_block = jnp.dot(blocks.astype(BF16), before.astype(BF16),
                          preferred_element_type=F32)
    tm = rt_ref.shape[2]
    expert_id = lax.broadcasted_iota(jnp.int32, (LANE, tm), 0).astype(F32)
    for k in range(2):
        hot = jnp.where(expert_id == rt_ref[0, k:k + 1, :], 1.0, 0.0)
        first = jnp.dot(first_block.astype(BF16), hot.astype(BF16),
                        preferred_element_type=F32)[0:1] * MOE_BLOCK
        slot_ref[0, k:k + 1, :] = (first + rt_ref[0, 4 + k:5 + k, :]).astype(jnp.int32)


def _slots(route_t, counts):
    n_tiles, _, tm = route_t.shape
    n_seg = counts.shape[0]
    kern = functools.partial(_slot_kernel, tiles_per_seg=n_tiles // n_seg)
    return pl.pallas_call(
        kern,
        grid=(n_tiles,),
        in_specs=[pl.BlockSpec((1, ROUTE_W, tm), lambda i: (i, 0, 0)),
                  pl.BlockSpec(counts.shape, lambda i: (0, 0, 0))],
        out_specs=pl.BlockSpec((1, 2, tm), lambda i: (i, 0, 0)),
        out_shape=jax.ShapeDtypeStruct((n_tiles, 2, tm), jnp.int32),
        compiler_params=_params(("arbitrary",)),
        name="moe_slots",
    )(route_t, counts)


def _rope_tables(lc, t):
    quarter = RET_DK // 4
    n_rows = t // GRID_W
    inv = ROPE_BASE ** (-np.arange(0, 2 * quarter, 2, dtype=np.float64) / (2 * quarter))
    ang_r = np.arange(n_rows, dtype=np.float64)[:, None] * inv
    ang_c = np.arange(GRID_W, dtype=np.float64)[:, None] * inv

    def expand(row_part, col_part):
        r = np.broadcast_to(row_part[:, None, :], (n_rows, GRID_W, 2 * quarter))
        c = np.broadcast_to(col_part[None, :, :], (n_rows, GRID_W, 2 * quarter))
        return np.concatenate([r, c], axis=2).reshape(t, RET_DK)

    two = lambda a: np.concatenate([a, a], axis=1)
    cos = expand(two(np.cos(ang_r)), two(np.cos(ang_c)))
    sin = expand(np.concatenate([-np.sin(ang_r), np.sin(ang_r)], axis=1),
                 np.concatenate([-np.sin(ang_c), np.sin(ang_c)], axis=1))
    cos = np.concatenate([np.ones((lc, RET_DK)), cos], axis=0).astype(np.float32)
    sin = np.concatenate([np.zeros((lc, RET_DK)), sin], axis=0).astype(np.float32)
    return jnp.asarray(cos), jnp.asarray(sin)


def kernel(x, c, ctx, c_ctx, w_mod, b_mod, norm_mix_w, norm_ffn_w, w_in, hgrn_lb_logits, hgrn_norm_w,
           ret_decay_logit, ret_gn_w, w_o_hgrn, w_o_ret, w_out, router_group_w, router_group_b,
           router_expert_w, router_expert_b, expert_w_gate, expert_w_up, expert_w_down, final_norm_w):
    b_, t_, d = x.shape
    assert b_ == 1
    lc = ctx.shape[1]
    depth = w_mod.shape[0]
    l = lc + t_
    assert lc % HG_BLOCK == 0 and l % HG_BLOCK == 0 and lc % RET_BLOCK == 0 and l % RET_BLOCK == 0
    n_seg = MOE_SEGMENTS
    assert l % n_seg == 0
    seg_tokens = l // n_seg
    nbs = pl.cdiv(2 * seg_tokens + N_EXPERTS * (MOE_BLOCK - 1), MOE_BLOCK)
    nbs = pl.cdiv(nbs, MOE_PER_STEP) * MOE_PER_STEP
    assert nbs <= 256

    xc = jnp.concatenate([ctx[0], x[0]], axis=0)
    cc = jnp.zeros((8, d), F32).at[0].set(c[0]).at[1].set(c_ctx)
    mods_all = _modulation(cc, w_mod, b_mod)
    cos_t, sin_t = _rope_tables(lc, t_)
    w_in_bf = w_in.astype(BF16)
    w_gate_bf, w_up_bf, w_down_bf = (w.astype(BF16)
                                     for w in (expert_w_gate, expert_w_up, expert_w_down))

    for layer in range(depth):
        mods = mods_all[layer]
        p3, lf3 = _inproj(xc, mods, norm_mix_w[layer], w_in_bf, hgrn_lb_logits,
                          cos_t, sin_t, layer=layer, lc=lc)
        sb_hg, sb_ret = _mixer_bwd(ret_decay_logit[layer], p3, lf3, lc=lc)
        ohg, oret = _mixer_fwd(ret_decay_logit[layer], p3, lf3, sb_hg, sb_ret,
                               hgrn_norm_w[layer], ret_gn_w[layer])

        wr = jnp.concatenate([router_group_w[layer], router_expert_w[layer]], axis=1)
        wr = jnp.pad(wr, ((0, 0), (0, LANE - wr.shape[1])))
        wr_hi = wr.astype(BF16)
        wr_lo = (wr - wr_hi.astype(F32)).astype(BF16)
        rbias = jnp.pad(jnp.concatenate([router_group_b[layer], router_expert_b[layer]]),
                        (0, LANE - N_GROUPS - N_EXPERTS)).reshape(1, LANE)
        xc, hp, route, route_t, counts = _merge(
            ohg, oret, p3, xc, mods, w_o_hgrn[layer].astype(BF16), w_o_ret[layer].astype(BF16),
            w_out[layer].astype(BF16), norm_ffn_w[layer], wr_hi, wr_lo, rbias, lc=lc, n_seg=n_seg)

        slot, block_expert, n_used = _dispatch_plan(route_t, counts, nbs)
        xc = _moe(block_expert, n_used, slot, hp, w_gate_bf, w_up_bf, w_down_bf, route, xc, mods,
                  final_norm_w, n_seg=n_seg, nbs=nbs, layer=layer, lc=lc,
                  final=(layer == depth - 1))

    return xc[None]
```
